```python
import math
import jax, jax.numpy as jnp
from jax import lax
import numpy as np

D_MODEL = 1024
BATCH = 8
SEQ = 4096
DEPTH = 1

CHUNK = 64
Q_BLOCK = 128
SB_HEADS = 8
SB_HEAD_DIM = 64
SB_WIDTH = SB_HEADS * SB_HEAD_DIM
HG_HEADS = 4
HG_HEAD_DIM = 128
HG_WIDTH = HG_HEADS * HG_HEAD_DIM
N_BRANCHES = 2
IN_COLS = 3 * SB_WIDTH + 4 * HG_WIDTH + N_BRANCHES * D_MODEL
N_EXPERTS = 64
TOP_K = 8
N_GROUPS = 8
TOPK_GROUPS = 4
EXPERT_DIM = 256
SHARED_DIM = 256
ROUTED_SCALE = 2.5
DISPATCH_BLOCK = 256
N_MOD = 6
EPS = 1e-6

kernel_name = 'chunk_causal_sb_hgrn2_moe_block'


def rms_norm(x, g):
    xf = x.astype(jnp.float32)
    y = xf * lax.rsqrt(jnp.mean(xf * xf, axis=-1, keepdims=True) + EPS)
    return (y * g.astype(jnp.float32)).astype(x.dtype)


def swiglu(h, w_gate, w_up, w_down):
    return (jax.nn.silu(h @ w_gate) * (h @ w_up)) @ w_down


def stick_breaking_attention(q, k, v):
    B, H, S, dh = q.shape
    scale = dh ** -0.5
    outs = []
    for blk in range(S // Q_BLOCK):
        q0 = blk * Q_BLOCK
        k_end = q0 + Q_BLOCK
        qb = q[:, :, q0:k_end].astype(jnp.float32)
        kb = k[:, :, :k_end].astype(jnp.float32)
        vb = v[:, :, :k_end].astype(jnp.float32)
        z = jnp.einsum('bhqd,bhkd->bhqk', qb, kb) * scale
        t_pos = q0 + jnp.arange(Q_BLOCK)[:, None]
        s_pos = jnp.arange(k_end)[None, :]
        strict = s_pos < t_pos
        log_beta = jax.nn.log_sigmoid(z)
        log_keep = jnp.where(strict, log_beta - z, 0.0)
        after = lax.cumsum(log_keep, axis=3, reverse=True) - log_keep
        a = jnp.where(strict, jnp.exp(log_beta + after), 0.0)
        outs.append(jnp.einsum('bhqk,bhkd->bhqd', a, vb))
    return jnp.concatenate(outs, axis=2).astype(v.dtype)


def hgrn2_chunkwise(q, f_logits, i, gate, lb, norm_g):
    B, H, S, dk = q.shape
    dv = i.shape[-1]
    n = S // CHUNK
    f32 = jnp.float32
    lbb = lb.astype(f32)[None, :, None, :]
    forget = lbb + (1.0 - lbb) * jax.nn.sigmoid(f_logits.astype(f32))
    chunks = lambda t: t.reshape(B, H, n, CHUNK, t.shape[-1])
    qc = chunks(jax.nn.silu(q.astype(f32)))
    kc = chunks(1.0 - forget)
    vc = chunks(i.astype(f32))
    gc = jnp.cumsum(chunks(jnp.log(forget)), axis=3)
    g_ref = gc[:, :, :, CHUNK // 2 - 1:CHUNK // 2]
    causal = jnp.tril(jnp.ones((CHUNK, CHUNK), bool))
    scores = jnp.einsum('bhncd,bhnsd->bhncs', qc * jnp.exp(gc - g_ref), kc * jnp.exp(g_ref - gc))
    o_intra = jnp.einsum('bhncs,bhnsv->bhncv', jnp.where(causal, scores, 0.0), vc)
    g_last = gc[:, :, :, -1]
    chunk_kv = jnp.einsum('bhncd,bhncv->bhndv', kc * jnp.exp(g_last[:, :, :, None] - gc), vc)

    def carry_state(state, inp):
        decay, kv = inp
        return decay[..., None] * state + kv, state

    _, s_prev = lax.scan(carry_state, jnp.zeros((B, H, dk, dv), f32),
                         (jnp.moveaxis(jnp.exp(g_last), 2, 0), jnp.moveaxis(chunk_kv, 2, 0)))
    s_prev = jnp.moveaxis(s_prev, 0, 2)
    o_inter = jnp.einsum('bhncd,bhndv->bhncv', qc * jnp.exp(gc), s_prev)
    o = (o_intra + o_inter).reshape(B, H, S, dv)
    o = rms_norm(o, norm_g) * jax.nn.silu(gate.astype(f32))
    return o.astype(i.dtype)


def routed_experts(h, w_router, router_bias, w_gate, w_up, w_down):
    T, D = h.shape
    scores = jax.nn.sigmoid(h.astype(jnp.float32) @ w_router.astype(jnp.float32))
    choice = scores + router_bias.astype(jnp.float32)
    grouped = choice.reshape(T, N_GROUPS, N_EXPERTS // N_GROUPS)
    group_score = lax.top_k(grouped, 2)[0].sum(-1)
    _, top_groups = lax.top_k(group_score, TOPK_GROUPS)
    group_mask = jnp.zeros((T, N_GROUPS), bool).at[jnp.arange(T)[:, None], top_groups].set(True)
    expert_mask = jnp.repeat(group_mask, N_EXPERTS // N_GROUPS, axis=1)
    _, idx = lax.top_k(jnp.where(expert_mask, choice, -jnp.inf), TOP_K)
    sel = jnp.take_along_axis(scores, idx, axis=1)
    gates = sel / jnp.sum(sel, axis=1, keepdims=True) * ROUTED_SCALE

    n_assign = T * TOP_K
    flat_e = idx.reshape(-1).astype(jnp.int32)
    flat_tok = jnp.repeat(jnp.arange(T, dtype=jnp.int32), TOP_K)
    flat_w = gates.reshape(-1)
    order = jnp.argsort(flat_e)
    e_sorted = flat_e[order]
    counts = jnp.zeros((N_EXPERTS,), jnp.int32).at[flat_e].add(1)
    padded = (counts + DISPATCH_BLOCK - 1) // DISPATCH_BLOCK * DISPATCH_BLOCK
    pad_end = jnp.cumsum(padded)
    pad_start = pad_end - padded
    start = jnp.cumsum(counts) - counts
    rank = jnp.arange(n_assign, dtype=jnp.int32) - start[e_sorted]
    dest = pad_start[e_sorted] + rank
    n_slots = -(-(n_assign + N_EXPERTS * (DISPATCH_BLOCK - 1)) // DISPATCH_BLOCK) * DISPATCH_BLOCK
    n_blocks = n_slots // DISPATCH_BLOCK
    slot_tok = jnp.full((n_slots,), T, jnp.int32).at[dest].set(flat_tok[order])
    slot_w = jnp.zeros((n_slots,), jnp.float32).at[dest].set(flat_w[order])
    block_start = jnp.arange(n_blocks, dtype=jnp.int32) * DISPATCH_BLOCK
    block_expert = jnp.minimum(jnp.searchsorted(pad_end, block_start, side='right'), N_EXPERTS - 1)
    h_pad = jnp.concatenate([h, jnp.zeros((1, D), h.dtype)], axis=0)

    def expert_block(acc, inp):
        tok, w, e = inp
        rows = h_pad[tok]
        y = swiglu(rows, w_gate[e], w_up[e], w_down[e])
        return acc.at[tok].add((y * w[:, None].astype(y.dtype)).astype(acc.dtype)), None

    acc, _ = lax.scan(expert_block, jnp.zeros((T + 1, D), h.dtype),
                      (slot_tok.reshape(n_blocks, DISPATCH_BLOCK),
                       slot_w.reshape(n_blocks, DISPATCH_BLOCK), block_expert))
    return acc[:T]


def setup_inputs(seed: int = 0) -> dict:
    key = jax.random.key(seed)
    ks = jax.random.split(key, 22)
    f32 = jnp.float32
    L, D = DEPTH, D_MODEL

    def nrm(k, shape, scale):
        return jax.random.normal(k, shape, f32) * scale

    return {
        'x': nrm(ks[0], (BATCH, SEQ, D), 1.0),
        'c': nrm(ks[1], (BATCH, D), 1.0),
        'w_ada': nrm(ks[2], (L, D, N_MOD * D), 0.5 * D ** -0.5),
        'b_ada': nrm(ks[3], (L, N_MOD * D), 0.02),
        'norm1_g': 1.0 + nrm(ks[4], (L, D), 0.02),
        'w_in': nrm(ks[5], (L, D, IN_COLS), D ** -0.5),
        'sb_q_norm_g': 1.0 + nrm(ks[6], (L, SB_HEAD_DIM), 0.02),
        'sb_k_norm_g': 1.0 + nrm(ks[7], (L, SB_HEAD_DIM), 0.02),
        'hg_lb_logits': nrm(ks[8], (DEPTH + 1, HG_WIDTH), 0.1),
        'hg_norm_g': 1.0 + nrm(ks[9], (L, HG_HEAD_DIM), 0.02),
        'w_branch_sb': nrm(ks[10], (L, SB_WIDTH, D), SB_WIDTH ** -0.5),
        'w_branch_hg': nrm(ks[11], (L, HG_WIDTH, D), HG_WIDTH ** -0.5),
        'w_out': nrm(ks[12], (L, D, D), D ** -0.5),
        'norm2_g': 1.0 + nrm(ks[13], (L, D), 0.02),
        'w_router': nrm(ks[14], (L, D, N_EXPERTS), D ** -0.5),
        'router_bias': nrm(ks[15], (L, N_EXPERTS), 0.01),
        'w_e_gate': nrm(ks[16], (L, N_EXPERTS, D, EXPERT_DIM), D ** -0.5),
        'w_e_up': nrm(ks[17], (L, N_EXPERTS, D, EXPERT_DIM), D ** -0.5),
        'w_e_down': nrm(ks[18], (L, N_EXPERTS, EXPERT_DIM, D), EXPERT_DIM ** -0.5),
        'w_s_gate': nrm(ks[19], (L, D, SHARED_DIM), D ** -0.5),
        'w_s_up': nrm(ks[20], (L, D, SHARED_DIM), D ** -0.5),
        'w_s_down': nrm(ks[21], (L, SHARED_DIM, D), SHARED_DIM ** -0.5),
    }


def reference(x, c, w_ada, b_ada, norm1_g, w_in, sb_q_norm_g, sb_k_norm_g, hg_lb_logits,
              hg_norm_g, w_branch_sb, w_branch_hg, w_out, norm2_g, w_router, router_bias,
              w_e_gate, w_e_up, w_e_down, w_s_gate, w_s_up, w_s_down):
    B, S, D = x.shape
    cond = jax.nn.silu(c)
    lower_bounds = jnp.cumsum(jax.nn.softmax(hg_lb_logits.astype(jnp.float32), axis=0), axis=0)
    splits = list(np.cumsum([SB_WIDTH] * 3 + [HG_WIDTH] * 4 + [D_MODEL])[:])

    def heads(t, n):
        return t.reshape(B, S, n, -1).transpose(0, 2, 1, 3)

    for l in range(DEPTH):
        mod = cond @ w_ada[l] + b_ada[l]
        shift1, scale1, gate1, shift2, scale2, gate2 = jnp.split(mod[:, None, :], N_MOD, axis=-1)

        h = rms_norm(x, norm1_g[l]) * (1 + scale1) + shift1
        proj = h @ w_in[l]
        sb_q, sb_k, sb_v, hg_f, hg_i, hg_q, hg_g, g_sb, g_hg = jnp.split(proj, splits, axis=-1)

        q = rms_norm(heads(sb_q, SB_HEADS), sb_q_norm_g[l])
        k = rms_norm(heads(sb_k, SB_HEADS), sb_k_norm_g[l])
        o_sb = stick_breaking_attention(q, k, heads(sb_v, SB_HEADS))
        o_sb = o_sb.transpose(0, 2, 1, 3).reshape(B, S, SB_WIDTH)

        o_hg = hgrn2_chunkwise(heads(hg_q, HG_HEADS), heads(hg_f, HG_HEADS), heads(hg_i, HG_HEADS),
                               heads(hg_g, HG_HEADS),
                               lower_bounds[l].reshape(HG_HEADS, HG_HEAD_DIM), hg_norm_g[l])
        o_hg = o_hg.transpose(0, 2, 1, 3).reshape(B, S, HG_WIDTH)

        merged = (jax.nn.sigmoid(g_sb) * (o_sb @ w_branch_sb[l])
                  + jax.nn.sigmoid(g_hg) * (o_hg @ w_branch_hg[l]))
        x = x + gate1 * (merged @ w_out[l])

        h2 = (rms_norm(x, norm2_g[l]) * (1 + scale2) + shift2).reshape(B * S, D)
        ffn = (swiglu(h2, w_s_gate[l], w_s_up[l], w_s_down[l])
               + routed_experts(h2, w_router[l], router_bias[l], w_e_gate[l], w_e_up[l], w_e_down[l]))
        x = x + gate2 * ffn.reshape(B, S, D)
    return x
```

```python
import functools

import jax
import jax.numpy as jnp
from jax import lax
from jax.experimental import pallas as pl
from jax.experimental.pallas import tpu as pltpu

F32 = jnp.float32
BF16 = jnp.bfloat16

SB_HEADS = 8
SB_HEAD_DIM = 64
SB_WIDTH = SB_HEADS * SB_HEAD_DIM
HG_HEADS = 4
HG_HEAD_DIM = 128
HG_WIDTH = HG_HEADS * HG_HEAD_DIM
HG_CHUNK = 64
N_EXPERTS = 64
TOP_K = 8
N_GROUPS = 8
TOPK_GROUPS = 4
GROUP_SIZE = N_EXPERTS // N_GROUPS
ROUTED_SCALE = 2.5
N_MOD = 6
EPS = 1e-6

LANES = 128
VMEM_LIMIT = 56 * 1024 * 1024


def _cparams(*sem):
    return pltpu.CompilerParams(dimension_semantics=sem, vmem_limit_bytes=VMEM_LIMIT)


def _silu(t):
    return t * jax.nn.sigmoid(t)


def _dot(a, b):
    return jnp.dot(a, b, preferred_element_type=F32)


def _dot_nt(a, b):
    return lax.dot_general(a, b, (((1,), (1,)), ((), ())), preferred_element_type=F32)


def _split_bf16(t):
    hi = t.astype(BF16)
    lo = (t - hi.astype(F32)).astype(BF16)
    return hi, lo


def _ada_kernel(c_ref, w_ref, b_ref, o_ref):
    cond = _silu(c_ref[...])
    o_ref[...] = _dot(cond, w_ref[...]) + b_ref[...]


def _ada(c, w, b):
    bsz, d = c.shape
    n = w.shape[1]
    tn = 1024
    return pl.pallas_call(
        _ada_kernel,
        grid=(n // tn,),
        in_specs=[pl.BlockSpec((bsz, d), lambda j: (0, 0)),
                  pl.BlockSpec((d, tn), lambda j: (0, j)),
                  pl.BlockSpec((1, tn), lambda j: (0, j))],
        out_specs=pl.BlockSpec((bsz, tn), lambda j: (0, j)),
        out_shape=jax.ShapeDtypeStruct((bsz, n), F32),
        compiler_params=_cparams("parallel"),
    )(c, w, b.reshape(1, n))


def _modulated_norm(x, g, shift, scale):
    y = x * lax.rsqrt(jnp.mean(x * x, axis=-1, keepdims=True) + EPS) * g
    return y * (1.0 + scale) + shift


def _inproj_kernel(x_ref, mod_ref, g_ref, w_ref, o_ref, h_ref):
    @pl.when(pl.program_id(2) == 0)
    def _():
        h = _modulated_norm(x_ref[0], g_ref[...], mod_ref[0, 0:1, :], mod_ref[0, 1:2, :])
        h_ref[...] = h.astype(BF16)

    o_ref[0] = _dot(h_ref[...], w_ref[...]).astype(o_ref.dtype)


def _inproj(x, mod, g, w):
    bsz, s, d = x.shape
    n = w.shape[1]
    tm = min(1024, s)
    tn = 512
    return pl.pallas_call(
        _inproj_kernel,
        grid=(bsz, s // tm, n // tn),
        in_specs=[pl.BlockSpec((1, tm, d), lambda b, i, j: (b, i, 0)),
                  pl.BlockSpec((1, N_MOD, d), lambda b, i, j: (b, 0, 0)),
                  pl.BlockSpec((1, d), lambda b, i, j: (0, 0)),
                  pl.BlockSpec((d, tn), lambda b, i, j: (0, j))],
        out_specs=pl.BlockSpec((1, tm, tn), lambda b, i, j: (b, i, j)),
        out_shape=jax.ShapeDtypeStruct((bsz, s, n), BF16),
        scratch_shapes=[pltpu.VMEM((tm, d), BF16)],
        compiler_params=_cparams("parallel", "parallel", "arbitrary"),
    )(x, mod, g.reshape(1, d), w)


def _pair_norm(t, g, lo_half):
    sq = t * t
    s_lo = jnp.sum(jnp.where(lo_half, sq, 0.0), axis=-1, keepdims=True)
    s_hi = jnp.sum(jnp.where(lo_half, 0.0, sq), axis=-1, keepdims=True)
    ms = jnp.where(lo_half, s_lo, s_hi) * (1.0 / SB_HEAD_DIM)
    return t * lax.rsqrt(ms + EPS) * g


def _sb_kernel(q_ref, k_ref, v_ref, qg_ref, kg_ref, o_ref, kn_ref, *, tq):
    qi = pl.program_id(2)
    s = k_ref.shape[1]
    lane = lax.broadcasted_iota(jnp.int32, (1, LANES), 1)
    lo_half = lane < SB_HEAD_DIM

    @pl.when(qi == 0)
    def _():
        def norm_block(j, c):
            r0 = pl.multiple_of(j * tq, tq)
            kb = k_ref[0, pl.ds(r0, tq), :].astype(F32)
            kn_ref[pl.ds(r0, tq), :] = _pair_norm(kb, kg_ref[...], lo_half).astype(BF16)
            return c
        lax.fori_loop(0, s // tq, norm_block, 0)

    scale = SB_HEAD_DIM ** -0.5
    q = _pair_norm(q_ref[0].astype(F32), qg_ref[...], lo_half) * scale

    row = lax.broadcasted_iota(jnp.int32, (tq, tq), 0)
    col = lax.broadcasted_iota(jnp.int32, (tq, tq), 1)
    strict = col < row
    upper = (row > col).astype(BF16)

    def block(qh, j, run, acc, masked):
        r0 = pl.multiple_of(j * tq, tq)
        z = _dot_nt(qh, kn_ref[pl.ds(r0, tq), :])
        sp = jnp.maximum(z, 0.0) + jnp.log(1.0 + jnp.exp(-jnp.abs(z)))
        if masked:
            sp = jnp.where(strict, sp, 0.0)
        hi, lo = _split_bf16(sp)
        cs = _dot(hi, upper) + _dot(lo, upper)
        a = jnp.exp(z - sp - cs - run)
        if masked:
            a = jnp.where(strict, a, 0.0)
        acc = acc + _dot(a.astype(BF16), v_ref[0, pl.ds(r0, tq), :])
        run = run + jnp.sum(sp, axis=-1, keepdims=True)
        return run, acc

    outs = []
    for half in (lo_half, jnp.logical_not(lo_half)):
        qh = jnp.where(half, q, 0.0).astype(BF16)
        run0 = jnp.zeros((tq, 1), F32)
        acc0 = jnp.zeros((tq, LANES), F32)
        run1, acc1 = block(qh, qi, run0, acc0, True)

        def body(i, carry, qh=qh):
            return block(qh, qi - 1 - i, carry[0], carry[1], False)

        _, acc2 = lax.fori_loop(0, qi, body, (run1, acc1))
        outs.append(acc2)
    o_ref[0] = jnp.where(lo_half, outs[0], outs[1]).astype(o_ref.dtype)


def _sb_attention(proj, qg, kg, col0):
    bsz, s, _ = proj.shape
    tq = min(256, s)
    npair = SB_WIDTH // LANES
    qg2 = jnp.tile(qg.reshape(1, SB_HEAD_DIM), (1, 2))
    kg2 = jnp.tile(kg.reshape(1, SB_HEAD_DIM), (1, 2))
    return pl.pallas_call(
        functools.partial(_sb_kernel, tq=tq),
        grid=(bsz, npair, s // tq),
        in_specs=[pl.BlockSpec((1, tq, LANES), lambda b, h, i: (b, i, col0 + h)),
                  pl.BlockSpec((1, s, LANES), lambda b, h, i: (b, 0, col0 + npair + h)),
                  pl.BlockSpec((1, s, LANES), lambda b, h, i: (b, 0, col0 + 2 * npair + h)),
                  pl.BlockSpec((1, LANES), lambda b, h, i: (0, 0)),
                  pl.BlockSpec((1, LANES), lambda b, h, i: (0, 0))],
        out_specs=pl.BlockSpec((1, tq, LANES), lambda b, h, i: (b, i, h)),
        out_shape=jax.ShapeDtypeStruct((bsz, s, SB_WIDTH), BF16),
        scratch_shapes=[pltpu.VMEM((s, LANES), BF16)],
        compiler_params=_cparams("parallel", "parallel", "arbitrary"),
    )(proj, proj, proj, qg2, kg2)


def _hgrn_kernel(f_ref, i_ref, q_ref, g_ref, lbl_ref, ng_ref, o_ref, *, layer):
    s = f_ref.shape[1]
    c = HG_CHUNK
    lg = lbl_ref[...]
    e = jnp.exp(lg - jnp.max(lg, axis=0, keepdims=True))
    lb = jnp.sum(e[:layer + 1], axis=0, keepdims=True) / jnp.sum(e, axis=0, keepdims=True)

    row = lax.broadcasted_iota(jnp.int32, (c, c), 0)
    col = lax.broadcasted_iota(jnp.int32, (c, c), 1)
    causal = col <= row
    lower = causal.astype(BF16)

    def body(n, st):
        r0 = pl.multiple_of(n * c, c)
        rows = pl.ds(r0, c)
        forget = lb + (1.0 - lb) * jax.nn.sigmoid(f_ref[0, rows, :].astype(F32))
        kc = 1.0 - forget
        hi, lo = _split_bf16(jnp.log(forget))
        gc = _dot(lower, hi) + _dot(lower, lo)
        g_mid = gc[c // 2 - 1:c // 2, :]
        g_last = gc[c - 1:c, :]
        qs = _silu(q_ref[0, rows, :].astype(F32))
        v = i_ref[0, rows, :]
        scores = _dot_nt((qs * jnp.exp(gc - g_mid)).astype(BF16),
                         (kc * jnp.exp(g_mid - gc)).astype(BF16))
        o_intra = _dot(jnp.where(causal, scores, 0.0).astype(BF16), v)
        kd = (kc * jnp.exp(g_last - gc)).astype(BF16)
        kv_t = _dot(v.astype(F32).T.astype(BF16), kd)
        o_inter = _dot_nt((qs * jnp.exp(gc)).astype(BF16), st.astype(BF16))
        o = o_intra + o_inter
        y = o * lax.rsqrt(jnp.mean(o * o, axis=-1, keepdims=True) + EPS) * ng_ref[...]
        y = y * _silu(g_ref[0, rows, :].astype(F32))
        o_ref[0, rows, :] = y.astype(o_ref.dtype)
        return st * jnp.exp(g_last) + kv_t

    lax.fori_loop(0, s // c, body, jnp.zeros((HG_HEAD_DIM, HG_HEAD_DIM), F32))


def _hgrn(proj, lb_logits, norm_g, col0, layer):
    bsz, s, _ = proj.shape
    nl = lb_logits.shape[0]
    spec = lambda off: pl.BlockSpec((1, s, LANES), lambda b, h: (b, 0, col0 + off * HG_HEADS + h))
    return pl.pallas_call(
        functools.partial(_hgrn_kernel, layer=layer),
        grid=(bsz, HG_HEADS),
        in_specs=[spec(0), spec(1), spec(2), spec(3),
                  pl.BlockSpec((nl, LANES), lambda b, h: (0, h)),
                  pl.BlockSpec((1, LANES), lambda b, h: (0, 0))],
        out_specs=pl.BlockSpec((1, s, LANES), lambda b, h: (b, 0, h)),
        out_shape=jax.ShapeDtypeStruct((bsz, s, HG_WIDTH), BF16),
        compiler_params=_cparams("parallel", "parallel"),
    )(proj, proj, proj, proj, lb_logits, norm_g.reshape(1, HG_HEAD_DIM))


def _merge_kernel(x_ref, osb_ref, ohg_ref, gsb_ref, ghg_ref, mod_ref, g2_ref, wsb_ref, whg_ref,
                  wout_ref, wrh_ref, wrl_ref, x1_ref, h2_ref, lg_ref):
    m_sb = _dot(osb_ref[0], wsb_ref[...])
    m_hg = _dot(ohg_ref[0], whg_ref[...])
    merged = (jax.nn.sigmoid(gsb_ref[0].astype(F32)) * m_sb
              + jax.nn.sigmoid(ghg_ref[0].astype(F32)) * m_hg)
    x1 = x_ref[0] + mod_ref[0, 2:3, :] * _dot(merged.astype(BF16), wout_ref[...])
    x1_ref[0] = x1
    h2 = _modulated_norm(x1, g2_ref[...], mod_ref[0, 3:4, :], mod_ref[0, 4:5, :])
    hi, lo = _split_bf16(h2)
    h2_ref[0] = hi
    lg_ref[...] = _dot_nt(wrh_ref[...], hi) + _dot_nt(wrh_ref[...], lo) + _dot_nt(wrl_ref[...], hi)


def _merge(x, o_sb, o_hg, proj, mod, g2, w_sb, w_hg, w_out, wr_hi, wr_lo):
    bsz, s, d = x.shape
    tm = min(512, s)
    ns = s // tm
    full = lambda shape: pl.BlockSpec(shape, lambda b, i: (0,) * len(shape))
    return pl.pallas_call(
        _merge_kernel,
        grid=(bsz, ns),
        in_specs=[pl.BlockSpec((1, tm, d), lambda b, i: (b, i, 0)),
                  pl.BlockSpec((1, tm, SB_WIDTH), lambda b, i: (b, i, 0)),
                  pl.BlockSpec((1, tm, HG_WIDTH), lambda b, i: (b, i, 0)),
                  pl.BlockSpec((1, tm, d), lambda b, i: (b, i, 0)),
                  pl.BlockSpec((1, tm, d), lambda b, i: (b, i, 1)),
                  pl.BlockSpec((1, N_MOD, d), lambda b, i: (b, 0, 0)),
                  full((1, d)), full(w_sb.shape), full(w_hg.shape), full(w_out.shape),
                  full(wr_hi.shape), full(wr_lo.shape)],
        out_specs=[pl.BlockSpec((1, tm, d), lambda b, i: (b, i, 0)),
                   pl.BlockSpec((1, tm, d), lambda b, i: (b, i, 0)),
                   pl.BlockSpec((N_EXPERTS, tm), lambda b, i: (0, b * ns + i))],
        out_shape=[jax.ShapeDtypeStruct((bsz, s, d), F32),
                   jax.ShapeDtypeStruct((bsz, s, d), BF16),
                   jax.ShapeDtypeStruct((N_EXPERTS, bsz * s), F32)],
        compiler_params=_cparams("parallel", "parallel"),
    )(x, o_sb, o_hg, proj, proj, mod, g2.reshape(1, d), w_sb, w_hg, w_out, wr_hi, wr_lo)


def _first_argmax(vals, idx, sentinel):
    m = jnp.max(vals, axis=0, keepdims=True)
    first = jnp.min(jnp.where(vals == m, idx, sentinel), axis=0, keepdims=True)
    return m, first


def _route_kernel(lg_ref, bias_ref, o_ref):
    tn = lg_ref.shape[1]
    neg = -jnp.inf
    scores = jax.nn.sigmoid(lg_ref[...])
    choice = scores + bias_ref[...]

    gidx = lax.broadcasted_iota(jnp.int32, (GROUP_SIZE, tn), 0)
    group_rows = []
    for g in range(N_GROUPS):
        cg = choice[g * GROUP_SIZE:(g + 1) * GROUP_SIZE, :]
        m1, i1 = _first_argmax(cg, gidx, GROUP_SIZE)
        m2 = jnp.max(jnp.where(gidx == i1, neg, cg), axis=0, keepdims=True)
        group_rows.append(m1 + m2)
    work = jnp.concatenate(group_rows, axis=0)
    ggi = lax.broadcasted_iota(jnp.int32, (N_GROUPS, tn), 0)
    gmask = jnp.zeros((N_GROUPS, tn), F32)
    for _ in range(TOPK_GROUPS):
        _, first = _first_argmax(work, ggi, N_GROUPS)
        pick = ggi == first
        gmask = jnp.where(pick, 1.0, gmask)
        work = jnp.where(pick, neg, work)

    masked = jnp.concatenate(
        [jnp.where(gmask[g:g + 1, :] > 0.0, choice[g * GROUP_SIZE:(g + 1) * GROUP_SIZE, :], neg)
         for g in range(N_GROUPS)], axis=0)
    eidx = lax.broadcasted_iota(jnp.int32, (N_EXPERTS, tn), 0)
    sel = jnp.zeros((N_EXPERTS, tn), F32)
    for _ in range(TOP_K):
        _, first = _first_argmax(masked, eidx, N_EXPERTS)
        pick = eidx == first
        sel = jnp.where(pick, 1.0, sel)
        masked = jnp.where(pick, neg, masked)

    chosen = jnp.where(sel > 0.0, scores, 0.0)
    gates = chosen / jnp.sum(chosen, axis=0, keepdims=True) * ROUTED_SCALE
    o_ref[...] = gates.T


def _route(logits_t, bias):
    e, t = logits_t.shape
    tn = min(1024, t)
    return pl.pallas_call(
        _route_kernel,
        grid=(t // tn,),
        in_specs=[pl.BlockSpec((e, tn), lambda i: (0, i)),
                  pl.BlockSpec((e, 1), lambda i: (0, 0))],
        out_specs=pl.BlockSpec((tn, e), lambda i: (i, 0)),
        out_shape=jax.ShapeDtypeStruct((t, e), F32),
        compiler_params=_cparams("parallel"),
    )(logits_t, bias.reshape(e, 1))


def _swiglu_hidden(h, w_gu):
    gu = _dot(h, w_gu)
    hid = w_gu.shape[1] // 2
    return _silu(gu[:, :hid]) * gu[:, hid:]


def _moe_kernel(h_ref, gate_ref, x1_ref, mod_ref, wsgu_ref, wsd_ref, wgu_ref, wd_ref, o_ref, acc_ref):
    e = pl.program_id(2)
    h = h_ref[0]

    @pl.when(e == 0)
    def _():
        acc_ref[...] = _dot(_swiglu_hidden(h, wsgu_ref[...]).astype(BF16), wsd_ref[...])

    lane = lax.broadcasted_iota(jnp.int32, (1, N_EXPERTS), 1)
    gcol = jnp.sum(jnp.where(lane == e, gate_ref[...], 0.0), axis=-1, keepdims=True)
    act = _swiglu_hidden(h, wgu_ref[0])
    act = jnp.where(gcol > 0.0, act * gcol, 0.0)
    acc_ref[...] += _dot(act.astype(BF16), wd_ref[0])

    @pl.when(e == pl.num_programs(2) - 1)
    def _():
        o_ref[0] = x1_ref[0] + mod_ref[0, 5:6, :] * acc_ref[...]


def _moe(h2, gates, x1, mod, ws_gu, ws_d, we_gu, we_d):
    bsz, s, d = h2.shape
    ne = we_gu.shape[0]
    tm = min(1024, s)
    ns = s // tm
    full = lambda shape: pl.BlockSpec(shape, lambda b, i, e: (0,) * len(shape))
    return pl.pallas_call(
        _moe_kernel,
        grid=(bsz, ns, ne),
        in_specs=[pl.BlockSpec((1, tm, d), lambda b, i, e: (b, i, 0)),
                  pl.BlockSpec((tm, ne), lambda b, i, e: (b * ns + i, 0)),
                  pl.BlockSpec((1, tm, d), lambda b, i, e: (b, i, 0)),
                  pl.BlockSpec((1, N_MOD, d), lambda b, i, e: (b, 0, 0)),
                  full(ws_gu.shape), full(ws_d.shape),
                  pl.BlockSpec((1,) + we_gu.shape[1:], lambda b, i, e: (e, 0, 0)),
                  pl.BlockSpec((1,) + we_d.shape[1:], lambda b, i, e: (e, 0, 0))],
        out_specs=pl.BlockSpec((1, tm, d), lambda b, i, e: (b, i, 0)),
        out_shape=jax.ShapeDtypeStruct((bsz, s, d), F32),
        scratch_shapes=[pltpu.VMEM((tm, d), F32)],
        compiler_params=_cparams("parallel", "parallel", "arbitrary"),
    )(h2, gates, x1, mod, ws_gu, ws_d, we_gu, we_d)


def kernel(x, c, w_ada, b_ada, norm1_g, w_in, sb_q_norm_g, sb_k_norm_g, hg_lb_logits, hg_norm_g,
           w_branch_sb, w_branch_hg, w_out, norm2_g, w_router, router_bias, w_e_gate, w_e_up,
           w_e_down, w_s_gate, w_s_up, w_s_down):
    bsz, s, d = x.shape
    depth = w_ada.shape[0]
    n_gate_cols = 2 * d
    qkv_col0 = n_gate_cols // LANES
    hg_col0 = qkv_col0 + 3 * SB_WIDTH // LANES
    for l in range(depth):
        n_mix = 3 * SB_WIDTH + 4 * HG_WIDTH
        w_in_l = jnp.concatenate([w_in[l][:, n_mix:], w_in[l][:, :n_mix]], axis=1).astype(BF16)
        wr_t = w_router[l].T
        wr_hi = wr_t.astype(BF16)
        wr_lo = (wr_t - wr_hi.astype(F32)).astype(BF16)
        ws_gu = jnp.concatenate([w_s_gate[l], w_s_up[l]], axis=1).astype(BF16)
        we_gu = jnp.concatenate([w_e_gate[l], w_e_up[l]], axis=2).astype(BF16)

        mod = _ada(c, w_ada[l], b_ada[l]).reshape(bsz, N_MOD, d)
        proj = _inproj(x, mod, norm1_g[l], w_in_l)
        o_sb = _sb_attention(proj, sb_q_norm_g[l], sb_k_norm_g[l], qkv_col0)
        o_hg = _hgrn(proj, hg_lb_logits, hg_norm_g[l], hg_col0, l)
        x1, h2, logits_t = _merge(x, o_sb, o_hg, proj, mod, norm2_g[l],
                                  w_branch_sb[l].astype(BF16), w_branch_hg[l].astype(BF16),
                                  w_out[l].astype(BF16), wr_hi, wr_lo)
        gates = _route(logits_t, router_bias[l])
        x = _moe(h2, gates, x1, mod, ws_gu, w_s_down[l].astype(BF16), we_gu,
                 w_e_down[l].astype(BF16))
    return x
```

```python
import functools

import jax
import jax.numpy as jnp
from jax import lax
from jax.experimental import pallas as pl
from jax.experimental.pallas import tpu as pltpu

F32 = jnp.float32
BF16 = jnp.bfloat16

SB_HEADS = 8
SB_HEAD_DIM = 64
SB_WIDTH = SB_HEADS * SB_HEAD_DIM
HG_HEADS = 4
HG_HEAD_DIM = 128
HG_WIDTH = HG_HEADS * HG_HEAD_DIM
HG_CHUNK = 64
N_EXPERTS = 64
TOP_K = 8
N_GROUPS = 8
TOPK_GROUPS = 4
GROUP_SIZE = N_EXPERTS // N_GROUPS
ROUTED_SCALE = 2.5
N_MOD = 6
EPS = 1e-6
LOG2_E = 1.4426950408889634

LANES = 128
VMEM_LIMIT = 56 * 1024 * 1024


def _cparams(*sem):
    return pltpu.CompilerParams(dimension_semantics=sem, vmem_limit_bytes=VMEM_LIMIT)


def _silu(t):
    return t * jax.nn.sigmoid(t)


def _dot(a, b):
    return jnp.dot(a, b, preferred_element_type=F32)


def _dot_nt(a, b):
    return lax.dot_general(a, b, (((1,), (1,)), ((), ())), preferred_element_type=F32)


def _split_bf16(t):
    hi = t.astype(BF16)
    lo = (t - hi.astype(F32)).astype(BF16)
    return hi, lo


def _ada_kernel(c_ref, w_ref, b_ref, o_ref):
    cond = _silu(c_ref[...])
    o_ref[...] = _dot(cond, w_ref[...]) + b_ref[...]


def _ada(c, w, b):
    bsz, d = c.shape
    n = w.shape[1]
    tn = 1024
    return pl.pallas_call(
        _ada_kernel,
        grid=(n // tn,),
        in_specs=[pl.BlockSpec((bsz, d), lambda j: (0, 0)),
                  pl.BlockSpec((d, tn), lambda j: (0, j)),
                  pl.BlockSpec((1, tn), lambda j: (0, j))],
        out_specs=pl.BlockSpec((bsz, tn), lambda j: (0, j)),
        out_shape=jax.ShapeDtypeStruct((bsz, n), F32),
        compiler_params=_cparams("parallel"),
    )(c, w, b.reshape(1, n))


def _modulated_norm(x, g, shift, scale):
    y = x * lax.rsqrt(jnp.mean(x * x, axis=-1, keepdims=True) + EPS) * g
    return y * (1.0 + scale) + shift


def _inproj_kernel(x_ref, mod_ref, g_ref, w_ref, o_ref, h_ref):
    @pl.when(pl.program_id(2) == 0)
    def _():
        h = _modulated_norm(x_ref[0], g_ref[...], mod_ref[0, 0:1, :], mod_ref[0, 1:2, :])
        h_ref[...] = h.astype(BF16)

    o_ref[0] = _dot(h_ref[...], w_ref[...]).astype(o_ref.dtype)


def _inproj(x, mod, g, w):
    bsz, s, d = x.shape
    n = w.shape[1]
    tm = min(1024, s)
    tn = 512
    return pl.pallas_call(
        _inproj_kernel,
        grid=(bsz, s // tm, n // tn),
        in_specs=[pl.BlockSpec((1, tm, d), lambda b, i, j: (b, i, 0)),
                  pl.BlockSpec((1, N_MOD, d), lambda b, i, j: (b, 0, 0)),
                  pl.BlockSpec((1, d), lambda b, i, j: (0, 0)),
                  pl.BlockSpec((d, tn), lambda b, i, j: (0, j))],
        out_specs=pl.BlockSpec((1, tm, tn), lambda b, i, j: (b, i, j)),
        out_shape=jax.ShapeDtypeStruct((bsz, s, n), BF16),
        scratch_shapes=[pltpu.VMEM((tm, d), BF16)],
        compiler_params=_cparams("parallel", "parallel", "arbitrary"),
    )(x, mod, g.reshape(1, d), w)


def _pair_norm(t, g, lo_half):
    sq = t * t
    s_lo = jnp.sum(jnp.where(lo_half, sq, 0.0), axis=-1, keepdims=True)
    s_hi = jnp.sum(jnp.where(lo_half, 0.0, sq), axis=-1, keepdims=True)
    ms = jnp.where(lo_half, s_lo, s_hi) * (1.0 / SB_HEAD_DIM)
    return t * lax.rsqrt(ms + EPS) * g


def _neg_abs(t):
    bits = lax.bitcast_convert_type(t, jnp.uint32) | jnp.uint32(0x80000000)
    return lax.bitcast_convert_type(bits, F32)


def _split_trunc(t):
    bits = lax.bitcast_convert_type(t, jnp.uint32) & jnp.uint32(0xFFFF0000)
    hi = lax.bitcast_convert_type(bits, F32)
    return hi.astype(BF16), (t - hi).astype(BF16)


def _sb_kernel(q_ref, k_ref, v_ref, qg_ref, kg_ref, o_ref, knt_ref, va_ref, vb_ref, *, tq):
    qi = pl.program_id(2)
    s = k_ref.shape[1]
    lane = lax.broadcasted_iota(jnp.int32, (1, LANES), 1)
    lo_half = lane < SB_HEAD_DIM

    @pl.when(qi == 0)
    def _():
        def prep_block(j, c):
            rows = pl.ds(pl.multiple_of(j * tq, tq), tq)
            kb = _pair_norm(k_ref[0, rows, :].astype(F32), kg_ref[...], lo_half)
            knt_ref[:, rows] = kb.T.astype(BF16)
            vb = v_ref[0, rows, :]
            va_ref[rows, :] = jnp.where(lo_half, vb, jnp.zeros_like(vb))
            vb_ref[rows, :] = jnp.where(lo_half, jnp.zeros_like(vb), vb)
            return c
        lax.fori_loop(0, s // tq, prep_block, 0)

    scale = SB_HEAD_DIM ** -0.5 * LOG2_E
    q = _pair_norm(q_ref[0].astype(F32), qg_ref[...], lo_half) * scale
    q_heads = (jnp.where(lo_half, q, 0.0).astype(BF16), jnp.where(lo_half, 0.0, q).astype(BF16))
    v_heads = (va_ref, vb_ref)

    row = lax.broadcasted_iota(jnp.int32, (tq, tq), 0)
    col = lax.broadcasted_iota(jnp.int32, (tq, tq), 1)
    strict = col < row
    tri = (row >= col).astype(BF16)

    def sweep(blocks, masked, runs, acc):
        cols = [pl.ds(pl.multiple_of(j * tq, tq), tq) for j in blocks]
        z = [[_dot(q_heads[h], knt_ref[:, c]) for c in cols] for h in range(2)]
        cs = [[None] * len(cols) for _ in range(2)]
        for h in range(2):
            for b in range(len(cols)):
                sp = jnp.maximum(z[h][b], 0.0) + jnp.log2(1.0 + jnp.exp2(_neg_abs(z[h][b])))
                if masked:
                    sp = jnp.where(strict, sp, 0.0)
                hi, lo = _split_trunc(sp)
                cs[h][b] = _dot(hi, tri) + _dot(lo, tri)
        runs = list(runs)
        for h in range(2):
            for b in range(len(cols)):
                a = jnp.exp2(z[h][b] - cs[h][b] - runs[h])
                if masked:
                    a = jnp.where(strict, a, 0.0)
                acc = acc + _dot(a.astype(BF16), v_heads[h][cols[b], :])
                runs[h] = runs[h] + cs[h][b][:, 0:1]
        return tuple(runs), acc

    zero_run = jnp.zeros((tq, 1), F32)
    runs, acc = sweep([qi], True, (zero_run, zero_run), jnp.zeros((tq, LANES), F32))
    odd = qi % 2
    runs, acc = lax.cond(odd == 1,
                         lambda r, a: sweep([qi - 1], False, r, a),
                         lambda r, a: (r, a), runs, acc)

    def pair(i, carry):
        j = qi - 1 - odd - 2 * i
        return sweep([j, j - 1], False, carry[0], carry[1])

    _, acc = lax.fori_loop(0, qi // 2, pair, (runs, acc))
    o_ref[0] = acc.astype(o_ref.dtype)


def _sb_attention(proj, qg, kg, col0):
    bsz, s, _ = proj.shape
    tq = min(256, s)
    npair = SB_WIDTH // LANES
    qg2 = jnp.tile(qg.reshape(1, SB_HEAD_DIM), (1, 2))
    kg2 = jnp.tile(kg.reshape(1, SB_HEAD_DIM), (1, 2))
    return pl.pallas_call(
        functools.partial(_sb_kernel, tq=tq),
        grid=(bsz, npair, s // tq),
        in_specs=[pl.BlockSpec((1, tq, LANES), lambda b, h, i: (b, i, col0 + h)),
                  pl.BlockSpec((1, s, LANES), lambda b, h, i: (b, 0, col0 + npair + h)),
                  pl.BlockSpec((1, s, LANES), lambda b, h, i: (b, 0, col0 + 2 * npair + h)),
                  pl.BlockSpec((1, LANES), lambda b, h, i: (0, 0)),
                  pl.BlockSpec((1, LANES), lambda b, h, i: (0, 0))],
        out_specs=pl.BlockSpec((1, tq, LANES), lambda b, h, i: (b, i, h)),
        out_shape=jax.ShapeDtypeStruct((bsz, s, SB_WIDTH), BF16),
        scratch_shapes=[pltpu.VMEM((LANES, s), BF16),
                        pltpu.VMEM((s, LANES), BF16),
                        pltpu.VMEM((s, LANES), BF16)],
        compiler_params=_cparams("parallel", "parallel", "arbitrary"),
    )(proj, proj, proj, qg2, kg2)


def _hgrn_kernel(f_ref, i_ref, q_ref, g_ref, lbl_ref, ng_ref, o_ref, *, layer):
    s = f_ref.shape[1]
    c = HG_CHUNK
    lg = lbl_ref[...]
    e = jnp.exp(lg - jnp.max(lg, axis=0, keepdims=True))
    lb = jnp.sum(e[:layer + 1], axis=0, keepdims=True) / jnp.sum(e, axis=0, keepdims=True)

    row = lax.broadcasted_iota(jnp.int32, (c, c), 0)
    col = lax.broadcasted_iota(jnp.int32, (c, c), 1)
    causal = col <= row
    lower = causal.astype(BF16)

    def body(n, st):
        r0 = pl.multiple_of(n * c, c)
        rows = pl.ds(r0, c)
        forget = lb + (1.0 - lb) * jax.nn.sigmoid(f_ref[0, rows, :].astype(F32))
        kc = 1.0 - forget
        hi, lo = _split_bf16(jnp.log(forget))
        gc = _dot(lower, hi) + _dot(lower, lo)
        g_mid = gc[c // 2 - 1:c // 2, :]
        g_last = gc[c - 1:c, :]
        qs = _silu(q_ref[0, rows, :].astype(F32))
        v = i_ref[0, rows, :]
        scores = _dot_nt((qs * jnp.exp(gc - g_mid)).astype(BF16),
                         (kc * jnp.exp(g_mid - gc)).astype(BF16))
        o_intra = _dot(jnp.where(causal, scores, 0.0).astype(BF16), v)
        kd = (kc * jnp.exp(g_last - gc)).astype(BF16)
        kv_t = _dot(v.astype(F32).T.astype(BF16), kd)
        o_inter = _dot_nt((qs * jnp.exp(gc)).astype(BF16), st.astype(BF16))
        o = o_intra + o_inter
        y = o * lax.rsqrt(jnp.mean(o * o, axis=-1, keepdims=True) + EPS) * ng_ref[...]
        y = y * _silu(g_ref[0, rows, :].astype(F32))
        o_ref[0, rows, :] = y.astype(o_ref.dtype)
        return st * jnp.exp(g_last) + kv_t

    lax.fori_loop(0, s // c, body, jnp.zeros((HG_HEAD_DIM, HG_HEAD_DIM), F32))


def _hgrn(proj, lb_logits, norm_g, col0, layer):
    bsz, s, _ = proj.shape
    nl = lb_logits.shape[0]
    spec = lambda off: pl.BlockSpec((1, s, LANES), lambda b, h: (b, 0, col0 + off * HG_HEADS + h))
    return pl.pallas_call(
        functools.partial(_hgrn_kernel, layer=layer),
        grid=(bsz, HG_HEADS),
        in_specs=[spec(0), spec(1), spec(2), spec(3),
                  pl.BlockSpec((nl, LANES), lambda b, h: (0, h)),
                  pl.BlockSpec((1, LANES), lambda b, h: (0, 0))],
        out_specs=pl.BlockSpec((1, s, LANES), lambda b, h: (b, 0, h)),
        out_shape=jax.ShapeDtypeStruct((bsz, s, HG_WIDTH), BF16),
        compiler_params=_cparams("parallel", "parallel"),
    )(proj, proj, proj, proj, lb_logits, norm_g.reshape(1, HG_HEAD_DIM))


def _merge_kernel(x_ref, osb_ref, ohg_ref, gsb_ref, ghg_ref, mod_ref, g2_ref, wsb_ref, whg_ref,
                  wout_ref, wrh_ref, wrl_ref, x1_ref, h2_ref, lg_ref):
    m_sb = _dot(osb_ref[0], wsb_ref[...])
    m_hg = _dot(ohg_ref[0], whg_ref[...])
    merged = (jax.nn.sigmoid(gsb_ref[0].astype(F32)) * m_sb
              + jax.nn.sigmoid(ghg_ref[0].astype(F32)) * m_hg)
    x1 = x_ref[0] + mod_ref[0, 2:3, :] * _dot(merged.astype(BF16), wout_ref[...])
    x1_ref[0] = x1
    h2 = _modulated_norm(x1, g2_ref[...], mod_ref[0, 3:4, :], mod_ref[0, 4:5, :])
    hi, lo = _split_bf16(h2)
    h2_ref[0] = hi
    lg_ref[...] = _dot_nt(wrh_ref[...], hi) + _dot_nt(wrh_ref[...], lo) + _dot_nt(wrl_ref[...], hi)


def _merge(x, o_sb, o_hg, proj, mod, g2, w_sb, w_hg, w_out, wr_hi, wr_lo):
    bsz, s, d = x.shape
    tm = min(512, s)
    ns = s // tm
    full = lambda shape: pl.BlockSpec(shape, lambda b, i: (0,) * len(shape))
    return pl.pallas_call(
        _merge_kernel,
        grid=(bsz, ns),
        in_specs=[pl.BlockSpec((1, tm, d), lambda b, i: (b, i, 0)),
                  pl.BlockSpec((1, tm, SB_WIDTH), lambda b, i: (b, i, 0)),
                  pl.BlockSpec((1, tm, HG_WIDTH), lambda b, i: (b, i, 0)),
                  pl.BlockSpec((1, tm, d), lambda b, i: (b, i, 0)),
                  pl.BlockSpec((1, tm, d), lambda b, i: (b, i, 1)),
                  pl.BlockSpec((1, N_MOD, d), lambda b, i: (b, 0, 0)),
                  full((1, d)), full(w_sb.shape), full(w_hg.shape), full(w_out.shape),
                  full(wr_hi.shape), full(wr_lo.shape)],
        out_specs=[pl.BlockSpec((1, tm, d), lambda b, i: (b, i, 0)),
                   pl.BlockSpec((1, tm, d), lambda b, i: (b, i, 0)),
                   pl.BlockSpec((N_EXPERTS, tm), lambda b, i: (0, b * ns + i))],
        out_shape=[jax.ShapeDtypeStruct((bsz, s, d), F32),
                   jax.ShapeDtypeStruct((bsz, s, d), BF16),
                   jax.ShapeDtypeStruct((N_EXPERTS, bsz * s), F32)],
        compiler_params=_cparams("parallel", "parallel"),
    )(x, o_sb, o_hg, proj, proj, mod, g2.reshape(1, d), w_sb, w_hg, w_out, wr_hi, wr_lo)


def _first_argmax(vals, idx, sentinel):
    m = jnp.max(vals, axis=0, keepdims=True)
    first = jnp.min(jnp.where(vals == m, idx, sentinel), axis=0, keepdims=True)
    return m, first


def _route_kernel(lg_ref, bias_ref, o_ref):
    tn = lg_ref.shape[1]
    neg = -jnp.inf
    scores = jax.nn.sigmoid(lg_ref[...])
    choice = scores + bias_ref[...]

    gidx = lax.broadcasted_iota(jnp.int32, (GROUP_SIZE, tn), 0)
    group_rows = []
    for g in range(N_GROUPS):
        cg = choice[g * GROUP_SIZE:(g + 1) * GROUP_SIZE, :]
        m1, i1 = _first_argmax(cg, gidx, GROUP_SIZE)
        m2 = jnp.max(jnp.where(gidx == i1, neg, cg), axis=0, keepdims=True)
        group_rows.append(m1 + m2)
    work = jnp.concatenate(group_rows, axis=0)
    ggi = lax.broadcasted_iota(jnp.int32, (N_GROUPS, tn), 0)
    gmask = jnp.zeros((N_GROUPS, tn), F32)
    for _ in range(TOPK_GROUPS):
        _, first = _first_argmax(work, ggi, N_GROUPS)
        pick = ggi == first
        gmask = jnp.where(pick, 1.0, gmask)
        work = jnp.where(pick, neg, work)

    masked = jnp.concatenate(
        [jnp.where(gmask[g:g + 1, :] > 0.0, choice[g * GROUP_SIZE:(g + 1) * GROUP_SIZE, :], neg)
         for g in range(N_GROUPS)], axis=0)
    eidx = lax.broadcasted_iota(jnp.int32, (N_EXPERTS, tn), 0)
    sel = jnp.zeros((N_EXPERTS, tn), F32)
    for _ in range(TOP_K):
        _, first = _first_argmax(masked, eidx, N_EXPERTS)
        pick = eidx == first
        sel = jnp.where(pick, 1.0, sel)
        masked = jnp.where(pick, neg, masked)

    chosen = jnp.where(sel > 0.0, scores, 0.0)
    gates = chosen / jnp.sum(chosen, axis=0, keepdims=True) * ROUTED_SCALE
    o_ref[...] = gates.T


def _route(logits_t, bias):
    e, t = logits_t.shape
    tn = min(1024, t)
    return pl.pallas_call(
        _route_kernel,
        grid=(t // tn,),
        in_specs=[pl.BlockSpec((e, tn), lambda i: (0, i)),
                  pl.BlockSpec((e, 1), lambda i: (0, 0))],
        out_specs=pl.BlockSpec((tn, e), lambda i: (i, 0)),
        out_shape=jax.ShapeDtypeStruct((t, e), F32),
        compiler_params=_cparams("parallel"),
    )(logits_t, bias.reshape(e, 1))


def _swiglu_hidden(h, w_gu):
    gu = _dot(h, w_gu)
    hid = w_gu.shape[1] // 2
    return _silu(gu[:, :hid]) * gu[:, hid:]


def _moe_kernel(h_ref, gate_ref, x1_ref, mod_ref, wsgu_ref, wsd_ref, wgu_ref, wd_ref, o_ref, acc_ref):
    e = pl.program_id(2)
    h = h_ref[0]

    @pl.when(e == 0)
    def _():
        acc_ref[...] = _dot(_swiglu_hidden(h, wsgu_ref[...]).astype(BF16), wsd_ref[...])

    lane = lax.broadcasted_iota(jnp.int32, (1, N_EXPERTS), 1)
    gcol = jnp.sum(jnp.where(lane == e, gate_ref[...], 0.0), axis=-1, keepdims=True)
    act = _swiglu_hidden(h, wgu_ref[0])
    act = jnp.where(gcol > 0.0, act * gcol, 0.0)
    acc_ref[...] += _dot(act.astype(BF16), wd_ref[0])

    @pl.when(e == pl.num_programs(2) - 1)
    def _():
        o_ref[0] = x1_ref[0] + mod_ref[0, 5:6, :] * acc_ref[...]


def _moe(h2, gates, x1, mod, ws_gu, ws_d, we_gu, we_d):
    bsz, s, d = h2.shape
    ne = we_gu.shape[0]
    tm = min(1024, s)
    ns = s // tm
    full = lambda shape: pl.BlockSpec(shape, lambda b, i, e: (0,) * len(shape))
    return pl.pallas_call(
        _moe_kernel,
        grid=(bsz, ns, ne),
        in_specs=[pl.BlockSpec((1, tm, d), lambda b, i, e: (b, i, 0)),
                  pl.BlockSpec((tm, ne), lambda b, i, e: (b * ns + i, 0)),
                  pl.BlockSpec((1, tm, d), lambda b, i, e: (b, i, 0)),
                  pl.BlockSpec((1, N_MOD, d), lambda b, i, e: (b, 0, 0)),
                  full(ws_gu.shape), full(ws_d.shape),
                  pl.BlockSpec((1,) + we_gu.shape[1:], lambda b, i, e: (e, 0, 0)),
                  pl.BlockSpec((1,) + we_d.shape[1:], lambda b, i, e: (e, 0, 0))],
        out_specs=pl.BlockSpec((1, tm, d), lambda b, i, e: (b, i, 0)),
        out_shape=jax.ShapeDtypeStruct((bsz, s, d), F32),
        scratch_shapes=[pltpu.VMEM((tm, d), F32)],
        compiler_params=_cparams("parallel", "parallel", "arbitrary"),
    )(h2, gates, x1, mod, ws_gu, ws_d, we_gu, we_d)


def kernel(x, c, w_ada, b_ada, norm1_g, w_in, sb_q_norm_g, sb_k_norm_g, hg_lb_logits, hg_norm_g,
           w_branch_sb, w_branch_hg, w_out, norm2_g, w_router, router_bias, w_e_gate, w_e_up,
           w_e_down, w_s_gate, w_s_up, w_s_down):
    bsz, s, d = x.shape
    depth = w_ada.shape[0]
    n_gate_cols = 2 * d
    qkv_col0 = n_gate_cols // LANES
    hg_col0 = qkv_col0 + 3 * SB_WIDTH // LANES
    for l in range(depth):
        n_mix = 3 * SB_WIDTH + 4 * HG_WIDTH
        w_in_l = jnp.concatenate([w_in[l][:, n_mix:], w_in[l][:, :n_mix]], axis=1).astype(BF16)
        wr_t = w_router[l].T
        wr_hi = wr_t.astype(BF16)
        wr_lo = (wr_t - wr_hi.astype(F32)).astype(BF16)
        ws_gu = jnp.concatenate([w_s_gate[l], w_s_up[l]], axis=1).astype(BF16)
        we_gu = jnp.concatenate([w_e_gate[l], w_e_up[l]], axis=2).astype(BF16)

        mod = _ada(c, w_ada[l], b_ada[l]).reshape(bsz, N_MOD, d)
        proj = _inproj(x, mod, norm1_g[l], w_in_l)
        o_sb = _sb_attention(proj, sb_q_norm_g[l], sb_k_norm_g[l], qkv_col0)
        o_hg = _hgrn(proj, hg_lb_logits, hg_norm_g[l], hg_col0, l)
        x1, h2, logits_t = _merge(x, o_sb, o_hg, proj, mod, norm2_g[l],
                                  w_branch_sb[l].astype(BF16), w_branch_hg[l].astype(BF16),
                                  w_out[l].astype(BF16), wr_hi, wr_lo)
        gates = _route(logits_t, router_bias[l])
        x = _moe(h2, gates, x1, mod, ws_gu, w_s_down[l].astype(BF16), we_gu,
                 w_e_down[l].astype(BF16))
    return x
```

```python
import functools

import jax
import jax.numpy as jnp
from jax import lax
from jax.experimental import pallas as pl
from jax.experimental.pallas import tpu as pltpu
from jax.experimental.pallas import tpu_sc as plsc

F32 = jnp.float32
BF16 = jnp.bfloat16

SB_HEADS = 8
SB_HEAD_DIM = 64
SB_WIDTH = SB_HEADS * SB_HEAD_DIM
HG_HEADS = 4
HG_HEAD_DIM = 128
HG_WIDTH = HG_HEADS * HG_HEAD_DIM
HG_CHUNK = 64
N_EXPERTS = 64
TOP_K = 8
N_GROUPS = 8
TOPK_GROUPS = 4
GROUP_SIZE = N_EXPERTS // N_GROUPS
ROUTED_SCALE = 2.5
DISPATCH_ROWS = 512
N_MOD = 6
EPS = 1e-6
LOG2_E = 1.4426950408889634

LANES = 128
VMEM_LIMIT = 56 * 1024 * 1024


def _cparams(*sem):
    return pltpu.CompilerParams(dimension_semantics=sem, vmem_limit_bytes=VMEM_LIMIT)


def _silu(t):
    return t * jax.nn.sigmoid(t)


def _dot(a, b):
    return jnp.dot(a, b, preferred_element_type=F32)


def _dot_nt(a, b):
    return lax.dot_general(a, b, (((1,), (1,)), ((), ())), preferred_element_type=F32)


def _split_bf16(t):
    hi = t.astype(BF16)
    lo = (t - hi.astype(F32)).astype(BF16)
    return hi, lo


def _ada_kernel(c_ref, w_ref, b_ref, o_ref):
    cond = _silu(c_ref[...])
    o_ref[...] = _dot(cond, w_ref[...]) + b_ref[...]


def _ada(c, w, b):
    bsz, d = c.shape
    n = w.shape[1]
    tn = 1024
    return pl.pallas_call(
        _ada_kernel,
        grid=(n // tn,),
        in_specs=[pl.BlockSpec((bsz, d), lambda j: (0, 0)),
                  pl.BlockSpec((d, tn), lambda j: (0, j)),
                  pl.BlockSpec((1, tn), lambda j: (0, j))],
        out_specs=pl.BlockSpec((bsz, tn), lambda j: (0, j)),
        out_shape=jax.ShapeDtypeStruct((bsz, n), F32),
        compiler_params=_cparams("parallel"),
    )(c, w, b.reshape(1, n))


def _modulated_norm(x, g, shift, scale):
    y = x * lax.rsqrt(jnp.mean(x * x, axis=-1, keepdims=True) + EPS) * g
    return y * (1.0 + scale) + shift


def _inproj_kernel(x_ref, mod_ref, g_ref, w_ref, o_ref, h_ref):
    @pl.when(pl.program_id(2) == 0)
    def _():
        h = _modulated_norm(x_ref[0], g_ref[...], mod_ref[0, 0:1, :], mod_ref[0, 1:2, :])
        h_ref[...] = h.astype(BF16)

    o_ref[0] = _dot(h_ref[...], w_ref[...]).astype(o_ref.dtype)


def _inproj(x, mod, g, w):
    bsz, s, d = x.shape
    n = w.shape[1]
    tm = min(1024, s)
    tn = 512
    return pl.pallas_call(
        _inproj_kernel,
        grid=(bsz, s // tm, n // tn),
        in_specs=[pl.BlockSpec((1, tm, d), lambda b, i, j: (b, i, 0)),
                  pl.BlockSpec((1, N_MOD, d), lambda b, i, j: (b, 0, 0)),
                  pl.BlockSpec((1, d), lambda b, i, j: (0, 0)),
                  pl.BlockSpec((d, tn), lambda b, i, j: (0, j))],
        out_specs=pl.BlockSpec((1, tm, tn), lambda b, i, j: (b, i, j)),
        out_shape=jax.ShapeDtypeStruct((bsz, s, n), BF16),
        scratch_shapes=[pltpu.VMEM((tm, d), BF16)],
        compiler_params=_cparams("parallel", "parallel", "arbitrary"),
    )(x, mod, g.reshape(1, d), w)


def _pair_norm(t, g, lo_half):
    sq = t * t
    s_lo = jnp.sum(jnp.where(lo_half, sq, 0.0), axis=-1, keepdims=True)
    s_hi = jnp.sum(jnp.where(lo_half, 0.0, sq), axis=-1, keepdims=True)
    ms = jnp.where(lo_half, s_lo, s_hi) * (1.0 / SB_HEAD_DIM)
    return t * lax.rsqrt(ms + EPS) * g


def _neg_abs(t):
    bits = lax.bitcast_convert_type(t, jnp.uint32) | jnp.uint32(0x80000000)
    return lax.bitcast_convert_type(bits, F32)


def _split_trunc(t):
    bits = lax.bitcast_convert_type(t, jnp.uint32) & jnp.uint32(0xFFFF0000)
    hi = lax.bitcast_convert_type(bits, F32)
    return hi.astype(BF16), (t - hi).astype(BF16)


def _sb_kernel(q_ref, k_ref, v_ref, qg_ref, kg_ref, o_ref, knt_ref, va_ref, vb_ref, *, tq):
    qi = pl.program_id(2)
    s = k_ref.shape[1]
    lane = lax.broadcasted_iota(jnp.int32, (1, LANES), 1)
    lo_half = lane < SB_HEAD_DIM

    @pl.when(qi == 0)
    def _():
        def prep_block(j, c):
            rows = pl.ds(pl.multiple_of(j * tq, tq), tq)
            kb = _pair_norm(k_ref[0, rows, :].astype(F32), kg_ref[...], lo_half)
            knt_ref[:, rows] = kb.T.astype(BF16)
            vb = v_ref[0, rows, :]
            va_ref[rows, :] = jnp.where(lo_half, vb, jnp.zeros_like(vb))
            vb_ref[rows, :] = jnp.where(lo_half, jnp.zeros_like(vb), vb)
            return c
        lax.fori_loop(0, s // tq, prep_block, 0)

    scale = SB_HEAD_DIM ** -0.5 * LOG2_E
    q = _pair_norm(q_ref[0].astype(F32), qg_ref[...], lo_half) * scale
    q_heads = (jnp.where(lo_half, q, 0.0).astype(BF16), jnp.where(lo_half, 0.0, q).astype(BF16))
    v_heads = (va_ref, vb_ref)

    row = lax.broadcasted_iota(jnp.int32, (tq, tq), 0)
    col = lax.broadcasted_iota(jnp.int32, (tq, tq), 1)
    strict = col < row
    tri = (row >= col).astype(BF16)

    def sweep(blocks, masked, runs, acc):
        cols = [pl.ds(pl.multiple_of(j * tq, tq), tq) for j in blocks]
        z = [[_dot(q_heads[h], knt_ref[:, c]) for c in cols] for h in range(2)]
        cs = [[None] * len(cols) for _ in range(2)]
        for h in range(2):
            for b in range(len(cols)):
                sp = jnp.maximum(z[h][b], 0.0) + jnp.log2(1.0 + jnp.exp2(_neg_abs(z[h][b])))
                if masked:
                    sp = jnp.where(strict, sp, 0.0)
                hi, lo = _split_trunc(sp)
                cs[h][b] = _dot(hi, tri) + _dot(lo, tri)
        runs = list(runs)
        for h in range(2):
            for b in range(len(cols)):
                a = jnp.exp2(z[h][b] - cs[h][b] - runs[h])
                if masked:
                    a = jnp.where(strict, a, 0.0)
                acc = acc + _dot(a.astype(BF16), v_heads[h][cols[b], :])
                runs[h] = runs[h] + cs[h][b][:, 0:1]
        return tuple(runs), acc

    zero_run = jnp.zeros((tq, 1), F32)
    runs, acc = sweep([qi], True, (zero_run, zero_run), jnp.zeros((tq, LANES), F32))
    odd = qi % 2
    runs, acc = lax.cond(odd == 1,
                         lambda r, a: sweep([qi - 1], False, r, a),
                         lambda r, a: (r, a), runs, acc)

    def pair(i, carry):
        j = qi - 1 - odd - 2 * i
        return sweep([j, j - 1], False, carry[0], carry[1])

    _, acc = lax.fori_loop(0, qi // 2, pair, (runs, acc))
    o_ref[0] = acc.astype(o_ref.dtype)


def _sb_attention(proj, qg, kg, col0):
    bsz, s, _ = proj.shape
    tq = min(256, s)
    npair = SB_WIDTH // LANES
    qg2 = jnp.tile(qg.reshape(1, SB_HEAD_DIM), (1, 2))
    kg2 = jnp.tile(kg.reshape(1, SB_HEAD_DIM), (1, 2))
    return pl.pallas_call(
        functools.partial(_sb_kernel, tq=tq),
        grid=(bsz, npair, s // tq),
        in_specs=[pl.BlockSpec((1, tq, LANES), lambda b, h, i: (b, i, col0 + h)),
                  pl.BlockSpec((1, s, LANES), lambda b, h, i: (b, 0, col0 + npair + h)),
                  pl.BlockSpec((1, s, LANES), lambda b, h, i: (b, 0, col0 + 2 * npair + h)),
                  pl.BlockSpec((1, LANES), lambda b, h, i: (0, 0)),
                  pl.BlockSpec((1, LANES), lambda b, h, i: (0, 0))],
        out_specs=pl.BlockSpec((1, tq, LANES), lambda b, h, i: (b, i, h)),
        out_shape=jax.ShapeDtypeStruct((bsz, s, SB_WIDTH), BF16),
        scratch_shapes=[pltpu.VMEM((LANES, s), BF16),
                        pltpu.VMEM((s, LANES), BF16),
                        pltpu.VMEM((s, LANES), BF16)],
        compiler_params=_cparams("parallel", "parallel", "arbitrary"),
    )(proj, proj, proj, qg2, kg2)


def _hgrn_kernel(f_ref, i_ref, q_ref, g_ref, lbl_ref, ng_ref, o_ref, *, layer):
    s = f_ref.shape[1]
    c = HG_CHUNK
    lg = lbl_ref[...]
    e = jnp.exp(lg - jnp.max(lg, axis=0, keepdims=True))
    lb = jnp.sum(e[:layer + 1], axis=0, keepdims=True) / jnp.sum(e, axis=0, keepdims=True)

    row = lax.broadcasted_iota(jnp.int32, (c, c), 0)
    col = lax.broadcasted_iota(jnp.int32, (c, c), 1)
    causal = col <= row
    lower = causal.astype(BF16)

    def body(n, st):
        r0 = pl.multiple_of(n * c, c)
        rows = pl.ds(r0, c)
        forget = lb + (1.0 - lb) * jax.nn.sigmoid(f_ref[0, rows, :].astype(F32))
        kc = 1.0 - forget
        hi, lo = _split_bf16(jnp.log(forget))
        gc = _dot(lower, hi) + _dot(lower, lo)
        g_mid = gc[c // 2 - 1:c // 2, :]
        g_last = gc[c - 1:c, :]
        qs = _silu(q_ref[0, rows, :].astype(F32))
        v = i_ref[0, rows, :]
        scores = _dot_nt((qs * jnp.exp(gc - g_mid)).astype(BF16),
                         (kc * jnp.exp(g_mid - gc)).astype(BF16))
        o_intra = _dot(jnp.where(causal, scores, 0.0).astype(BF16), v)
        kd = (kc * jnp.exp(g_last - gc)).astype(BF16)
        kv_t = _dot(v.astype(F32).T.astype(BF16), kd)
        o_inter = _dot_nt((qs * jnp.exp(gc)).astype(BF16), st.astype(BF16))
        o = o_intra + o_inter
        y = o * lax.rsqrt(jnp.mean(o * o, axis=-1, keepdims=True) + EPS) * ng_ref[...]
        y = y * _silu(g_ref[0, rows, :].astype(F32))
        o_ref[0, rows, :] = y.astype(o_ref.dtype)
        return st * jnp.exp(g_last) + kv_t

    lax.fori_loop(0, s // c, body, jnp.zeros((HG_HEAD_DIM, HG_HEAD_DIM), F32))


def _hgrn(proj, lb_logits, norm_g, col0, layer):
    bsz, s, _ = proj.shape
    nl = lb_logits.shape[0]
    spec = lambda off: pl.BlockSpec((1, s, LANES), lambda b, h: (b, 0, col0 + off * HG_HEADS + h))
    return pl.pallas_call(
        functools.partial(_hgrn_kernel, layer=layer),
        grid=(bsz, HG_HEADS),
        in_specs=[spec(0), spec(1), spec(2), spec(3),
                  pl.BlockSpec((nl, LANES), lambda b, h: (0, h)),
                  pl.BlockSpec((1, LANES), lambda b, h: (0, 0))],
        out_specs=pl.BlockSpec((1, s, LANES), lambda b, h: (b, 0, h)),
        out_shape=jax.ShapeDtypeStruct((bsz, s, HG_WIDTH), BF16),
        compiler_params=_cparams("parallel", "parallel"),
    )(proj, proj, proj, proj, lb_logits, norm_g.reshape(1, HG_HEAD_DIM))


def _pack_halves(t):
    bits = lax.bitcast_convert_type(t.astype(BF16).astype(F32), jnp.uint32)
    w = t.shape[1] // 2
    return lax.bitcast_convert_type(bits[:, :w] | (bits[:, w:] >> 16), jnp.int32)


def _unpack_halves(p):
    u = lax.bitcast_convert_type(p, jnp.uint32)
    return (lax.bitcast_convert_type(u & jnp.uint32(0xFFFF0000), F32),
            lax.bitcast_convert_type(u << 16, F32))


def _store_planes(ref, t):
    p = _pack_halves(t)
    q = p.shape[1] // 2
    ref[0] = p[:, :q]
    ref[1] = p[:, q:]


def _load_planes(p0, p1):
    a0, b0 = _unpack_halves(p0)
    a1, b1 = _unpack_halves(p1)
    return jnp.concatenate([a0, a1, b0, b1], axis=1)


def _swiglu_hidden(h, w_gu):
    gu = _dot(h, w_gu)
    hid = w_gu.shape[1] // 2
    return _silu(gu[:, :hid]) * gu[:, hid:]


def _merge_kernel(x_ref, osb_ref, ohg_ref, gsb_ref, ghg_ref, mod_ref, g2_ref, wsb_ref, whg_ref,
                  wout_ref, wrh_ref, wrl_ref, wsgu_ref, wsd_ref, x2_ref, h2p_ref, lg_ref):
    m_sb = _dot(osb_ref[0], wsb_ref[...])
    m_hg = _dot(ohg_ref[0], whg_ref[...])
    merged = (jax.nn.sigmoid(gsb_ref[0].astype(F32)) * m_sb
              + jax.nn.sigmoid(ghg_ref[0].astype(F32)) * m_hg)
    x1 = x_ref[0] + mod_ref[0, 2:3, :] * _dot(merged.astype(BF16), wout_ref[...])
    h2 = _modulated_norm(x1, g2_ref[...], mod_ref[0, 3:4, :], mod_ref[0, 4:5, :])
    hi, lo = _split_bf16(h2)
    _store_planes(h2p_ref, h2)
    lg_ref[...] = _dot_nt(wrh_ref[...], hi) + _dot_nt(wrh_ref[...], lo) + _dot_nt(wrl_ref[...], hi)
    shared = _dot(_swiglu_hidden(hi, wsgu_ref[...]).astype(BF16), wsd_ref[...])
    x2_ref[0] = x1 + mod_ref[0, 5:6, :] * shared


def _merge(x, o_sb, o_hg, proj, mod, g2, w_sb, w_hg, w_out, wr_hi, wr_lo, ws_gu, ws_d):
    bsz, s, d = x.shape
    tm = min(512, s)
    ns = s // tm
    full = lambda shape: pl.BlockSpec(shape, lambda b, i: (0,) * len(shape))
    return pl.pallas_call(
        _merge_kernel,
        grid=(bsz, ns),
        in_specs=[pl.BlockSpec((1, tm, d), lambda b, i: (b, i, 0)),
                  pl.BlockSpec((1, tm, SB_WIDTH), lambda b, i: (b, i, 0)),
                  pl.BlockSpec((1, tm, HG_WIDTH), lambda b, i: (b, i, 0)),
                  pl.BlockSpec((1, tm, d), lambda b, i: (b, i, 0)),
                  pl.BlockSpec((1, tm, d), lambda b, i: (b, i, 1)),
                  pl.BlockSpec((1, N_MOD, d), lambda b, i: (b, 0, 0)),
                  full((1, d)), full(w_sb.shape), full(w_hg.shape), full(w_out.shape),
                  full(wr_hi.shape), full(wr_lo.shape), full(ws_gu.shape), full(ws_d.shape)],
        out_specs=[pl.BlockSpec((1, tm, d), lambda b, i: (b, i, 0)),
                   pl.BlockSpec((2, tm, d // 4), lambda b, i: (0, b * ns + i, 0)),
                   pl.BlockSpec((N_EXPERTS, tm), lambda b, i: (0, b * ns + i))],
        out_shape=[jax.ShapeDtypeStruct((bsz, s, d), F32),
                   jax.ShapeDtypeStruct((2, bsz * s, d // 4), jnp.int32),
                   jax.ShapeDtypeStruct((N_EXPERTS, bsz * s), F32)],
        compiler_params=_cparams("parallel", "parallel"),
    )(x, o_sb, o_hg, proj, proj, mod, g2.reshape(1, d), w_sb, w_hg, w_out, wr_hi, wr_lo,
      ws_gu, ws_d)


def _first_argmax(vals, idx, sentinel):
    m = jnp.max(vals, axis=0, keepdims=True)
    first = jnp.min(jnp.where(vals == m, idx, sentinel), axis=0, keepdims=True)
    return m, first


def _route_kernel(lg_ref, bias_ref, gates_ref, rank_ref, cnt_ref, run_ref):
    tn = lg_ref.shape[1]

    @pl.when(pl.program_id(0) == 0)
    def _():
        run_ref[...] = jnp.zeros_like(run_ref)

    neg = -jnp.inf
    scores = jax.nn.sigmoid(lg_ref[...])
    choice = scores + bias_ref[...]

    gidx = lax.broadcasted_iota(jnp.int32, (GROUP_SIZE, tn), 0)
    group_rows = []
    for g in range(N_GROUPS):
        cg = choice[g * GROUP_SIZE:(g + 1) * GROUP_SIZE, :]
        m1, i1 = _first_argmax(cg, gidx, GROUP_SIZE)
        m2 = jnp.max(jnp.where(gidx == i1, neg, cg), axis=0, keepdims=True)
        group_rows.append(m1 + m2)
    work = jnp.concatenate(group_rows, axis=0)
    ggi = lax.broadcasted_iota(jnp.int32, (N_GROUPS, tn), 0)
    gmask = jnp.zeros((N_GROUPS, tn), F32)
    for _ in range(TOPK_GROUPS):
        _, first = _first_argmax(work, ggi, N_GROUPS)
        pick = ggi == first
        gmask = jnp.where(pick, 1.0, gmask)
        work = jnp.where(pick, neg, work)

    masked = jnp.concatenate(
        [jnp.where(gmask[g:g + 1, :] > 0.0, choice[g * GROUP_SIZE:(g + 1) * GROUP_SIZE, :], neg)
         for g in range(N_GROUPS)], axis=0)
    eidx = lax.broadcasted_iota(jnp.int32, (N_EXPERTS, tn), 0)
    sel = jnp.zeros((N_EXPERTS, tn), F32)
    for _ in range(TOP_K):
        _, first = _first_argmax(masked, eidx, N_EXPERTS)
        pick = eidx == first
        sel = jnp.where(pick, 1.0, sel)
        masked = jnp.where(pick, neg, masked)

    chosen = jnp.where(sel > 0.0, scores, 0.0)
    gates_ref[...] = chosen / jnp.sum(chosen, axis=0, keepdims=True) * ROUTED_SCALE

    r = lax.broadcasted_iota(jnp.int32, (tn, tn), 0)
    c = lax.broadcasted_iota(jnp.int32, (tn, tn), 1)
    local = _dot(sel.astype(BF16), (r < c).astype(BF16))
    run = run_ref[:, 0:1]
    rank_ref[...] = jnp.where(sel > 0.0, run + local, -1.0)
    total = run + jnp.sum(sel, axis=1, keepdims=True)
    run_ref[...] = jnp.broadcast_to(total, run_ref.shape)
    cnt_ref[...] = jnp.broadcast_to(total, cnt_ref.shape)


def _route(logits_t, bias):
    e, t = logits_t.shape
    tn = min(1024, t)
    return pl.pallas_call(
        _route_kernel,
        grid=(t // tn,),
        in_specs=[pl.BlockSpec((e, tn), lambda i: (0, i)),
                  pl.BlockSpec((e, 1), lambda i: (0, 0))],
        out_specs=[pl.BlockSpec((e, tn), lambda i: (0, i)),
                   pl.BlockSpec((e, tn), lambda i: (0, i)),
                   pl.BlockSpec((e, LANES), lambda i: (0, 0))],
        out_shape=[jax.ShapeDtypeStruct((e, t), F32),
                   jax.ShapeDtypeStruct((e, t), F32),
                   jax.ShapeDtypeStruct((e, LANES), F32)],
        scratch_shapes=[pltpu.VMEM((e, LANES), F32)],
        compiler_params=_cparams("arbitrary"),
    )(logits_t, bias.reshape(e, 1))


def _slots_kernel(gates_ref, rank_ref, cnt_ref, slot_ref, gate8_ref, blk_ref, *, rows):
    ne, tn = gates_ref.shape
    cnt = cnt_ref[...]
    nblk = jnp.floor((cnt + (rows - 1.0)) * (1.0 / rows))
    er = lax.broadcasted_iota(jnp.int32, (ne, ne), 0)
    ec = lax.broadcasted_iota(jnp.int32, (ne, ne), 1)
    lower = (ec < er).astype(BF16)
    pad_start = _dot(lower, nblk.astype(BF16))[:, 0:1] * rows
    pad_end = pad_start + nblk[:, 0:1] * rows

    rank = rank_ref[...]
    sel = rank >= 0.0
    slot_e = pad_start + rank
    kidx = _dot(lower, sel.astype(BF16))
    gates = gates_ref[...]
    slot_rows, gate_rows = [], []
    for k in range(TOP_K):
        m = jnp.logical_and(sel, kidx == k)
        slot_rows.append(jnp.sum(jnp.where(m, slot_e, 0.0), axis=0, keepdims=True))
        gate_rows.append(jnp.sum(jnp.where(m, gates, 0.0), axis=0, keepdims=True))
    slot_ref[...] = jnp.concatenate(slot_rows, axis=0).astype(jnp.int32)
    gate8_ref[...] = jnp.concatenate(gate_rows, axis=0).T

    nbp = blk_ref.shape[1]
    bstart = lax.broadcasted_iota(jnp.int32, (1, nbp), 1).astype(F32) * rows
    e_of = jnp.sum((pad_end <= bstart).astype(F32), axis=0, keepdims=True)
    e_of = jnp.minimum(e_of, ne - 1.0)
    eidx = lax.broadcasted_iota(jnp.int32, (ne, nbp), 0).astype(F32)
    valid_e = jnp.clip(cnt[:, 0:1] - (bstart - pad_start), 0.0, rows)
    valid = jnp.sum(jnp.where(eidx == e_of, valid_e, 0.0), axis=0, keepdims=True)
    blk_ref[...] = jnp.concatenate(
        [e_of, valid, jnp.zeros((blk_ref.shape[0] - 2, nbp), F32)], axis=0).astype(jnp.int32)


def _slots(gates_t, rank_t, counts, rows, nbp):
    e, t = gates_t.shape
    assert t // rows <= 256
    tn = min(1024, t)
    return pl.pallas_call(
        functools.partial(_slots_kernel, rows=rows),
        grid=(t // tn,),
        in_specs=[pl.BlockSpec((e, tn), lambda i: (0, i)),
                  pl.BlockSpec((e, tn), lambda i: (0, i)),
                  pl.BlockSpec((e, LANES), lambda i: (0, 0))],
        out_specs=[pl.BlockSpec((TOP_K, tn), lambda i: (0, i)),
                   pl.BlockSpec((tn, TOP_K), lambda i: (i, 0)),
                   pl.BlockSpec((8, nbp), lambda i: (0, 0))],
        out_shape=[jax.ShapeDtypeStruct((TOP_K, t), jnp.int32),
                   jax.ShapeDtypeStruct((t, TOP_K), F32),
                   jax.ShapeDtypeStruct((8, nbp), jnp.int32)],
        compiler_params=_cparams("arbitrary"),
    )(gates_t, rank_t, counts)


SC_WINDOW = 128


def _sc_mesh():
    return plsc.VectorSubcoreMesh(core_axis_name="core", subcore_axis_name="subcore")


def _sc_gather_rows(table, idx):
    n = idx.shape[1]
    w = table.shape[1]

    @pl.kernel(out_type=jax.ShapeDtypeStruct((n, w), table.dtype), mesh=_sc_mesh())
    def gather(t_hbm, i_hbm, o_hbm):
        def body(i_vmem, o_vmem):
            pltpu.sync_copy(t_hbm.at[i_vmem.at[0]], o_vmem)

        pltpu.emit_pipeline(
            body, grid=(n // SC_WINDOW,),
            in_specs=[pl.BlockSpec((1, SC_WINDOW), lambda i: (0, i))],
            out_specs=[pl.BlockSpec((SC_WINDOW, w), lambda i: (i, 0))],
            core_axis_name=("core", "subcore"),
            dimension_semantics=(pltpu.PARALLEL,),
        )(i_hbm, o_hbm)

    return gather(table, idx)


def _sc_scatter_rows(rows, idx, n_out):
    n = idx.shape[1]
    m, w = rows.shape
    per = m // SC_WINDOW

    @pl.kernel(out_type=jax.ShapeDtypeStruct((n_out, w), rows.dtype), mesh=_sc_mesh())
    def scatter(r_hbm, i_hbm, o_hbm):
        def body(r_vmem, i_vmem):
            pltpu.sync_copy(r_vmem, o_hbm.at[i_vmem.at[0]])

        pltpu.emit_pipeline(
            body, grid=(n // SC_WINDOW,),
            in_specs=[pl.BlockSpec((SC_WINDOW, w), lambda i: (i % per, 0)),
                      pl.BlockSpec((1, SC_WINDOW), lambda i: (0, i))],
            out_specs=[],
            core_axis_name=("core", "subcore"),
            dimension_semantics=(pltpu.PARALLEL,),
        )(r_hbm, i_hbm)

    return scatter(rows, idx)


def _gmm_kernel(be_ref, bv_ref, xs_ref, wgu_ref, wd_ref, ys_ref):
    valid = bv_ref[pl.program_id(0)]

    @pl.when(valid > 0)
    def _():
        x = _load_planes(xs_ref[0], xs_ref[1]).astype(BF16)
        y = _dot(_swiglu_hidden(x, wgu_ref[0]).astype(BF16), wd_ref[0])
        row = lax.broadcasted_iota(jnp.int32, (y.shape[0], 1), 0)
        _store_planes(ys_ref, jnp.where(row < valid, y, 0.0))

    @pl.when(valid <= 0)
    def _():
        ys_ref[...] = jnp.zeros_like(ys_ref)


def _gmm(blk_expert, blk_valid, xs, we_gu, we_d, rows):
    _, n_slots, q = xs.shape
    return pl.pallas_call(
        _gmm_kernel,
        grid_spec=pltpu.PrefetchScalarGridSpec(
            num_scalar_prefetch=2,
            grid=(n_slots // rows,),
            in_specs=[pl.BlockSpec((2, rows, q), lambda b, be, bv: (0, b, 0)),
                      pl.BlockSpec((1,) + we_gu.shape[1:], lambda b, be, bv: (be[b], 0, 0)),
                      pl.BlockSpec((1,) + we_d.shape[1:], lambda b, be, bv: (be[b], 0, 0))],
            out_specs=pl.BlockSpec((2, rows, q), lambda b, be, bv: (0, b, 0))),
        out_shape=jax.ShapeDtypeStruct(xs.shape, jnp.int32),
        compiler_params=_cparams("arbitrary"),
    )(blk_expert, blk_valid, xs, we_gu, we_d)


def _combine_kernel(x2_ref, mod_ref, y8_ref, g8_ref, o_ref):
    g8 = g8_ref[...]
    routed = None
    for k in range(TOP_K):
        term = g8[:, k:k + 1] * _load_planes(y8_ref[k, 0], y8_ref[k, 1])
        routed = term if routed is None else routed + term
    o_ref[0] = x2_ref[0] + mod_ref[0, 5:6, :] * routed


def _combine(x2, mod, y8, gate8):
    bsz, s, d = x2.shape
    tm = min(256, s)
    ns = s // tm
    return pl.pallas_call(
        _combine_kernel,
        grid=(bsz, ns),
        in_specs=[pl.BlockSpec((1, tm, d), lambda b, i: (b, i, 0)),
                  pl.BlockSpec((1, N_MOD, d), lambda b, i: (b, 0, 0)),
                  pl.BlockSpec((TOP_K, 2, tm, d // 4), lambda b, i: (0, 0, b * ns + i, 0)),
                  pl.BlockSpec((tm, TOP_K), lambda b, i: (b * ns + i, 0))],
        out_specs=pl.BlockSpec((1, tm, d), lambda b, i: (b, i, 0)),
        out_shape=jax.ShapeDtypeStruct((bsz, s, d), F32),
        compiler_params=_cparams("parallel", "parallel"),
    )(x2, mod, y8, gate8)


def kernel(x, c, w_ada, b_ada, norm1_g, w_in, sb_q_norm_g, sb_k_norm_g, hg_lb_logits, hg_norm_g,
           w_branch_sb, w_branch_hg, w_out, norm2_g, w_router, router_bias, w_e_gate, w_e_up,
           w_e_down, w_s_gate, w_s_up, w_s_down):
    bsz, s, d = x.shape
    depth = w_ada.shape[0]
    n_gate_cols = 2 * d
    qkv_col0 = n_gate_cols // LANES
    hg_col0 = qkv_col0 + 3 * SB_WIDTH // LANES
    for l in range(depth):
        n_mix = 3 * SB_WIDTH + 4 * HG_WIDTH
        w_in_l = jnp.concatenate([w_in[l][:, n_mix:], w_in[l][:, :n_mix]], axis=1).astype(BF16)
        wr_t = w_router[l].T
        wr_hi = wr_t.astype(BF16)
        wr_lo = (wr_t - wr_hi.astype(F32)).astype(BF16)
        ws_gu = jnp.concatenate([w_s_gate[l], w_s_up[l]], axis=1).astype(BF16)
        we_gu = jnp.concatenate([w_e_gate[l], w_e_up[l]], axis=2).astype(BF16)

        mod = _ada(c, w_ada[l], b_ada[l]).reshape(bsz, N_MOD, d)
        proj = _inproj(x, mod, norm1_g[l], w_in_l)
        o_sb = _sb_attention(proj, sb_q_norm_g[l], sb_k_norm_g[l], qkv_col0)
        o_hg = _hgrn(proj, hg_lb_logits, hg_norm_g[l], hg_col0, l)
        x2, h2p, logits_t = _merge(x, o_sb, o_hg, proj, mod, norm2_g[l],
                                   w_branch_sb[l].astype(BF16), w_branch_hg[l].astype(BF16),
                                   w_out[l].astype(BF16), wr_hi, wr_lo, ws_gu,
                                   w_s_down[l].astype(BF16))
        t = bsz * s
        n_blocks = -(-(t * TOP_K + N_EXPERTS * (DISPATCH_ROWS - 1)) // DISPATCH_ROWS)
        nbp = -(-n_blocks // LANES) * LANES
        gates_t, rank_t, counts = _route(logits_t, router_bias[l])
        slot8, gate8, blk = _slots(gates_t, rank_t, counts, DISPATCH_ROWS, nbp)
        n_slots = n_blocks * DISPATCH_ROWS
        plane_off = jnp.array([0, n_slots], jnp.int32)[None, :, None]
        row_idx = (slot8[:, None, :] + plane_off).reshape(1, TOP_K * 2 * t)
        q = d // 4
        xs = _sc_scatter_rows(h2p.reshape(2 * t, q), row_idx, 2 * n_slots).reshape(2, n_slots, q)
        ys = _gmm(blk[0], blk[1], xs, we_gu, w_e_down[l].astype(BF16), DISPATCH_ROWS)
        y8 = _sc_gather_rows(ys.reshape(2 * n_slots, q), row_idx).reshape(TOP_K, 2, t, q)
        x = _combine(x2, mod, y8, gate8)
    return x
```

```python
import functools

import jax
import jax.numpy as jnp
from jax import lax
from jax.experimental import pallas as pl
from jax.experimental.pallas import tpu as pltpu
from jax.experimental.pallas import tpu_sc as plsc

F32 = jnp.float32
BF16 = jnp.bfloat16

SB_HEADS = 8
SB_HEAD_DIM = 64
SB_WIDTH = SB_HEADS * SB_HEAD_DIM
HG_HEADS = 4
HG_HEAD_DIM = 128
HG_WIDTH = HG_HEADS * HG_HEAD_DIM
HG_CHUNK = 64
N_EXPERTS = 64
TOP_K = 8
N_GROUPS = 8
TOPK_GROUPS = 4
GROUP_SIZE = N_EXPERTS // N_GROUPS
ROUTED_SCALE = 2.5
DISPATCH_ROWS = 512
N_MOD = 6
EPS = 1e-6
LOG2_E = 1.4426950408889634
SB_DEAD_LOG2 = 160.0

LANES = 128
VMEM_LIMIT = 56 * 1024 * 1024


def _cparams(*sem):
    return pltpu.CompilerParams(dimension_semantics=sem, vmem_limit_bytes=VMEM_LIMIT)


def _silu(t):
    return t * jax.nn.sigmoid(t)


def _dot(a, b):
    return jnp.dot(a, b, preferred_element_type=F32)


def _dot_nt(a, b):
    return lax.dot_general(a, b, (((1,), (1,)), ((), ())), preferred_element_type=F32)


def _split_bf16(t):
    hi = t.astype(BF16)
    lo = (t - hi.astype(F32)).astype(BF16)
    return hi, lo


def _ada_kernel(c_ref, w_ref, b_ref, o_ref):
    cond = _silu(c_ref[...])
    o_ref[...] = _dot(cond, w_ref[...]) + b_ref[...]


def _ada(c, w, b):
    bsz, d = c.shape
    n = w.shape[1]
    tn = 1024
    return pl.pallas_call(
        _ada_kernel,
        grid=(n // tn,),
        in_specs=[pl.BlockSpec((bsz, d), lambda j: (0, 0)),
                  pl.BlockSpec((d, tn), lambda j: (0, j)),
                  pl.BlockSpec((1, tn), lambda j: (0, j))],
        out_specs=pl.BlockSpec((bsz, tn), lambda j: (0, j)),
        out_shape=jax.ShapeDtypeStruct((bsz, n), F32),
        compiler_params=_cparams("parallel"),
    )(c, w, b.reshape(1, n))


def _modulated_norm(x, g, shift, scale):
    y = x * lax.rsqrt(jnp.mean(x * x, axis=-1, keepdims=True) + EPS) * g
    return y * (1.0 + scale) + shift


def _inproj_kernel(x_ref, mod_ref, g_ref, w_ref, o_ref, h_ref):
    @pl.when(pl.program_id(2) == 0)
    def _():
        h = _modulated_norm(x_ref[0], g_ref[...], mod_ref[0, 0:1, :], mod_ref[0, 1:2, :])
        h_ref[...] = h.astype(BF16)

    o_ref[0] = _dot(h_ref[...], w_ref[...]).astype(o_ref.dtype)


def _inproj(x, mod, g, w):
    bsz, s, d = x.shape
    n = w.shape[1]
    tm = min(1024, s)
    tn = 512
    return pl.pallas_call(
        _inproj_kernel,
        grid=(bsz, s // tm, n // tn),
        in_specs=[pl.BlockSpec((1, tm, d), lambda b, i, j: (b, i, 0)),
                  pl.BlockSpec((1, N_MOD, d), lambda b, i, j: (b, 0, 0)),
                  pl.BlockSpec((1, d), lambda b, i, j: (0, 0)),
                  pl.BlockSpec((d, tn), lambda b, i, j: (0, j))],
        out_specs=pl.BlockSpec((1, tm, tn), lambda b, i, j: (b, i, j)),
        out_shape=jax.ShapeDtypeStruct((bsz, s, n), BF16),
        scratch_shapes=[pltpu.VMEM((tm, d), BF16)],
        compiler_params=_cparams("parallel", "parallel", "arbitrary"),
    )(x, mod, g.reshape(1, d), w)


def _pair_norm(t, g, lo_half):
    sq = t * t
    s_lo = jnp.sum(jnp.where(lo_half, sq, 0.0), axis=-1, keepdims=True)
    s_hi = jnp.sum(jnp.where(lo_half, 0.0, sq), axis=-1, keepdims=True)
    ms = jnp.where(lo_half, s_lo, s_hi) * (1.0 / SB_HEAD_DIM)
    return t * lax.rsqrt(ms + EPS) * g


def _neg_abs(t):
    bits = lax.bitcast_convert_type(t, jnp.uint32) | jnp.uint32(0x80000000)
    return lax.bitcast_convert_type(bits, F32)


def _split_trunc(t):
    bits = lax.bitcast_convert_type(t, jnp.uint32) & jnp.uint32(0xFFFF0000)
    hi = lax.bitcast_convert_type(bits, F32)
    return hi.astype(BF16), (t - hi).astype(BF16)


def _sb_kernel(q_ref, k_ref, v_ref, qg_ref, kg_ref, o_ref, knt_ref, va_ref, vb_ref, *, tq):
    qi = pl.program_id(2)
    s = k_ref.shape[1]
    lane = lax.broadcasted_iota(jnp.int32, (1, LANES), 1)
    lo_half = lane < SB_HEAD_DIM

    tk = tq // 2

    @pl.when(qi == 0)
    def _():
        def prep_block(j, c):
            rows = pl.ds(pl.multiple_of(j * tk, tk), tk)
            kb = _pair_norm(k_ref[0, rows, :].astype(F32), kg_ref[...], lo_half)
            knt_ref[:, rows] = kb.T.astype(BF16)
            vb = v_ref[0, rows, :]
            va_ref[rows, :] = jnp.where(lo_half, vb, jnp.zeros_like(vb))
            vb_ref[rows, :] = jnp.where(lo_half, jnp.zeros_like(vb), vb)
            return c
        lax.fori_loop(0, s // tk, prep_block, 0)

    scale = SB_HEAD_DIM ** -0.5 * LOG2_E
    q = _pair_norm(q_ref[0].astype(F32), qg_ref[...], lo_half) * scale
    q_heads = (jnp.where(lo_half, q, 0.0).astype(BF16), jnp.where(lo_half, 0.0, q).astype(BF16))
    v_heads = (va_ref, vb_ref)

    row = lax.broadcasted_iota(jnp.int32, (tk, tk), 0)
    col = lax.broadcasted_iota(jnp.int32, (tk, tk), 1)
    strict = col < row
    tri = (row >= col).astype(BF16)
    tri2 = jnp.concatenate([tri, tri], axis=0)

    def sweep(streams):
        cols = [[pl.ds(pl.multiple_of(j * tk, tk), tk) for j, _ in blocks] for _, _, _, blocks in streams]
        z = [[[_dot(qh[h], knt_ref[:, c]) for c in cols[i]] for h in range(2)]
             for i, (qh, _, _, _) in enumerate(streams)]
        cs = []
        for i, (_, _, _, blocks) in enumerate(streams):
            cs.append([[None] * len(blocks) for _ in range(2)])
            for h in range(2):
                for b, (_, masked) in enumerate(blocks):
                    zb = z[i][h][b]
                    sp = jnp.maximum(zb, 0.0) + jnp.log2(1.0 + jnp.exp2(_neg_abs(zb)))
                    if masked:
                        sp = jnp.where(strict, sp, 0.0)
                    cs[i][h][b] = _dot(jnp.concatenate(_split_trunc(sp), axis=1), tri2)
        results = []
        for i, (_, runs, acc, blocks) in enumerate(streams):
            runs = list(runs)
            for h in range(2):
                for b, (_, masked) in enumerate(blocks):
                    a = jnp.exp2(z[i][h][b] - cs[i][h][b] - runs[h])
                    if masked:
                        a = jnp.where(strict, a, 0.0)
                    acc = acc + _dot(a.astype(BF16), v_heads[h][cols[i][b], :])
                    runs[h] = runs[h] + cs[i][h][b][:, 0:1]
            results.append((tuple(runs), acc))
        return results

    zero_run = jnp.zeros((tk, 1), F32)
    zero_acc = jnp.zeros((tk, LANES), F32)
    left, right = 2 * qi, 2 * qi + 1
    (runs_t, acc_t), (runs_b, acc_b) = sweep([
        (tuple(qh[:tk] for qh in q_heads), (zero_run, zero_run), zero_acc, [(left, True)]),
        (tuple(qh[tk:] for qh in q_heads), (zero_run, zero_run), zero_acc,
         [(right, True), (left, False)])])
    runs = tuple(jnp.concatenate([runs_t[h], runs_b[h]], axis=0) for h in range(2))
    acc = jnp.concatenate([acc_t, acc_b], axis=0)

    def min_run(rs):
        return jnp.min(jnp.minimum(rs[0], rs[1]))

    def alive(carry):
        i, _, _, low = carry
        return jnp.logical_and(i < qi, low < SB_DEAD_LOG2)

    def earlier_pair(carry):
        i, rs, ac, _ = carry
        j = 2 * qi - 1 - 2 * i
        rs, ac = sweep([(q_heads, rs, ac, [(j, False), (j - 1, False)])])[0]
        return i + 1, rs, ac, min_run(rs)

    _, _, acc, _ = lax.while_loop(alive, earlier_pair, (jnp.int32(0), runs, acc, min_run(runs)))
    o_ref[0] = acc.astype(o_ref.dtype)


def _sb_attention(proj, qg, kg, col0):
    bsz, s, _ = proj.shape
    tq = min(512, s)
    npair = SB_WIDTH // LANES
    qg2 = jnp.tile(qg.reshape(1, SB_HEAD_DIM), (1, 2))
    kg2 = jnp.tile(kg.reshape(1, SB_HEAD_DIM), (1, 2))
    return pl.pallas_call(
        functools.partial(_sb_kernel, tq=tq),
        grid=(bsz, npair, s // tq),
        in_specs=[pl.BlockSpec((1, tq, LANES), lambda b, h, i: (b, i, col0 + h)),
                  pl.BlockSpec((1, s, LANES), lambda b, h, i: (b, 0, col0 + npair + h)),
                  pl.BlockSpec((1, s, LANES), lambda b, h, i: (b, 0, col0 + 2 * npair + h)),
                  pl.BlockSpec((1, LANES), lambda b, h, i: (0, 0)),
                  pl.BlockSpec((1, LANES), lambda b, h, i: (0, 0))],
        out_specs=pl.BlockSpec((1, tq, LANES), lambda b, h, i: (b, i, h)),
        out_shape=jax.ShapeDtypeStruct((bsz, s, SB_WIDTH), BF16),
        scratch_shapes=[pltpu.VMEM((LANES, s), BF16),
                        pltpu.VMEM((s, LANES), BF16),
                        pltpu.VMEM((s, LANES), BF16)],
        compiler_params=_cparams("parallel", "parallel", "arbitrary"),
    )(proj, proj, proj, qg2, kg2)


HG_CHUNKS_PER_STEP = 4


def _hgrn_kernel(f_ref, i_ref, q_ref, g_ref, lbl_ref, ng_ref, o_ref, st_ref, *, layer):
    ts = f_ref.shape[1]
    c = HG_CHUNK
    dh = HG_HEAD_DIM

    @pl.when(pl.program_id(1) == 0)
    def _():
        st_ref[...] = jnp.zeros_like(st_ref)

    lg = lbl_ref[...]
    e = jnp.exp(lg - jnp.max(lg, axis=0, keepdims=True))
    lb = jnp.sum(e[:layer + 1], axis=0, keepdims=True) / jnp.sum(e, axis=0, keepdims=True)

    row = lax.broadcasted_iota(jnp.int32, (c, c), 0)
    col = lax.broadcasted_iota(jnp.int32, (c, c), 1)
    causal = col <= row
    lower = causal.astype(BF16)
    width = f_ref.shape[2]

    def body(it, carry):
        states = [st_ref[h] for h in range(HG_HEADS)]
        for u in range(HG_CHUNKS_PER_STEP):
            rows = pl.ds(pl.multiple_of((it * HG_CHUNKS_PER_STEP + u) * c, c), c)
            forget = lb + (1.0 - lb) * jax.nn.sigmoid(f_ref[0, rows, :].astype(F32))
            kc = 1.0 - forget
            hi, lo = _split_trunc(jnp.log(forget))
            cum = _dot(lower, jnp.concatenate([hi, lo], axis=1))
            gc = cum[:, :width] + cum[:, width:]
            g_mid = gc[c // 2 - 1:c // 2, :]
            g_last = gc[c - 1:c, :]
            qe = _silu(q_ref[0, rows, :].astype(F32)) * jnp.exp(gc - g_mid)
            ke = kc * jnp.exp(g_mid - gc)
            qg = (qe * jnp.exp(g_mid)).astype(BF16)
            kd = (ke * jnp.exp(g_last - g_mid)).astype(BF16)
            qe = qe.astype(BF16)
            ke = ke.astype(BF16)
            decay = jnp.exp(g_last)
            v = i_ref[0, rows, :]
            v_t = v.astype(F32).T.astype(BF16)
            outs = []
            for h in range(HG_HEADS):
                sl = slice(h * dh, (h + 1) * dh)
                scores = _dot_nt(qe[:, sl], ke[:, sl])
                o = (_dot(jnp.where(causal, scores, 0.0).astype(BF16), v[:, sl])
                     + _dot_nt(qg[:, sl], states[h].astype(BF16)))
                outs.append(o * lax.rsqrt(jnp.mean(o * o, axis=-1, keepdims=True) + EPS))
                states[h] = states[h] * decay[:, sl] + _dot(v_t[sl, :], kd[:, sl])
            y = jnp.concatenate(outs, axis=1) * ng_ref[...] * _silu(g_ref[0, rows, :].astype(F32))
            o_ref[0, rows, :] = y.astype(o_ref.dtype)
        for h in range(HG_HEADS):
            st_ref[h] = states[h]
        return carry

    lax.fori_loop(0, ts // (c * HG_CHUNKS_PER_STEP), body, 0)


def _hgrn(proj, lb_logits, norm_g, col0, layer):
    bsz, s, _ = proj.shape
    nl = lb_logits.shape[0]
    ts = min(1024, s)
    blk0 = col0 * LANES // HG_WIDTH
    spec = lambda off: pl.BlockSpec((1, ts, HG_WIDTH), lambda b, i: (b, i, blk0 + off))
    return pl.pallas_call(
        functools.partial(_hgrn_kernel, layer=layer),
        grid=(bsz, s // ts),
        in_specs=[spec(0), spec(1), spec(2), spec(3),
                  pl.BlockSpec((nl, HG_WIDTH), lambda b, i: (0, 0)),
                  pl.BlockSpec((1, HG_WIDTH), lambda b, i: (0, 0))],
        out_specs=pl.BlockSpec((1, ts, HG_WIDTH), lambda b, i: (b, i, 0)),
        out_shape=jax.ShapeDtypeStruct((bsz, s, HG_WIDTH), BF16),
        scratch_shapes=[pltpu.VMEM((HG_HEADS, HG_HEAD_DIM, HG_HEAD_DIM), F32)],
        compiler_params=_cparams("parallel", "arbitrary"),
    )(proj, proj, proj, proj, lb_logits, jnp.tile(norm_g.reshape(1, HG_HEAD_DIM), (1, HG_HEADS)))


def _pack_halves(t):
    bits = lax.bitcast_convert_type(t.astype(BF16).astype(F32), jnp.uint32)
    w = t.shape[1] // 2
    return lax.bitcast_convert_type(bits[:, :w] | (bits[:, w:] >> 16), jnp.int32)


def _unpack_halves(p):
    u = lax.bitcast_convert_type(p, jnp.uint32)
    return (lax.bitcast_convert_type(u & jnp.uint32(0xFFFF0000), F32),
            lax.bitcast_convert_type(u << 16, F32))


def _store_planes(ref, t):
    p = _pack_halves(t)
    q = p.shape[1] // 2
    ref[0] = p[:, :q]
    ref[1] = p[:, q:]


def _load_planes(p0, p1):
    a0, b0 = _unpack_halves(p0)
    a1, b1 = _unpack_halves(p1)
    return jnp.concatenate([a0, a1, b0, b1], axis=1)


def _swiglu_hidden(h, w_gu):
    gu = _dot(h, w_gu)
    hid = w_gu.shape[1] // 2
    return _silu(gu[:, :hid]) * gu[:, hid:]


def _merge_kernel(x_ref, osb_ref, ohg_ref, gsb_ref, ghg_ref, mod_ref, g2_ref, wsb_ref, whg_ref,
                  wout_ref, wrh_ref, wrl_ref, wsgu_ref, wsd_ref, x2_ref, h2p_ref, lg_ref):
    m_sb = _dot(osb_ref[0], wsb_ref[...])
    m_hg = _dot(ohg_ref[0], whg_ref[...])
    merged = (jax.nn.sigmoid(gsb_ref[0].astype(F32)) * m_sb
              + jax.nn.sigmoid(ghg_ref[0].astype(F32)) * m_hg)
    x1 = x_ref[0] + mod_ref[0, 2:3, :] * _dot(merged.astype(BF16), wout_ref[...])
    h2 = _modulated_norm(x1, g2_ref[...], mod_ref[0, 3:4, :], mod_ref[0, 4:5, :])
    hi, lo = _split_bf16(h2)
    _store_planes(h2p_ref, h2)
    lg_ref[...] = _dot_nt(wrh_ref[...], hi) + _dot_nt(wrh_ref[...], lo) + _dot_nt(wrl_ref[...], hi)
    shared = _dot(_swiglu_hidden(hi, wsgu_ref[...]).astype(BF16), wsd_ref[...])
    x2_ref[0] = x1 + mod_ref[0, 5:6, :] * shared


def _merge(x, o_sb, o_hg, proj, mod, g2, w_sb, w_hg, w_out, wr_hi, wr_lo, ws_gu, ws_d):
    bsz, s, d = x.shape
    tm = min(512, s)
    ns = s // tm
    full = lambda shape: pl.BlockSpec(shape, lambda b, i: (0,) * len(shape))
    return pl.pallas_call(
        _merge_kernel,
        grid=(bsz, ns),
        in_specs=[pl.BlockSpec((1, tm, d), lambda b, i: (b, i, 0)),
                  pl.BlockSpec((1, tm, SB_WIDTH), lambda b, i: (b, i, 0)),
                  pl.BlockSpec((1, tm, HG_WIDTH), lambda b, i: (b, i, 0)),
                  pl.BlockSpec((1, tm, d), lambda b, i: (b, i, 0)),
                  pl.BlockSpec((1, tm, d), lambda b, i: (b, i, 1)),
                  pl.BlockSpec((1, N_MOD, d), lambda b, i: (b, 0, 0)),
                  full((1, d)), full(w_sb.shape), full(w_hg.shape), full(w_out.shape),
                  full(wr_hi.shape), full(wr_lo.shape), full(ws_gu.shape), full(ws_d.shape)],
        out_specs=[pl.BlockSpec((1, tm, d), lambda b, i: (b, i, 0)),
                   pl.BlockSpec((2, tm, d // 4), lambda b, i: (0, b * ns + i, 0)),
                   pl.BlockSpec((N_EXPERTS, tm), lambda b, i: (0, b * ns + i))],
        out_shape=[jax.ShapeDtypeStruct((bsz, s, d), F32),
                   jax.ShapeDtypeStruct((2, bsz * s, d // 4), jnp.int32),
                   jax.ShapeDtypeStruct((N_EXPERTS, bsz * s), F32)],
        compiler_params=_cparams("parallel", "parallel"),
    )(x, o_sb, o_hg, proj, proj, mod, g2.reshape(1, d), w_sb, w_hg, w_out, wr_hi, wr_lo,
      ws_gu, ws_d)


def _first_argmax(vals, idx, sentinel):
    m = jnp.max(vals, axis=0, keepdims=True)
    first = jnp.min(jnp.where(vals == m, idx, sentinel), axis=0, keepdims=True)
    return m, first


def _route_kernel(lg_ref, bias_ref, gates_ref, rank_ref, cnt_ref, run_ref):
    tn = lg_ref.shape[1]

    @pl.when(pl.program_id(0) == 0)
    def _():
        run_ref[...] = jnp.zeros_like(run_ref)

    neg = -jnp.inf
    scores = jax.nn.sigmoid(lg_ref[...])
    choice = scores + bias_ref[...]

    gidx = lax.broadcasted_iota(jnp.int32, (GROUP_SIZE, tn), 0)
    group_rows = []
    for g in range(N_GROUPS):
        cg = choice[g * GROUP_SIZE:(g + 1) * GROUP_SIZE, :]
        m1, i1 = _first_argmax(cg, gidx, GROUP_SIZE)
        m2 = jnp.max(jnp.where(gidx == i1, neg, cg), axis=0, keepdims=True)
        group_rows.append(m1 + m2)
    work = jnp.concatenate(group_rows, axis=0)
    ggi = lax.broadcasted_iota(jnp.int32, (N_GROUPS, tn), 0)
    gmask = jnp.zeros((N_GROUPS, tn), F32)
    for _ in range(TOPK_GROUPS):
        _, first = _first_argmax(work, ggi, N_GROUPS)
        pick = ggi == first
        gmask = jnp.where(pick, 1.0, gmask)
        work = jnp.where(pick, neg, work)

    masked = jnp.concatenate(
        [jnp.where(gmask[g:g + 1, :] > 0.0, choice[g * GROUP_SIZE:(g + 1) * GROUP_SIZE, :], neg)
         for g in range(N_GROUPS)], axis=0)
    eidx = lax.broadcasted_iota(jnp.int32, (N_EXPERTS, tn), 0)
    sel = jnp.zeros((N_EXPERTS, tn), F32)
    for _ in range(TOP_K):
        _, first = _first_argmax(masked, eidx, N_EXPERTS)
        pick = eidx == first
        sel = jnp.where(pick, 1.0, sel)
        masked = jnp.where(pick, neg, masked)

    chosen = jnp.where(sel > 0.0, scores, 0.0)
    gates_ref[...] = chosen / jnp.sum(chosen, axis=0, keepdims=True) * ROUTED_SCALE

    r = lax.broadcasted_iota(jnp.int32, (tn, tn), 0)
    c = lax.broadcasted_iota(jnp.int32, (tn, tn), 1)
    local = _dot(sel.astype(BF16), (r < c).astype(BF16))
    run = run_ref[:, 0:1]
    rank_ref[...] = jnp.where(sel > 0.0, run + local, -1.0)
    total = run + jnp.sum(sel, axis=1, keepdims=True)
    run_ref[...] = jnp.broadcast_to(total, run_ref.shape)
    cnt_ref[...] = jnp.broadcast_to(total, cnt_ref.shape)


def _route(logits_t, bias):
    e, t = logits_t.shape
    tn = min(1024, t)
    return pl.pallas_call(
        _route_kernel,
        grid=(t // tn,),
        in_specs=[pl.BlockSpec((e, tn), lambda i: (0, i)),
                  pl.BlockSpec((e, 1), lambda i: (0, 0))],
        out_specs=[pl.BlockSpec((e, tn), lambda i: (0, i)),
                   pl.BlockSpec((e, tn), lambda i: (0, i)),
                   pl.BlockSpec((e, LANES), lambda i: (0, 0))],
        out_shape=[jax.ShapeDtypeStruct((e, t), F32),
                   jax.ShapeDtypeStruct((e, t), F32),
                   jax.ShapeDtypeStruct((e, LANES), F32)],
        scratch_shapes=[pltpu.VMEM((e, LANES), F32)],
        compiler_params=_cparams("arbitrary"),
    )(logits_t, bias.reshape(e, 1))


def _slots_kernel(gates_ref, rank_ref, cnt_ref, slot_ref, gate8_ref, blk_ref, *, rows):
    ne, tn = gates_ref.shape
    cnt = cnt_ref[...]
    nblk = jnp.floor((cnt + (rows - 1.0)) * (1.0 / rows))
    er = lax.broadcasted_iota(jnp.int32, (ne, ne), 0)
    ec = lax.broadcasted_iota(jnp.int32, (ne, ne), 1)
    lower = (ec < er).astype(BF16)
    pad_start = _dot(lower, nblk.astype(BF16))[:, 0:1] * rows
    pad_end = pad_start + nblk[:, 0:1] * rows

    rank = rank_ref[...]
    sel = rank >= 0.0
    slot_e = pad_start + rank
    kidx = _dot(lower, sel.astype(BF16))
    gates = gates_ref[...]
    slot_rows, gate_rows = [], []
    for k in range(TOP_K):
        m = jnp.logical_and(sel, kidx == k)
        slot_rows.append(jnp.sum(jnp.where(m, slot_e, 0.0), axis=0, keepdims=True))
        gate_rows.append(jnp.sum(jnp.where(m, gates, 0.0), axis=0, keepdims=True))
    slot_ref[...] = jnp.concatenate(slot_rows, axis=0).astype(jnp.int32)
    gate8_ref[...] = jnp.concatenate(gate_rows, axis=0).T

    nbp = blk_ref.shape[1]
    bstart = lax.broadcasted_iota(jnp.int32, (1, nbp), 1).astype(F32) * rows
    e_of = jnp.sum((pad_end <= bstart).astype(F32), axis=0, keepdims=True)
    e_of = jnp.minimum(e_of, ne - 1.0)
    eidx = lax.broadcasted_iota(jnp.int32, (ne, nbp), 0).astype(F32)
    valid_e = jnp.clip(cnt[:, 0:1] - (bstart - pad_start), 0.0, rows)
    valid = jnp.sum(jnp.where(eidx == e_of, valid_e, 0.0), axis=0, keepdims=True)
    blk_ref[...] = jnp.concatenate(
        [e_of, valid, jnp.zeros((blk_ref.shape[0] - 2, nbp), F32)], axis=0).astype(jnp.int32)


def _slots(gates_t, rank_t, counts, rows, nbp):
    e, t = gates_t.shape
    assert t // rows <= 256
    tn = min(1024, t)
    return pl.pallas_call(
        functools.partial(_slots_kernel, rows=rows),
        grid=(t // tn,),
        in_specs=[pl.BlockSpec((e, tn), lambda i: (0, i)),
                  pl.BlockSpec((e, tn), lambda i: (0, i)),
                  pl.BlockSpec((e, LANES), lambda i: (0, 0))],
        out_specs=[pl.BlockSpec((TOP_K, tn), lambda i: (0, i)),
                   pl.BlockSpec((tn, TOP_K), lambda i: (i, 0)),
                   pl.BlockSpec((8, nbp), lambda i: (0, 0))],
        out_shape=[jax.ShapeDtypeStruct((TOP_K, t), jnp.int32),
                   jax.ShapeDtypeStruct((t, TOP_K), F32),
                   jax.ShapeDtypeStruct((8, nbp), jnp.int32)],
        compiler_params=_cparams("arbitrary"),
    )(gates_t, rank_t, counts)


SC_WINDOW = 128


def _sc_mesh():
    return plsc.VectorSubcoreMesh(core_axis_name="core", subcore_axis_name="subcore")


def _sc_gather_rows(table, idx):
    n = idx.shape[1]
    w = table.shape[1]

    @pl.kernel(out_type=jax.ShapeDtypeStruct((n, w), table.dtype), mesh=_sc_mesh())
    def gather(t_hbm, i_hbm, o_hbm):
        def body(i_vmem, o_vmem):
            pltpu.sync_copy(t_hbm.at[i_vmem.at[0]], o_vmem)

        pltpu.emit_pipeline(
            body, grid=(n // SC_WINDOW,),
            in_specs=[pl.BlockSpec((1, SC_WINDOW), lambda i: (0, i))],
            out_specs=[pl.BlockSpec((SC_WINDOW, w), lambda i: (i, 0))],
            core_axis_name=("core", "subcore"),
            dimension_semantics=(pltpu.PARALLEL,),
        )(i_hbm, o_hbm)

    return gather(table, idx)


def _sc_scatter_rows(rows, idx, n_out):
    n = idx.shape[1]
    m, w = rows.shape
    per = m // SC_WINDOW

    @pl.kernel(out_type=jax.ShapeDtypeStruct((n_out, w), rows.dtype), mesh=_sc_mesh())
    def scatter(r_hbm, i_hbm, o_hbm):
        def body(r_vmem, i_vmem):
            pltpu.sync_copy(r_vmem, o_hbm.at[i_vmem.at[0]])

        pltpu.emit_pipeline(
            body, grid=(n // SC_WINDOW,),
            in_specs=[pl.BlockSpec((SC_WINDOW, w), lambda i: (i % per, 0)),
                      pl.BlockSpec((1, SC_WINDOW), lambda i: (0, i))],
            out_specs=[],
            core_axis_name=("core", "subcore"),
            dimension_semantics=(pltpu.PARALLEL,),
        )(r_hbm, i_hbm)

    return scatter(rows, idx)


def _gmm_kernel(be_ref, bv_ref, xs_ref, wg_ref, wu_ref, wd_ref, ys_ref, wgu_s, wd_s):
    b = pl.program_id(0)
    valid = bv_ref[b]
    hid = wg_ref.shape[2]

    @pl.when(jnp.logical_or(b == 0, be_ref[b] != be_ref[jnp.maximum(b - 1, 0)]))
    def _():
        wgu_s[:, :hid] = wg_ref[0].astype(BF16)
        wgu_s[:, hid:] = wu_ref[0].astype(BF16)
        wd_s[...] = wd_ref[0].astype(BF16)

    @pl.when(valid > 0)
    def _():
        x = _load_planes(xs_ref[0], xs_ref[1]).astype(BF16)
        y = _dot(_swiglu_hidden(x, wgu_s[...]).astype(BF16), wd_s[...])
        row = lax.broadcasted_iota(jnp.int32, (y.shape[0], 1), 0)
        _store_planes(ys_ref, jnp.where(row < valid, y, 0.0))

    @pl.when(valid <= 0)
    def _():
        ys_ref[...] = jnp.zeros_like(ys_ref)


def _gmm(blk_expert, blk_valid, xs, we_gate, we_up, we_down, rows):
    _, n_slots, q = xs.shape
    _, d, hid = we_gate.shape
    w_spec = lambda w: pl.BlockSpec((1,) + w.shape[1:], lambda b, be, bv: (be[b], 0, 0))
    return pl.pallas_call(
        _gmm_kernel,
        grid_spec=pltpu.PrefetchScalarGridSpec(
            num_scalar_prefetch=2,
            grid=(n_slots // rows,),
            in_specs=[pl.BlockSpec((2, rows, q), lambda b, be, bv: (0, b, 0)),
                      w_spec(we_gate), w_spec(we_up), w_spec(we_down)],
            out_specs=pl.BlockSpec((2, rows, q), lambda b, be, bv: (0, b, 0)),
            scratch_shapes=[pltpu.VMEM((d, 2 * hid), BF16), pltpu.VMEM((hid, d), BF16)]),
        out_shape=jax.ShapeDtypeStruct(xs.shape, jnp.int32),
        compiler_params=_cparams("arbitrary"),
    )(blk_expert, blk_valid, xs, we_gate, we_up, we_down)


def _combine_kernel(x2_ref, mod_ref, y8_ref, g8_ref, o_ref):
    g8 = g8_ref[...]
    routed = None
    for k in range(TOP_K):
        term = g8[:, k:k + 1] * _load_planes(y8_ref[k, 0], y8_ref[k, 1])
        routed = term if routed is None else routed + term
    o_ref[0] = x2_ref[0] + mod_ref[0, 5:6, :] * routed


def _combine(x2, mod, y8, gate8):
    bsz, s, d = x2.shape
    tm = min(256, s)
    ns = s // tm
    return pl.pallas_call(
        _combine_kernel,
        grid=(bsz, ns),
        in_specs=[pl.BlockSpec((1, tm, d), lambda b, i: (b, i, 0)),
                  pl.BlockSpec((1, N_MOD, d), lambda b, i: (b, 0, 0)),
                  pl.BlockSpec((TOP_K, 2, tm, d // 4), lambda b, i: (0, 0, b * ns + i, 0)),
                  pl.BlockSpec((tm, TOP_K), lambda b, i: (b * ns + i, 0))],
        out_specs=pl.BlockSpec((1, tm, d), lambda b, i: (b, i, 0)),
        out_shape=jax.ShapeDtypeStruct((bsz, s, d), F32),
        compiler_params=_cparams("parallel", "parallel"),
    )(x2, mod, y8, gate8)


def kernel(x, c, w_ada, b_ada, norm1_g, w_in, sb_q_norm_g, sb_k_norm_g, hg_lb_logits, hg_norm_g,
           w_branch_sb, w_branch_hg, w_out, norm2_g, w_router, router_bias, w_e_gate, w_e_up,
           w_e_down, w_s_gate, w_s_up, w_s_down):
    bsz, s, d = x.shape
    depth = w_ada.shape[0]
    n_gate_cols = 2 * d
    qkv_col0 = n_gate_cols // LANES
    hg_col0 = qkv_col0 + 3 * SB_WIDTH // LANES
    for l in range(depth):
        n_mix = 3 * SB_WIDTH + 4 * HG_WIDTH
        w_in_l = jnp.concatenate([w_in[l][:, n_mix:], w_in[l][:, :n_mix]], axis=1).astype(BF16)
        wr_t = w_router[l].T
        wr_hi = wr_t.astype(BF16)
        wr_lo = (wr_t - wr_hi.astype(F32)).astype(BF16)
        ws_gu = jnp.concatenate([w_s_gate[l], w_s_up[l]], axis=1).astype(BF16)

        mod = _ada(c, w_ada[l], b_ada[l]).reshape(bsz, N_MOD, d)
        proj = _inproj(x, mod, norm1_g[l], w_in_l)
        o_sb = _sb_attention(proj, sb_q_norm_g[l], sb_k_norm_g[l], qkv_col0)
        o_hg = _hgrn(proj, hg_lb_logits, hg_norm_g[l], hg_col0, l)
        x2, h2p, logits_t = _merge(x, o_sb, o_hg, proj, mod, norm2_g[l],
                                   w_branch_sb[l].astype(BF16), w_branch_hg[l].astype(BF16),
                                   w_out[l].astype(BF16), wr_hi, wr_lo, ws_gu,
                                   w_s_down[l].astype(BF16))
        t = bsz * s
        n_blocks = -(-(t * TOP_K + N_EXPERTS * (DISPATCH_ROWS - 1)) // DISPATCH_ROWS)
        nbp = -(-n_blocks // LANES) * LANES
        gates_t, rank_t, counts = _route(logits_t, router_bias[l])
        slot8, gate8, blk = _slots(gates_t, rank_t, counts, DISPATCH_ROWS, nbp)
        n_slots = n_blocks * DISPATCH_ROWS
        plane_off = jnp.array([0, n_slots], jnp.int32)[None, :, None]
        row_idx = (slot8[:, None, :] + plane_off).reshape(1, TOP_K * 2 * t)
        q = d // 4
        xs = _sc_scatter_rows(h2p.reshape(2 * t, q), row_idx, 2 * n_slots).reshape(2, n_slots, q)
        ys = _gmm(blk[0], blk[1], xs, w_e_gate[l], w_e_up[l], w_e_down[l], DISPATCH_ROWS)
        y8 = _sc_gather_rows(ys.reshape(2 * n_slots, q), row_idx).reshape(TOP_K, 2, t, q)
        x = _combine(x2, mod, y8, gate8)
    return x
```

```python
import functools

import jax
import jax.numpy as jnp
from jax import lax
from jax.experimental import pallas as pl
from jax.experimental.pallas import tpu as pltpu
from jax.experimental.pallas import tpu_sc as plsc

F32 = jnp.float32
BF16 = jnp.bfloat16

SB_HEADS = 8
SB_HEAD_DIM = 64
SB_WIDTH = SB_HEADS * SB_HEAD_DIM
HG_HEADS = 4
HG_HEAD_DIM = 128
HG_WIDTH = HG_HEADS * HG_HEAD_DIM
HG_CHUNK = 64
N_EXPERTS = 64
TOP_K = 8
N_GROUPS = 8
TOPK_GROUPS = 4
GROUP_SIZE = N_EXPERTS // N_GROUPS
ROUTED_SCALE = 2.5
DISPATCH_ROWS = 1024
N_MOD = 6
EPS = 1e-6
LOG2_E = 1.4426950408889634
SB_DEAD_LOG2 = 160.0

LANES = 128
VMEM_LIMIT = 56 * 1024 * 1024


def _cparams(*sem):
    return pltpu.CompilerParams(dimension_semantics=sem, vmem_limit_bytes=VMEM_LIMIT)


def _silu(t):
    return t * jax.nn.sigmoid(t)


def _dot(a, b):
    return jnp.dot(a, b, preferred_element_type=F32)


def _dot_nt(a, b):
    return lax.dot_general(a, b, (((1,), (1,)), ((), ())), preferred_element_type=F32)


def _split_bf16(t):
    hi = t.astype(BF16)
    lo = (t - hi.astype(F32)).astype(BF16)
    return hi, lo


def _ada_kernel(c_ref, w_ref, b_ref, o_ref):
    cond = _silu(c_ref[...])
    o_ref[...] = _dot(cond, w_ref[...]) + b_ref[...]


def _ada(c, w, b):
    bsz, d = c.shape
    n = w.shape[1]
    tn = 1024
    return pl.pallas_call(
        _ada_kernel,
        grid=(n // tn,),
        in_specs=[pl.BlockSpec((bsz, d), lambda j: (0, 0)),
                  pl.BlockSpec((d, tn), lambda j: (0, j)),
                  pl.BlockSpec((1, tn), lambda j: (0, j))],
        out_specs=pl.BlockSpec((bsz, tn), lambda j: (0, j)),
        out_shape=jax.ShapeDtypeStruct((bsz, n), F32),
        compiler_params=_cparams("parallel"),
    )(c, w, b.reshape(1, n))


def _modulated_norm(x, g, shift, scale):
    y = x * lax.rsqrt(jnp.mean(x * x, axis=-1, keepdims=True) + EPS) * g
    return y * (1.0 + scale) + shift


INPROJ_COLS = 512


def _inproj_kernel(x_ref, mod_ref, g_ref, w_ref, o_ref):
    h = _modulated_norm(x_ref[0], g_ref[...], mod_ref[0, 0:1, :], mod_ref[0, 1:2, :]).astype(BF16)
    for j in range(w_ref.shape[1] // INPROJ_COLS):
        cols = slice(j * INPROJ_COLS, (j + 1) * INPROJ_COLS)
        o_ref[0, :, cols] = _dot(h, w_ref[:, cols]).astype(o_ref.dtype)


def _inproj(x, mod, g, w):
    bsz, s, d = x.shape
    n = w.shape[1]
    tm = min(512, s)
    return pl.pallas_call(
        _inproj_kernel,
        grid=(bsz, s // tm),
        in_specs=[pl.BlockSpec((1, tm, d), lambda b, i: (b, i, 0)),
                  pl.BlockSpec((1, N_MOD, d), lambda b, i: (b, 0, 0)),
                  pl.BlockSpec((1, d), lambda b, i: (0, 0)),
                  pl.BlockSpec((d, n), lambda b, i: (0, 0))],
        out_specs=pl.BlockSpec((1, tm, n), lambda b, i: (b, i, 0)),
        out_shape=jax.ShapeDtypeStruct((bsz, s, n), BF16),
        compiler_params=_cparams("parallel", "parallel"),
    )(x, mod, g.reshape(1, d), w)


def _pair_norm(t, g, lo_half):
    sq = t * t
    s_lo = jnp.sum(jnp.where(lo_half, sq, 0.0), axis=-1, keepdims=True)
    s_hi = jnp.sum(jnp.where(lo_half, 0.0, sq), axis=-1, keepdims=True)
    ms = jnp.where(lo_half, s_lo, s_hi) * (1.0 / SB_HEAD_DIM)
    return t * lax.rsqrt(ms + EPS) * g


def _neg_abs(t):
    bits = lax.bitcast_convert_type(t, jnp.uint32) | jnp.uint32(0x80000000)
    return lax.bitcast_convert_type(bits, F32)


def _split_trunc(t):
    bits = lax.bitcast_convert_type(t, jnp.uint32) & jnp.uint32(0xFFFF0000)
    hi = lax.bitcast_convert_type(bits, F32)
    return hi.astype(BF16), (t - hi).astype(BF16)


def _sb_kernel(q_ref, k_ref, v_ref, qg_ref, kg_ref, o_ref, knt_ref, va_ref, vb_ref, *, tq):
    qi = pl.program_id(2)
    s = k_ref.shape[1]
    lane = lax.broadcasted_iota(jnp.int32, (1, LANES), 1)
    lo_half = lane < SB_HEAD_DIM

    tk = tq // 2

    @pl.when(qi == 0)
    def _():
        def prep_block(j, c):
            rows = pl.ds(pl.multiple_of(j * tk, tk), tk)
            kb = _pair_norm(k_ref[0, rows, :].astype(F32), kg_ref[...], lo_half)
            knt_ref[:, rows] = kb.T.astype(BF16)
            vb = v_ref[0, rows, :]
            va_ref[rows, :] = jnp.where(lo_half, vb, jnp.zeros_like(vb))
            vb_ref[rows, :] = jnp.where(lo_half, jnp.zeros_like(vb), vb)
            return c
        lax.fori_loop(0, s // tk, prep_block, 0)

    scale = SB_HEAD_DIM ** -0.5 * LOG2_E
    q = _pair_norm(q_ref[0].astype(F32), qg_ref[...], lo_half) * scale
    q_heads = (jnp.where(lo_half, q, 0.0).astype(BF16), jnp.where(lo_half, 0.0, q).astype(BF16))
    v_heads = (va_ref, vb_ref)

    row = lax.broadcasted_iota(jnp.int32, (tk, tk), 0)
    col = lax.broadcasted_iota(jnp.int32, (tk, tk), 1)
    strict = col < row
    tri = (row >= col).astype(BF16)
    tri2 = jnp.concatenate([tri, tri], axis=0)

    def sweep(streams):
        cols = [[pl.ds(pl.multiple_of(j * tk, tk), tk) for j, _ in blocks] for _, _, _, blocks in streams]
        z = [[[_dot(qh[h], knt_ref[:, c]) for c in cols[i]] for h in range(2)]
             for i, (qh, _, _, _) in enumerate(streams)]
        cs = []
        for i, (_, _, _, blocks) in enumerate(streams):
            cs.append([[None] * len(blocks) for _ in range(2)])
            for h in range(2):
                for b, (_, masked) in enumerate(blocks):
                    zb = z[i][h][b]
                    sp = jnp.maximum(zb, 0.0) + jnp.log2(1.0 + jnp.exp2(_neg_abs(zb)))
                    if masked:
                        sp = jnp.where(strict, sp, 0.0)
                    cs[i][h][b] = _dot(jnp.concatenate(_split_trunc(sp), axis=1), tri2)
        results = []
        for i, (_, runs, acc, blocks) in enumerate(streams):
            runs = list(runs)
            for h in range(2):
                for b, (_, masked) in enumerate(blocks):
                    a = jnp.exp2(z[i][h][b] - cs[i][h][b] - runs[h])
                    if masked:
                        a = jnp.where(strict, a, 0.0)
                    acc = acc + _dot(a.astype(BF16), v_heads[h][cols[i][b], :])
                    runs[h] = runs[h] + cs[i][h][b][:, 0:1]
            results.append((tuple(runs), acc))
        return results

    zero_run = jnp.zeros((tk, 1), F32)
    zero_acc = jnp.zeros((tk, LANES), F32)
    left, right = 2 * qi, 2 * qi + 1
    (runs_t, acc_t), (runs_b, acc_b) = sweep([
        (tuple(qh[:tk] for qh in q_heads), (zero_run, zero_run), zero_acc, [(left, True)]),
        (tuple(qh[tk:] for qh in q_heads), (zero_run, zero_run), zero_acc,
         [(right, True), (left, False)])])
    runs = tuple(jnp.concatenate([runs_t[h], runs_b[h]], axis=0) for h in range(2))
    acc = jnp.concatenate([acc_t, acc_b], axis=0)

    def min_run(rs):
        return jnp.min(jnp.minimum(rs[0], rs[1]))

    def alive(carry):
        j, _, _, low = carry
        return jnp.logical_and(j >= 0, low < SB_DEAD_LOG2)

    def earlier_block(carry):
        j, rs, ac, _ = carry
        rs, ac = sweep([(q_heads, rs, ac, [(j, False)])])[0]
        return j - 1, rs, ac, min_run(rs)

    _, _, acc, _ = lax.while_loop(alive, earlier_block, (left - 1, runs, acc, min_run(runs)))
    o_ref[0] = acc.astype(o_ref.dtype)


def _sb_attention(proj, qg, kg, col0):
    bsz, s, _ = proj.shape
    tq = min(512, s)
    npair = SB_WIDTH // LANES
    qg2 = jnp.tile(qg.reshape(1, SB_HEAD_DIM), (1, 2))
    kg2 = jnp.tile(kg.reshape(1, SB_HEAD_DIM), (1, 2))
    return pl.pallas_call(
        functools.partial(_sb_kernel, tq=tq),
        grid=(bsz, npair, s // tq),
        in_specs=[pl.BlockSpec((1, tq, LANES), lambda b, h, i: (b, i, col0 + h)),
                  pl.BlockSpec((1, s, LANES), lambda b, h, i: (b, 0, col0 + npair + h)),
                  pl.BlockSpec((1, s, LANES), lambda b, h, i: (b, 0, col0 + 2 * npair + h)),
                  pl.BlockSpec((1, LANES), lambda b, h, i: (0, 0)),
                  pl.BlockSpec((1, LANES), lambda b, h, i: (0, 0))],
        out_specs=pl.BlockSpec((1, tq, LANES), lambda b, h, i: (b, i, h)),
        out_shape=jax.ShapeDtypeStruct((bsz, s, SB_WIDTH), BF16),
        scratch_shapes=[pltpu.VMEM((LANES, s), BF16),
                        pltpu.VMEM((s, LANES), BF16),
                        pltpu.VMEM((s, LANES), BF16)],
        compiler_params=_cparams("parallel", "parallel", "arbitrary"),
    )(proj, proj, proj, qg2, kg2)


HG_CHUNKS_PER_STEP = 4


def _hgrn_kernel(f_ref, i_ref, q_ref, g_ref, lbl_ref, ng_ref, o_ref, st_ref, *, layer):
    ts = f_ref.shape[1]
    c = HG_CHUNK
    dh = HG_HEAD_DIM

    @pl.when(pl.program_id(1) == 0)
    def _():
        st_ref[...] = jnp.zeros_like(st_ref)

    lg = lbl_ref[...]
    e = jnp.exp(lg - jnp.max(lg, axis=0, keepdims=True))
    lb = jnp.sum(e[:layer + 1], axis=0, keepdims=True) / jnp.sum(e, axis=0, keepdims=True)

    row = lax.broadcasted_iota(jnp.int32, (c, c), 0)
    col = lax.broadcasted_iota(jnp.int32, (c, c), 1)
    causal = col <= row
    lower = causal.astype(BF16)
    width = f_ref.shape[2]

    def body(it, carry):
        states = [st_ref[h] for h in range(HG_HEADS)]
        for u in range(HG_CHUNKS_PER_STEP):
            rows = pl.ds(pl.multiple_of((it * HG_CHUNKS_PER_STEP + u) * c, c), c)
            forget = lb + (1.0 - lb) * jax.nn.sigmoid(f_ref[0, rows, :].astype(F32))
            kc = 1.0 - forget
            hi, lo = _split_trunc(jnp.log(forget))
            cum = _dot(lower, jnp.concatenate([hi, lo], axis=1))
            gc = cum[:, :width] + cum[:, width:]
            g_mid = gc[c // 2 - 1:c // 2, :]
            g_last = gc[c - 1:c, :]
            qe = _silu(q_ref[0, rows, :].astype(F32)) * jnp.exp(gc - g_mid)
            ke = kc * jnp.exp(g_mid - gc)
            qg = (qe * jnp.exp(g_mid)).astype(BF16)
            kd = (ke * jnp.exp(g_last - g_mid)).astype(BF16)
            qe = qe.astype(BF16)
            ke = ke.astype(BF16)
            decay = jnp.exp(g_last)
            v = i_ref[0, rows, :]
            v_t = v.astype(F32).T.astype(BF16)
            outs = []
            for h in range(HG_HEADS):
                sl = slice(h * dh, (h + 1) * dh)
                scores = _dot_nt(qe[:, sl], ke[:, sl])
                o = (_dot(jnp.where(causal, scores, 0.0).astype(BF16), v[:, sl])
                     + _dot_nt(qg[:, sl], states[h].astype(BF16)))
                outs.append(o * lax.rsqrt(jnp.mean(o * o, axis=-1, keepdims=True) + EPS))
                states[h] = states[h] * decay[:, sl] + _dot(v_t[sl, :], kd[:, sl])
            y = jnp.concatenate(outs, axis=1) * ng_ref[...] * _silu(g_ref[0, rows, :].astype(F32))
            o_ref[0, rows, :] = y.astype(o_ref.dtype)
        for h in range(HG_HEADS):
            st_ref[h] = states[h]
        return carry

    lax.fori_loop(0, ts // (c * HG_CHUNKS_PER_STEP), body, 0)


def _hgrn(proj, lb_logits, norm_g, col0, layer):
    bsz, s, _ = proj.shape
    nl = lb_logits.shape[0]
    ts = min(1024, s)
    blk0 = col0 * LANES // HG_WIDTH
    spec = lambda off: pl.BlockSpec((1, ts, HG_WIDTH), lambda b, i: (b, i, blk0 + off))
    return pl.pallas_call(
        functools.partial(_hgrn_kernel, layer=layer),
        grid=(bsz, s // ts),
        in_specs=[spec(0), spec(1), spec(2), spec(3),
                  pl.BlockSpec((nl, HG_WIDTH), lambda b, i: (0, 0)),
                  pl.BlockSpec((1, HG_WIDTH), lambda b, i: (0, 0))],
        out_specs=pl.BlockSpec((1, ts, HG_WIDTH), lambda b, i: (b, i, 0)),
        out_shape=jax.ShapeDtypeStruct((bsz, s, HG_WIDTH), BF16),
        scratch_shapes=[pltpu.VMEM((HG_HEADS, HG_HEAD_DIM, HG_HEAD_DIM), F32)],
        compiler_params=_cparams("parallel", "arbitrary"),
    )(proj, proj, proj, proj, lb_logits, jnp.tile(norm_g.reshape(1, HG_HEAD_DIM), (1, HG_HEADS)))


def _pack_halves(t):
    bits = lax.bitcast_convert_type(t.astype(BF16).astype(F32), jnp.uint32)
    w = t.shape[1] // 2
    return lax.bitcast_convert_type(bits[:, :w] | (bits[:, w:] >> 16), jnp.int32)


def _unpack_halves(p):
    u = lax.bitcast_convert_type(p, jnp.uint32)
    return (lax.bitcast_convert_type(u & jnp.uint32(0xFFFF0000), F32),
            lax.bitcast_convert_type(u << 16, F32))


def _store_planes(ref, rows, t):
    p = _pack_halves(t)
    q = p.shape[1] // 2
    ref[0, rows, :] = p[:, :q]
    ref[1, rows, :] = p[:, q:]


def _load_planes(p0, p1):
    a0, b0 = _unpack_halves(p0)
    a1, b1 = _unpack_halves(p1)
    return jnp.concatenate([a0, a1, b0, b1], axis=1)


def _swiglu_hidden(h, w_gu):
    gu = _dot(h, w_gu)
    hid = w_gu.shape[1] // 2
    return _silu(gu[:, :hid]) * gu[:, hid:]


def _merge_kernel(x_ref, osb_ref, ohg_ref, gsb_ref, ghg_ref, mod_ref, g2_ref, wsb_ref, whg_ref,
                  wout_ref, wrh_ref, wrl_ref, wsgu_ref, wsd_ref, x2_ref, h2p_ref, lg_ref):
    m_sb = _dot(osb_ref[0], wsb_ref[...])
    m_hg = _dot(ohg_ref[0], whg_ref[...])
    merged = (jax.nn.sigmoid(gsb_ref[0].astype(F32)) * m_sb
              + jax.nn.sigmoid(ghg_ref[0].astype(F32)) * m_hg)
    x1 = x_ref[0] + mod_ref[0, 2:3, :] * _dot(merged.astype(BF16), wout_ref[...])
    h2 = _modulated_norm(x1, g2_ref[...], mod_ref[0, 3:4, :], mod_ref[0, 4:5, :])
    hi, lo = _split_bf16(h2)
    _store_planes(h2p_ref, slice(None), h2)
    lg_ref[...] = _dot_nt(wrh_ref[...], hi) + _dot_nt(wrh_ref[...], lo) + _dot_nt(wrl_ref[...], hi)
    shared = _dot(_swiglu_hidden(hi, wsgu_ref[...]).astype(BF16), wsd_ref[...])
    x2_ref[0] = x1 + mod_ref[0, 5:6, :] * shared


def _merge(x, o_sb, o_hg, proj, mod, g2, w_sb, w_hg, w_out, wr_hi, wr_lo, ws_gu, ws_d):
    bsz, s, d = x.shape
    tm = min(512, s)
    ns = s // tm
    full = lambda shape: pl.BlockSpec(shape, lambda b, i: (0,) * len(shape))
    return pl.pallas_call(
        _merge_kernel,
        grid=(bsz, ns),
        in_specs=[pl.BlockSpec((1, tm, d), lambda b, i: (b, i, 0)),
                  pl.BlockSpec((1, tm, SB_WIDTH), lambda b, i: (b, i, 0)),
                  pl.BlockSpec((1, tm, HG_WIDTH), lambda b, i: (b, i, 0)),
                  pl.BlockSpec((1, tm, d), lambda b, i: (b, i, 0)),
                  pl.BlockSpec((1, tm, d), lambda b, i: (b, i, 1)),
                  pl.BlockSpec((1, N_MOD, d), lambda b, i: (b, 0, 0)),
                  full((1, d)), full(w_sb.shape), full(w_hg.shape), full(w_out.shape),
                  full(wr_hi.shape), full(wr_lo.shape), full(ws_gu.shape), full(ws_d.shape)],
        out_specs=[pl.BlockSpec((1, tm, d), lambda b, i: (b, i, 0)),
                   pl.BlockSpec((2, tm, d // 4), lambda b, i: (0, b * ns + i, 0)),
                   pl.BlockSpec((N_EXPERTS, tm), lambda b, i: (0, b * ns + i))],
        out_shape=[jax.ShapeDtypeStruct((bsz, s, d), F32),
                   jax.ShapeDtypeStruct((2, bsz * s, d // 4), jnp.int32),
                   jax.ShapeDtypeStruct((N_EXPERTS, bsz * s), F32)],
        compiler_params=_cparams("parallel", "parallel"),
    )(x, o_sb, o_hg, proj, proj, mod, g2.reshape(1, d), w_sb, w_hg, w_out, wr_hi, wr_lo,
      ws_gu, ws_d)


def _first_argmax(vals, idx, sentinel):
    m = jnp.max(vals, axis=0, keepdims=True)
    first = jnp.min(jnp.where(vals == m, idx, sentinel), axis=0, keepdims=True)
    return m, first


def _route_kernel(lg_ref, bias_ref, gates_ref, rank_ref, cnt_ref, run_ref):
    tn = lg_ref.shape[1]

    @pl.when(pl.program_id(0) == 0)
    def _():
        run_ref[...] = jnp.zeros_like(run_ref)

    neg = -jnp.inf
    scores = jax.nn.sigmoid(lg_ref[...])
    choice = scores + bias_ref[...]

    gidx = lax.broadcasted_iota(jnp.int32, (GROUP_SIZE, tn), 0)
    group_rows = []
    for g in range(N_GROUPS):
        cg = choice[g * GROUP_SIZE:(g + 1) * GROUP_SIZE, :]
        m1, i1 = _first_argmax(cg, gidx, GROUP_SIZE)
        m2 = jnp.max(jnp.where(gidx == i1, neg, cg), axis=0, keepdims=True)
        group_rows.append(m1 + m2)
    work = jnp.concatenate(group_rows, axis=0)
    ggi = lax.broadcasted_iota(jnp.int32, (N_GROUPS, tn), 0)
    gmask = jnp.zeros((N_GROUPS, tn), F32)
    for _ in range(TOPK_GROUPS):
        _, first = _first_argmax(work, ggi, N_GROUPS)
        pick = ggi == first
        gmask = jnp.where(pick, 1.0, gmask)
        work = jnp.where(pick, neg, work)

    masked = jnp.concatenate(
        [jnp.where(gmask[g:g + 1, :] > 0.0, choice[g * GROUP_SIZE:(g + 1) * GROUP_SIZE, :], neg)
         for g in range(N_GROUPS)], axis=0)
    eidx = lax.broadcasted_iota(jnp.int32, (N_EXPERTS, tn), 0)
    sel = jnp.zeros((N_EXPERTS, tn), F32)
    for _ in range(TOP_K):
        _, first = _first_argmax(masked, eidx, N_EXPERTS)
        pick = eidx == first
        sel = jnp.where(pick, 1.0, sel)
        masked = jnp.where(pick, neg, masked)

    chosen = jnp.where(sel > 0.0, scores, 0.0)
    gates_ref[...] = chosen / jnp.sum(chosen, axis=0, keepdims=True) * ROUTED_SCALE

    r = lax.broadcasted_iota(jnp.int32, (tn, tn), 0)
    c = lax.broadcasted_iota(jnp.int32, (tn, tn), 1)
    local = _dot(sel.astype(BF16), (r < c).astype(BF16))
    run = run_ref[:, 0:1]
    rank_ref[...] = jnp.where(sel > 0.0, run + local, -1.0)
    total = run + jnp.sum(sel, axis=1, keepdims=True)
    run_ref[...] = jnp.broadcast_to(total, run_ref.shape)
    cnt_ref[...] = jnp.broadcast_to(total, cnt_ref.shape)


def _route(logits_t, bias):
    e, t = logits_t.shape
    tn = min(1024, t)
    return pl.pallas_call(
        _route_kernel,
        grid=(t // tn,),
        in_specs=[pl.BlockSpec((e, tn), lambda i: (0, i)),
                  pl.BlockSpec((e, 1), lambda i: (0, 0))],
        out_specs=[pl.BlockSpec((e, tn), lambda i: (0, i)),
                   pl.BlockSpec((e, tn), lambda i: (0, i)),
                   pl.BlockSpec((e, LANES), lambda i: (0, 0))],
        out_shape=[jax.ShapeDtypeStruct((e, t), F32),
                   jax.ShapeDtypeStruct((e, t), F32),
                   jax.ShapeDtypeStruct((e, LANES), F32)],
        scratch_shapes=[pltpu.VMEM((e, LANES), F32)],
        compiler_params=_cparams("arbitrary"),
    )(logits_t, bias.reshape(e, 1))


def _slots_kernel(gates_ref, rank_ref, cnt_ref, slot_ref, gate8_ref, blk_ref, *, rows):
    ne, tn = gates_ref.shape
    cnt = cnt_ref[...]
    nblk = jnp.floor((cnt + (rows - 1.0)) * (1.0 / rows))
    er = lax.broadcasted_iota(jnp.int32, (ne, ne), 0)
    ec = lax.broadcasted_iota(jnp.int32, (ne, ne), 1)
    lower = (ec < er).astype(BF16)
    pad_start = _dot(lower, nblk.astype(BF16))[:, 0:1] * rows
    pad_end = pad_start + nblk[:, 0:1] * rows

    rank = rank_ref[...]
    sel = rank >= 0.0
    slot_e = pad_start + rank
    kidx = _dot(lower, sel.astype(BF16))
    gates = gates_ref[...]
    slot_rows, gate_rows = [], []
    for k in range(TOP_K):
        m = jnp.logical_and(sel, kidx == k)
        slot_rows.append(jnp.sum(jnp.where(m, slot_e, 0.0), axis=0, keepdims=True))
        gate_rows.append(jnp.sum(jnp.where(m, gates, 0.0), axis=0, keepdims=True))
    slot_ref[...] = jnp.concatenate(slot_rows, axis=0).astype(jnp.int32)
    gate8_ref[...] = jnp.concatenate(gate_rows, axis=0).T

    nbp = blk_ref.shape[1]
    bstart = lax.broadcasted_iota(jnp.int32, (1, nbp), 1).astype(F32) * rows
    e_of = jnp.sum((pad_end <= bstart).astype(F32), axis=0, keepdims=True)
    e_of = jnp.minimum(e_of, ne - 1.0)
    eidx = lax.broadcasted_iota(jnp.int32, (ne, nbp), 0).astype(F32)
    valid_e = jnp.clip(cnt[:, 0:1] - (bstart - pad_start), 0.0, rows)
    valid = jnp.sum(jnp.where(eidx == e_of, valid_e, 0.0), axis=0, keepdims=True)
    blk_ref[...] = jnp.concatenate(
        [e_of, valid, jnp.zeros((blk_ref.shape[0] - 2, nbp), F32)], axis=0).astype(jnp.int32)


def _slots(gates_t, rank_t, counts, rows, nbp):
    e, t = gates_t.shape
    assert t // rows <= 256
    tn = min(1024, t)
    return pl.pallas_call(
        functools.partial(_slots_kernel, rows=rows),
        grid=(t // tn,),
        in_specs=[pl.BlockSpec((e, tn), lambda i: (0, i)),
                  pl.BlockSpec((e, tn), lambda i: (0, i)),
                  pl.BlockSpec((e, LANES), lambda i: (0, 0))],
        out_specs=[pl.BlockSpec((TOP_K, tn), lambda i: (0, i)),
                   pl.BlockSpec((tn, TOP_K), lambda i: (i, 0)),
                   pl.BlockSpec((8, nbp), lambda i: (0, 0))],
        out_shape=[jax.ShapeDtypeStruct((TOP_K, t), jnp.int32),
                   jax.ShapeDtypeStruct((t, TOP_K), F32),
                   jax.ShapeDtypeStruct((8, nbp), jnp.int32)],
        compiler_params=_cparams("arbitrary"),
    )(gates_t, rank_t, counts)


SC_WINDOW = 128


def _sc_mesh():
    return plsc.VectorSubcoreMesh(core_axis_name="core", subcore_axis_name="subcore")


def _sc_gather_rows(table, idx):
    n = idx.shape[1]
    w = table.shape[1]

    @pl.kernel(out_type=jax.ShapeDtypeStruct((n, w), table.dtype), mesh=_sc_mesh())
    def gather(t_hbm, i_hbm, o_hbm):
        def body(i_vmem, o_vmem):
            pltpu.sync_copy(t_hbm.at[i_vmem.at[0]], o_vmem)

        pltpu.emit_pipeline(
            body, grid=(n // SC_WINDOW,),
            in_specs=[pl.BlockSpec((1, SC_WINDOW), lambda i: (0, i))],
            out_specs=[pl.BlockSpec((SC_WINDOW, w), lambda i: (i, 0))],
            core_axis_name=("core", "subcore"),
            dimension_semantics=(pltpu.PARALLEL,),
        )(i_hbm, o_hbm)

    return gather(table, idx)


def _sc_scatter_rows(rows, idx, n_out):
    fan, m = idx.shape
    w = rows.shape[1]

    @pl.kernel(out_type=jax.ShapeDtypeStruct((n_out, w), rows.dtype), mesh=_sc_mesh())
    def scatter(r_hbm, i_hbm, o_hbm):
        def body(r_vmem, i_vmem):
            for k in range(fan):
                pltpu.sync_copy(r_vmem, o_hbm.at[i_vmem.at[k]])

        pltpu.emit_pipeline(
            body, grid=(m // SC_WINDOW,),
            in_specs=[pl.BlockSpec((SC_WINDOW, w), lambda i: (i, 0)),
                      pl.BlockSpec((fan, SC_WINDOW), lambda i: (0, i))],
            out_specs=[],
            core_axis_name=("core", "subcore"),
            dimension_semantics=(pltpu.PARALLEL,),
        )(r_hbm, i_hbm)

    return scatter(rows, idx)


def _gmm_kernel(be_ref, bv_ref, xs_ref, wg_ref, wu_ref, wd_ref, ys_ref, wgu_s, wd_s):
    b = pl.program_id(0)
    valid = bv_ref[b]
    hid = wg_ref.shape[2]

    @pl.when(jnp.logical_or(b == 0, be_ref[b] != be_ref[jnp.maximum(b - 1, 0)]))
    def _():
        wgu_s[:, :hid] = wg_ref[0].astype(BF16)
        wgu_s[:, hid:] = wu_ref[0].astype(BF16)
        wd_s[...] = wd_ref[0].astype(BF16)

    @pl.when(valid > 0)
    def _():
        x = _load_planes(xs_ref[0], xs_ref[1]).astype(BF16)
        y = _dot(_swiglu_hidden(x, wgu_s[...]).astype(BF16), wd_s[...])
        row = lax.broadcasted_iota(jnp.int32, (y.shape[0], 1), 0)
        _store_planes(ys_ref, slice(None), jnp.where(row < valid, y, 0.0))

    @pl.when(valid <= 0)
    def _():
        ys_ref[...] = jnp.zeros_like(ys_ref)


def _gmm(blk_expert, blk_valid, xs, we_gate, we_up, we_down, rows):
    _, n_slots, q = xs.shape
    _, d, hid = we_gate.shape
    w_spec = lambda w: pl.BlockSpec((1,) + w.shape[1:], lambda b, be, bv: (be[b], 0, 0))
    return pl.pallas_call(
        _gmm_kernel,
        grid_spec=pltpu.PrefetchScalarGridSpec(
            num_scalar_prefetch=2,
            grid=(n_slots // rows,),
            in_specs=[pl.BlockSpec((2, rows, q), lambda b, be, bv: (0, b, 0)),
                      w_spec(we_gate), w_spec(we_up), w_spec(we_down)],
            out_specs=pl.BlockSpec((2, rows, q), lambda b, be, bv: (0, b, 0)),
            scratch_shapes=[pltpu.VMEM((d, 2 * hid), BF16), pltpu.VMEM((hid, d), BF16)]),
        out_shape=jax.ShapeDtypeStruct(xs.shape, jnp.int32),
        compiler_params=_cparams("arbitrary"),
    )(blk_expert, blk_valid, xs, we_gate, we_up, we_down)


def _combine_kernel(x2_ref, mod_ref, y8_ref, g8_ref, o_ref):
    g8 = g8_ref[...]
    routed = None
    for k in range(TOP_K):
        term = g8[:, k:k + 1] * _load_planes(y8_ref[k, 0], y8_ref[k, 1])
        routed = term if routed is None else routed + term
    o_ref[0] = x2_ref[0] + mod_ref[0, 5:6, :] * routed


def _combine(x2, mod, y8, gate8):
    bsz, s, d = x2.shape
    tm = min(256, s)
    ns = s // tm
    return pl.pallas_call(
        _combine_kernel,
        grid=(bsz, ns),
        in_specs=[pl.BlockSpec((1, tm, d), lambda b, i: (b, i, 0)),
                  pl.BlockSpec((1, N_MOD, d), lambda b, i: (b, 0, 0)),
                  pl.BlockSpec((TOP_K, 2, tm, d // 4), lambda b, i: (0, 0, b * ns + i, 0)),
                  pl.BlockSpec((tm, TOP_K), lambda b, i: (b * ns + i, 0))],
        out_specs=pl.BlockSpec((1, tm, d), lambda b, i: (b, i, 0)),
        out_shape=jax.ShapeDtypeStruct((bsz, s, d), F32),
        compiler_params=_cparams("parallel", "parallel"),
    )(x2, mod, y8, gate8)


def kernel(x, c, w_ada, b_ada, norm1_g, w_in, sb_q_norm_g, sb_k_norm_g, hg_lb_logits, hg_norm_g,
           w_branch_sb, w_branch_hg, w_out, norm2_g, w_router, router_bias, w_e_gate, w_e_up,
           w_e_down, w_s_gate, w_s_up, w_s_down):
    bsz, s, d = x.shape
    depth = w_ada.shape[0]
    n_gate_cols = 2 * d
    qkv_col0 = n_gate_cols // LANES
    hg_col0 = qkv_col0 + 3 * SB_WIDTH // LANES
    for l in range(depth):
        n_mix = 3 * SB_WIDTH + 4 * HG_WIDTH
        w_in_l = jnp.concatenate([w_in[l][:, n_mix:], w_in[l][:, :n_mix]], axis=1).astype(BF16)
        wr_t = w_router[l].T
        wr_hi = wr_t.astype(BF16)
        wr_lo = (wr_t - wr_hi.astype(F32)).astype(BF16)
        ws_gu = jnp.concatenate([w_s_gate[l], w_s_up[l]], axis=1).astype(BF16)

        mod = _ada(c, w_ada[l], b_ada[l]).reshape(bsz, N_MOD, d)
        proj = _inproj(x, mod, norm1_g[l], w_in_l)
        o_sb = _sb_attention(proj, sb_q_norm_g[l], sb_k_norm_g[l], qkv_col0)
        o_hg = _hgrn(proj, hg_lb_logits, hg_norm_g[l], hg_col0, l)
        x2, h2p, logits_t = _merge(x, o_sb, o_hg, proj, mod, norm2_g[l],
                                   w_branch_sb[l].astype(BF16), w_branch_hg[l].astype(BF16),
                                   w_out[l].astype(BF16), wr_hi, wr_lo, ws_gu,
                                   w_s_down[l].astype(BF16))
        t = bsz * s
        n_blocks = -(-(t * TOP_K + N_EXPERTS * (DISPATCH_ROWS - 1)) // DISPATCH_ROWS)
        nbp = -(-n_blocks // LANES) * LANES
        gates_t, rank_t, counts = _route(logits_t, router_bias[l])
        slot8, gate8, blk = _slots(gates_t, rank_t, counts, DISPATCH_ROWS, nbp)
        n_slots = n_blocks * DISPATCH_ROWS
        plane_off = jnp.array([0, n_slots], jnp.int32)[None, :, None]
        row_idx = (slot8[:, None, :] + plane_off).reshape(TOP_K, 2 * t)
        q = d // 4
        xs = _sc_scatter_rows(h2p.reshape(2 * t, q), row_idx, 2 * n_slots).reshape(2, n_slots, q)
        row_idx = row_idx.reshape(1, TOP_K * 2 * t)
        ys = _gmm(blk[0], blk[1], xs, w_e_gate[l], w_e_up[l], w_e_down[l], DISPATCH_ROWS)
        y8 = _sc_gather_rows(ys.reshape(2 * n_slots, q), row_idx).reshape(TOP_K, 2, t, q)
        x = _combine(x2, mod, y8, gate8)
    return x
```

```python
import functools

import jax
import jax.numpy as jnp
from jax import lax
from jax.experimental import pallas as pl
from jax.experimental.pallas import tpu as pltpu
from jax.experimental.pallas import tpu_sc as plsc

F32 = jnp.float32
BF16 = jnp.bfloat16

SB_HEADS = 8
SB_HEAD_DIM = 64
SB_WIDTH = SB_HEADS * SB_HEAD_DIM
HG_HEADS = 4
HG_HEAD_DIM = 128
HG_WIDTH = HG_HEADS * HG_HEAD_DIM
HG_CHUNK = 64
N_EXPERTS = 64
TOP_K = 8
N_GROUPS = 8
TOPK_GROUPS = 4
GROUP_SIZE = N_EXPERTS // N_GROUPS
ROUTED_SCALE = 2.5
DISPATCH_ROWS = 1024
BATCH_PARTS = 2
N_MOD = 6
EPS = 1e-6
LOG2_E = 1.4426950408889634
SB_DEAD_LOG2 = 160.0

LANES = 128
VMEM_LIMIT = 56 * 1024 * 1024


def _cparams(*sem):
    return pltpu.CompilerParams(dimension_semantics=sem, vmem_limit_bytes=VMEM_LIMIT)


def _silu(t):
    return t * jax.nn.sigmoid(t)


def _dot(a, b):
    return jnp.dot(a, b, preferred_element_type=F32)


def _dot_nt(a, b):
    return lax.dot_general(a, b, (((1,), (1,)), ((), ())), preferred_element_type=F32)


def _split_bf16(t):
    hi = t.astype(BF16)
    lo = (t - hi.astype(F32)).astype(BF16)
    return hi, lo


def _ada_kernel(c_ref, w_ref, b_ref, o_ref):
    cond = _silu(c_ref[...])
    o_ref[...] = _dot(cond, w_ref[...]) + b_ref[...]


def _ada(c, w, b):
    bsz, d = c.shape
    n = w.shape[1]
    tn = 1024
    return pl.pallas_call(
        _ada_kernel,
        grid=(n // tn,),
        in_specs=[pl.BlockSpec((bsz, d), lambda j: (0, 0)),
                  pl.BlockSpec((d, tn), lambda j: (0, j)),
                  pl.BlockSpec((1, tn), lambda j: (0, j))],
        out_specs=pl.BlockSpec((bsz, tn), lambda j: (0, j)),
        out_shape=jax.ShapeDtypeStruct((bsz, n), F32),
        compiler_params=_cparams("parallel"),
    )(c, w, b.reshape(1, n))


def _modulated_norm(x, g, shift, scale):
    y = x * lax.rsqrt(jnp.mean(x * x, axis=-1, keepdims=True) + EPS) * g
    return y * (1.0 + scale) + shift


INPROJ_COLS = 512


def _inproj_kernel(x_ref, mod_ref, g_ref, w_ref, o_ref):
    h = _modulated_norm(x_ref[0], g_ref[...], mod_ref[0, 0:1, :], mod_ref[0, 1:2, :]).astype(BF16)
    for j in range(w_ref.shape[1] // INPROJ_COLS):
        cols = slice(j * INPROJ_COLS, (j + 1) * INPROJ_COLS)
        o_ref[0, :, cols] = _dot(h, w_ref[:, cols]).astype(o_ref.dtype)


def _inproj(x, mod, g, w, b0):
    _, s, d = x.shape
    bsz = mod.shape[0]
    n = w.shape[1]
    tm = min(512, s)
    return pl.pallas_call(
        _inproj_kernel,
        grid=(bsz, s // tm),
        in_specs=[pl.BlockSpec((1, tm, d), lambda b, i: (b0 + b, i, 0)),
                  pl.BlockSpec((1, N_MOD, d), lambda b, i: (b, 0, 0)),
                  pl.BlockSpec((1, d), lambda b, i: (0, 0)),
                  pl.BlockSpec((d, n), lambda b, i: (0, 0))],
        out_specs=pl.BlockSpec((1, tm, n), lambda b, i: (b, i, 0)),
        out_shape=jax.ShapeDtypeStruct((bsz, s, n), BF16),
        compiler_params=_cparams("parallel", "parallel"),
    )(x, mod, g.reshape(1, d), w)


def _pair_norm(t, g, lo_half):
    sq = t * t
    s_lo = jnp.sum(jnp.where(lo_half, sq, 0.0), axis=-1, keepdims=True)
    s_hi = jnp.sum(jnp.where(lo_half, 0.0, sq), axis=-1, keepdims=True)
    ms = jnp.where(lo_half, s_lo, s_hi) * (1.0 / SB_HEAD_DIM)
    return t * lax.rsqrt(ms + EPS) * g


def _neg_abs(t):
    bits = lax.bitcast_convert_type(t, jnp.uint32) | jnp.uint32(0x80000000)
    return lax.bitcast_convert_type(bits, F32)


def _split_trunc(t):
    bits = lax.bitcast_convert_type(t, jnp.uint32) & jnp.uint32(0xFFFF0000)
    hi = lax.bitcast_convert_type(bits, F32)
    return hi.astype(BF16), (t - hi).astype(BF16)


def _sb_kernel(q_ref, k_ref, v_ref, qg_ref, kg_ref, o_ref, knt_ref, va_ref, vb_ref, *, tq):
    qi = pl.program_id(2)
    s = k_ref.shape[1]
    lane = lax.broadcasted_iota(jnp.int32, (1, LANES), 1)
    lo_half = lane < SB_HEAD_DIM

    tk = tq // 2

    @pl.when(qi == 0)
    def _():
        def prep_block(j, c):
            rows = pl.ds(pl.multiple_of(j * tk, tk), tk)
            kb = _pair_norm(k_ref[0, rows, :].astype(F32), kg_ref[...], lo_half)
            knt_ref[:, rows] = kb.T.astype(BF16)
            vb = v_ref[0, rows, :]
            va_ref[rows, :] = jnp.where(lo_half, vb, jnp.zeros_like(vb))
            vb_ref[rows, :] = jnp.where(lo_half, jnp.zeros_like(vb), vb)
            return c
        lax.fori_loop(0, s // tk, prep_block, 0)

    scale = SB_HEAD_DIM ** -0.5 * LOG2_E
    q = _pair_norm(q_ref[0].astype(F32), qg_ref[...], lo_half) * scale
    q_heads = (jnp.where(lo_half, q, 0.0).astype(BF16), jnp.where(lo_half, 0.0, q).astype(BF16))
    v_heads = (va_ref, vb_ref)

    row = lax.broadcasted_iota(jnp.int32, (tk, tk), 0)
    col = lax.broadcasted_iota(jnp.int32, (tk, tk), 1)
    strict = col < row
    tri = (row >= col).astype(BF16)
    tri2 = jnp.concatenate([tri, tri], axis=0)

    def sweep(streams):
        cols = [[pl.ds(pl.multiple_of(j * tk, tk), tk) for j, _, _ in blocks]
                for _, _, _, blocks in streams]
        z = [[[_dot(qh[h], knt_ref[:, c]) for c in cols[i]] for h in range(2)]
             for i, (qh, _, _, _) in enumerate(streams)]
        cs = []
        for i, (_, _, _, blocks) in enumerate(streams):
            cs.append([[None] * len(blocks) for _ in range(2)])
            for h in range(2):
                for b, (_, masked, _) in enumerate(blocks):
                    zb = z[i][h][b]
                    sp = jnp.maximum(zb, 0.0) + jnp.log2(1.0 + jnp.exp2(_neg_abs(zb)))
                    if masked:
                        sp = jnp.where(strict, sp, 0.0)
                    cs[i][h][b] = _dot(jnp.concatenate(_split_trunc(sp), axis=1), tri2)
        results = []
        for i, (_, runs, acc, blocks) in enumerate(streams):
            runs = list(runs)
            for h in range(2):
                for b, (_, masked, gate) in enumerate(blocks):
                    a = jnp.exp2(z[i][h][b] - cs[i][h][b] - runs[h])
                    if masked:
                        a = jnp.where(strict, a, 0.0)
                    vb = v_heads[h][cols[i][b], :]
                    step = cs[i][h][b][:, 0:1]
                    if gate is not None:
                        vb = jnp.where(gate, vb, jnp.zeros_like(vb))
                        step = jnp.where(gate, step, 0.0)
                    acc = acc + _dot(a.astype(BF16), vb)
                    runs[h] = runs[h] + step
            results.append((tuple(runs), acc))
        return results

    zero_run = jnp.zeros((tk, 1), F32)
    zero_acc = jnp.zeros((tk, LANES), F32)
    left, right = 2 * qi, 2 * qi + 1
    q_top = tuple(qh[:tk] for qh in q_heads)
    q_bot = tuple(qh[tk:] for qh in q_heads)
    top, bot = sweep([
        (q_top, (zero_run, zero_run), zero_acc,
         [(left, True, None), (jnp.maximum(left - 1, 0), False, qi > 0)]),
        (q_bot, (zero_run, zero_run), zero_acc, [(right, True, None), (left, False, None)])])

    def min_run(rt, rb):
        return jnp.min(jnp.minimum(jnp.minimum(rt[0], rt[1]), jnp.minimum(rb[0], rb[1])))

    def alive(carry):
        j, _, _, low = carry
        return jnp.logical_and(j >= 0, low < SB_DEAD_LOG2)

    def earlier_block(carry):
        j, (rt, at), (rb, ab), _ = carry
        t, b = sweep([(q_top, rt, at, [(jnp.maximum(j - 1, 0), False, j > 0)]),
                      (q_bot, rb, ab, [(j, False, None)])])
        return j - 1, t, b, min_run(t[0], b[0])

    _, top, bot, _ = lax.while_loop(alive, earlier_block,
                                    (left - 1, top, bot, min_run(top[0], bot[0])))
    o_ref[0] = jnp.concatenate([top[1], bot[1]], axis=0).astype(o_ref.dtype)


def _sb_attention(proj, qg, kg, col0):
    bsz, s, _ = proj.shape
    tq = min(512, s)
    npair = SB_WIDTH // LANES
    qg2 = jnp.tile(qg.reshape(1, SB_HEAD_DIM), (1, 2))
    kg2 = jnp.tile(kg.reshape(1, SB_HEAD_DIM), (1, 2))
    return pl.pallas_call(
        functools.partial(_sb_kernel, tq=tq),
        grid=(bsz, npair, s // tq),
        in_specs=[pl.BlockSpec((1, tq, LANES), lambda b, h, i: (b, i, col0 + h)),
                  pl.BlockSpec((1, s, LANES), lambda b, h, i: (b, 0, col0 + npair + h)),
                  pl.BlockSpec((1, s, LANES), lambda b, h, i: (b, 0, col0 + 2 * npair + h)),
                  pl.BlockSpec((1, LANES), lambda b, h, i: (0, 0)),
                  pl.BlockSpec((1, LANES), lambda b, h, i: (0, 0))],
        out_specs=pl.BlockSpec((1, tq, LANES), lambda b, h, i: (b, i, h)),
        out_shape=jax.ShapeDtypeStruct((bsz, s, SB_WIDTH), BF16),
        scratch_shapes=[pltpu.VMEM((LANES, s), BF16),
                        pltpu.VMEM((s, LANES), BF16),
                        pltpu.VMEM((s, LANES), BF16)],
        compiler_params=_cparams("parallel", "parallel", "arbitrary"),
    )(proj, proj, proj, qg2, kg2)


HG_CHUNKS_PER_STEP = 4


def _hgrn_kernel(f_ref, i_ref, q_ref, g_ref, lbl_ref, ng_ref, o_ref, st_ref, *, layer):
    ts = f_ref.shape[1]
    c = HG_CHUNK
    dh = HG_HEAD_DIM

    @pl.when(pl.program_id(1) == 0)
    def _():
        st_ref[...] = jnp.zeros_like(st_ref)

    lg = lbl_ref[...]
    e = jnp.exp(lg - jnp.max(lg, axis=0, keepdims=True))
    lb = jnp.sum(e[:layer + 1], axis=0, keepdims=True) / jnp.sum(e, axis=0, keepdims=True)

    row = lax.broadcasted_iota(jnp.int32, (c, c), 0)
    col = lax.broadcasted_iota(jnp.int32, (c, c), 1)
    causal = col <= row
    lower = causal.astype(BF16)
    width = f_ref.shape[2]

    def body(it, carry):
        states = [st_ref[h] for h in range(HG_HEADS)]
        for u in range(HG_CHUNKS_PER_STEP):
            rows = pl.ds(pl.multiple_of((it * HG_CHUNKS_PER_STEP + u) * c, c), c)
            forget = lb + (1.0 - lb) * jax.nn.sigmoid(f_ref[0, rows, :].astype(F32))
            kc = 1.0 - forget
            hi, lo = _split_trunc(jnp.log(forget))
            cum = _dot(lower, jnp.concatenate([hi, lo], axis=1))
            gc = cum[:, :width] + cum[:, width:]
            g_mid = gc[c // 2 - 1:c // 2, :]
            g_last = gc[c - 1:c, :]
            qe = _silu(q_ref[0, rows, :].astype(F32)) * jnp.exp(gc - g_mid)
            ke = kc * jnp.exp(g_mid - gc)
            qg = (qe * jnp.exp(g_mid)).astype(BF16)
            kd = (ke * jnp.exp(g_last - g_mid)).astype(BF16)
            qe = qe.astype(BF16)
            ke = ke.astype(BF16)
            decay = jnp.exp(g_last)
            v = i_ref[0, rows, :]
            v_t = v.astype(F32).T.astype(BF16)
            outs = []
            for h in range(HG_HEADS):
                sl = slice(h * dh, (h + 1) * dh)
                scores = _dot_nt(qe[:, sl], ke[:, sl])
                o = (_dot(jnp.where(causal, scores, 0.0).astype(BF16), v[:, sl])
                     + _dot_nt(qg[:, sl], states[h].astype(BF16)))
                outs.append(o * lax.rsqrt(jnp.mean(o * o, axis=-1, keepdims=True) + EPS))
                states[h] = states[h] * decay[:, sl] + _dot(v_t[sl, :], kd[:, sl])
            y = jnp.concatenate(outs, axis=1) * ng_ref[...] * _silu(g_ref[0, rows, :].astype(F32))
            o_ref[0, rows, :] = y.astype(o_ref.dtype)
        for h in range(HG_HEADS):
            st_ref[h] = states[h]
        return carry

    lax.fori_loop(0, ts // (c * HG_CHUNKS_PER_STEP), body, 0)


def _hgrn(proj, lb_logits, norm_g, col0, layer):
    bsz, s, _ = proj.shape
    nl = lb_logits.shape[0]
    ts = min(1024, s)
    blk0 = col0 * LANES // HG_WIDTH
    spec = lambda off: pl.BlockSpec((1, ts, HG_WIDTH), lambda b, i: (b, i, blk0 + off))
    return pl.pallas_call(
        functools.partial(_hgrn_kernel, layer=layer),
        grid=(bsz, s // ts),
        in_specs=[spec(0), spec(1), spec(2), spec(3),
                  pl.BlockSpec((nl, HG_WIDTH), lambda b, i: (0, 0)),
                  pl.BlockSpec((1, HG_WIDTH), lambda b, i: (0, 0))],
        out_specs=pl.BlockSpec((1, ts, HG_WIDTH), lambda b, i: (b, i, 0)),
        out_shape=jax.ShapeDtypeStruct((bsz, s, HG_WIDTH), BF16),
        scratch_shapes=[pltpu.VMEM((HG_HEADS, HG_HEAD_DIM, HG_HEAD_DIM), F32)],
        compiler_params=_cparams("parallel", "arbitrary"),
    )(proj, proj, proj, proj, lb_logits, jnp.tile(norm_g.reshape(1, HG_HEAD_DIM), (1, HG_HEADS)))


def _pack_halves(t):
    bits = lax.bitcast_convert_type(t.astype(BF16).astype(F32), jnp.uint32)
    w = t.shape[1] // 2
    return lax.bitcast_convert_type(bits[:, :w] | (bits[:, w:] >> 16), jnp.int32)


def _unpack_halves(p):
    u = lax.bitcast_convert_type(p, jnp.uint32)
    return (lax.bitcast_convert_type(u & jnp.uint32(0xFFFF0000), F32),
            lax.bitcast_convert_type(u << 16, F32))


def _store_planes(ref, rows, t):
    p = _pack_halves(t)
    q = p.shape[1] // 2
    ref[0, rows, :] = p[:, :q]
    ref[1, rows, :] = p[:, q:]


def _load_planes(p0, p1):
    a0, b0 = _unpack_halves(p0)
    a1, b1 = _unpack_halves(p1)
    return jnp.concatenate([a0, a1, b0, b1], axis=1)


def _swiglu_hidden(h, w_gu):
    gu = _dot(h, w_gu)
    hid = w_gu.shape[1] // 2
    return _silu(gu[:, :hid]) * gu[:, hid:]


def _merge_kernel(x_ref, osb_ref, ohg_ref, gsb_ref, ghg_ref, mod_ref, g2_ref, wsb_ref, whg_ref,
                  wout_ref, wrh_ref, wrl_ref, wsgu_ref, wsd_ref, x2_ref, h2p_ref, lg_ref):
    m_sb = _dot(osb_ref[0], wsb_ref[...])
    m_hg = _dot(ohg_ref[0], whg_ref[...])
    merged = (jax.nn.sigmoid(gsb_ref[0].astype(F32)) * m_sb
              + jax.nn.sigmoid(ghg_ref[0].astype(F32)) * m_hg)
    x1 = x_ref[0] + mod_ref[0, 2:3, :] * _dot(merged.astype(BF16), wout_ref[...])
    h2 = _modulated_norm(x1, g2_ref[...], mod_ref[0, 3:4, :], mod_ref[0, 4:5, :])
    hi, lo = _split_bf16(h2)
    _store_planes(h2p_ref, slice(None), h2)
    lg_ref[...] = _dot_nt(wrh_ref[...], hi) + _dot_nt(wrh_ref[...], lo) + _dot_nt(wrl_ref[...], hi)
    shared = _dot(_swiglu_hidden(hi, wsgu_ref[...]).astype(BF16), wsd_ref[...])
    x2_ref[0] = x1 + mod_ref[0, 5:6, :] * shared


def _merge(x, o_sb, o_hg, proj, mod, g2, w_sb, w_hg, w_out, wr_hi, wr_lo, ws_gu, ws_d, b0):
    _, s, d = x.shape
    bsz = mod.shape[0]
    tm = min(512, s)
    ns = s // tm
    full = lambda shape: pl.BlockSpec(shape, lambda b, i: (0,) * len(shape))
    return pl.pallas_call(
        _merge_kernel,
        grid=(bsz, ns),
        in_specs=[pl.BlockSpec((1, tm, d), lambda b, i: (b0 + b, i, 0)),
                  pl.BlockSpec((1, tm, SB_WIDTH), lambda b, i: (b, i, 0)),
                  pl.BlockSpec((1, tm, HG_WIDTH), lambda b, i: (b, i, 0)),
                  pl.BlockSpec((1, tm, d), lambda b, i: (b, i, 0)),
                  pl.BlockSpec((1, tm, d), lambda b, i: (b, i, 1)),
                  pl.BlockSpec((1, N_MOD, d), lambda b, i: (b, 0, 0)),
                  full((1, d)), full(w_sb.shape), full(w_hg.shape), full(w_out.shape),
                  full(wr_hi.shape), full(wr_lo.shape), full(ws_gu.shape), full(ws_d.shape)],
        out_specs=[pl.BlockSpec((1, tm, d), lambda b, i: (b, i, 0)),
                   pl.BlockSpec((2, tm, d // 4), lambda b, i: (0, b * ns + i, 0)),
                   pl.BlockSpec((N_EXPERTS, tm), lambda b, i: (0, b * ns + i))],
        out_shape=[jax.ShapeDtypeStruct((bsz, s, d), F32),
                   jax.ShapeDtypeStruct((2, bsz * s, d // 4), jnp.int32),
                   jax.ShapeDtypeStruct((N_EXPERTS, bsz * s), F32)],
        compiler_params=_cparams("parallel", "parallel"),
    )(x, o_sb, o_hg, proj, proj, mod, g2.reshape(1, d), w_sb, w_hg, w_out, wr_hi, wr_lo,
      ws_gu, ws_d)


def _first_argmax(vals, idx, sentinel):
    m = jnp.max(vals, axis=0, keepdims=True)
    first = jnp.min(jnp.where(vals == m, idx, sentinel), axis=0, keepdims=True)
    return m, first


def _route_kernel(lg_ref, bias_ref, gates_ref, rank_ref, cnt_ref, run_ref):
    tn = lg_ref.shape[1]

    @pl.when(pl.program_id(0) == 0)
    def _():
        run_ref[...] = jnp.zeros_like(run_ref)

    neg = -jnp.inf
    scores = jax.nn.sigmoid(lg_ref[...])
    choice = scores + bias_ref[...]

    gidx = lax.broadcasted_iota(jnp.int32, (GROUP_SIZE, tn), 0)
    group_rows = []
    for g in range(N_GROUPS):
        cg = choice[g * GROUP_SIZE:(g + 1) * GROUP_SIZE, :]
        m1, i1 = _first_argmax(cg, gidx, GROUP_SIZE)
        m2 = jnp.max(jnp.where(gidx == i1, neg, cg), axis=0, keepdims=True)
        group_rows.append(m1 + m2)
    work = jnp.concatenate(group_rows, axis=0)
    ggi = lax.broadcasted_iota(jnp.int32, (N_GROUPS, tn), 0)
    gmask = jnp.zeros((N_GROUPS, tn), F32)
    for _ in range(TOPK_GROUPS):
        _, first = _first_argmax(work, ggi, N_GROUPS)
        pick = ggi == first
        gmask = jnp.where(pick, 1.0, gmask)
        work = jnp.where(pick, neg, work)

    masked = jnp.concatenate(
        [jnp.where(gmask[g:g + 1, :] > 0.0, choice[g * GROUP_SIZE:(g + 1) * GROUP_SIZE, :], neg)
         for g in range(N_GROUPS)], axis=0)
    eidx = lax.broadcasted_iota(jnp.int32, (N_EXPERTS, tn), 0)
    sel = jnp.zeros((N_EXPERTS, tn), F32)
    for _ in range(TOP_K):
        _, first = _first_argmax(masked, eidx, N_EXPERTS)
        pick = eidx == first
        sel = jnp.where(pick, 1.0, sel)
        masked = jnp.where(pick, neg, masked)

    chosen = jnp.where(sel > 0.0, scores, 0.0)
    gates_ref[...] = chosen / jnp.sum(chosen, axis=0, keepdims=True) * ROUTED_SCALE

    r = lax.broadcasted_iota(jnp.int32, (tn, tn), 0)
    c = lax.broadcasted_iota(jnp.int32, (tn, tn), 1)
    local = _dot(sel.astype(BF16), (r < c).astype(BF16))
    run = run_ref[:, 0:1]
    rank_ref[...] = jnp.where(sel > 0.0, run + local, -1.0)
    total = run + jnp.sum(sel, axis=1, keepdims=True)
    run_ref[...] = jnp.broadcast_to(total, run_ref.shape)
    cnt_ref[...] = jnp.broadcast_to(total, cnt_ref.shape)


def _route(logits_t, bias):
    e, t = logits_t.shape
    tn = min(1024, t)
    return pl.pallas_call(
        _route_kernel,
        grid=(t // tn,),
        in_specs=[pl.BlockSpec((e, tn), lambda i: (0, i)),
                  pl.BlockSpec((e, 1), lambda i: (0, 0))],
        out_specs=[pl.BlockSpec((e, tn), lambda i: (0, i)),
                   pl.BlockSpec((e, tn), lambda i: (0, i)),
                   pl.BlockSpec((e, LANES), lambda i: (0, 0))],
        out_shape=[jax.ShapeDtypeStruct((e, t), F32),
                   jax.ShapeDtypeStruct((e, t), F32),
                   jax.ShapeDtypeStruct((e, LANES), F32)],
        scratch_shapes=[pltpu.VMEM((e, LANES), F32)],
        compiler_params=_cparams("arbitrary"),
    )(logits_t, bias.reshape(e, 1))


def _slots_kernel(gates_ref, rank_ref, cnt_ref, slot_ref, gate8_ref, blk_ref, *, rows):
    ne, tn = gates_ref.shape
    cnt = cnt_ref[...]
    nblk = jnp.floor((cnt + (rows - 1.0)) * (1.0 / rows))
    er = lax.broadcasted_iota(jnp.int32, (ne, ne), 0)
    ec = lax.broadcasted_iota(jnp.int32, (ne, ne), 1)
    lower = (ec < er).astype(BF16)
    pad_start = _dot(lower, nblk.astype(BF16))[:, 0:1] * rows
    pad_end = pad_start + nblk[:, 0:1] * rows

    rank = rank_ref[...]
    sel = rank >= 0.0
    slot_e = pad_start + rank
    kidx = _dot(lower, sel.astype(BF16))
    gates = gates_ref[...]
    slot_rows, gate_rows = [], []
    for k in range(TOP_K):
        m = jnp.logical_and(sel, kidx == k)
        slot_rows.append(jnp.sum(jnp.where(m, slot_e, 0.0), axis=0, keepdims=True))
        gate_rows.append(jnp.sum(jnp.where(m, gates, 0.0), axis=0, keepdims=True))
    slot_ref[...] = jnp.concatenate(slot_rows, axis=0).astype(jnp.int32)
    gate8_ref[...] = jnp.concatenate(gate_rows, axis=0).T

    nbp = blk_ref.shape[1]
    bstart = lax.broadcasted_iota(jnp.int32, (1, nbp), 1).astype(F32) * rows
    e_of = jnp.sum((pad_end <= bstart).astype(F32), axis=0, keepdims=True)
    e_of = jnp.minimum(e_of, ne - 1.0)
    eidx = lax.broadcasted_iota(jnp.int32, (ne, nbp), 0).astype(F32)
    valid_e = jnp.clip(cnt[:, 0:1] - (bstart - pad_start), 0.0, rows)
    valid = jnp.sum(jnp.where(eidx == e_of, valid_e, 0.0), axis=0, keepdims=True)
    blk_ref[...] = jnp.concatenate(
        [e_of, valid, jnp.zeros((blk_ref.shape[0] - 2, nbp), F32)], axis=0).astype(jnp.int32)


def _slots(gates_t, rank_t, counts, rows, nbp):
    e, t = gates_t.shape
    assert t // rows <= 256
    tn = min(1024, t)
    return pl.pallas_call(
        functools.partial(_slots_kernel, rows=rows),
        grid=(t // tn,),
        in_specs=[pl.BlockSpec((e, tn), lambda i: (0, i)),
                  pl.BlockSpec((e, tn), lambda i: (0, i)),
                  pl.BlockSpec((e, LANES), lambda i: (0, 0))],
        out_specs=[pl.BlockSpec((TOP_K, tn), lambda i: (0, i)),
                   pl.BlockSpec((tn, TOP_K), lambda i: (i, 0)),
                   pl.BlockSpec((8, nbp), lambda i: (0, 0))],
        out_shape=[jax.ShapeDtypeStruct((TOP_K, t), jnp.int32),
                   jax.ShapeDtypeStruct((t, TOP_K), F32),
                   jax.ShapeDtypeStruct((8, nbp), jnp.int32)],
        compiler_params=_cparams("arbitrary"),
    )(gates_t, rank_t, counts)


SC_WINDOW = 128


def _sc_mesh():
    return plsc.VectorSubcoreMesh(core_axis_name="core", subcore_axis_name="subcore")


def _sc_gather_rows(table, idx):
    n = idx.shape[1]
    w = table.shape[1]

    @pl.kernel(out_type=jax.ShapeDtypeStruct((n, w), table.dtype), mesh=_sc_mesh())
    def gather(t_hbm, i_hbm, o_hbm):
        def body(i_vmem, o_vmem):
            pltpu.sync_copy(t_hbm.at[i_vmem.at[0]], o_vmem)

        pltpu.emit_pipeline(
            body, grid=(n // SC_WINDOW,),
            in_specs=[pl.BlockSpec((1, SC_WINDOW), lambda i: (0, i))],
            out_specs=[pl.BlockSpec((SC_WINDOW, w), lambda i: (i, 0))],
            core_axis_name=("core", "subcore"),
            dimension_semantics=(pltpu.PARALLEL,),
        )(i_hbm, o_hbm)

    return gather(table, idx)


def _sc_scatter_rows(rows, idx, n_out):
    fan, m = idx.shape
    w = rows.shape[1]

    @pl.kernel(out_type=jax.ShapeDtypeStruct((n_out, w), rows.dtype), mesh=_sc_mesh())
    def scatter(r_hbm, i_hbm, o_hbm):
        def body(r_vmem, i_vmem):
            for k in range(fan):
                pltpu.sync_copy(r_vmem, o_hbm.at[i_vmem.at[k]])

        pltpu.emit_pipeline(
            body, grid=(m // SC_WINDOW,),
            in_specs=[pl.BlockSpec((SC_WINDOW, w), lambda i: (i, 0)),
                      pl.BlockSpec((fan, SC_WINDOW), lambda i: (0, i))],
            out_specs=[],
            core_axis_name=("core", "subcore"),
            dimension_semantics=(pltpu.PARALLEL,),
        )(r_hbm, i_hbm)

    return scatter(rows, idx)


def _gmm_kernel(be_ref, bv_ref, xs_ref, wg_ref, wu_ref, wd_ref, ys_ref, wgu_s, wd_s):
    b = pl.program_id(0)
    valid = bv_ref[b]
    hid = wg_ref.shape[2]

    @pl.when(jnp.logical_or(b == 0, be_ref[b] != be_ref[jnp.maximum(b - 1, 0)]))
    def _():
        wgu_s[:, :hid] = wg_ref[0].astype(BF16)
        wgu_s[:, hid:] = wu_ref[0].astype(BF16)
        wd_s[...] = wd_ref[0].astype(BF16)

    @pl.when(valid > 0)
    def _():
        x = _load_planes(xs_ref[0], xs_ref[1]).astype(BF16)
        y = _dot(_swiglu_hidden(x, wgu_s[...]).astype(BF16), wd_s[...])
        row = lax.broadcasted_iota(jnp.int32, (y.shape[0], 1), 0)
        _store_planes(ys_ref, slice(None), jnp.where(row < valid, y, 0.0))

    @pl.when(valid <= 0)
    def _():
        ys_ref[...] = jnp.zeros_like(ys_ref)


def _gmm(blk_expert, blk_valid, xs, we_gate, we_up, we_down, rows):
    _, n_slots, q = xs.shape
    _, d, hid = we_gate.shape
    w_spec = lambda w: pl.BlockSpec((1,) + w.shape[1:], lambda b, be, bv: (be[b], 0, 0))
    return pl.pallas_call(
        _gmm_kernel,
        grid_spec=pltpu.PrefetchScalarGridSpec(
            num_scalar_prefetch=2,
            grid=(n_slots // rows,),
            in_specs=[pl.BlockSpec((2, rows, q), lambda b, be, bv: (0, b, 0)),
                      w_spec(we_gate), w_spec(we_up), w_spec(we_down)],
            out_specs=pl.BlockSpec((2, rows, q), lambda b, be, bv: (0, b, 0)),
            scratch_shapes=[pltpu.VMEM((d, 2 * hid), BF16), pltpu.VMEM((hid, d), BF16)]),
        out_shape=jax.ShapeDtypeStruct(xs.shape, jnp.int32),
        compiler_params=_cparams("arbitrary"),
    )(blk_expert, blk_valid, xs, we_gate, we_up, we_down)


def _combine_kernel(x2_ref, mod_ref, y8_ref, g8_ref, *rest):
    o_ref = rest[-1]
    g8 = g8_ref[...]
    routed = None
    for k in range(TOP_K):
        term = g8[:, k:k + 1] * _load_planes(y8_ref[k, 0], y8_ref[k, 1])
        routed = term if routed is None else routed + term
    o_ref[0] = x2_ref[0] + mod_ref[0, 5:6, :] * routed


def _combine(x2, mod, y8, gate8, b0, total, earlier):
    bsz, s, d = x2.shape
    tm = min(256, s)
    ns = s // tm
    in_specs = [pl.BlockSpec((1, tm, d), lambda b, i: (b, i, 0)),
                pl.BlockSpec((1, N_MOD, d), lambda b, i: (b, 0, 0)),
                pl.BlockSpec((TOP_K, 2, tm, d // 4), lambda b, i: (0, 0, b * ns + i, 0)),
                pl.BlockSpec((tm, TOP_K), lambda b, i: (b * ns + i, 0))]
    args = [x2, mod, y8, gate8]
    aliases = {}
    if earlier is not None:
        in_specs.append(pl.BlockSpec(memory_space=pl.ANY))
        args.append(earlier)
        aliases = {len(args) - 1: 0}
    return pl.pallas_call(
        _combine_kernel,
        grid=(bsz, ns),
        in_specs=in_specs,
        out_specs=pl.BlockSpec((1, tm, d), lambda b, i: (b0 + b, i, 0)),
        out_shape=jax.ShapeDtypeStruct((total, s, d), F32),
        input_output_aliases=aliases,
        compiler_params=_cparams("parallel", "parallel"),
    )(*args)


def kernel(x, c, w_ada, b_ada, norm1_g, w_in, sb_q_norm_g, sb_k_norm_g, hg_lb_logits, hg_norm_g,
           w_branch_sb, w_branch_hg, w_out, norm2_g, w_router, router_bias, w_e_gate, w_e_up,
           w_e_down, w_s_gate, w_s_up, w_s_down):
    bsz, s, d = x.shape
    depth = w_ada.shape[0]
    n_gate_cols = 2 * d
    qkv_col0 = n_gate_cols // LANES
    hg_col0 = qkv_col0 + 3 * SB_WIDTH // LANES
    for l in range(depth):
        n_mix = 3 * SB_WIDTH + 4 * HG_WIDTH
        w_in_l = jnp.concatenate([w_in[l][:, n_mix:], w_in[l][:, :n_mix]], axis=1).astype(BF16)
        wr_t = w_router[l].T
        wr_hi = wr_t.astype(BF16)
        wr_lo = (wr_t - wr_hi.astype(F32)).astype(BF16)
        ws_gu = jnp.concatenate([w_s_gate[l], w_s_up[l]], axis=1).astype(BF16)

        w_sb, w_hg = w_branch_sb[l].astype(BF16), w_branch_hg[l].astype(BF16)
        w_o, ws_d = w_out[l].astype(BF16), w_s_down[l].astype(BF16)

        mod = _ada(c, w_ada[l], b_ada[l]).reshape(bsz, N_MOD, d)
        n_parts = BATCH_PARTS if bsz % BATCH_PARTS == 0 else 1
        pb = bsz // n_parts
        out = None
        for p in range(n_parts):
            b0 = p * pb
            mod_p = mod[b0:b0 + pb]
            proj = _inproj(x, mod_p, norm1_g[l], w_in_l, b0)
            o_sb = _sb_attention(proj, sb_q_norm_g[l], sb_k_norm_g[l], qkv_col0)
            o_hg = _hgrn(proj, hg_lb_logits, hg_norm_g[l], hg_col0, l)
            x2, h2p, logits_t = _merge(x, o_sb, o_hg, proj, mod_p, norm2_g[l], w_sb, w_hg, w_o,
                                       wr_hi, wr_lo, ws_gu, ws_d, b0)
            t = pb * s
            n_blocks = -(-(t * TOP_K + N_EXPERTS * (DISPATCH_ROWS - 1)) // DISPATCH_ROWS)
            nbp = -(-n_blocks // LANES) * LANES
            gates_t, rank_t, counts = _route(logits_t, router_bias[l])
            slot8, gate8, blk = _slots(gates_t, rank_t, counts, DISPATCH_ROWS, nbp)
            n_slots = n_blocks * DISPATCH_ROWS
            plane_off = jnp.array([0, n_slots], jnp.int32)[None, :, None]
            row_idx = (slot8[:, None, :] + plane_off).reshape(TOP_K, 2 * t)
            q = d // 4
            xs = _sc_scatter_rows(h2p.reshape(2 * t, q), row_idx, 2 * n_slots)
            ys = _gmm(blk[0], blk[1], xs.reshape(2, n_slots, q), w_e_gate[l], w_e_up[l],
                      w_e_down[l], DISPATCH_ROWS)
            y8 = _sc_gather_rows(ys.reshape(2 * n_slots, q), row_idx.reshape(1, TOP_K * 2 * t))
            out = _combine(x2, mod_p, y8.reshape(TOP_K, 2, t, q), gate8, b0, bsz, out)
        x = out
    return x
```

```python
import functools

import jax
import jax.numpy as jnp
from jax import lax
from jax.experimental import pallas as pl
from jax.experimental.pallas import tpu as pltpu
from jax.experimental.pallas import tpu_sc as plsc

F32 = jnp.float32
BF16 = jnp.bfloat16

SB_HEADS = 8
SB_HEAD_DIM = 64
SB_WIDTH = SB_HEADS * SB_HEAD_DIM
HG_HEADS = 4
HG_HEAD_DIM = 128
HG_WIDTH = HG_HEADS * HG_HEAD_DIM
HG_CHUNK = 64
N_EXPERTS = 64
TOP_K = 8
N_GROUPS = 8
TOPK_GROUPS = 4
GROUP_SIZE = N_EXPERTS // N_GROUPS
ROUTED_SCALE = 2.5
DISPATCH_ROWS = 1024
BATCH_PARTS = 2
N_MOD = 6
EPS = 1e-6
LOG2_E = 1.4426950408889634
SB_DEAD_LOG2 = 160.0

LANES = 128
VMEM_LIMIT = 56 * 1024 * 1024


def _cparams(*sem):
    return pltpu.CompilerParams(dimension_semantics=sem, vmem_limit_bytes=VMEM_LIMIT)


def _silu(t):
    return t * jax.nn.sigmoid(t)


def _dot(a, b):
    return jnp.dot(a, b, preferred_element_type=F32)


def _run_after(body, in_specs, args, after):
    if after is None:
        return body
    pos = len(args)
    in_specs.append(pl.BlockSpec(memory_space=pl.ANY))
    args.append(after)
    return lambda *refs: body(*refs[:pos], *refs[pos + 1:])


def _dot_nt(a, b):
    return lax.dot_general(a, b, (((1,), (1,)), ((), ())), preferred_element_type=F32)


def _split_bf16(t):
    hi = t.astype(BF16)
    lo = (t - hi.astype(F32)).astype(BF16)
    return hi, lo


def _ada_kernel(c_ref, w_ref, b_ref, o_ref):
    cond = _silu(c_ref[...])
    o_ref[...] = _dot(cond, w_ref[...]) + b_ref[...]


def _ada(c, w, b):
    bsz, d = c.shape
    n = w.shape[1]
    tn = 1024
    return pl.pallas_call(
        _ada_kernel,
        grid=(n // tn,),
        in_specs=[pl.BlockSpec((bsz, d), lambda j: (0, 0)),
                  pl.BlockSpec((d, tn), lambda j: (0, j)),
                  pl.BlockSpec((1, tn), lambda j: (0, j))],
        out_specs=pl.BlockSpec((bsz, tn), lambda j: (0, j)),
        out_shape=jax.ShapeDtypeStruct((bsz, n), F32),
        compiler_params=_cparams("parallel"),
    )(c, w, b.reshape(1, n))


def _modulated_norm(x, g, shift, scale):
    y = x * lax.rsqrt(jnp.mean(x * x, axis=-1, keepdims=True) + EPS) * g
    return y * (1.0 + scale) + shift


INPROJ_COLS = 512


def _inproj_kernel(x_ref, mod_ref, g_ref, w_ref, o_ref):
    h = _modulated_norm(x_ref[0], g_ref[...], mod_ref[0, 0:1, :], mod_ref[0, 1:2, :]).astype(BF16)
    for j in range(w_ref.shape[1] // INPROJ_COLS):
        cols = slice(j * INPROJ_COLS, (j + 1) * INPROJ_COLS)
        o_ref[0, :, cols] = _dot(h, w_ref[:, cols]).astype(o_ref.dtype)


def _inproj(x, mod, g, w, b0, after):
    _, s, d = x.shape
    bsz = mod.shape[0]
    n = w.shape[1]
    tm = min(512, s)
    in_specs = [pl.BlockSpec((1, tm, d), lambda b, i: (b0 + b, i, 0)),
                pl.BlockSpec((1, N_MOD, d), lambda b, i: (b, 0, 0)),
                pl.BlockSpec((1, d), lambda b, i: (0, 0)),
                pl.BlockSpec((d, n), lambda b, i: (0, 0))]
    args = [x, mod, g.reshape(1, d), w]
    body = _run_after(_inproj_kernel, in_specs, args, after)
    return pl.pallas_call(
        body,
        grid=(bsz, s // tm),
        in_specs=in_specs,
        out_specs=pl.BlockSpec((1, tm, n), lambda b, i: (b, i, 0)),
        out_shape=jax.ShapeDtypeStruct((bsz, s, n), BF16),
        compiler_params=_cparams("parallel", "parallel"),
    )(*args)


def _pair_norm(t, g, lo_half):
    sq = t * t
    s_lo = jnp.sum(jnp.where(lo_half, sq, 0.0), axis=-1, keepdims=True)
    s_hi = jnp.sum(jnp.where(lo_half, 0.0, sq), axis=-1, keepdims=True)
    ms = jnp.where(lo_half, s_lo, s_hi) * (1.0 / SB_HEAD_DIM)
    return t * lax.rsqrt(ms + EPS) * g


def _neg_abs(t):
    bits = lax.bitcast_convert_type(t, jnp.uint32) | jnp.uint32(0x80000000)
    return lax.bitcast_convert_type(bits, F32)


def _split_trunc(t):
    bits = lax.bitcast_convert_type(t, jnp.uint32) & jnp.uint32(0xFFFF0000)
    hi = lax.bitcast_convert_type(bits, F32)
    return hi.astype(BF16), (t - hi).astype(BF16)


def _sb_kernel(q_ref, k_ref, v_ref, qg_ref, kg_ref, o_ref, knt_ref, va_ref, vb_ref, *, tq):
    qi = pl.program_id(2)
    s = k_ref.shape[1]
    lane = lax.broadcasted_iota(jnp.int32, (1, LANES), 1)
    lo_half = lane < SB_HEAD_DIM

    tk = tq // 2

    @pl.when(qi == 0)
    def _():
        def prep_block(j, c):
            rows = pl.ds(pl.multiple_of(j * tk, tk), tk)
            kb = _pair_norm(k_ref[0, rows, :].astype(F32), kg_ref[...], lo_half)
            knt_ref[:, rows] = kb.T.astype(BF16)
            vb = v_ref[0, rows, :]
            va_ref[rows, :] = jnp.where(lo_half, vb, jnp.zeros_like(vb))
            vb_ref[rows, :] = jnp.where(lo_half, jnp.zeros_like(vb), vb)
            return c
        lax.fori_loop(0, s // tk, prep_block, 0)

    scale = SB_HEAD_DIM ** -0.5 * LOG2_E
    q = _pair_norm(q_ref[0].astype(F32), qg_ref[...], lo_half) * scale
    q_heads = (jnp.where(lo_half, q, 0.0).astype(BF16), jnp.where(lo_half, 0.0, q).astype(BF16))
    v_heads = (va_ref, vb_ref)

    row = lax.broadcasted_iota(jnp.int32, (tk, tk), 0)
    col = lax.broadcasted_iota(jnp.int32, (tk, tk), 1)
    strict = col < row
    tri = (row >= col).astype(BF16)
    tri2 = jnp.concatenate([tri, tri], axis=0)

    def sweep(streams):
        cols = [[pl.ds(pl.multiple_of(j * tk, tk), tk) for j, _, _ in blocks]
                for _, _, _, blocks in streams]
        z = [[[_dot(qh[h], knt_ref[:, c]) for c in cols[i]] for h in range(2)]
             for i, (qh, _, _, _) in enumerate(streams)]
        cs = []
        for i, (_, _, _, blocks) in enumerate(streams):
            cs.append([[None] * len(blocks) for _ in range(2)])
            for h in range(2):
                for b, (_, masked, _) in enumerate(blocks):
                    zb = z[i][h][b]
                    sp = jnp.maximum(zb, 0.0) + jnp.log2(1.0 + jnp.exp2(_neg_abs(zb)))
                    if masked:
                        sp = jnp.where(strict, sp, 0.0)
                    cs[i][h][b] = _dot(jnp.concatenate(_split_trunc(sp), axis=1), tri2)
        results = []
        for i, (_, runs, acc, blocks) in enumerate(streams):
            runs = list(runs)
            for h in range(2):
                for b, (_, masked, gate) in enumerate(blocks):
                    a = jnp.exp2(z[i][h][b] - cs[i][h][b] - runs[h])
                    if masked:
                        a = jnp.where(strict, a, 0.0)
                    vb = v_heads[h][cols[i][b], :]
                    step = cs[i][h][b][:, 0:1]
                    if gate is not None:
                        vb = jnp.where(gate, vb, jnp.zeros_like(vb))
                        step = jnp.where(gate, step, 0.0)
                    acc = acc + _dot(a.astype(BF16), vb)
                    runs[h] = runs[h] + step
            results.append((tuple(runs), acc))
        return results

    zero_run = jnp.zeros((tk, 1), F32)
    zero_acc = jnp.zeros((tk, LANES), F32)
    left, right = 2 * qi, 2 * qi + 1
    q_top = tuple(qh[:tk] for qh in q_heads)
    q_bot = tuple(qh[tk:] for qh in q_heads)
    top, bot = sweep([
        (q_top, (zero_run, zero_run), zero_acc,
         [(left, True, None), (jnp.maximum(left - 1, 0), False, qi > 0)]),
        (q_bot, (zero_run, zero_run), zero_acc, [(right, True, None), (left, False, None)])])

    def min_run(rt, rb):
        return jnp.min(jnp.minimum(jnp.minimum(rt[0], rt[1]), jnp.minimum(rb[0], rb[1])))

    def alive(carry):
        j, _, _, low = carry
        return jnp.logical_and(j >= 0, low < SB_DEAD_LOG2)

    def earlier_block(carry):
        j, (rt, at), (rb, ab), _ = carry
        t, b = sweep([(q_top, rt, at, [(jnp.maximum(j - 1, 0), False, j > 0)]),
                      (q_bot, rb, ab, [(j, False, None)])])
        return j - 1, t, b, min_run(t[0], b[0])

    _, top, bot, _ = lax.while_loop(alive, earlier_block,
                                    (left - 1, top, bot, min_run(top[0], bot[0])))
    o_ref[0] = jnp.concatenate([top[1], bot[1]], axis=0).astype(o_ref.dtype)


def _sb_attention(proj, qg, kg, col0, after):
    bsz, s, _ = proj.shape
    tq = min(512, s)
    npair = SB_WIDTH // LANES
    qg2 = jnp.tile(qg.reshape(1, SB_HEAD_DIM), (1, 2))
    kg2 = jnp.tile(kg.reshape(1, SB_HEAD_DIM), (1, 2))
    in_specs = [pl.BlockSpec((1, tq, LANES), lambda b, h, i: (b, i, col0 + h)),
                pl.BlockSpec((1, s, LANES), lambda b, h, i: (b, 0, col0 + npair + h)),
                pl.BlockSpec((1, s, LANES), lambda b, h, i: (b, 0, col0 + 2 * npair + h)),
                pl.BlockSpec((1, LANES), lambda b, h, i: (0, 0)),
                pl.BlockSpec((1, LANES), lambda b, h, i: (0, 0))]
    args = [proj, proj, proj, qg2, kg2]
    body = _run_after(functools.partial(_sb_kernel, tq=tq), in_specs, args, after)
    return pl.pallas_call(
        body,
        grid=(bsz, npair, s // tq),
        in_specs=in_specs,
        out_specs=pl.BlockSpec((1, tq, LANES), lambda b, h, i: (b, i, h)),
        out_shape=jax.ShapeDtypeStruct((bsz, s, SB_WIDTH), BF16),
        scratch_shapes=[pltpu.VMEM((LANES, s), BF16),
                        pltpu.VMEM((s, LANES), BF16),
                        pltpu.VMEM((s, LANES), BF16)],
        compiler_params=_cparams("parallel", "parallel", "arbitrary"),
    )(*args)


HG_CHUNKS_PER_STEP = 4


def _hgrn_kernel(f_ref, i_ref, q_ref, g_ref, lbl_ref, ng_ref, o_ref, st_ref, *, layer):
    ts = f_ref.shape[1]
    c = HG_CHUNK
    dh = HG_HEAD_DIM

    @pl.when(pl.program_id(1) == 0)
    def _():
        st_ref[...] = jnp.zeros_like(st_ref)

    lg = lbl_ref[...]
    e = jnp.exp(lg - jnp.max(lg, axis=0, keepdims=True))
    lb = jnp.sum(e[:layer + 1], axis=0, keepdims=True) / jnp.sum(e, axis=0, keepdims=True)

    row = lax.broadcasted_iota(jnp.int32, (c, c), 0)
    col = lax.broadcasted_iota(jnp.int32, (c, c), 1)
    causal = col <= row
    lower = causal.astype(BF16)
    width = f_ref.shape[2]

    def body(it, carry):
        states = [st_ref[h] for h in range(HG_HEADS)]
        for u in range(HG_CHUNKS_PER_STEP):
            rows = pl.ds(pl.multiple_of((it * HG_CHUNKS_PER_STEP + u) * c, c), c)
            forget = lb + (1.0 - lb) * jax.nn.sigmoid(f_ref[0, rows, :].astype(F32))
            kc = 1.0 - forget
            hi, lo = _split_trunc(jnp.log(forget))
            cum = _dot(lower, jnp.concatenate([hi, lo], axis=1))
            gc = cum[:, :width] + cum[:, width:]
            g_mid = gc[c // 2 - 1:c // 2, :]
            g_last = gc[c - 1:c, :]
            qe = _silu(q_ref[0, rows, :].astype(F32)) * jnp.exp(gc - g_mid)
            ke = kc * jnp.exp(g_mid - gc)
            qg = (qe * jnp.exp(g_mid)).astype(BF16)
            kd = (ke * jnp.exp(g_last - g_mid)).astype(BF16)
            qe = qe.astype(BF16)
            ke = ke.astype(BF16)
            decay = jnp.exp(g_last)
            v = i_ref[0, rows, :]
            v_t = v.astype(F32).T.astype(BF16)
            outs = []
            for h in range(HG_HEADS):
                sl = slice(h * dh, (h + 1) * dh)
                scores = _dot_nt(qe[:, sl], ke[:, sl])
                o = (_dot(jnp.where(causal, scores, 0.0).astype(BF16), v[:, sl])
                     + _dot_nt(qg[:, sl], states[h].astype(BF16)))
                outs.append(o * lax.rsqrt(jnp.mean(o * o, axis=-1, keepdims=True) + EPS))
                states[h] = states[h] * decay[:, sl] + _dot(v_t[sl, :], kd[:, sl])
            y = jnp.concatenate(outs, axis=1) * ng_ref[...] * _silu(g_ref[0, rows, :].astype(F32))
            o_ref[0, rows, :] = y.astype(o_ref.dtype)
        for h in range(HG_HEADS):
            st_ref[h] = states[h]
        return carry

    lax.fori_loop(0, ts // (c * HG_CHUNKS_PER_STEP), body, 0)


def _hgrn(proj, lb_logits, norm_g, col0, layer):
    bsz, s, _ = proj.shape
    nl = lb_logits.shape[0]
    ts = min(1024, s)
    blk0 = col0 * LANES // HG_WIDTH
    spec = lambda off: pl.BlockSpec((1, ts, HG_WIDTH), lambda b, i: (b, i, blk0 + off))
    return pl.pallas_call(
        functools.partial(_hgrn_kernel, layer=layer),
        grid=(bsz, s // ts),
        in_specs=[spec(0), spec(1), spec(2), spec(3),
                  pl.BlockSpec((nl, HG_WIDTH), lambda b, i: (0, 0)),
                  pl.BlockSpec((1, HG_WIDTH), lambda b, i: (0, 0))],
        out_specs=pl.BlockSpec((1, ts, HG_WIDTH), lambda b, i: (b, i, 0)),
        out_shape=jax.ShapeDtypeStruct((bsz, s, HG_WIDTH), BF16),
        scratch_shapes=[pltpu.VMEM((HG_HEADS, HG_HEAD_DIM, HG_HEAD_DIM), F32)],
        compiler_params=_cparams("parallel", "arbitrary"),
    )(proj, proj, proj, proj, lb_logits, jnp.tile(norm_g.reshape(1, HG_HEAD_DIM), (1, HG_HEADS)))


def _pack_halves(t):
    bits = lax.bitcast_convert_type(t.astype(BF16).astype(F32), jnp.uint32)
    w = t.shape[1] // 2
    return lax.bitcast_convert_type(bits[:, :w] | (bits[:, w:] >> 16), jnp.int32)


def _unpack_halves(p):
    u = lax.bitcast_convert_type(p, jnp.uint32)
    return (lax.bitcast_convert_type(u & jnp.uint32(0xFFFF0000), F32),
            lax.bitcast_convert_type(u << 16, F32))


def _store_planes(ref, rows, t):
    p = _pack_halves(t)
    q = p.shape[1] // 2
    ref[0, rows, :] = p[:, :q]
    ref[1, rows, :] = p[:, q:]


def _load_planes(p0, p1):
    a0, b0 = _unpack_halves(p0)
    a1, b1 = _unpack_halves(p1)
    return jnp.concatenate([a0, a1, b0, b1], axis=1)


def _swiglu_hidden(h, w_gu):
    gu = _dot(h, w_gu)
    hid = w_gu.shape[1] // 2
    return _silu(gu[:, :hid]) * gu[:, hid:]


def _merge_kernel(x_ref, osb_ref, ohg_ref, gsb_ref, ghg_ref, mod_ref, g2_ref, wsb_ref, whg_ref,
                  wout_ref, wrh_ref, wrl_ref, wsgu_ref, wsd_ref, x2_ref, h2p_ref, lg_ref):
    m_sb = _dot(osb_ref[0], wsb_ref[...])
    m_hg = _dot(ohg_ref[0], whg_ref[...])
    merged = (jax.nn.sigmoid(gsb_ref[0].astype(F32)) * m_sb
              + jax.nn.sigmoid(ghg_ref[0].astype(F32)) * m_hg)
    x1 = x_ref[0] + mod_ref[0, 2:3, :] * _dot(merged.astype(BF16), wout_ref[...])
    h2 = _modulated_norm(x1, g2_ref[...], mod_ref[0, 3:4, :], mod_ref[0, 4:5, :])
    hi, lo = _split_bf16(h2)
    _store_planes(h2p_ref, slice(None), h2)
    lg_ref[...] = _dot_nt(wrh_ref[...], hi) + _dot_nt(wrh_ref[...], lo) + _dot_nt(wrl_ref[...], hi)
    shared = _dot(_swiglu_hidden(hi, wsgu_ref[...]).astype(BF16), wsd_ref[...])
    x2_ref[0] = x1 + mod_ref[0, 5:6, :] * shared


def _merge(x, o_sb, o_hg, proj, mod, g2, w_sb, w_hg, w_out, wr_hi, wr_lo, ws_gu, ws_d, b0):
    _, s, d = x.shape
    bsz = mod.shape[0]
    tm = min(512, s)
    ns = s // tm
    full = lambda shape: pl.BlockSpec(shape, lambda b, i: (0,) * len(shape))
    return pl.pallas_call(
        _merge_kernel,
        grid=(bsz, ns),
        in_specs=[pl.BlockSpec((1, tm, d), lambda b, i: (b0 + b, i, 0)),
                  pl.BlockSpec((1, tm, SB_WIDTH), lambda b, i: (b, i, 0)),
                  pl.BlockSpec((1, tm, HG_WIDTH), lambda b, i: (b, i, 0)),
                  pl.BlockSpec((1, tm, d), lambda b, i: (b, i, 0)),
                  pl.BlockSpec((1, tm, d), lambda b, i: (b, i, 1)),
                  pl.BlockSpec((1, N_MOD, d), lambda b, i: (b, 0, 0)),
                  full((1, d)), full(w_sb.shape), full(w_hg.shape), full(w_out.shape),
                  full(wr_hi.shape), full(wr_lo.shape), full(ws_gu.shape), full(ws_d.shape)],
        out_specs=[pl.BlockSpec((1, tm, d), lambda b, i: (b, i, 0)),
                   pl.BlockSpec((2, tm, d // 4), lambda b, i: (0, b * ns + i, 0)),
                   pl.BlockSpec((N_EXPERTS, tm), lambda b, i: (0, b * ns + i))],
        out_shape=[jax.ShapeDtypeStruct((bsz, s, d), F32),
                   jax.ShapeDtypeStruct((2, bsz * s, d // 4), jnp.int32),
                   jax.ShapeDtypeStruct((N_EXPERTS, bsz * s), F32)],
        compiler_params=_cparams("parallel", "parallel"),
    )(x, o_sb, o_hg, proj, proj, mod, g2.reshape(1, d), w_sb, w_hg, w_out, wr_hi, wr_lo,
      ws_gu, ws_d)


def _first_argmax(vals, idx, sentinel):
    m = jnp.max(vals, axis=0, keepdims=True)
    first = jnp.min(jnp.where(vals == m, idx, sentinel), axis=0, keepdims=True)
    return m, first


def _route_kernel(lg_ref, bias_ref, gates_ref, rank_ref, cnt_ref, run_ref):
    tn = lg_ref.shape[1]

    @pl.when(pl.program_id(0) == 0)
    def _():
        run_ref[...] = jnp.zeros_like(run_ref)

    neg = -jnp.inf
    scores = jax.nn.sigmoid(lg_ref[...])
    choice = scores + bias_ref[...]

    gidx = lax.broadcasted_iota(jnp.int32, (GROUP_SIZE, tn), 0)
    group_rows = []
    for g in range(N_GROUPS):
        cg = choice[g * GROUP_SIZE:(g + 1) * GROUP_SIZE, :]
        m1, i1 = _first_argmax(cg, gidx, GROUP_SIZE)
        m2 = jnp.max(jnp.where(gidx == i1, neg, cg), axis=0, keepdims=True)
        group_rows.append(m1 + m2)
    work = jnp.concatenate(group_rows, axis=0)
    ggi = lax.broadcasted_iota(jnp.int32, (N_GROUPS, tn), 0)
    gmask = jnp.zeros((N_GROUPS, tn), F32)
    for _ in range(TOPK_GROUPS):
        _, first = _first_argmax(work, ggi, N_GROUPS)
        pick = ggi == first
        gmask = jnp.where(pick, 1.0, gmask)
        work = jnp.where(pick, neg, work)

    masked = jnp.concatenate(
        [jnp.where(gmask[g:g + 1, :] > 0.0, choice[g * GROUP_SIZE:(g + 1) * GROUP_SIZE, :], neg)
         for g in range(N_GROUPS)], axis=0)
    eidx = lax.broadcasted_iota(jnp.int32, (N_EXPERTS, tn), 0)
    sel = jnp.zeros((N_EXPERTS, tn), F32)
    for _ in range(TOP_K):
        _, first = _first_argmax(masked, eidx, N_EXPERTS)
        pick = eidx == first
        sel = jnp.where(pick, 1.0, sel)
        masked = jnp.where(pick, neg, masked)

    chosen = jnp.where(sel > 0.0, scores, 0.0)
    gates_ref[...] = chosen / jnp.sum(chosen, axis=0, keepdims=True) * ROUTED_SCALE

    r = lax.broadcasted_iota(jnp.int32, (tn, tn), 0)
    c = lax.broadcasted_iota(jnp.int32, (tn, tn), 1)
    local = _dot(sel.astype(BF16), (r < c).astype(BF16))
    run = run_ref[:, 0:1]
    rank_ref[...] = jnp.where(sel > 0.0, run + local, -1.0)
    total = run + jnp.sum(sel, axis=1, keepdims=True)
    run_ref[...] = jnp.broadcast_to(total, run_ref.shape)
    cnt_ref[...] = jnp.broadcast_to(total, cnt_ref.shape)


def _route(logits_t, bias):
    e, t = logits_t.shape
    tn = min(1024, t)
    return pl.pallas_call(
        _route_kernel,
        grid=(t // tn,),
        in_specs=[pl.BlockSpec((e, tn), lambda i: (0, i)),
                  pl.BlockSpec((e, 1), lambda i: (0, 0))],
        out_specs=[pl.BlockSpec((e, tn), lambda i: (0, i)),
                   pl.BlockSpec((e, tn), lambda i: (0, i)),
                   pl.BlockSpec((e, LANES), lambda i: (0, 0))],
        out_shape=[jax.ShapeDtypeStruct((e, t), F32),
                   jax.ShapeDtypeStruct((e, t), F32),
                   jax.ShapeDtypeStruct((e, LANES), F32)],
        scratch_shapes=[pltpu.VMEM((e, LANES), F32)],
        compiler_params=_cparams("arbitrary"),
    )(logits_t, bias.reshape(e, 1))


def _slots_kernel(gates_ref, rank_ref, cnt_ref, slot_ref, gate8_ref, blk_ref, *, rows, n_blocks):
    ne, tn = gates_ref.shape
    cnt = cnt_ref[...]
    nblk = jnp.floor((cnt + (rows - 1.0)) * (1.0 / rows))
    er = lax.broadcasted_iota(jnp.int32, (ne, ne), 0)
    ec = lax.broadcasted_iota(jnp.int32, (ne, ne), 1)
    lower = (ec < er).astype(BF16)
    pad_start = _dot(lower, nblk.astype(BF16))[:, 0:1] * rows
    pad_end = pad_start + nblk[:, 0:1] * rows

    rank = rank_ref[...]
    sel = rank >= 0.0
    slot_e = pad_start + rank
    kidx = _dot(lower, sel.astype(BF16))
    gates = gates_ref[...]
    slot_rows, gate_rows = [], []
    for k in range(TOP_K):
        m = jnp.logical_and(sel, kidx == k)
        slot_rows.append(jnp.sum(jnp.where(m, slot_e, 0.0), axis=0, keepdims=True))
        gate_rows.append(jnp.sum(jnp.where(m, gates, 0.0), axis=0, keepdims=True))
    slot_ref[...] = jnp.concatenate(slot_rows, axis=0).astype(jnp.int32)
    gate8_ref[...] = jnp.concatenate(gate_rows, axis=0).T

    nbp = blk_ref.shape[1]
    bstart = lax.broadcasted_iota(jnp.int32, (1, nbp), 1).astype(F32) * rows
    e_of = jnp.sum((pad_end <= bstart).astype(F32), axis=0, keepdims=True)
    e_of = jnp.minimum(e_of, ne - 1.0)
    eidx = lax.broadcasted_iota(jnp.int32, (ne, nbp), 0).astype(F32)
    valid_e = jnp.clip(cnt[:, 0:1] - (bstart - pad_start), 0.0, rows)
    valid = jnp.sum(jnp.where(eidx == e_of, valid_e, 0.0), axis=0, keepdims=True)
    total = pad_end[ne - 1:ne, :]
    used = bstart < total
    bidx = bstart * (1.0 / rows)
    src = jnp.where(used, bidx, total * (1.0 / rows) - 1.0)
    dst = jnp.where(used, bidx, float(n_blocks))
    blk_ref[...] = jnp.concatenate(
        [e_of, valid, src, dst, jnp.zeros((blk_ref.shape[0] - 4, nbp), F32)], axis=0).astype(jnp.int32)


def _slots(gates_t, rank_t, counts, rows, n_blocks):
    nbp = -(-n_blocks // LANES) * LANES
    e, t = gates_t.shape
    assert t // rows <= 256
    tn = min(1024, t)
    return pl.pallas_call(
        functools.partial(_slots_kernel, rows=rows, n_blocks=n_blocks),
        grid=(t // tn,),
        in_specs=[pl.BlockSpec((e, tn), lambda i: (0, i)),
                  pl.BlockSpec((e, tn), lambda i: (0, i)),
                  pl.BlockSpec((e, LANES), lambda i: (0, 0))],
        out_specs=[pl.BlockSpec((TOP_K, tn), lambda i: (0, i)),
                   pl.BlockSpec((tn, TOP_K), lambda i: (i, 0)),
                   pl.BlockSpec((8, nbp), lambda i: (0, 0))],
        out_shape=[jax.ShapeDtypeStruct((TOP_K, t), jnp.int32),
                   jax.ShapeDtypeStruct((t, TOP_K), F32),
                   jax.ShapeDtypeStruct((8, nbp), jnp.int32)],
        compiler_params=_cparams("arbitrary"),
    )(gates_t, rank_t, counts)


SC_WINDOW = 128


def _sc_mesh():
    return plsc.VectorSubcoreMesh(core_axis_name="core", subcore_axis_name="subcore")


def _sc_gather_rows(table, idx):
    n = idx.shape[1]
    w = table.shape[1]

    @pl.kernel(out_type=jax.ShapeDtypeStruct((n, w), table.dtype), mesh=_sc_mesh())
    def gather(t_hbm, i_hbm, o_hbm):
        def body(i_vmem, o_vmem):
            pltpu.sync_copy(t_hbm.at[i_vmem.at[0]], o_vmem)

        pltpu.emit_pipeline(
            body, grid=(n // SC_WINDOW,),
            in_specs=[pl.BlockSpec((1, SC_WINDOW), lambda i: (0, i))],
            out_specs=[pl.BlockSpec((SC_WINDOW, w), lambda i: (i, 0))],
            core_axis_name=("core", "subcore"),
            dimension_semantics=(pltpu.PARALLEL,),
        )(i_hbm, o_hbm)

    return gather(table, idx)


def _sc_scatter_rows(rows, idx, n_out):
    fan, m = idx.shape
    w = rows.shape[1]

    @pl.kernel(out_type=jax.ShapeDtypeStruct((n_out, w), rows.dtype), mesh=_sc_mesh())
    def scatter(r_hbm, i_hbm, o_hbm):
        def body(r_vmem, i_vmem):
            for k in range(fan):
                pltpu.sync_copy(r_vmem, o_hbm.at[i_vmem.at[k]])

        pltpu.emit_pipeline(
            body, grid=(m // SC_WINDOW,),
            in_specs=[pl.BlockSpec((SC_WINDOW, w), lambda i: (i, 0)),
                      pl.BlockSpec((fan, SC_WINDOW), lambda i: (0, i))],
            out_specs=[],
            core_axis_name=("core", "subcore"),
            dimension_semantics=(pltpu.PARALLEL,),
        )(r_hbm, i_hbm)

    return scatter(rows, idx)


def _gmm_kernel(be_ref, bv_ref, bs_ref, bd_ref, xs_ref, wg_ref, wu_ref, wd_ref, ys_ref, wgu_s, wd_s):
    b = pl.program_id(0)
    valid = bv_ref[b]
    hid = wg_ref.shape[2]

    @pl.when(jnp.logical_or(b == 0, be_ref[b] != be_ref[jnp.maximum(b - 1, 0)]))
    def _():
        wgu_s[:, :hid] = wg_ref[0].astype(BF16)
        wgu_s[:, hid:] = wu_ref[0].astype(BF16)
        wd_s[...] = wd_ref[0].astype(BF16)

    @pl.when(valid > 0)
    def _():
        x = _load_planes(xs_ref[0], xs_ref[1]).astype(BF16)
        y = _dot(_swiglu_hidden(x, wgu_s[...]).astype(BF16), wd_s[...])
        row = lax.broadcasted_iota(jnp.int32, (y.shape[0], 1), 0)
        _store_planes(ys_ref, slice(None), jnp.where(row < valid, y, 0.0))

    @pl.when(valid <= 0)
    def _():
        ys_ref[...] = jnp.zeros_like(ys_ref)


def _gmm(blk, xs, we_gate, we_up, we_down, rows):
    _, n_rows, q = xs.shape
    _, d, hid = we_gate.shape
    w_spec = lambda w: pl.BlockSpec((1,) + w.shape[1:], lambda b, be, bv, bs, bd: (be[b], 0, 0))
    return pl.pallas_call(
        _gmm_kernel,
        grid_spec=pltpu.PrefetchScalarGridSpec(
            num_scalar_prefetch=4,
            grid=(n_rows // rows - 1,),
            in_specs=[pl.BlockSpec((2, rows, q), lambda b, be, bv, bs, bd: (0, bs[b], 0)),
                      w_spec(we_gate), w_spec(we_up), w_spec(we_down)],
            out_specs=pl.BlockSpec((2, rows, q), lambda b, be, bv, bs, bd: (0, bd[b], 0)),
            scratch_shapes=[pltpu.VMEM((d, 2 * hid), BF16), pltpu.VMEM((hid, d), BF16)]),
        out_shape=jax.ShapeDtypeStruct(xs.shape, jnp.int32),
        compiler_params=_cparams("arbitrary"),
    )(blk[0], blk[1], blk[2], blk[3], xs, we_gate, we_up, we_down)


def _combine_kernel(x2_ref, mod_ref, y8_ref, g8_ref, *rest):
    o_ref = rest[-1]
    g8 = g8_ref[...]
    routed = None
    for k in range(TOP_K):
        term = g8[:, k:k + 1] * _load_planes(y8_ref[k, 0], y8_ref[k, 1])
        routed = term if routed is None else routed + term
    o_ref[0] = x2_ref[0] + mod_ref[0, 5:6, :] * routed


def _combine(x2, mod, y8, gate8, b0, total, earlier):
    bsz, s, d = x2.shape
    tm = min(256, s)
    ns = s // tm
    in_specs = [pl.BlockSpec((1, tm, d), lambda b, i: (b, i, 0)),
                pl.BlockSpec((1, N_MOD, d), lambda b, i: (b, 0, 0)),
                pl.BlockSpec((TOP_K, 2, tm, d // 4), lambda b, i: (0, 0, b * ns + i, 0)),
                pl.BlockSpec((tm, TOP_K), lambda b, i: (b * ns + i, 0))]
    args = [x2, mod, y8, gate8]
    aliases = {}
    if earlier is not None:
        in_specs.append(pl.BlockSpec(memory_space=pl.ANY))
        args.append(earlier)
        aliases = {len(args) - 1: 0}
    return pl.pallas_call(
        _combine_kernel,
        grid=(bsz, ns),
        in_specs=in_specs,
        out_specs=pl.BlockSpec((1, tm, d), lambda b, i: (b0 + b, i, 0)),
        out_shape=jax.ShapeDtypeStruct((total, s, d), F32),
        input_output_aliases=aliases,
        compiler_params=_cparams("parallel", "parallel"),
    )(*args)


def kernel(x, c, w_ada, b_ada, norm1_g, w_in, sb_q_norm_g, sb_k_norm_g, hg_lb_logits, hg_norm_g,
           w_branch_sb, w_branch_hg, w_out, norm2_g, w_router, router_bias, w_e_gate, w_e_up,
           w_e_down, w_s_gate, w_s_up, w_s_down):
    bsz, s, d = x.shape
    depth = w_ada.shape[0]
    n_gate_cols = 2 * d
    qkv_col0 = n_gate_cols // LANES
    hg_col0 = qkv_col0 + 3 * SB_WIDTH // LANES
    for l in range(depth):
        n_mix = 3 * SB_WIDTH + 4 * HG_WIDTH
        w_in_l = jnp.concatenate([w_in[l][:, n_mix:], w_in[l][:, :n_mix]], axis=1).astype(BF16)
        wr_t = w_router[l].T
        wr_hi = wr_t.astype(BF16)
        wr_lo = (wr_t - wr_hi.astype(F32)).astype(BF16)
        ws_gu = jnp.concatenate([w_s_gate[l], w_s_up[l]], axis=1).astype(BF16)

        w_sb, w_hg = w_branch_sb[l].astype(BF16), w_branch_hg[l].astype(BF16)
        w_o, ws_d = w_out[l].astype(BF16), w_s_down[l].astype(BF16)

        mod = _ada(c, w_ada[l], b_ada[l]).reshape(bsz, N_MOD, d)
        n_parts = BATCH_PARTS if bsz % BATCH_PARTS == 0 else 1
        pb = bsz // n_parts
        out = blk = ys = None
        for p in range(n_parts):
            b0 = p * pb
            mod_p = mod[b0:b0 + pb]
            proj = _inproj(x, mod_p, norm1_g[l], w_in_l, b0, blk)
            o_hg = _hgrn(proj, hg_lb_logits, hg_norm_g[l], hg_col0, l)
            o_sb = _sb_attention(proj, sb_q_norm_g[l], sb_k_norm_g[l], qkv_col0, ys)
            x2, h2p, logits_t = _merge(x, o_sb, o_hg, proj, mod_p, norm2_g[l], w_sb, w_hg, w_o,
                                       wr_hi, wr_lo, ws_gu, ws_d, b0)
            t = pb * s
            n_blocks = -(-(t * TOP_K + N_EXPERTS * (DISPATCH_ROWS - 1)) // DISPATCH_ROWS)
            gates_t, rank_t, counts = _route(logits_t, router_bias[l])
            slot8, gate8, blk = _slots(gates_t, rank_t, counts, DISPATCH_ROWS, n_blocks)
            n_rows = (n_blocks + 1) * DISPATCH_ROWS
            plane_off = jnp.array([0, n_rows], jnp.int32)[None, :, None]
            row_idx = (slot8[:, None, :] + plane_off).reshape(TOP_K, 2 * t)
            q = d // 4
            xs = _sc_scatter_rows(h2p.reshape(2 * t, q), row_idx, 2 * n_rows)
            ys = _gmm(blk, xs.reshape(2, n_rows, q), w_e_gate[l], w_e_up[l], w_e_down[l],
                      DISPATCH_ROWS)
            y8 = _sc_gather_rows(ys.reshape(2 * n_rows, q), row_idx.reshape(1, TOP_K * 2 * t))
            out = _combine(x2, mod_p, y8.reshape(TOP_K, 2, t, q), gate8, b0, bsz, out)
        x = out
    return x
```

```python
import functools

import jax
import jax.numpy as jnp
from jax import lax
from jax.experimental import pallas as pl
from jax.experimental.pallas import tpu as pltpu
from jax.experimental.pallas import tpu_sc as plsc

F32 = jnp.float32
BF16 = jnp.bfloat16

SB_HEADS = 8
SB_HEAD_DIM = 64
SB_WIDTH = SB_HEADS * SB_HEAD_DIM
HG_HEADS = 4
HG_HEAD_DIM = 128
HG_WIDTH = HG_HEADS * HG_HEAD_DIM
HG_CHUNK = 64
N_EXPERTS = 64
TOP_K = 8
N_GROUPS = 8
TOPK_GROUPS = 4
GROUP_SIZE = N_EXPERTS // N_GROUPS
ROUTED_SCALE = 2.5
DISPATCH_ROWS = 1024
BATCH_PARTS = 2
N_MOD = 6
EPS = 1e-6
LOG2_E = 1.4426950408889634
SB_DEAD_LOG2 = 160.0

LANES = 128
VMEM_LIMIT = 56 * 1024 * 1024


def _cparams(*sem):
    return pltpu.CompilerParams(dimension_semantics=sem, vmem_limit_bytes=VMEM_LIMIT)


def _silu(t):
    return t * jax.nn.sigmoid(t)


def _dot(a, b):
    return jnp.dot(a, b, preferred_element_type=F32)


def _run_after(body, in_specs, args, after):
    if after is None:
        return body
    pos = len(args)
    in_specs.append(pl.BlockSpec(memory_space=pl.ANY))
    args.append(after)
    return lambda *refs: body(*refs[:pos], *refs[pos + 1:])


def _dot_nt(a, b):
    return lax.dot_general(a, b, (((1,), (1,)), ((), ())), preferred_element_type=F32)


def _split_bf16(t):
    hi = t.astype(BF16)
    lo = (t - hi.astype(F32)).astype(BF16)
    return hi, lo


def _ada_kernel(c_ref, w_ref, b_ref, o_ref):
    cond = _silu(c_ref[...])
    o_ref[...] = _dot(cond, w_ref[...]) + b_ref[...]


def _ada(c, w, b):
    bsz, d = c.shape
    n = w.shape[1]
    tn = 1024
    return pl.pallas_call(
        _ada_kernel,
        grid=(n // tn,),
        in_specs=[pl.BlockSpec((bsz, d), lambda j: (0, 0)),
                  pl.BlockSpec((d, tn), lambda j: (0, j)),
                  pl.BlockSpec((1, tn), lambda j: (0, j))],
        out_specs=pl.BlockSpec((bsz, tn), lambda j: (0, j)),
        out_shape=jax.ShapeDtypeStruct((bsz, n), F32),
        compiler_params=_cparams("parallel"),
    )(c, w, b.reshape(1, n))


def _modulated_norm(x, g, shift, scale):
    y = x * lax.rsqrt(jnp.mean(x * x, axis=-1, keepdims=True) + EPS) * g
    return y * (1.0 + scale) + shift


INPROJ_COLS = 512


def _inproj_kernel(x_ref, mod_ref, g_ref, w_ref, o_ref):
    h = _modulated_norm(x_ref[0], g_ref[...], mod_ref[0, 0:1, :], mod_ref[0, 1:2, :]).astype(BF16)
    for j in range(w_ref.shape[1] // INPROJ_COLS):
        cols = slice(j * INPROJ_COLS, (j + 1) * INPROJ_COLS)
        o_ref[0, :, cols] = _dot(h, w_ref[:, cols]).astype(o_ref.dtype)


def _inproj(x, mod, g, w, b0, after):
    _, s, d = x.shape
    bsz = mod.shape[0]
    n = w.shape[1]
    tm = min(512, s)
    in_specs = [pl.BlockSpec((1, tm, d), lambda b, i: (b0 + b, i, 0)),
                pl.BlockSpec((1, N_MOD, d), lambda b, i: (b, 0, 0)),
                pl.BlockSpec((1, d), lambda b, i: (0, 0)),
                pl.BlockSpec((d, n), lambda b, i: (0, 0))]
    args = [x, mod, g.reshape(1, d), w]
    body = _run_after(_inproj_kernel, in_specs, args, after)
    return pl.pallas_call(
        body,
        grid=(bsz, s // tm),
        in_specs=in_specs,
        out_specs=pl.BlockSpec((1, tm, n), lambda b, i: (b, i, 0)),
        out_shape=jax.ShapeDtypeStruct((bsz, s, n), BF16),
        compiler_params=_cparams("parallel", "parallel"),
    )(*args)


def _pair_norm(t, g, lo_half):
    sq = t * t
    s_lo = jnp.sum(jnp.where(lo_half, sq, 0.0), axis=-1, keepdims=True)
    s_hi = jnp.sum(jnp.where(lo_half, 0.0, sq), axis=-1, keepdims=True)
    ms = jnp.where(lo_half, s_lo, s_hi) * (1.0 / SB_HEAD_DIM)
    return t * lax.rsqrt(ms + EPS) * g


def _neg_abs(t):
    bits = lax.bitcast_convert_type(t, jnp.uint32) | jnp.uint32(0x80000000)
    return lax.bitcast_convert_type(bits, F32)


def _split_trunc(t):
    bits = lax.bitcast_convert_type(t, jnp.uint32) & jnp.uint32(0xFFFF0000)
    hi = lax.bitcast_convert_type(bits, F32)
    return hi.astype(BF16), (t - hi).astype(BF16)


def _sb_kernel(q_ref, k_ref, v_ref, qg_ref, kg_ref, o_ref, knt_ref, va_ref, vb_ref, *, tq):
    qi = pl.program_id(2)
    s = k_ref.shape[1]
    lane = lax.broadcasted_iota(jnp.int32, (1, LANES), 1)
    lo_half = lane < SB_HEAD_DIM

    tk = tq // 2

    @pl.when(qi == 0)
    def _():
        def prep_block(j, c):
            rows = pl.ds(pl.multiple_of(j * tk, tk), tk)
            kb = _pair_norm(k_ref[0, rows, :].astype(F32), kg_ref[...], lo_half)
            knt_ref[:, rows] = kb.T.astype(BF16)
            vb = v_ref[0, rows, :]
            va_ref[rows, :] = jnp.where(lo_half, vb, jnp.zeros_like(vb))
            vb_ref[rows, :] = jnp.where(lo_half, jnp.zeros_like(vb), vb)
            return c
        lax.fori_loop(0, s // tk, prep_block, 0)

    scale = SB_HEAD_DIM ** -0.5 * LOG2_E
    q = _pair_norm(q_ref[0].astype(F32), qg_ref[...], lo_half) * scale
    q_heads = (jnp.where(lo_half, q, 0.0).astype(BF16), jnp.where(lo_half, 0.0, q).astype(BF16))
    v_heads = (va_ref, vb_ref)

    row = lax.broadcasted_iota(jnp.int32, (tk, tk), 0)
    col = lax.broadcasted_iota(jnp.int32, (tk, tk), 1)
    strict = col < row
    tri = (row >= col).astype(BF16)
    tri2 = jnp.concatenate([tri, tri], axis=0)

    def sweep(streams):
        cols = [[pl.ds(pl.multiple_of(j * tk, tk), tk) for j, _, _ in blocks]
                for _, _, _, blocks in streams]
        z = [[[_dot(qh[h], knt_ref[:, c]) for c in cols[i]] for h in range(2)]
             for i, (qh, _, _, _) in enumerate(streams)]
        cs = []
        for i, (_, _, _, blocks) in enumerate(streams):
            cs.append([[None] * len(blocks) for _ in range(2)])
            for h in range(2):
                for b, (_, masked, _) in enumerate(blocks):
                    zb = z[i][h][b]
                    sp = jnp.maximum(zb, 0.0) + jnp.log2(1.0 + jnp.exp2(_neg_abs(zb)))
                    if masked:
                        sp = jnp.where(strict, sp, 0.0)
                    cs[i][h][b] = _dot(jnp.concatenate(_split_trunc(sp), axis=1), tri2)
        results = []
        for i, (_, runs, acc, blocks) in enumerate(streams):
            runs = list(runs)
            for h in range(2):
                for b, (_, masked, gate) in enumerate(blocks):
                    a = jnp.exp2(z[i][h][b] - cs[i][h][b] - runs[h])
                    if masked:
                        a = jnp.where(strict, a, 0.0)
                    vb = v_heads[h][cols[i][b], :]
                    step = cs[i][h][b][:, 0:1]
                    if gate is not None:
                        vb = jnp.where(gate, vb, jnp.zeros_like(vb))
                        step = jnp.where(gate, step, 0.0)
                    acc = acc + _dot(a.astype(BF16), vb)
                    runs[h] = runs[h] + step
            results.append((tuple(runs), acc))
        return results

    zero_run = jnp.zeros((tk, 1), F32)
    zero_acc = jnp.zeros((tk, LANES), F32)
    left, right = 2 * qi, 2 * qi + 1
    q_top = tuple(qh[:tk] for qh in q_heads)
    q_bot = tuple(qh[tk:] for qh in q_heads)
    top, bot = sweep([
        (q_top, (zero_run, zero_run), zero_acc,
         [(left, True, None), (jnp.maximum(left - 1, 0), False, qi > 0)]),
        (q_bot, (zero_run, zero_run), zero_acc, [(right, True, None), (left, False, None)])])

    def min_run(rt, rb):
        return jnp.min(jnp.minimum(jnp.minimum(rt[0], rt[1]), jnp.minimum(rb[0], rb[1])))

    def alive(carry):
        j, _, _, low = carry
        return jnp.logical_and(j >= 0, low < SB_DEAD_LOG2)

    def earlier_block(carry):
        j, (rt, at), (rb, ab), _ = carry
        t, b = sweep([(q_top, rt, at, [(jnp.maximum(j - 1, 0), False, j > 0)]),
                      (q_bot, rb, ab, [(j, False, None)])])
        return j - 1, t, b, min_run(t[0], b[0])

    _, top, bot, _ = lax.while_loop(alive, earlier_block,
                                    (left - 1, top, bot, min_run(top[0], bot[0])))
    o_ref[0] = jnp.concatenate([top[1], bot[1]], axis=0).astype(o_ref.dtype)


def _sb_attention(proj, qg, kg, col0, after):
    bsz, s, _ = proj.shape
    tq = min(512, s)
    npair = SB_WIDTH // LANES
    qg2 = jnp.tile(qg.reshape(1, SB_HEAD_DIM), (1, 2))
    kg2 = jnp.tile(kg.reshape(1, SB_HEAD_DIM), (1, 2))
    in_specs = [pl.BlockSpec((1, tq, LANES), lambda b, h, i: (b, i, col0 + h)),
                pl.BlockSpec((1, s, LANES), lambda b, h, i: (b, 0, col0 + npair + h)),
                pl.BlockSpec((1, s, LANES), lambda b, h, i: (b, 0, col0 + 2 * npair + h)),
                pl.BlockSpec((1, LANES), lambda b, h, i: (0, 0)),
                pl.BlockSpec((1, LANES), lambda b, h, i: (0, 0))]
    args = [proj, proj, proj, qg2, kg2]
    body = _run_after(functools.partial(_sb_kernel, tq=tq), in_specs, args, after)
    return pl.pallas_call(
        body,
        grid=(bsz, npair, s // tq),
        in_specs=in_specs,
        out_specs=pl.BlockSpec((1, tq, LANES), lambda b, h, i: (b, i, h)),
        out_shape=jax.ShapeDtypeStruct((bsz, s, SB_WIDTH), BF16),
        scratch_shapes=[pltpu.VMEM((LANES, s), BF16),
                        pltpu.VMEM((s, LANES), BF16),
                        pltpu.VMEM((s, LANES), BF16)],
        compiler_params=_cparams("parallel", "parallel", "arbitrary"),
    )(*args)


HG_CHUNKS_PER_STEP = 4


def _hgrn_kernel(f_ref, i_ref, q_ref, g_ref, lbl_ref, ng_ref, o_ref, st_ref, *, layer):
    ts = f_ref.shape[1]
    c = HG_CHUNK
    dh = HG_HEAD_DIM

    @pl.when(pl.program_id(1) == 0)
    def _():
        st_ref[...] = jnp.zeros_like(st_ref)

    lg = lbl_ref[...]
    e = jnp.exp(lg - jnp.max(lg, axis=0, keepdims=True))
    lb = jnp.sum(e[:layer + 1], axis=0, keepdims=True) / jnp.sum(e, axis=0, keepdims=True)

    row = lax.broadcasted_iota(jnp.int32, (c, c), 0)
    col = lax.broadcasted_iota(jnp.int32, (c, c), 1)
    causal = col <= row
    lower = causal.astype(BF16)
    width = f_ref.shape[2]

    def body(it, carry):
        states = [st_ref[h] for h in range(HG_HEADS)]
        for u in range(HG_CHUNKS_PER_STEP):
            rows = pl.ds(pl.multiple_of((it * HG_CHUNKS_PER_STEP + u) * c, c), c)
            forget = lb + (1.0 - lb) * jax.nn.sigmoid(f_ref[0, rows, :].astype(F32))
            kc = 1.0 - forget
            hi, lo = _split_trunc(jnp.log(forget))
            cum = _dot(lower, jnp.concatenate([hi, lo], axis=1))
            gc = cum[:, :width] + cum[:, width:]
            g_mid = gc[c // 2 - 1:c // 2, :]
            g_last = gc[c - 1:c, :]
            qe = _silu(q_ref[0, rows, :].astype(F32)) * jnp.exp(gc - g_mid)
            ke = kc * jnp.exp(g_mid - gc)
            qg = (qe * jnp.exp(g_mid)).astype(BF16)
            kd = (ke * jnp.exp(g_last - g_mid)).astype(BF16)
            qe = qe.astype(BF16)
            ke = ke.astype(BF16)
            decay = jnp.exp(g_last)
            v = i_ref[0, rows, :]
            v_t = v.astype(F32).T.astype(BF16)
            outs = []
            for h in range(HG_HEADS):
                sl = slice(h * dh, (h + 1) * dh)
                scores = _dot_nt(qe[:, sl], ke[:, sl])
                o = (_dot(jnp.where(causal, scores, 0.0).astype(BF16), v[:, sl])
                     + _dot_nt(qg[:, sl], states[h].astype(BF16)))
                outs.append(o * lax.rsqrt(jnp.mean(o * o, axis=-1, keepdims=True) + EPS))
                states[h] = states[h] * decay[:, sl] + _dot(v_t[sl, :], kd[:, sl])
            y = jnp.concatenate(outs, axis=1) * ng_ref[...] * _silu(g_ref[0, rows, :].astype(F32))
            o_ref[0, rows, :] = y.astype(o_ref.dtype)
        for h in range(HG_HEADS):
            st_ref[h] = states[h]
        return carry

    lax.fori_loop(0, ts // (c * HG_CHUNKS_PER_STEP), body, 0)


def _hgrn(proj, lb_logits, norm_g, col0, layer):
    bsz, s, _ = proj.shape
    nl = lb_logits.shape[0]
    ts = min(1024, s)
    blk0 = col0 * LANES // HG_WIDTH
    spec = lambda off: pl.BlockSpec((1, ts, HG_WIDTH), lambda b, i: (b, i, blk0 + off))
    return pl.pallas_call(
        functools.partial(_hgrn_kernel, layer=layer),
        grid=(bsz, s // ts),
        in_specs=[spec(0), spec(1), spec(2), spec(3),
                  pl.BlockSpec((nl, HG_WIDTH), lambda b, i: (0, 0)),
                  pl.BlockSpec((1, HG_WIDTH), lambda b, i: (0, 0))],
        out_specs=pl.BlockSpec((1, ts, HG_WIDTH), lambda b, i: (b, i, 0)),
        out_shape=jax.ShapeDtypeStruct((bsz, s, HG_WIDTH), BF16),
        scratch_shapes=[pltpu.VMEM((HG_HEADS, HG_HEAD_DIM, HG_HEAD_DIM), F32)],
        compiler_params=_cparams("parallel", "arbitrary"),
    )(proj, proj, proj, proj, lb_logits, jnp.tile(norm_g.reshape(1, HG_HEAD_DIM), (1, HG_HEADS)))


def _pack_halves(t):
    bits = lax.bitcast_convert_type(t.astype(BF16).astype(F32), jnp.uint32)
    w = t.shape[1] // 2
    return lax.bitcast_convert_type(bits[:, :w] | (bits[:, w:] >> 16), jnp.int32)


def _unpack_halves(p):
    u = lax.bitcast_convert_type(p, jnp.uint32)
    return (lax.bitcast_convert_type(u & jnp.uint32(0xFFFF0000), F32),
            lax.bitcast_convert_type(u << 16, F32))


def _store_planes(ref, rows, t):
    p = _pack_halves(t)
    q = p.shape[1] // 2
    ref[0, rows, :] = p[:, :q]
    ref[1, rows, :] = p[:, q:]


def _load_planes(p0, p1):
    a0, b0 = _unpack_halves(p0)
    a1, b1 = _unpack_halves(p1)
    return jnp.concatenate([a0, a1, b0, b1], axis=1)


def _swiglu_hidden(h, w_gu):
    gu = _dot(h, w_gu)
    hid = w_gu.shape[1] // 2
    return _silu(gu[:, :hid]) * gu[:, hid:]


def _merge_kernel(x_ref, osb_ref, ohg_ref, gsb_ref, ghg_ref, mod_ref, g2_ref, wsb_ref, whg_ref,
                  wout_ref, wrh_ref, wrl_ref, wsgu_ref, wsd_ref, x2_ref, h2p_ref, lg_ref):
    m_sb = _dot(osb_ref[0], wsb_ref[...])
    m_hg = _dot(ohg_ref[0], whg_ref[...])
    merged = (jax.nn.sigmoid(gsb_ref[0].astype(F32)) * m_sb
              + jax.nn.sigmoid(ghg_ref[0].astype(F32)) * m_hg)
    x1 = x_ref[0] + mod_ref[0, 2:3, :] * _dot(merged.astype(BF16), wout_ref[...])
    h2 = _modulated_norm(x1, g2_ref[...], mod_ref[0, 3:4, :], mod_ref[0, 4:5, :])
    hi, lo = _split_bf16(h2)
    _store_planes(h2p_ref, slice(None), h2)
    lg_ref[...] = _dot_nt(wrh_ref[...], hi) + _dot_nt(wrh_ref[...], lo) + _dot_nt(wrl_ref[...], hi)
    shared = _dot(_swiglu_hidden(hi, wsgu_ref[...]).astype(BF16), wsd_ref[...])
    x2_ref[0] = x1 + mod_ref[0, 5:6, :] * shared


def _merge(x, o_sb, o_hg, proj, mod, g2, w_sb, w_hg, w_out, wr_hi, wr_lo, ws_gu, ws_d, b0, after):
    _, s, d = x.shape
    bsz = mod.shape[0]
    tm = min(512, s)
    ns = s // tm
    full = lambda shape: pl.BlockSpec(shape, lambda b, i: (0,) * len(shape))
    in_specs = [pl.BlockSpec((1, tm, d), lambda b, i: (b0 + b, i, 0)),
                pl.BlockSpec((1, tm, SB_WIDTH), lambda b, i: (b, i, 0)),
                pl.BlockSpec((1, tm, HG_WIDTH), lambda b, i: (b, i, 0)),
                pl.BlockSpec((1, tm, d), lambda b, i: (b, i, 0)),
                pl.BlockSpec((1, tm, d), lambda b, i: (b, i, 1)),
                pl.BlockSpec((1, N_MOD, d), lambda b, i: (b, 0, 0)),
                full((1, d)), full(w_sb.shape), full(w_hg.shape), full(w_out.shape),
                full(wr_hi.shape), full(wr_lo.shape), full(ws_gu.shape), full(ws_d.shape)]
    args = [x, o_sb, o_hg, proj, proj, mod, g2.reshape(1, d), w_sb, w_hg, w_out, wr_hi, wr_lo,
            ws_gu, ws_d]
    body = _run_after(_merge_kernel, in_specs, args, after)
    return pl.pallas_call(
        body,
        grid=(bsz, ns),
        in_specs=in_specs,
        out_specs=[pl.BlockSpec((1, tm, d), lambda b, i: (b, i, 0)),
                   pl.BlockSpec((2, tm, d // 4), lambda b, i: (0, b * ns + i, 0)),
                   pl.BlockSpec((N_EXPERTS, tm), lambda b, i: (0, b * ns + i))],
        out_shape=[jax.ShapeDtypeStruct((bsz, s, d), F32),
                   jax.ShapeDtypeStruct((2, bsz * s, d // 4), jnp.int32),
                   jax.ShapeDtypeStruct((N_EXPERTS, bsz * s), F32)],
        compiler_params=_cparams("parallel", "parallel"),
    )(*args)


def _first_argmax(vals, idx, sentinel):
    m = jnp.max(vals, axis=0, keepdims=True)
    first = jnp.min(jnp.where(vals == m, idx, sentinel), axis=0, keepdims=True)
    return m, first


def _route_kernel(lg_ref, bias_ref, gates_ref, rank_ref, cnt_ref, run_ref):
    tn = lg_ref.shape[1]

    @pl.when(pl.program_id(0) == 0)
    def _():
        run_ref[...] = jnp.zeros_like(run_ref)

    neg = -jnp.inf
    scores = jax.nn.sigmoid(lg_ref[...])
    choice = scores + bias_ref[...]

    gidx = lax.broadcasted_iota(jnp.int32, (GROUP_SIZE, tn), 0)
    group_rows = []
    for g in range(N_GROUPS):
        cg = choice[g * GROUP_SIZE:(g + 1) * GROUP_SIZE, :]
        m1, i1 = _first_argmax(cg, gidx, GROUP_SIZE)
        m2 = jnp.max(jnp.where(gidx == i1, neg, cg), axis=0, keepdims=True)
        group_rows.append(m1 + m2)
    work = jnp.concatenate(group_rows, axis=0)
    ggi = lax.broadcasted_iota(jnp.int32, (N_GROUPS, tn), 0)
    gmask = jnp.zeros((N_GROUPS, tn), F32)
    for _ in range(TOPK_GROUPS):
        _, first = _first_argmax(work, ggi, N_GROUPS)
        pick = ggi == first
        gmask = jnp.where(pick, 1.0, gmask)
        work = jnp.where(pick, neg, work)

    masked = jnp.concatenate(
        [jnp.where(gmask[g:g + 1, :] > 0.0, choice[g * GROUP_SIZE:(g + 1) * GROUP_SIZE, :], neg)
         for g in range(N_GROUPS)], axis=0)
    eidx = lax.broadcasted_iota(jnp.int32, (N_EXPERTS, tn), 0)
    sel = jnp.zeros((N_EXPERTS, tn), F32)
    for _ in range(TOP_K):
        _, first = _first_argmax(masked, eidx, N_EXPERTS)
        pick = eidx == first
        sel = jnp.where(pick, 1.0, sel)
        masked = jnp.where(pick, neg, masked)

    chosen = jnp.where(sel > 0.0, scores, 0.0)
    gates_ref[...] = chosen / jnp.sum(chosen, axis=0, keepdims=True) * ROUTED_SCALE

    r = lax.broadcasted_iota(jnp.int32, (tn, tn), 0)
    c = lax.broadcasted_iota(jnp.int32, (tn, tn), 1)
    local = _dot(sel.astype(BF16), (r < c).astype(BF16))
    run = run_ref[:, 0:1]
    rank_ref[...] = jnp.where(sel > 0.0, run + local, -1.0)
    total = run + jnp.sum(sel, axis=1, keepdims=True)
    run_ref[...] = jnp.broadcast_to(total, run_ref.shape)
    cnt_ref[...] = jnp.broadcast_to(total, cnt_ref.shape)


def _route(logits_t, bias):
    e, t = logits_t.shape
    tn = min(1024, t)
    return pl.pallas_call(
        _route_kernel,
        grid=(t // tn,),
        in_specs=[pl.BlockSpec((e, tn), lambda i: (0, i)),
                  pl.BlockSpec((e, 1), lambda i: (0, 0))],
        out_specs=[pl.BlockSpec((e, tn), lambda i: (0, i)),
                   pl.BlockSpec((e, tn), lambda i: (0, i)),
                   pl.BlockSpec((e, LANES), lambda i: (0, 0))],
        out_shape=[jax.ShapeDtypeStruct((e, t), F32),
                   jax.ShapeDtypeStruct((e, t), F32),
                   jax.ShapeDtypeStruct((e, LANES), F32)],
        scratch_shapes=[pltpu.VMEM((e, LANES), F32)],
        compiler_params=_cparams("arbitrary"),
    )(logits_t, bias.reshape(e, 1))


def _slots_kernel(gates_ref, rank_ref, cnt_ref, slot_ref, gate8_ref, blk_ref, *, rows, n_blocks):
    ne, tn = gates_ref.shape
    cnt = cnt_ref[...]
    nblk = jnp.floor((cnt + (rows - 1.0)) * (1.0 / rows))
    er = lax.broadcasted_iota(jnp.int32, (ne, ne), 0)
    ec = lax.broadcasted_iota(jnp.int32, (ne, ne), 1)
    lower = (ec < er).astype(BF16)
    pad_start = _dot(lower, nblk.astype(BF16))[:, 0:1] * rows
    pad_end = pad_start + nblk[:, 0:1] * rows

    rank = rank_ref[...]
    sel = rank >= 0.0
    slot_e = pad_start + rank
    kidx = _dot(lower, sel.astype(BF16))
    gates = gates_ref[...]
    slot_rows, gate_rows = [], []
    for k in range(TOP_K):
        m = jnp.logical_and(sel, kidx == k)
        slot_rows.append(jnp.sum(jnp.where(m, slot_e, 0.0), axis=0, keepdims=True))
        gate_rows.append(jnp.sum(jnp.where(m, gates, 0.0), axis=0, keepdims=True))
    slot_ref[...] = jnp.concatenate(slot_rows, axis=0).astype(jnp.int32)
    gate8_ref[...] = jnp.concatenate(gate_rows, axis=0).T

    nbp = blk_ref.shape[1]
    bstart = lax.broadcasted_iota(jnp.int32, (1, nbp), 1).astype(F32) * rows
    e_of = jnp.sum((pad_end <= bstart).astype(F32), axis=0, keepdims=True)
    e_of = jnp.minimum(e_of, ne - 1.0)
    eidx = lax.broadcasted_iota(jnp.int32, (ne, nbp), 0).astype(F32)
    valid_e = jnp.clip(cnt[:, 0:1] - (bstart - pad_start), 0.0, rows)
    valid = jnp.sum(jnp.where(eidx == e_of, valid_e, 0.0), axis=0, keepdims=True)
    total = pad_end[ne - 1:ne, :]
    used = bstart < total
    bidx = bstart * (1.0 / rows)
    src = jnp.where(used, bidx, total * (1.0 / rows) - 1.0)
    dst = jnp.where(used, bidx, float(n_blocks))
    blk_ref[...] = jnp.concatenate(
        [e_of, valid, src, dst, jnp.zeros((blk_ref.shape[0] - 4, nbp), F32)], axis=0).astype(jnp.int32)


def _slots(gates_t, rank_t, counts, rows, n_blocks):
    nbp = -(-n_blocks // LANES) * LANES
    e, t = gates_t.shape
    assert t // rows <= 256
    tn = min(1024, t)
    return pl.pallas_call(
        functools.partial(_slots_kernel, rows=rows, n_blocks=n_blocks),
        grid=(t // tn,),
        in_specs=[pl.BlockSpec((e, tn), lambda i: (0, i)),
                  pl.BlockSpec((e, tn), lambda i: (0, i)),
                  pl.BlockSpec((e, LANES), lambda i: (0, 0))],
        out_specs=[pl.BlockSpec((TOP_K, tn), lambda i: (0, i)),
                   pl.BlockSpec((tn, TOP_K), lambda i: (i, 0)),
                   pl.BlockSpec((8, nbp), lambda i: (0, 0))],
        out_shape=[jax.ShapeDtypeStruct((TOP_K, t), jnp.int32),
                   jax.ShapeDtypeStruct((t, TOP_K), F32),
                   jax.ShapeDtypeStruct((8, nbp), jnp.int32)],
        compiler_params=_cparams("arbitrary"),
    )(gates_t, rank_t, counts)


SC_WINDOW = 128


def _sc_mesh():
    return plsc.VectorSubcoreMesh(core_axis_name="core", subcore_axis_name="subcore")


def _sc_gather_rows(table, idx):
    n = idx.shape[1]
    w = table.shape[1]

    @pl.kernel(out_type=jax.ShapeDtypeStruct((n, w), table.dtype), mesh=_sc_mesh())
    def gather(t_hbm, i_hbm, o_hbm):
        def body(i_vmem, o_vmem):
            pltpu.sync_copy(t_hbm.at[i_vmem.at[0]], o_vmem)

        pltpu.emit_pipeline(
            body, grid=(n // SC_WINDOW,),
            in_specs=[pl.BlockSpec((1, SC_WINDOW), lambda i: (0, i))],
            out_specs=[pl.BlockSpec((SC_WINDOW, w), lambda i: (i, 0))],
            core_axis_name=("core", "subcore"),
            dimension_semantics=(pltpu.PARALLEL,),
        )(i_hbm, o_hbm)

    return gather(table, idx)


def _sc_scatter_rows(rows, idx, n_out):
    fan, m = idx.shape
    w = rows.shape[1]

    @pl.kernel(out_type=jax.ShapeDtypeStruct((n_out, w), rows.dtype), mesh=_sc_mesh())
    def scatter(r_hbm, i_hbm, o_hbm):
        def body(r_vmem, i_vmem):
            for k in range(fan):
                pltpu.sync_copy(r_vmem, o_hbm.at[i_vmem.at[k]])

        pltpu.emit_pipeline(
            body, grid=(m // SC_WINDOW,),
            in_specs=[pl.BlockSpec((SC_WINDOW, w), lambda i: (i, 0)),
                      pl.BlockSpec((fan, SC_WINDOW), lambda i: (0, i))],
            out_specs=[],
            core_axis_name=("core", "subcore"),
            dimension_semantics=(pltpu.PARALLEL,),
        )(r_hbm, i_hbm)

    return scatter(rows, idx)


def _gmm_kernel(be_ref, bv_ref, bs_ref, bd_ref, xs_ref, wg_ref, wu_ref, wd_ref, ys_ref, wgu_s, wd_s):
    b = pl.program_id(0)
    valid = bv_ref[b]
    hid = wg_ref.shape[2]

    @pl.when(jnp.logical_or(b == 0, be_ref[b] != be_ref[jnp.maximum(b - 1, 0)]))
    def _():
        wgu_s[:, :hid] = wg_ref[0].astype(BF16)
        wgu_s[:, hid:] = wu_ref[0].astype(BF16)
        wd_s[...] = wd_ref[0].astype(BF16)

    @pl.when(valid > 0)
    def _():
        x = _load_planes(xs_ref[0], xs_ref[1]).astype(BF16)
        y = _dot(_swiglu_hidden(x, wgu_s[...]).astype(BF16), wd_s[...])
        row = lax.broadcasted_iota(jnp.int32, (y.shape[0], 1), 0)
        _store_planes(ys_ref, slice(None), jnp.where(row < valid, y, 0.0))

    @pl.when(valid <= 0)
    def _():
        ys_ref[...] = jnp.zeros_like(ys_ref)


def _gmm(blk, xs, we_gate, we_up, we_down, rows, after):
    _, n_rows, q = xs.shape
    _, d, hid = we_gate.shape
    w_spec = lambda w: pl.BlockSpec((1,) + w.shape[1:], lambda b, be, bv, bs, bd: (be[b], 0, 0))
    in_specs = [pl.BlockSpec((2, rows, q), lambda b, be, bv, bs, bd: (0, bs[b], 0)),
                w_spec(we_gate), w_spec(we_up), w_spec(we_down)]
    args = [xs, we_gate, we_up, we_down]
    n_prefetch = 4
    body = _gmm_kernel
    if after is not None:
        pos = n_prefetch + len(args)
        in_specs.append(pl.BlockSpec(memory_space=pl.ANY))
        args.append(after)
        body = lambda *refs: _gmm_kernel(*refs[:pos], *refs[pos + 1:])
    return pl.pallas_call(
        body,
        grid_spec=pltpu.PrefetchScalarGridSpec(
            num_scalar_prefetch=n_prefetch,
            grid=(n_rows // rows - 1,),
            in_specs=in_specs,
            out_specs=pl.BlockSpec((2, rows, q), lambda b, be, bv, bs, bd: (0, bd[b], 0)),
            scratch_shapes=[pltpu.VMEM((d, 2 * hid), BF16), pltpu.VMEM((hid, d), BF16)]),
        out_shape=jax.ShapeDtypeStruct(xs.shape, jnp.int32),
        compiler_params=_cparams("arbitrary"),
    )(blk[0], blk[1], blk[2], blk[3], *args)


def _combine_kernel(x2_ref, mod_ref, y8_ref, g8_ref, *rest):
    o_ref = rest[-1]
    g8 = g8_ref[...]
    routed = None
    for k in range(TOP_K):
        term = g8[:, k:k + 1] * _load_planes(y8_ref[k, 0], y8_ref[k, 1])
        routed = term if routed is None else routed + term
    o_ref[0] = x2_ref[0] + mod_ref[0, 5:6, :] * routed


def _combine(x2, mod, y8, gate8, b0, total, earlier):
    bsz, s, d = x2.shape
    tm = min(256, s)
    ns = s // tm
    in_specs = [pl.BlockSpec((1, tm, d), lambda b, i: (b, i, 0)),
                pl.BlockSpec((1, N_MOD, d), lambda b, i: (b, 0, 0)),
                pl.BlockSpec((TOP_K, 2, tm, d // 4), lambda b, i: (0, 0, b * ns + i, 0)),
                pl.BlockSpec((tm, TOP_K), lambda b, i: (b * ns + i, 0))]
    args = [x2, mod, y8, gate8]
    aliases = {}
    if earlier is not None:
        in_specs.append(pl.BlockSpec(memory_space=pl.ANY))
        args.append(earlier)
        aliases = {len(args) - 1: 0}
    return pl.pallas_call(
        _combine_kernel,
        grid=(bsz, ns),
        in_specs=in_specs,
        out_specs=pl.BlockSpec((1, tm, d), lambda b, i: (b0 + b, i, 0)),
        out_shape=jax.ShapeDtypeStruct((total, s, d), F32),
        input_output_aliases=aliases,
        compiler_params=_cparams("parallel", "parallel"),
    )(*args)


def kernel(x, c, w_ada, b_ada, norm1_g, w_in, sb_q_norm_g, sb_k_norm_g, hg_lb_logits, hg_norm_g,
           w_branch_sb, w_branch_hg, w_out, norm2_g, w_router, router_bias, w_e_gate, w_e_up,
           w_e_down, w_s_gate, w_s_up, w_s_down):
    bsz, s, d = x.shape
    depth = w_ada.shape[0]
    n_gate_cols = 2 * d
    qkv_col0 = n_gate_cols // LANES
    hg_col0 = qkv_col0 + 3 * SB_WIDTH // LANES
    for l in range(depth):
        n_mix = 3 * SB_WIDTH + 4 * HG_WIDTH
        w_in_l = jnp.concatenate([w_in[l][:, n_mix:], w_in[l][:, :n_mix]], axis=1).astype(BF16)
        wr_t = w_router[l].T
        wr_hi = wr_t.astype(BF16)
        wr_lo = (wr_t - wr_hi.astype(F32)).astype(BF16)
        ws_gu = jnp.concatenate([w_s_gate[l], w_s_up[l]], axis=1).astype(BF16)

        w_sb, w_hg = w_branch_sb[l].astype(BF16), w_branch_hg[l].astype(BF16)
        w_o, ws_d = w_out[l].astype(BF16), w_s_down[l].astype(BF16)

        mod = _ada(c, w_ada[l], b_ada[l]).reshape(bsz, N_MOD, d)
        n_parts = BATCH_PARTS if bsz % BATCH_PARTS == 0 else 1
        pb = bsz // n_parts
        out = blk = ys = y8 = None
        for p in range(n_parts):
            b0 = p * pb
            mod_p = mod[b0:b0 + pb]
            proj = _inproj(x, mod_p, norm1_g[l], w_in_l, b0, blk)
            o_hg = _hgrn(proj, hg_lb_logits, hg_norm_g[l], hg_col0, l)
            o_sb = _sb_attention(proj, sb_q_norm_g[l], sb_k_norm_g[l], qkv_col0, ys)
            x2, h2p, logits_t = _merge(x, o_sb, o_hg, proj, mod_p, norm2_g[l], w_sb, w_hg, w_o,
                                       wr_hi, wr_lo, ws_gu, ws_d, b0, y8)
            t = pb * s
            n_blocks = -(-(t * TOP_K + N_EXPERTS * (DISPATCH_ROWS - 1)) // DISPATCH_ROWS)
            gates_t, rank_t, counts = _route(logits_t, router_bias[l])
            slot8, gate8, blk = _slots(gates_t, rank_t, counts, DISPATCH_ROWS, n_blocks)
            n_rows = (n_blocks + 1) * DISPATCH_ROWS
            plane_off = jnp.array([0, n_rows], jnp.int32)[None, :, None]
            row_idx = (slot8[:, None, :] + plane_off).reshape(TOP_K, 2 * t)
            q = d // 4
            xs = _sc_scatter_rows(h2p.reshape(2 * t, q), row_idx, 2 * n_rows)
            ys = _gmm(blk, xs.reshape(2, n_rows, q), w_e_gate[l], w_e_up[l], w_e_down[l],
                      DISPATCH_ROWS, out)
            y8 = _sc_gather_rows(ys.reshape(2 * n_rows, q), row_idx.reshape(1, TOP_K * 2 * t))
            out = _combine(x2, mod_p, y8.reshape(TOP_K, 2, t, q), gate8, b0, bsz, out)
        x = out
    return x
```

```python
import functools

import jax
import jax.numpy as jnp
from jax import lax
from jax.experimental import pallas as pl
from jax.experimental.pallas import tpu as pltpu
from jax.experimental.pallas import tpu_sc as plsc

F32 = jnp.float32
BF16 = jnp.bfloat16

SB_HEADS = 8
SB_HEAD_DIM = 64
SB_WIDTH = SB_HEADS * SB_HEAD_DIM
HG_HEADS = 4
HG_HEAD_DIM = 128
HG_WIDTH = HG_HEADS * HG_HEAD_DIM
HG_CHUNK = 64
N_EXPERTS = 64
TOP_K = 8
N_GROUPS = 8
TOPK_GROUPS = 4
GROUP_SIZE = N_EXPERTS // N_GROUPS
ROUTED_SCALE = 2.5
DISPATCH_ROWS = 1024
BATCH_PARTS = 2
N_MOD = 6
EPS = 1e-6
LOG2_E = 1.4426950408889634
SB_DEAD_LOG2 = 160.0

LANES = 128
VMEM_LIMIT = 56 * 1024 * 1024


def _cparams(*sem):
    return pltpu.CompilerParams(dimension_semantics=sem, vmem_limit_bytes=VMEM_LIMIT)


def _silu(t):
    return t * jax.nn.sigmoid(t)


def _dot(a, b):
    return jnp.dot(a, b, preferred_element_type=F32)


def _run_after(body, in_specs, args, after):
    if after is None:
        return body
    pos = len(args)
    in_specs.append(pl.BlockSpec(memory_space=pl.ANY))
    args.append(after)
    return lambda *refs: body(*refs[:pos], *refs[pos + 1:])


def _dot_nt(a, b):
    return lax.dot_general(a, b, (((1,), (1,)), ((), ())), preferred_element_type=F32)


def _split_bf16(t):
    hi = t.astype(BF16)
    lo = (t - hi.astype(F32)).astype(BF16)
    return hi, lo


def _ada_kernel(c_ref, w_ref, b_ref, o_ref):
    cond = _silu(c_ref[...])
    o_ref[...] = _dot(cond, w_ref[...]) + b_ref[...]


def _ada(c, w, b):
    bsz, d = c.shape
    n = w.shape[1]
    tn = 1024
    return pl.pallas_call(
        _ada_kernel,
        grid=(n // tn,),
        in_specs=[pl.BlockSpec((bsz, d), lambda j: (0, 0)),
                  pl.BlockSpec((d, tn), lambda j: (0, j)),
                  pl.BlockSpec((1, tn), lambda j: (0, j))],
        out_specs=pl.BlockSpec((bsz, tn), lambda j: (0, j)),
        out_shape=jax.ShapeDtypeStruct((bsz, n), F32),
        compiler_params=_cparams("parallel"),
    )(c, w, b.reshape(1, n))


def _modulated_norm(x, g, shift, scale):
    y = x * lax.rsqrt(jnp.mean(x * x, axis=-1, keepdims=True) + EPS) * g
    return y * (1.0 + scale) + shift


INPROJ_COLS = 512


def _inproj_kernel(x_ref, mod_ref, g_ref, w_ref, kg_ref, o_ref, knt_ref, *, k_chunk):
    h = _modulated_norm(x_ref[0], g_ref[...], mod_ref[0, 0:1, :], mod_ref[0, 1:2, :]).astype(BF16)
    lo_half = lax.broadcasted_iota(jnp.int32, (1, LANES), 1) < SB_HEAD_DIM
    for j in range(w_ref.shape[1] // INPROJ_COLS):
        cols = slice(j * INPROJ_COLS, (j + 1) * INPROJ_COLS)
        res = _dot(h, w_ref[:, cols])
        o_ref[0, :, cols] = res.astype(o_ref.dtype)
        if j == k_chunk:
            for p in range(INPROJ_COLS // LANES):
                pair = slice(p * LANES, (p + 1) * LANES)
                knt_ref[0, pair, :] = _pair_norm(res[:, pair], kg_ref[...], lo_half).T.astype(BF16)


def _inproj(x, mod, g, w, kg, k_col0, b0, after):
    _, s, d = x.shape
    bsz = mod.shape[0]
    n = w.shape[1]
    tm = min(512, s)
    assert INPROJ_COLS == SB_WIDTH and (k_col0 * LANES) % INPROJ_COLS == 0
    kg2 = jnp.tile(kg.reshape(1, SB_HEAD_DIM), (1, 2))
    in_specs = [pl.BlockSpec((1, tm, d), lambda b, i: (b0 + b, i, 0)),
                pl.BlockSpec((1, N_MOD, d), lambda b, i: (b, 0, 0)),
                pl.BlockSpec((1, d), lambda b, i: (0, 0)),
                pl.BlockSpec((d, n), lambda b, i: (0, 0)),
                pl.BlockSpec((1, LANES), lambda b, i: (0, 0))]
    args = [x, mod, g.reshape(1, d), w, kg2]
    body = _run_after(functools.partial(_inproj_kernel, k_chunk=k_col0 * LANES // INPROJ_COLS),
                      in_specs, args, after)
    return pl.pallas_call(
        body,
        grid=(bsz, s // tm),
        in_specs=in_specs,
        out_specs=[pl.BlockSpec((1, tm, n), lambda b, i: (b, i, 0)),
                   pl.BlockSpec((1, SB_WIDTH, tm), lambda b, i: (b, 0, i))],
        out_shape=[jax.ShapeDtypeStruct((bsz, s, n), BF16),
                   jax.ShapeDtypeStruct((bsz, SB_WIDTH, s), BF16)],
        compiler_params=_cparams("parallel", "parallel"),
    )(*args)


def _pair_norm(t, g, lo_half):
    sq = t * t
    s_lo = jnp.sum(jnp.where(lo_half, sq, 0.0), axis=-1, keepdims=True)
    s_hi = jnp.sum(jnp.where(lo_half, 0.0, sq), axis=-1, keepdims=True)
    ms = jnp.where(lo_half, s_lo, s_hi) * (1.0 / SB_HEAD_DIM)
    return t * lax.rsqrt(ms + EPS) * g


def _neg_abs(t):
    bits = lax.bitcast_convert_type(t, jnp.uint32) | jnp.uint32(0x80000000)
    return lax.bitcast_convert_type(bits, F32)


def _split_trunc(t):
    bits = lax.bitcast_convert_type(t, jnp.uint32) & jnp.uint32(0xFFFF0000)
    hi = lax.bitcast_convert_type(bits, F32)
    return hi.astype(BF16), (t - hi).astype(BF16)


def _sb_kernel(q_ref, knt_ref, v_ref, qg_ref, o_ref, va_ref, vb_ref, *, tq):
    qi = pl.program_id(2)
    s = v_ref.shape[1]
    lane = lax.broadcasted_iota(jnp.int32, (1, LANES), 1)
    lo_half = lane < SB_HEAD_DIM

    tk = tq // 2

    @pl.when(qi == 0)
    def _():
        def prep_block(j, c):
            rows = pl.ds(pl.multiple_of(j * tk, tk), tk)
            vb = v_ref[0, rows, :]
            va_ref[rows, :] = jnp.where(lo_half, vb, jnp.zeros_like(vb))
            vb_ref[rows, :] = jnp.where(lo_half, jnp.zeros_like(vb), vb)
            return c
        lax.fori_loop(0, s // tk, prep_block, 0)

    scale = SB_HEAD_DIM ** -0.5 * LOG2_E
    q = _pair_norm(q_ref[0].astype(F32), qg_ref[...], lo_half) * scale
    q_heads = (jnp.where(lo_half, q, 0.0).astype(BF16), jnp.where(lo_half, 0.0, q).astype(BF16))
    v_heads = (va_ref, vb_ref)

    row = lax.broadcasted_iota(jnp.int32, (tk, tk), 0)
    col = lax.broadcasted_iota(jnp.int32, (tk, tk), 1)
    strict = col < row
    tri = (row >= col).astype(BF16)
    tri2 = jnp.concatenate([tri, tri], axis=0)

    def sweep(streams):
        cols = [[pl.ds(pl.multiple_of(j * tk, tk), tk) for j, _, _ in blocks]
                for _, _, _, blocks in streams]
        z = [[[_dot(qh[h], knt_ref[0, :, c]) for c in cols[i]] for h in range(2)]
             for i, (qh, _, _, _) in enumerate(streams)]
        cs = []
        for i, (_, _, _, blocks) in enumerate(streams):
            cs.append([[None] * len(blocks) for _ in range(2)])
            for h in range(2):
                for b, (_, masked, _) in enumerate(blocks):
                    zb = z[i][h][b]
                    sp = jnp.maximum(zb, 0.0) + jnp.log2(1.0 + jnp.exp2(_neg_abs(zb)))
                    if masked:
                        sp = jnp.where(strict, sp, 0.0)
                    cs[i][h][b] = _dot(jnp.concatenate(_split_trunc(sp), axis=1), tri2)
        results = []
        for i, (_, runs, acc, blocks) in enumerate(streams):
            runs = list(runs)
            for h in range(2):
                for b, (_, masked, gate) in enumerate(blocks):
                    a = jnp.exp2(z[i][h][b] - cs[i][h][b] - runs[h])
                    if masked:
                        a = jnp.where(strict, a, 0.0)
                    vb = v_heads[h][cols[i][b], :]
                    step = cs[i][h][b][:, 0:1]
                    if gate is not None:
                        vb = jnp.where(gate, vb, jnp.zeros_like(vb))
                        step = jnp.where(gate, step, 0.0)
                    acc = acc + _dot(a.astype(BF16), vb)
                    runs[h] = runs[h] + step
            results.append((tuple(runs), acc))
        return results

    zero_run = jnp.zeros((tk, 1), F32)
    zero_acc = jnp.zeros((tk, LANES), F32)
    left, right = 2 * qi, 2 * qi + 1
    q_top = tuple(qh[:tk] for qh in q_heads)
    q_bot = tuple(qh[tk:] for qh in q_heads)
    top, bot = sweep([
        (q_top, (zero_run, zero_run), zero_acc,
         [(left, True, None), (jnp.maximum(left - 1, 0), False, qi > 0)]),
        (q_bot, (zero_run, zero_run), zero_acc, [(right, True, None), (left, False, None)])])

    def min_run(rt, rb):
        return jnp.min(jnp.minimum(jnp.minimum(rt[0], rt[1]), jnp.minimum(rb[0], rb[1])))

    def alive(carry):
        j, _, _, low = carry
        return jnp.logical_and(j >= 0, low < SB_DEAD_LOG2)

    def earlier_block(carry):
        j, (rt, at), (rb, ab), _ = carry
        t, b = sweep([(q_top, rt, at, [(jnp.maximum(j - 1, 0), False, j > 0)]),
                      (q_bot, rb, ab, [(j, False, None)])])
        return j - 1, t, b, min_run(t[0], b[0])

    _, top, bot, _ = lax.while_loop(alive, earlier_block,
                                    (left - 1, top, bot, min_run(top[0], bot[0])))
    o_ref[0] = jnp.concatenate([top[1], bot[1]], axis=0).astype(o_ref.dtype)


def _sb_attention(proj, knt, qg, col0, after):
    bsz, s, _ = proj.shape
    tq = min(512, s)
    npair = SB_WIDTH // LANES
    qg2 = jnp.tile(qg.reshape(1, SB_HEAD_DIM), (1, 2))
    in_specs = [pl.BlockSpec((1, tq, LANES), lambda b, h, i: (b, i, col0 + h)),
                pl.BlockSpec((1, LANES, s), lambda b, h, i: (b, h, 0)),
                pl.BlockSpec((1, s, LANES), lambda b, h, i: (b, 0, col0 + 2 * npair + h)),
                pl.BlockSpec((1, LANES), lambda b, h, i: (0, 0))]
    args = [proj, knt, proj, qg2]
    body = _run_after(functools.partial(_sb_kernel, tq=tq), in_specs, args, after)
    return pl.pallas_call(
        body,
        grid=(bsz, npair, s // tq),
        in_specs=in_specs,
        out_specs=pl.BlockSpec((1, tq, LANES), lambda b, h, i: (b, i, h)),
        out_shape=jax.ShapeDtypeStruct((bsz, s, SB_WIDTH), BF16),
        scratch_shapes=[pltpu.VMEM((s, LANES), BF16),
                        pltpu.VMEM((s, LANES), BF16)],
        compiler_params=_cparams("parallel", "parallel", "arbitrary"),
    )(*args)


HG_CHUNKS_PER_STEP = 4


def _hgrn_kernel(f_ref, i_ref, q_ref, g_ref, lbl_ref, ng_ref, o_ref, st_ref, *, layer):
    ts = f_ref.shape[1]
    c = HG_CHUNK
    dh = HG_HEAD_DIM

    @pl.when(pl.program_id(1) == 0)
    def _():
        st_ref[...] = jnp.zeros_like(st_ref)

    lg = lbl_ref[...]
    e = jnp.exp(lg - jnp.max(lg, axis=0, keepdims=True))
    lb = jnp.sum(e[:layer + 1], axis=0, keepdims=True) / jnp.sum(e, axis=0, keepdims=True)

    row = lax.broadcasted_iota(jnp.int32, (c, c), 0)
    col = lax.broadcasted_iota(jnp.int32, (c, c), 1)
    causal = col <= row
    lower = causal.astype(BF16)
    width = f_ref.shape[2]

    def body(it, carry):
        states = [st_ref[h] for h in range(HG_HEADS)]
        for u in range(HG_CHUNKS_PER_STEP):
            rows = pl.ds(pl.multiple_of((it * HG_CHUNKS_PER_STEP + u) * c, c), c)
            forget = lb + (1.0 - lb) * jax.nn.sigmoid(f_ref[0, rows, :].astype(F32))
            kc = 1.0 - forget
            hi, lo = _split_trunc(jnp.log(forget))
            cum = _dot(lower, jnp.concatenate([hi, lo], axis=1))
            gc = cum[:, :width] + cum[:, width:]
            g_mid = gc[c // 2 - 1:c // 2, :]
            g_last = gc[c - 1:c, :]
            qe = _silu(q_ref[0, rows, :].astype(F32)) * jnp.exp(gc - g_mid)
            ke = kc * jnp.exp(g_mid - gc)
            qg = (qe * jnp.exp(g_mid)).astype(BF16)
            kd = (ke * jnp.exp(g_last - g_mid)).astype(BF16)
            qe = qe.astype(BF16)
            ke = ke.astype(BF16)
            decay = jnp.exp(g_last)
            v = i_ref[0, rows, :]
            v_t = v.astype(F32).T.astype(BF16)
            outs = []
            for h in range(HG_HEADS):
                sl = slice(h * dh, (h + 1) * dh)
                scores = _dot_nt(qe[:, sl], ke[:, sl])
                o = (_dot(jnp.where(causal, scores, 0.0).astype(BF16), v[:, sl])
                     + _dot_nt(qg[:, sl], states[h].astype(BF16)))
                outs.append(o * lax.rsqrt(jnp.mean(o * o, axis=-1, keepdims=True) + EPS))
                states[h] = states[h] * decay[:, sl] + _dot(v_t[sl, :], kd[:, sl])
            y = jnp.concatenate(outs, axis=1) * ng_ref[...] * _silu(g_ref[0, rows, :].astype(F32))
            o_ref[0, rows, :] = y.astype(o_ref.dtype)
        for h in range(HG_HEADS):
            st_ref[h] = states[h]
        return carry

    lax.fori_loop(0, ts // (c * HG_CHUNKS_PER_STEP), body, 0)


def _hgrn(proj, lb_logits, norm_g, col0, layer):
    bsz, s, _ = proj.shape
    nl = lb_logits.shape[0]
    ts = min(1024, s)
    blk0 = col0 * LANES // HG_WIDTH
    spec = lambda off: pl.BlockSpec((1, ts, HG_WIDTH), lambda b, i: (b, i, blk0 + off))
    return pl.pallas_call(
        functools.partial(_hgrn_kernel, layer=layer),
        grid=(bsz, s // ts),
        in_specs=[spec(0), spec(1), spec(2), spec(3),
                  pl.BlockSpec((nl, HG_WIDTH), lambda b, i: (0, 0)),
                  pl.BlockSpec((1, HG_WIDTH), lambda b, i: (0, 0))],
        out_specs=pl.BlockSpec((1, ts, HG_WIDTH), lambda b, i: (b, i, 0)),
        out_shape=jax.ShapeDtypeStruct((bsz, s, HG_WIDTH), BF16),
        scratch_shapes=[pltpu.VMEM((HG_HEADS, HG_HEAD_DIM, HG_HEAD_DIM), F32)],
        compiler_params=_cparams("parallel", "arbitrary"),
    )(proj, proj, proj, proj, lb_logits, jnp.tile(norm_g.reshape(1, HG_HEAD_DIM), (1, HG_HEADS)))


def _pack_halves(t):
    bits = lax.bitcast_convert_type(t.astype(BF16).astype(F32), jnp.uint32)
    w = t.shape[1] // 2
    return lax.bitcast_convert_type(bits[:, :w] | (bits[:, w:] >> 16), jnp.int32)


def _unpack_halves(p):
    u = lax.bitcast_convert_type(p, jnp.uint32)
    return (lax.bitcast_convert_type(u & jnp.uint32(0xFFFF0000), F32),
            lax.bitcast_convert_type(u << 16, F32))


def _store_planes(ref, rows, t):
    p = _pack_halves(t)
    q = p.shape[1] // 2
    ref[0, rows, :] = p[:, :q]
    ref[1, rows, :] = p[:, q:]


def _load_planes(p0, p1):
    a0, b0 = _unpack_halves(p0)
    a1, b1 = _unpack_halves(p1)
    return jnp.concatenate([a0, a1, b0, b1], axis=1)


def _swiglu_hidden(h, w_gu):
    gu = _dot(h, w_gu)
    hid = w_gu.shape[1] // 2
    return _silu(gu[:, :hid]) * gu[:, hid:]


def _merge_kernel(x_ref, osb_ref, ohg_ref, gsb_ref, ghg_ref, mod_ref, g2_ref, wsb_ref, whg_ref,
                  wout_ref, wrh_ref, wrl_ref, wsgu_ref, wsd_ref, x2_ref, h2p_ref, lg_ref):
    m_sb = _dot(osb_ref[0], wsb_ref[...])
    m_hg = _dot(ohg_ref[0], whg_ref[...])
    merged = (jax.nn.sigmoid(gsb_ref[0].astype(F32)) * m_sb
              + jax.nn.sigmoid(ghg_ref[0].astype(F32)) * m_hg)
    x1 = x_ref[0] + mod_ref[0, 2:3, :] * _dot(merged.astype(BF16), wout_ref[...])
    h2 = _modulated_norm(x1, g2_ref[...], mod_ref[0, 3:4, :], mod_ref[0, 4:5, :])
    hi, lo = _split_bf16(h2)
    _store_planes(h2p_ref, slice(None), h2)
    lg_ref[...] = _dot_nt(wrh_ref[...], hi) + _dot_nt(wrh_ref[...], lo) + _dot_nt(wrl_ref[...], hi)
    shared = _dot(_swiglu_hidden(hi, wsgu_ref[...]).astype(BF16), wsd_ref[...])
    x2_ref[0] = x1 + mod_ref[0, 5:6, :] * shared


def _merge(x, o_sb, o_hg, proj, mod, g2, w_sb, w_hg, w_out, wr_hi, wr_lo, ws_gu, ws_d, b0):
    _, s, d = x.shape
    bsz = mod.shape[0]
    tm = min(512, s)
    ns = s // tm
    full = lambda shape: pl.BlockSpec(shape, lambda b, i: (0,) * len(shape))
    in_specs = [pl.BlockSpec((1, tm, d), lambda b, i: (b0 + b, i, 0)),
                pl.BlockSpec((1, tm, SB_WIDTH), lambda b, i: (b, i, 0)),
                pl.BlockSpec((1, tm, HG_WIDTH), lambda b, i: (b, i, 0)),
                pl.BlockSpec((1, tm, d), lambda b, i: (b, i, 0)),
                pl.BlockSpec((1, tm, d), lambda b, i: (b, i, 1)),
                pl.BlockSpec((1, N_MOD, d), lambda b, i: (b, 0, 0)),
                full((1, d)), full(w_sb.shape), full(w_hg.shape), full(w_out.shape),
                full(wr_hi.shape), full(wr_lo.shape), full(ws_gu.shape), full(ws_d.shape)]
    args = [x, o_sb, o_hg, proj, proj, mod, g2.reshape(1, d), w_sb, w_hg, w_out, wr_hi, wr_lo,
            ws_gu, ws_d]
    return pl.pallas_call(
        _merge_kernel,
        grid=(bsz, ns),
        in_specs=in_specs,
        out_specs=[pl.BlockSpec((1, tm, d), lambda b, i: (b, i, 0)),
                   pl.BlockSpec((2, tm, d // 4), lambda b, i: (0, b * ns + i, 0)),
                   pl.BlockSpec((N_EXPERTS, tm), lambda b, i: (0, b * ns + i))],
        out_shape=[jax.ShapeDtypeStruct((bsz, s, d), F32),
                   jax.ShapeDtypeStruct((2, bsz * s, d // 4), jnp.int32),
                   jax.ShapeDtypeStruct((N_EXPERTS, bsz * s), F32)],
        compiler_params=_cparams("parallel", "parallel"),
    )(*args)


def _first_argmax(vals, idx, sentinel):
    m = jnp.max(vals, axis=0, keepdims=True)
    first = jnp.min(jnp.where(vals == m, idx, sentinel), axis=0, keepdims=True)
    return m, first


def _route_kernel(lg_ref, bias_ref, gates_ref, rank_ref, cnt_ref, run_ref):
    tn = lg_ref.shape[1]

    @pl.when(pl.program_id(0) == 0)
    def _():
        run_ref[...] = jnp.zeros_like(run_ref)

    neg = -jnp.inf
    scores = jax.nn.sigmoid(lg_ref[...])
    choice = scores + bias_ref[...]

    gidx = lax.broadcasted_iota(jnp.int32, (GROUP_SIZE, tn), 0)
    group_rows = []
    for g in range(N_GROUPS):
        cg = choice[g * GROUP_SIZE:(g + 1) * GROUP_SIZE, :]
        m1, i1 = _first_argmax(cg, gidx, GROUP_SIZE)
        m2 = jnp.max(jnp.where(gidx == i1, neg, cg), axis=0, keepdims=True)
        group_rows.append(m1 + m2)
    work = jnp.concatenate(group_rows, axis=0)
    ggi = lax.broadcasted_iota(jnp.int32, (N_GROUPS, tn), 0)
    gmask = jnp.zeros((N_GROUPS, tn), F32)
    for _ in range(TOPK_GROUPS):
        _, first = _first_argmax(work, ggi, N_GROUPS)
        pick = ggi == first
        gmask = jnp.where(pick, 1.0, gmask)
        work = jnp.where(pick, neg, work)

    masked = jnp.concatenate(
        [jnp.where(gmask[g:g + 1, :] > 0.0, choice[g * GROUP_SIZE:(g + 1) * GROUP_SIZE, :], neg)
         for g in range(N_GROUPS)], axis=0)
    eidx = lax.broadcasted_iota(jnp.int32, (N_EXPERTS, tn), 0)
    sel = jnp.zeros((N_EXPERTS, tn), F32)
    for _ in range(TOP_K):
        _, first = _first_argmax(masked, eidx, N_EXPERTS)
        pick = eidx == first
        sel = jnp.where(pick, 1.0, sel)
        masked = jnp.where(pick, neg, masked)

    chosen = jnp.where(sel > 0.0, scores, 0.0)
    gates_ref[...] = chosen / jnp.sum(chosen, axis=0, keepdims=True) * ROUTED_SCALE

    r = lax.broadcasted_iota(jnp.int32, (tn, tn), 0)
    c = lax.broadcasted_iota(jnp.int32, (tn, tn), 1)
    local = _dot(sel.astype(BF16), (r < c).astype(BF16))
    run = run_ref[:, 0:1]
    rank_ref[...] = jnp.where(sel > 0.0, run + local, -1.0)
    total = run + jnp.sum(sel, axis=1, keepdims=True)
    run_ref[...] = jnp.broadcast_to(total, run_ref.shape)
    cnt_ref[...] = jnp.broadcast_to(total, cnt_ref.shape)


def _route(logits_t, bias):
    e, t = logits_t.shape
    tn = min(1024, t)
    return pl.pallas_call(
        _route_kernel,
        grid=(t // tn,),
        in_specs=[pl.BlockSpec((e, tn), lambda i: (0, i)),
                  pl.BlockSpec((e, 1), lambda i: (0, 0))],
        out_specs=[pl.BlockSpec((e, tn), lambda i: (0, i)),
                   pl.BlockSpec((e, tn), lambda i: (0, i)),
                   pl.BlockSpec((e, LANES), lambda i: (0, 0))],
        out_shape=[jax.ShapeDtypeStruct((e, t), F32),
                   jax.ShapeDtypeStruct((e, t), F32),
                   jax.ShapeDtypeStruct((e, LANES), F32)],
        scratch_shapes=[pltpu.VMEM((e, LANES), F32)],
        compiler_params=_cparams("arbitrary"),
    )(logits_t, bias.reshape(e, 1))


def _slots_kernel(gates_ref, rank_ref, cnt_ref, slot_ref, gate8_ref, blk_ref, *, rows, n_blocks):
    ne, tn = gates_ref.shape
    cnt = cnt_ref[...]
    nblk = jnp.floor((cnt + (rows - 1.0)) * (1.0 / rows))
    er = lax.broadcasted_iota(jnp.int32, (ne, ne), 0)
    ec = lax.broadcasted_iota(jnp.int32, (ne, ne), 1)
    lower = (ec < er).astype(BF16)
    pad_start = _dot(lower, nblk.astype(BF16))[:, 0:1] * rows
    pad_end = pad_start + nblk[:, 0:1] * rows

    rank = rank_ref[...]
    sel = rank >= 0.0
    slot_e = pad_start + rank
    kidx = _dot(lower, sel.astype(BF16))
    gates = gates_ref[...]
    slot_rows, gate_rows = [], []
    for k in range(TOP_K):
        m = jnp.logical_and(sel, kidx == k)
        slot_rows.append(jnp.sum(jnp.where(m, slot_e, 0.0), axis=0, keepdims=True))
        gate_rows.append(jnp.sum(jnp.where(m, gates, 0.0), axis=0, keepdims=True))
    slot_ref[...] = jnp.concatenate(slot_rows, axis=0).astype(jnp.int32)
    gate8_ref[...] = jnp.concatenate(gate_rows, axis=0).T

    nbp = blk_ref.shape[1]
    bstart = lax.broadcasted_iota(jnp.int32, (1, nbp), 1).astype(F32) * rows
    e_of = jnp.sum((pad_end <= bstart).astype(F32), axis=0, keepdims=True)
    e_of = jnp.minimum(e_of, ne - 1.0)
    eidx = lax.broadcasted_iota(jnp.int32, (ne, nbp), 0).astype(F32)
    valid_e = jnp.clip(cnt[:, 0:1] - (bstart - pad_start), 0.0, rows)
    valid = jnp.sum(jnp.where(eidx == e_of, valid_e, 0.0), axis=0, keepdims=True)
    total = pad_end[ne - 1:ne, :]
    used = bstart < total
    bidx = bstart * (1.0 / rows)
    src = jnp.where(used, bidx, total * (1.0 / rows) - 1.0)
    dst = jnp.where(used, bidx, float(n_blocks))
    blk_ref[...] = jnp.concatenate(
        [e_of, valid, src, dst, jnp.zeros((blk_ref.shape[0] - 4, nbp), F32)], axis=0).astype(jnp.int32)


def _slots(gates_t, rank_t, counts, rows, n_blocks):
    nbp = -(-n_blocks // LANES) * LANES
    e, t = gates_t.shape
    assert t // rows <= 256
    tn = min(1024, t)
    return pl.pallas_call(
        functools.partial(_slots_kernel, rows=rows, n_blocks=n_blocks),
        grid=(t // tn,),
        in_specs=[pl.BlockSpec((e, tn), lambda i: (0, i)),
                  pl.BlockSpec((e, tn), lambda i: (0, i)),
                  pl.BlockSpec((e, LANES), lambda i: (0, 0))],
        out_specs=[pl.BlockSpec((TOP_K, tn), lambda i: (0, i)),
                   pl.BlockSpec((tn, TOP_K), lambda i: (i, 0)),
                   pl.BlockSpec((8, nbp), lambda i: (0, 0))],
        out_shape=[jax.ShapeDtypeStruct((TOP_K, t), jnp.int32),
                   jax.ShapeDtypeStruct((t, TOP_K), F32),
                   jax.ShapeDtypeStruct((8, nbp), jnp.int32)],
        compiler_params=_cparams("arbitrary"),
    )(gates_t, rank_t, counts)


SC_WINDOW = 128


def _sc_mesh():
    return plsc.VectorSubcoreMesh(core_axis_name="core", subcore_axis_name="subcore")


def _sc_gather_rows(table, idx):
    n = idx.shape[1]
    w = table.shape[1]

    @pl.kernel(out_type=jax.ShapeDtypeStruct((n, w), table.dtype), mesh=_sc_mesh())
    def gather(t_hbm, i_hbm, o_hbm):
        def body(i_vmem, o_vmem):
            pltpu.sync_copy(t_hbm.at[i_vmem.at[0]], o_vmem)

        pltpu.emit_pipeline(
            body, grid=(n // SC_WINDOW,),
            in_specs=[pl.BlockSpec((1, SC_WINDOW), lambda i: (0, i))],
            out_specs=[pl.BlockSpec((SC_WINDOW, w), lambda i: (i, 0))],
            core_axis_name=("core", "subcore"),
            dimension_semantics=(pltpu.PARALLEL,),
        )(i_hbm, o_hbm)

    return gather(table, idx)


def _sc_scatter_rows(rows, idx, n_out):
    fan, m = idx.shape
    w = rows.shape[1]

    @pl.kernel(out_type=jax.ShapeDtypeStruct((n_out, w), rows.dtype), mesh=_sc_mesh())
    def scatter(r_hbm, i_hbm, o_hbm):
        def body(r_vmem, i_vmem):
            for k in range(fan):
                pltpu.sync_copy(r_vmem, o_hbm.at[i_vmem.at[k]])

        pltpu.emit_pipeline(
            body, grid=(m // SC_WINDOW,),
            in_specs=[pl.BlockSpec((SC_WINDOW, w), lambda i: (i, 0)),
                      pl.BlockSpec((fan, SC_WINDOW), lambda i: (0, i))],
            out_specs=[],
            core_axis_name=("core", "subcore"),
            dimension_semantics=(pltpu.PARALLEL,),
        )(r_hbm, i_hbm)

    return scatter(rows, idx)


def _gmm_kernel(be_ref, bv_ref, bs_ref, bd_ref, xs_ref, wg_ref, wu_ref, wd_ref, ys_ref, wgu_s, wd_s):
    b = pl.program_id(0)
    valid = bv_ref[b]
    hid = wg_ref.shape[2]

    @pl.when(jnp.logical_or(b == 0, be_ref[b] != be_ref[jnp.maximum(b - 1, 0)]))
    def _():
        wgu_s[:, :hid] = wg_ref[0].astype(BF16)
        wgu_s[:, hid:] = wu_ref[0].astype(BF16)
        wd_s[...] = wd_ref[0].astype(BF16)

    @pl.when(valid > 0)
    def _():
        x = _load_planes(xs_ref[0], xs_ref[1]).astype(BF16)
        y = _dot(_swiglu_hidden(x, wgu_s[...]).astype(BF16), wd_s[...])
        row = lax.broadcasted_iota(jnp.int32, (y.shape[0], 1), 0)
        _store_planes(ys_ref, slice(None), jnp.where(row < valid, y, 0.0))

    @pl.when(valid <= 0)
    def _():
        ys_ref[...] = jnp.zeros_like(ys_ref)


def _gmm(blk, xs, we_gate, we_up, we_down, rows):
    _, n_rows, q = xs.shape
    _, d, hid = we_gate.shape
    w_spec = lambda w: pl.BlockSpec((1,) + w.shape[1:], lambda b, be, bv, bs, bd: (be[b], 0, 0))
    in_specs = [pl.BlockSpec((2, rows, q), lambda b, be, bv, bs, bd: (0, bs[b], 0)),
                w_spec(we_gate), w_spec(we_up), w_spec(we_down)]
    args = [xs, we_gate, we_up, we_down]
    return pl.pallas_call(
        _gmm_kernel,
        grid_spec=pltpu.PrefetchScalarGridSpec(
            num_scalar_prefetch=4,
            grid=(n_rows // rows - 1,),
            in_specs=in_specs,
            out_specs=pl.BlockSpec((2, rows, q), lambda b, be, bv, bs, bd: (0, bd[b], 0)),
            scratch_shapes=[pltpu.VMEM((d, 2 * hid), BF16), pltpu.VMEM((hid, d), BF16)]),
        out_shape=jax.ShapeDtypeStruct(xs.shape, jnp.int32),
        compiler_params=_cparams("arbitrary"),
    )(blk[0], blk[1], blk[2], blk[3], *args)


def _combine_kernel(x2_ref, mod_ref, y8_ref, g8_ref, *rest):
    o_ref = rest[-1]
    g8 = g8_ref[...]
    routed = None
    for k in range(TOP_K):
        term = g8[:, k:k + 1] * _load_planes(y8_ref[k, 0], y8_ref[k, 1])
        routed = term if routed is None else routed + term
    o_ref[0] = x2_ref[0] + mod_ref[0, 5:6, :] * routed


def _combine(x2, mod, y8, gate8, b0, total, earlier):
    bsz, s, d = x2.shape
    tm = min(256, s)
    ns = s // tm
    in_specs = [pl.BlockSpec((1, tm, d), lambda b, i: (b, i, 0)),
                pl.BlockSpec((1, N_MOD, d), lambda b, i: (b, 0, 0)),
                pl.BlockSpec((TOP_K, 2, tm, d // 4), lambda b, i: (0, 0, b * ns + i, 0)),
                pl.BlockSpec((tm, TOP_K), lambda b, i: (b * ns + i, 0))]
    args = [x2, mod, y8, gate8]
    aliases = {}
    if earlier is not None:
        in_specs.append(pl.BlockSpec(memory_space=pl.ANY))
        args.append(earlier)
        aliases = {len(args) - 1: 0}
    return pl.pallas_call(
        _combine_kernel,
        grid=(bsz, ns),
        in_specs=in_specs,
        out_specs=pl.BlockSpec((1, tm, d), lambda b, i: (b0 + b, i, 0)),
        out_shape=jax.ShapeDtypeStruct((total, s, d), F32),
        input_output_aliases=aliases,
        compiler_params=_cparams("parallel", "parallel"),
    )(*args)


def kernel(x, c, w_ada, b_ada, norm1_g, w_in, sb_q_norm_g, sb_k_norm_g, hg_lb_logits, hg_norm_g,
           w_branch_sb, w_branch_hg, w_out, norm2_g, w_router, router_bias, w_e_gate, w_e_up,
           w_e_down, w_s_gate, w_s_up, w_s_down):
    bsz, s, d = x.shape
    depth = w_ada.shape[0]
    n_gate_cols = 2 * d
    qkv_col0 = n_gate_cols // LANES
    hg_col0 = qkv_col0 + 3 * SB_WIDTH // LANES
    for l in range(depth):
        n_mix = 3 * SB_WIDTH + 4 * HG_WIDTH
        w_in_l = jnp.concatenate([w_in[l][:, n_mix:], w_in[l][:, :n_mix]], axis=1).astype(BF16)
        wr_t = w_router[l].T
        wr_hi = wr_t.astype(BF16)
        wr_lo = (wr_t - wr_hi.astype(F32)).astype(BF16)
        ws_gu = jnp.concatenate([w_s_gate[l], w_s_up[l]], axis=1).astype(BF16)

        w_sb, w_hg = w_branch_sb[l].astype(BF16), w_branch_hg[l].astype(BF16)
        w_o, ws_d = w_out[l].astype(BF16), w_s_down[l].astype(BF16)

        mod = _ada(c, w_ada[l], b_ada[l]).reshape(bsz, N_MOD, d)
        n_parts = BATCH_PARTS if bsz % BATCH_PARTS == 0 else 1
        pb = bsz // n_parts
        out = blk = ys = None
        for p in range(n_parts):
            b0 = p * pb
            mod_p = mod[b0:b0 + pb]
            proj, knt = _inproj(x, mod_p, norm1_g[l], w_in_l, sb_k_norm_g[l],
                                qkv_col0 + SB_WIDTH // LANES, b0, blk)
            o_hg = _hgrn(proj, hg_lb_logits, hg_norm_g[l], hg_col0, l)
            o_sb = _sb_attention(proj, knt, sb_q_norm_g[l], qkv_col0, ys)
            x2, h2p, logits_t = _merge(x, o_sb, o_hg, proj, mod_p, norm2_g[l], w_sb, w_hg, w_o,
                                       wr_hi, wr_lo, ws_gu, ws_d, b0)
            t = pb * s
            n_blocks = -(-(t * TOP_K + N_EXPERTS * (DISPATCH_ROWS - 1)) // DISPATCH_ROWS)
            gates_t, rank_t, counts = _route(logits_t, router_bias[l])
            slot8, gate8, blk = _slots(gates_t, rank_t, counts, DISPATCH_ROWS, n_blocks)
            n_rows = (n_blocks + 1) * DISPATCH_ROWS
            plane_off = jnp.array([0, n_rows], jnp.int32)[None, :, None]
            row_idx = (slot8[:, None, :] + plane_off).reshape(TOP_K, 2 * t)
            q = d // 4
            xs = _sc_scatter_rows(h2p.reshape(2 * t, q), row_idx, 2 * n_rows)
            ys = _gmm(blk, xs.reshape(2, n_rows, q), w_e_gate[l], w_e_up[l], w_e_down[l],
                      DISPATCH_ROWS)
            y8 = _sc_gather_rows(ys.reshape(2 * n_rows, q), row_idx.reshape(1, TOP_K * 2 * t))
            out = _combine(x2, mod_p, y8.reshape(TOP_K, 2, t, q), gate8, b0, bsz, out)
        x = out
    return x
```

```python
import functools

import jax
import jax.numpy as jnp
from jax import lax
from jax.experimental import pallas as pl
from jax.experimental.pallas import tpu as pltpu
from jax.experimental.pallas import tpu_sc as plsc

F32 = jnp.float32
BF16 = jnp.bfloat16

SB_HEADS = 8
SB_HEAD_DIM = 64
SB_WIDTH = SB_HEADS * SB_HEAD_DIM
HG_HEADS = 4
HG_HEAD_DIM = 128
HG_WIDTH = HG_HEADS * HG_HEAD_DIM
HG_CHUNK = 64
N_EXPERTS = 64
TOP_K = 8
N_GROUPS = 8
TOPK_GROUPS = 4
GROUP_SIZE = N_EXPERTS // N_GROUPS
ROUTED_SCALE = 2.5
DISPATCH_ROWS = 1024
BATCH_PARTS = 2
N_MOD = 6
EPS = 1e-6
LOG2_E = 1.4426950408889634
SB_DEAD_LOG2 = 160.0

LANES = 128
VMEM_LIMIT = 56 * 1024 * 1024


def _cparams(*sem):
    return pltpu.CompilerParams(dimension_semantics=sem, vmem_limit_bytes=VMEM_LIMIT)


def _silu(t):
    return t * jax.nn.sigmoid(t)


def _dot(a, b):
    return jnp.dot(a, b, preferred_element_type=F32)


def _run_after(body, in_specs, args, after):
    if after is None:
        return body
    pos = len(args)
    in_specs.append(pl.BlockSpec(memory_space=pl.ANY))
    args.append(after)
    return lambda *refs: body(*refs[:pos], *refs[pos + 1:])


def _dot_nt(a, b):
    return lax.dot_general(a, b, (((1,), (1,)), ((), ())), preferred_element_type=F32)


def _split_bf16(t):
    hi = t.astype(BF16)
    lo = (t - hi.astype(F32)).astype(BF16)
    return hi, lo


def _ada_kernel(c_ref, w_ref, b_ref, o_ref):
    cond = _silu(c_ref[...])
    o_ref[...] = _dot(cond, w_ref[...]) + b_ref[...]


def _ada(c, w, b):
    bsz, d = c.shape
    n = w.shape[1]
    tn = 1024
    return pl.pallas_call(
        _ada_kernel,
        grid=(n // tn,),
        in_specs=[pl.BlockSpec((bsz, d), lambda j: (0, 0)),
                  pl.BlockSpec((d, tn), lambda j: (0, j)),
                  pl.BlockSpec((1, tn), lambda j: (0, j))],
        out_specs=pl.BlockSpec((bsz, tn), lambda j: (0, j)),
        out_shape=jax.ShapeDtypeStruct((bsz, n), F32),
        compiler_params=_cparams("parallel"),
    )(c, w, b.reshape(1, n))


def _modulated_norm(x, g, shift, scale):
    y = x * lax.rsqrt(jnp.mean(x * x, axis=-1, keepdims=True) + EPS) * g
    return y * (1.0 + scale) + shift


INPROJ_COLS = 512


def _inproj_kernel(x_ref, mod_ref, g_ref, w_ref, kg_ref, o_ref, knt_ref, *, k_chunk):
    h = _modulated_norm(x_ref[0], g_ref[...], mod_ref[0, 0:1, :], mod_ref[0, 1:2, :]).astype(BF16)
    lo_half = lax.broadcasted_iota(jnp.int32, (1, LANES), 1) < SB_HEAD_DIM
    for j in range(w_ref.shape[1] // INPROJ_COLS):
        cols = slice(j * INPROJ_COLS, (j + 1) * INPROJ_COLS)
        res = _dot(h, w_ref[:, cols])
        o_ref[0, :, cols] = res.astype(o_ref.dtype)
        if j == k_chunk:
            for p in range(INPROJ_COLS // LANES):
                pair = slice(p * LANES, (p + 1) * LANES)
                knt_ref[0, pair, :] = _pair_norm(res[:, pair], kg_ref[...], lo_half).T.astype(BF16)


def _inproj(x, mod, g, w, kg, k_col0, b0, after):
    _, s, d = x.shape
    bsz = mod.shape[0]
    n = w.shape[1]
    tm = min(512, s)
    assert INPROJ_COLS == SB_WIDTH and (k_col0 * LANES) % INPROJ_COLS == 0
    kg2 = jnp.tile(kg.reshape(1, SB_HEAD_DIM), (1, 2))
    in_specs = [pl.BlockSpec((1, tm, d), lambda b, i: (b0 + b, i, 0)),
                pl.BlockSpec((1, N_MOD, d), lambda b, i: (b, 0, 0)),
                pl.BlockSpec((1, d), lambda b, i: (0, 0)),
                pl.BlockSpec((d, n), lambda b, i: (0, 0)),
                pl.BlockSpec((1, LANES), lambda b, i: (0, 0))]
    args = [x, mod, g.reshape(1, d), w, kg2]
    body = _run_after(functools.partial(_inproj_kernel, k_chunk=k_col0 * LANES // INPROJ_COLS),
                      in_specs, args, after)
    return pl.pallas_call(
        body,
        grid=(bsz, s // tm),
        in_specs=in_specs,
        out_specs=[pl.BlockSpec((1, tm, n), lambda b, i: (b, i, 0)),
                   pl.BlockSpec((1, SB_WIDTH, tm), lambda b, i: (b, 0, i))],
        out_shape=[jax.ShapeDtypeStruct((bsz, s, n), BF16),
                   jax.ShapeDtypeStruct((bsz, SB_WIDTH, s), BF16)],
        compiler_params=_cparams("parallel", "parallel"),
    )(*args)


def _pair_norm(t, g, lo_half):
    sq = t * t
    s_lo = jnp.sum(jnp.where(lo_half, sq, 0.0), axis=-1, keepdims=True)
    s_hi = jnp.sum(jnp.where(lo_half, 0.0, sq), axis=-1, keepdims=True)
    ms = jnp.where(lo_half, s_lo, s_hi) * (1.0 / SB_HEAD_DIM)
    return t * lax.rsqrt(ms + EPS) * g


def _neg_abs(t):
    bits = lax.bitcast_convert_type(t, jnp.uint32) | jnp.uint32(0x80000000)
    return lax.bitcast_convert_type(bits, F32)


def _split_trunc(t):
    bits = lax.bitcast_convert_type(t, jnp.uint32) & jnp.uint32(0xFFFF0000)
    hi = lax.bitcast_convert_type(bits, F32)
    return hi.astype(BF16), (t - hi).astype(BF16)


def _sb_kernel(q_ref, knt_ref, v_ref, qg_ref, o_ref, va_ref, vb_ref, *, tq):
    qi = pl.program_id(2)
    s = v_ref.shape[1]
    lane = lax.broadcasted_iota(jnp.int32, (1, LANES), 1)
    lo_half = lane < SB_HEAD_DIM

    tk = tq // 2

    @pl.when(qi == 0)
    def _():
        def prep_block(j, c):
            rows = pl.ds(pl.multiple_of(j * tk, tk), tk)
            vb = v_ref[0, rows, :]
            va_ref[rows, :] = jnp.where(lo_half, vb, jnp.zeros_like(vb))
            vb_ref[rows, :] = jnp.where(lo_half, jnp.zeros_like(vb), vb)
            return c
        lax.fori_loop(0, s // tk, prep_block, 0)

    scale = SB_HEAD_DIM ** -0.5 * LOG2_E
    q = _pair_norm(q_ref[0].astype(F32), qg_ref[...], lo_half) * scale
    q_heads = (jnp.where(lo_half, q, 0.0).astype(BF16), jnp.where(lo_half, 0.0, q).astype(BF16))
    v_heads = (va_ref, vb_ref)

    row = lax.broadcasted_iota(jnp.int32, (tk, tk), 0)
    col = lax.broadcasted_iota(jnp.int32, (tk, tk), 1)
    strict = col < row
    tri = (row >= col).astype(BF16)
    tri2 = jnp.concatenate([tri, tri], axis=0)

    def sweep(streams):
        cols = [[pl.ds(pl.multiple_of(j * tk, tk), tk) for j, _, _ in blocks]
                for _, _, _, blocks in streams]
        z = [[[_dot(qh[h], knt_ref[0, :, c]) for c in cols[i]] for h in range(2)]
             for i, (qh, _, _, _) in enumerate(streams)]
        cs = []
        for i, (_, _, _, blocks) in enumerate(streams):
            cs.append([[None] * len(blocks) for _ in range(2)])
            for h in range(2):
                for b, (_, masked, _) in enumerate(blocks):
                    zb = z[i][h][b]
                    sp = jnp.maximum(zb, 0.0) + jnp.log2(1.0 + jnp.exp2(_neg_abs(zb)))
                    if masked:
                        sp = jnp.where(strict, sp, 0.0)
                    cs[i][h][b] = _dot(jnp.concatenate(_split_trunc(sp), axis=1), tri2)
        results = []
        for i, (_, runs, acc, blocks) in enumerate(streams):
            runs = list(runs)
            for h in range(2):
                for b, (_, masked, gate) in enumerate(blocks):
                    a = jnp.exp2(z[i][h][b] - cs[i][h][b] - runs[h])
                    if masked:
                        a = jnp.where(strict, a, 0.0)
                    vb = v_heads[h][cols[i][b], :]
                    step = cs[i][h][b][:, 0:1]
                    if gate is not None:
                        vb = jnp.where(gate, vb, jnp.zeros_like(vb))
                        step = jnp.where(gate, step, 0.0)
                    acc = acc + _dot(a.astype(BF16), vb)
                    runs[h] = runs[h] + step
            results.append((tuple(runs), acc))
        return results

    zero_run = jnp.zeros((tk, 1), F32)
    zero_acc = jnp.zeros((tk, LANES), F32)
    left, right = 2 * qi, 2 * qi + 1
    q_top = tuple(qh[:tk] for qh in q_heads)
    q_bot = tuple(qh[tk:] for qh in q_heads)
    top, bot = sweep([
        (q_top, (zero_run, zero_run), zero_acc,
         [(left, True, None), (jnp.maximum(left - 1, 0), False, qi > 0)]),
        (q_bot, (zero_run, zero_run), zero_acc, [(right, True, None), (left, False, None)])])

    def min_run(rt, rb):
        return jnp.min(jnp.minimum(jnp.minimum(rt[0], rt[1]), jnp.minimum(rb[0], rb[1])))

    def alive(carry):
        j, _, _, low = carry
        return jnp.logical_and(j >= 0, low < SB_DEAD_LOG2)

    def earlier_block(carry):
        j, (rt, at), (rb, ab), _ = carry
        t, b = sweep([(q_top, rt, at, [(jnp.maximum(j - 1, 0), False, j > 0)]),
                      (q_bot, rb, ab, [(j, False, None)])])
        return j - 1, t, b, min_run(t[0], b[0])

    _, top, bot, _ = lax.while_loop(alive, earlier_block,
                                    (left - 1, top, bot, min_run(top[0], bot[0])))
    o_ref[0] = jnp.concatenate([top[1], bot[1]], axis=0).astype(o_ref.dtype)


def _sb_attention(proj, knt, qg, col0, after):
    bsz, s, _ = proj.shape
    tq = min(512, s)
    npair = SB_WIDTH // LANES
    qg2 = jnp.tile(qg.reshape(1, SB_HEAD_DIM), (1, 2))
    in_specs = [pl.BlockSpec((1, tq, LANES), lambda b, h, i: (b, i, col0 + h)),
                pl.BlockSpec((1, LANES, s), lambda b, h, i: (b, h, 0)),
                pl.BlockSpec((1, s, LANES), lambda b, h, i: (b, 0, col0 + 2 * npair + h)),
                pl.BlockSpec((1, LANES), lambda b, h, i: (0, 0))]
    args = [proj, knt, proj, qg2]
    body = _run_after(functools.partial(_sb_kernel, tq=tq), in_specs, args, after)
    return pl.pallas_call(
        body,
        grid=(bsz, npair, s // tq),
        in_specs=in_specs,
        out_specs=pl.BlockSpec((1, tq, LANES), lambda b, h, i: (b, i, h)),
        out_shape=jax.ShapeDtypeStruct((bsz, s, SB_WIDTH), BF16),
        scratch_shapes=[pltpu.VMEM((s, LANES), BF16),
                        pltpu.VMEM((s, LANES), BF16)],
        compiler_params=_cparams("parallel", "parallel", "arbitrary"),
    )(*args)


HG_CHUNKS_PER_STEP = 4


def _hgrn_kernel(f_ref, i_ref, q_ref, g_ref, lbl_ref, ng_ref, o_ref, st_ref, *, layer):
    ts = f_ref.shape[1]
    c = HG_CHUNK
    dh = HG_HEAD_DIM

    @pl.when(pl.program_id(1) == 0)
    def _():
        st_ref[...] = jnp.zeros_like(st_ref)

    lg = lbl_ref[...]
    e = jnp.exp(lg - jnp.max(lg, axis=0, keepdims=True))
    lb = jnp.sum(e[:layer + 1], axis=0, keepdims=True) / jnp.sum(e, axis=0, keepdims=True)

    row = lax.broadcasted_iota(jnp.int32, (c, c), 0)
    col = lax.broadcasted_iota(jnp.int32, (c, c), 1)
    causal = col <= row
    lower = causal.astype(BF16)
    width = f_ref.shape[2]

    def body(it, carry):
        states = [st_ref[h] for h in range(HG_HEADS)]
        for u in range(HG_CHUNKS_PER_STEP):
            rows = pl.ds(pl.multiple_of((it * HG_CHUNKS_PER_STEP + u) * c, c), c)
            forget = lb + (1.0 - lb) * jax.nn.sigmoid(f_ref[0, rows, :].astype(F32))
            kc = 1.0 - forget
            hi, lo = _split_trunc(jnp.log(forget))
            cum = _dot(lower, jnp.concatenate([hi, lo], axis=1))
            gc = cum[:, :width] + cum[:, width:]
            g_mid = gc[c // 2 - 1:c // 2, :]
            g_last = gc[c - 1:c, :]
            qe = _silu(q_ref[0, rows, :].astype(F32)) * jnp.exp(gc - g_mid)
            ke = kc * jnp.exp(g_mid - gc)
            qg = (qe * jnp.exp(g_mid)).astype(BF16)
            kd = (ke * jnp.exp(g_last - g_mid)).astype(BF16)
            qe = qe.astype(BF16)
            ke = ke.astype(BF16)
            decay = jnp.exp(g_last)
            v = i_ref[0, rows, :]
            v_t = v.astype(F32).T.astype(BF16)
            outs = []
            for h in range(HG_HEADS):
                sl = slice(h * dh, (h + 1) * dh)
                scores = _dot_nt(qe[:, sl], ke[:, sl])
                o = (_dot(jnp.where(causal, scores, 0.0).astype(BF16), v[:, sl])
                     + _dot_nt(qg[:, sl], states[h].astype(BF16)))
                outs.append(o * lax.rsqrt(jnp.mean(o * o, axis=-1, keepdims=True) + EPS))
                states[h] = states[h] * decay[:, sl] + _dot(v_t[sl, :], kd[:, sl])
            y = jnp.concatenate(outs, axis=1) * ng_ref[...] * _silu(g_ref[0, rows, :].astype(F32))
            o_ref[0, rows, :] = y.astype(o_ref.dtype)
        for h in range(HG_HEADS):
            st_ref[h] = states[h]
        return carry

    lax.fori_loop(0, ts // (c * HG_CHUNKS_PER_STEP), body, 0)


def _hgrn(proj, lb_logits, norm_g, col0, layer):
    bsz, s, _ = proj.shape
    nl = lb_logits.shape[0]
    ts = min(1024, s)
    blk0 = col0 * LANES // HG_WIDTH
    spec = lambda off: pl.BlockSpec((1, ts, HG_WIDTH), lambda b, i: (b, i, blk0 + off))
    return pl.pallas_call(
        functools.partial(_hgrn_kernel, layer=layer),
        grid=(bsz, s // ts),
        in_specs=[spec(0), spec(1), spec(2), spec(3),
                  pl.BlockSpec((nl, HG_WIDTH), lambda b, i: (0, 0)),
                  pl.BlockSpec((1, HG_WIDTH), lambda b, i: (0, 0))],
        out_specs=pl.BlockSpec((1, ts, HG_WIDTH), lambda b, i: (b, i, 0)),
        out_shape=jax.ShapeDtypeStruct((bsz, s, HG_WIDTH), BF16),
        scratch_shapes=[pltpu.VMEM((HG_HEADS, HG_HEAD_DIM, HG_HEAD_DIM), F32)],
        compiler_params=_cparams("parallel", "arbitrary"),
    )(proj, proj, proj, proj, lb_logits, jnp.tile(norm_g.reshape(1, HG_HEAD_DIM), (1, HG_HEADS)))


def _pack_halves(t):
    bits = lax.bitcast_convert_type(t.astype(BF16).astype(F32), jnp.uint32)
    w = t.shape[1] // 2
    return lax.bitcast_convert_type(bits[:, :w] | (bits[:, w:] >> 16), jnp.int32)


def _unpack_halves(p):
    u = lax.bitcast_convert_type(p, jnp.uint32)
    return (lax.bitcast_convert_type(u & jnp.uint32(0xFFFF0000), F32),
            lax.bitcast_convert_type(u << 16, F32))


def _store_planes(ref, rows, t):
    p = _pack_halves(t)
    q = p.shape[1] // 2
    ref[0, rows, :] = p[:, :q]
    ref[1, rows, :] = p[:, q:]


def _load_planes(p0, p1):
    a0, b0 = _unpack_halves(p0)
    a1, b1 = _unpack_halves(p1)
    return jnp.concatenate([a0, a1, b0, b1], axis=1)


def _swiglu_hidden(h, w_gu):
    gu = _dot(h, w_gu)
    hid = w_gu.shape[1] // 2
    return _silu(gu[:, :hid]) * gu[:, hid:]


def _merge_kernel(x_ref, osb_ref, ohg_ref, gsb_ref, ghg_ref, mod_ref, g2_ref, wsb_ref, whg_ref,
                  wout_ref, wrh_ref, wrl_ref, wsgu_ref, wsd_ref, x2_ref, h2p_ref, lg_ref):
    m_sb = _dot(osb_ref[0], wsb_ref[...])
    m_hg = _dot(ohg_ref[0], whg_ref[...])
    merged = (jax.nn.sigmoid(gsb_ref[0].astype(F32)) * m_sb
              + jax.nn.sigmoid(ghg_ref[0].astype(F32)) * m_hg)
    x1 = x_ref[0] + mod_ref[0, 2:3, :] * _dot(merged.astype(BF16), wout_ref[...])
    h2 = _modulated_norm(x1, g2_ref[...], mod_ref[0, 3:4, :], mod_ref[0, 4:5, :])
    hi, lo = _split_bf16(h2)
    _store_planes(h2p_ref, slice(None), h2)
    lg_ref[...] = _dot_nt(wrh_ref[...], hi) + _dot_nt(wrh_ref[...], lo) + _dot_nt(wrl_ref[...], hi)
    shared = _dot(_swiglu_hidden(hi, wsgu_ref[...]).astype(BF16), wsd_ref[...])
    x2_ref[0] = x1 + mod_ref[0, 5:6, :] * shared


def _merge(x, o_sb, o_hg, proj, mod, g2, w_sb, w_hg, w_out, wr_hi, wr_lo, ws_gu, ws_d, b0, after):
    _, s, d = x.shape
    bsz = mod.shape[0]
    tm = min(512, s)
    ns = s // tm
    full = lambda shape: pl.BlockSpec(shape, lambda b, i: (0,) * len(shape))
    in_specs = [pl.BlockSpec((1, tm, d), lambda b, i: (b0 + b, i, 0)),
                pl.BlockSpec((1, tm, SB_WIDTH), lambda b, i: (b, i, 0)),
                pl.BlockSpec((1, tm, HG_WIDTH), lambda b, i: (b, i, 0)),
                pl.BlockSpec((1, tm, d), lambda b, i: (b, i, 0)),
                pl.BlockSpec((1, tm, d), lambda b, i: (b, i, 1)),
                pl.BlockSpec((1, N_MOD, d), lambda b, i: (b, 0, 0)),
                full((1, d)), full(w_sb.shape), full(w_hg.shape), full(w_out.shape),
                full(wr_hi.shape), full(wr_lo.shape), full(ws_gu.shape), full(ws_d.shape)]
    args = [x, o_sb, o_hg, proj, proj, mod, g2.reshape(1, d), w_sb, w_hg, w_out, wr_hi, wr_lo,
            ws_gu, ws_d]
    body = _run_after(_merge_kernel, in_specs, args, after)
    return pl.pallas_call(
        body,
        grid=(bsz, ns),
        in_specs=in_specs,
        out_specs=[pl.BlockSpec((1, tm, d), lambda b, i: (b, i, 0)),
                   pl.BlockSpec((2, tm, d // 4), lambda b, i: (0, b * ns + i, 0)),
                   pl.BlockSpec((N_EXPERTS, tm), lambda b, i: (0, b * ns + i))],
        out_shape=[jax.ShapeDtypeStruct((bsz, s, d), F32),
                   jax.ShapeDtypeStruct((2, bsz * s, d // 4), jnp.int32),
                   jax.ShapeDtypeStruct((N_EXPERTS, bsz * s), F32)],
        compiler_params=_cparams("parallel", "parallel"),
    )(*args)


def _first_argmax(vals, idx, sentinel):
    m = jnp.max(vals, axis=0, keepdims=True)
    first = jnp.min(jnp.where(vals == m, idx, sentinel), axis=0, keepdims=True)
    return m, first


def _route_kernel(lg_ref, bias_ref, gates_ref, rank_ref, cnt_ref, run_ref):
    tn = lg_ref.shape[1]

    @pl.when(pl.program_id(0) == 0)
    def _():
        run_ref[...] = jnp.zeros_like(run_ref)

    neg = -jnp.inf
    scores = jax.nn.sigmoid(lg_ref[...])
    choice = scores + bias_ref[...]

    gidx = lax.broadcasted_iota(jnp.int32, (GROUP_SIZE, tn), 0)
    group_rows = []
    for g in range(N_GROUPS):
        cg = choice[g * GROUP_SIZE:(g + 1) * GROUP_SIZE, :]
        m1, i1 = _first_argmax(cg, gidx, GROUP_SIZE)
        m2 = jnp.max(jnp.where(gidx == i1, neg, cg), axis=0, keepdims=True)
        group_rows.append(m1 + m2)
    work = jnp.concatenate(group_rows, axis=0)
    ggi = lax.broadcasted_iota(jnp.int32, (N_GROUPS, tn), 0)
    gmask = jnp.zeros((N_GROUPS, tn), F32)
    for _ in range(TOPK_GROUPS):
        _, first = _first_argmax(work, ggi, N_GROUPS)
        pick = ggi == first
        gmask = jnp.where(pick, 1.0, gmask)
        work = jnp.where(pick, neg, work)

    masked = jnp.concatenate(
        [jnp.where(gmask[g:g + 1, :] > 0.0, choice[g * GROUP_SIZE:(g + 1) * GROUP_SIZE, :], neg)
         for g in range(N_GROUPS)], axis=0)
    eidx = lax.broadcasted_iota(jnp.int32, (N_EXPERTS, tn), 0)
    sel = jnp.zeros((N_EXPERTS, tn), F32)
    for _ in range(TOP_K):
        _, first = _first_argmax(masked, eidx, N_EXPERTS)
        pick = eidx == first
        sel = jnp.where(pick, 1.0, sel)
        masked = jnp.where(pick, neg, masked)

    chosen = jnp.where(sel > 0.0, scores, 0.0)
    gates_ref[...] = chosen / jnp.sum(chosen, axis=0, keepdims=True) * ROUTED_SCALE

    r = lax.broadcasted_iota(jnp.int32, (tn, tn), 0)
    c = lax.broadcasted_iota(jnp.int32, (tn, tn), 1)
    local = _dot(sel.astype(BF16), (r < c).astype(BF16))
    run = run_ref[:, 0:1]
    rank_ref[...] = jnp.where(sel > 0.0, run + local, -1.0)
    total = run + jnp.sum(sel, axis=1, keepdims=True)
    run_ref[...] = jnp.broadcast_to(total, run_ref.shape)
    cnt_ref[...] = jnp.broadcast_to(total, cnt_ref.shape)


def _route(logits_t, bias):
    e, t = logits_t.shape
    tn = min(1024, t)
    return pl.pallas_call(
        _route_kernel,
        grid=(t // tn,),
        in_specs=[pl.BlockSpec((e, tn), lambda i: (0, i)),
                  pl.BlockSpec((e, 1), lambda i: (0, 0))],
        out_specs=[pl.BlockSpec((e, tn), lambda i: (0, i)),
                   pl.BlockSpec((e, tn), lambda i: (0, i)),
                   pl.BlockSpec((e, LANES), lambda i: (0, 0))],
        out_shape=[jax.ShapeDtypeStruct((e, t), F32),
                   jax.ShapeDtypeStruct((e, t), F32),
                   jax.ShapeDtypeStruct((e, LANES), F32)],
        scratch_shapes=[pltpu.VMEM((e, LANES), F32)],
        compiler_params=_cparams("arbitrary"),
    )(logits_t, bias.reshape(e, 1))


def _slots_kernel(gates_ref, rank_ref, cnt_ref, slot_ref, gate8_ref, blk_ref, *, rows, n_blocks):
    ne, tn = gates_ref.shape
    cnt = cnt_ref[...]
    nblk = jnp.floor((cnt + (rows - 1.0)) * (1.0 / rows))
    er = lax.broadcasted_iota(jnp.int32, (ne, ne), 0)
    ec = lax.broadcasted_iota(jnp.int32, (ne, ne), 1)
    lower = (ec < er).astype(BF16)
    pad_start = _dot(lower, nblk.astype(BF16))[:, 0:1] * rows
    pad_end = pad_start + nblk[:, 0:1] * rows

    rank = rank_ref[...]
    sel = rank >= 0.0
    slot_e = pad_start + rank
    kidx = _dot(lower, sel.astype(BF16))
    gates = gates_ref[...]
    slot_rows, gate_rows = [], []
    for k in range(TOP_K):
        m = jnp.logical_and(sel, kidx == k)
        slot_rows.append(jnp.sum(jnp.where(m, slot_e, 0.0), axis=0, keepdims=True))
        gate_rows.append(jnp.sum(jnp.where(m, gates, 0.0), axis=0, keepdims=True))
    slot_ref[...] = jnp.concatenate(slot_rows, axis=0).astype(jnp.int32)
    gate8_ref[...] = jnp.concatenate(gate_rows, axis=0).T

    nbp = blk_ref.shape[1]
    bstart = lax.broadcasted_iota(jnp.int32, (1, nbp), 1).astype(F32) * rows
    e_of = jnp.sum((pad_end <= bstart).astype(F32), axis=0, keepdims=True)
    e_of = jnp.minimum(e_of, ne - 1.0)
    eidx = lax.broadcasted_iota(jnp.int32, (ne, nbp), 0).astype(F32)
    valid_e = jnp.clip(cnt[:, 0:1] - (bstart - pad_start), 0.0, rows)
    valid = jnp.sum(jnp.where(eidx == e_of, valid_e, 0.0), axis=0, keepdims=True)
    total = pad_end[ne - 1:ne, :]
    used = bstart < total
    bidx = bstart * (1.0 / rows)
    src = jnp.where(used, bidx, total * (1.0 / rows) - 1.0)
    dst = jnp.where(used, bidx, float(n_blocks))
    blk_ref[...] = jnp.concatenate(
        [e_of, valid, src, dst, jnp.zeros((blk_ref.shape[0] - 4, nbp), F32)], axis=0).astype(jnp.int32)


def _slots(gates_t, rank_t, counts, rows, n_blocks):
    nbp = -(-n_blocks // LANES) * LANES
    e, t = gates_t.shape
    assert t // rows <= 256
    tn = min(1024, t)
    return pl.pallas_call(
        functools.partial(_slots_kernel, rows=rows, n_blocks=n_blocks),
        grid=(t // tn,),
        in_specs=[pl.BlockSpec((e, tn), lambda i: (0, i)),
                  pl.BlockSpec((e, tn), lambda i: (0, i)),
                  pl.BlockSpec((e, LANES), lambda i: (0, 0))],
        out_specs=[pl.BlockSpec((TOP_K, tn), lambda i: (0, i)),
                   pl.BlockSpec((tn, TOP_K), lambda i: (i, 0)),
                   pl.BlockSpec((8, nbp), lambda i: (0, 0))],
        out_shape=[jax.ShapeDtypeStruct((TOP_K, t), jnp.int32),
                   jax.ShapeDtypeStruct((t, TOP_K), F32),
                   jax.ShapeDtypeStruct((8, nbp), jnp.int32)],
        compiler_params=_cparams("arbitrary"),
    )(gates_t, rank_t, counts)


SC_WINDOW = 128


def _sc_mesh():
    return plsc.VectorSubcoreMesh(core_axis_name="core", subcore_axis_name="subcore")


def _sc_gather_rows(table, idx):
    n = idx.shape[1]
    w = table.shape[1]

    @pl.kernel(out_type=jax.ShapeDtypeStruct((n, w), table.dtype), mesh=_sc_mesh())
    def gather(t_hbm, i_hbm, o_hbm):
        def body(i_vmem, o_vmem):
            pltpu.sync_copy(t_hbm.at[i_vmem.at[0]], o_vmem)

        pltpu.emit_pipeline(
            body, grid=(n // SC_WINDOW,),
            in_specs=[pl.BlockSpec((1, SC_WINDOW), lambda i: (0, i))],
            out_specs=[pl.BlockSpec((SC_WINDOW, w), lambda i: (i, 0))],
            core_axis_name=("core", "subcore"),
            dimension_semantics=(pltpu.PARALLEL,),
        )(i_hbm, o_hbm)

    return gather(table, idx)


def _sc_scatter_rows(rows, idx, n_out):
    fan, m = idx.shape
    w = rows.shape[1]

    @pl.kernel(out_type=jax.ShapeDtypeStruct((n_out, w), rows.dtype), mesh=_sc_mesh())
    def scatter(r_hbm, i_hbm, o_hbm):
        def body(r_vmem, i_vmem):
            for k in range(fan):
                pltpu.sync_copy(r_vmem, o_hbm.at[i_vmem.at[k]])

        pltpu.emit_pipeline(
            body, grid=(m // SC_WINDOW,),
            in_specs=[pl.BlockSpec((SC_WINDOW, w), lambda i: (i, 0)),
                      pl.BlockSpec((fan, SC_WINDOW), lambda i: (0, i))],
            out_specs=[],
            core_axis_name=("core", "subcore"),
            dimension_semantics=(pltpu.PARALLEL,),
        )(r_hbm, i_hbm)

    return scatter(rows, idx)


def _gmm_kernel(be_ref, bv_ref, bs_ref, bd_ref, xs_ref, wg_ref, wu_ref, wd_ref, ys_ref, wgu_s, wd_s):
    b = pl.program_id(0)
    valid = bv_ref[b]
    hid = wg_ref.shape[2]

    @pl.when(jnp.logical_or(b == 0, be_ref[b] != be_ref[jnp.maximum(b - 1, 0)]))
    def _():
        wgu_s[:, :hid] = wg_ref[0].astype(BF16)
        wgu_s[:, hid:] = wu_ref[0].astype(BF16)
        wd_s[...] = wd_ref[0].astype(BF16)

    @pl.when(valid > 0)
    def _():
        x = _load_planes(xs_ref[0], xs_ref[1]).astype(BF16)
        y = _dot(_swiglu_hidden(x, wgu_s[...]).astype(BF16), wd_s[...])
        row = lax.broadcasted_iota(jnp.int32, (y.shape[0], 1), 0)
        _store_planes(ys_ref, slice(None), jnp.where(row < valid, y, 0.0))

    @pl.when(valid <= 0)
    def _():
        ys_ref[...] = jnp.zeros_like(ys_ref)


def _gmm(blk, xs, we_gate, we_up, we_down, rows, after):
    _, n_rows, q = xs.shape
    _, d, hid = we_gate.shape
    w_spec = lambda w: pl.BlockSpec((1,) + w.shape[1:], lambda b, be, bv, bs, bd: (be[b], 0, 0))
    in_specs = [pl.BlockSpec((2, rows, q), lambda b, be, bv, bs, bd: (0, bs[b], 0)),
                w_spec(we_gate), w_spec(we_up), w_spec(we_down)]
    args = [xs, we_gate, we_up, we_down]
    n_prefetch = 4
    body = _gmm_kernel
    if after is not None:
        pos = n_prefetch + len(args)
        in_specs.append(pl.BlockSpec(memory_space=pl.ANY))
        args.append(after)
        body = lambda *refs: _gmm_kernel(*refs[:pos], *refs[pos + 1:])
    return pl.pallas_call(
        body,
        grid_spec=pltpu.PrefetchScalarGridSpec(
            num_scalar_prefetch=n_prefetch,
            grid=(n_rows // rows - 1,),
            in_specs=in_specs,
            out_specs=pl.BlockSpec((2, rows, q), lambda b, be, bv, bs, bd: (0, bd[b], 0)),
            scratch_shapes=[pltpu.VMEM((d, 2 * hid), BF16), pltpu.VMEM((hid, d), BF16)]),
        out_shape=jax.ShapeDtypeStruct(xs.shape, jnp.int32),
        compiler_params=_cparams("arbitrary"),
    )(blk[0], blk[1], blk[2], blk[3], *args)


def _combine_kernel(x2_ref, mod_ref, y8_ref, g8_ref, *rest):
    o_ref = rest[-1]
    g8 = g8_ref[...]
    routed = None
    for k in range(TOP_K):
        term = g8[:, k:k + 1] * _load_planes(y8_ref[k, 0], y8_ref[k, 1])
        routed = term if routed is None else routed + term
    o_ref[0] = x2_ref[0] + mod_ref[0, 5:6, :] * routed


def _combine(x2, mod, y8, gate8, b0, total, earlier, after):
    bsz, s, d = x2.shape
    tm = min(256, s)
    ns = s // tm
    in_specs = [pl.BlockSpec((1, tm, d), lambda b, i: (b, i, 0)),
                pl.BlockSpec((1, N_MOD, d), lambda b, i: (b, 0, 0)),
                pl.BlockSpec((TOP_K, 2, tm, d // 4), lambda b, i: (0, 0, b * ns + i, 0)),
                pl.BlockSpec((tm, TOP_K), lambda b, i: (b * ns + i, 0))]
    args = [x2, mod, y8, gate8]
    aliases = {}
    if earlier is not None:
        in_specs.append(pl.BlockSpec(memory_space=pl.ANY))
        args.append(earlier)
        aliases = {len(args) - 1: 0}
    if after is not None:
        in_specs.append(pl.BlockSpec(memory_space=pl.ANY))
        args.append(after)
    return pl.pallas_call(
        _combine_kernel,
        grid=(bsz, ns),
        in_specs=in_specs,
        out_specs=pl.BlockSpec((1, tm, d), lambda b, i: (b0 + b, i, 0)),
        out_shape=jax.ShapeDtypeStruct((total, s, d), F32),
        input_output_aliases=aliases,
        compiler_params=_cparams("parallel", "parallel"),
    )(*args)


def kernel(x, c, w_ada, b_ada, norm1_g, w_in, sb_q_norm_g, sb_k_norm_g, hg_lb_logits, hg_norm_g,
           w_branch_sb, w_branch_hg, w_out, norm2_g, w_router, router_bias, w_e_gate, w_e_up,
           w_e_down, w_s_gate, w_s_up, w_s_down):
    bsz, s, d = x.shape
    depth = w_ada.shape[0]
    n_gate_cols = 2 * d
    qkv_col0 = n_gate_cols // LANES
    hg_col0 = qkv_col0 + 3 * SB_WIDTH // LANES
    for l in range(depth):
        n_mix = 3 * SB_WIDTH + 4 * HG_WIDTH
        w_in_l = jnp.concatenate([w_in[l][:, n_mix:], w_in[l][:, :n_mix]], axis=1).astype(BF16)
        wr_t = w_router[l].T
        wr_hi = wr_t.astype(BF16)
        wr_lo = (wr_t - wr_hi.astype(F32)).astype(BF16)
        ws_gu = jnp.concatenate([w_s_gate[l], w_s_up[l]], axis=1).astype(BF16)

        w_sb, w_hg = w_branch_sb[l].astype(BF16), w_branch_hg[l].astype(BF16)
        w_o, ws_d = w_out[l].astype(BF16), w_s_down[l].astype(BF16)

        mod = _ada(c, w_ada[l], b_ada[l]).reshape(bsz, N_MOD, d)
        n_parts = BATCH_PARTS if bsz % BATCH_PARTS == 0 else 1
        pb = bsz // n_parts
        t = pb * s
        q = d // 4
        n_blocks = -(-(t * TOP_K + N_EXPERTS * (DISPATCH_ROWS - 1)) // DISPATCH_ROWS)
        n_rows = (n_blocks + 1) * DISPATCH_ROWS
        plane_off = jnp.array([0, n_rows], jnp.int32)[None, :, None]

        def experts(st, after):
            ys = _gmm(st["blk"], st["xs"].reshape(2, n_rows, q), w_e_gate[l], w_e_up[l],
                      w_e_down[l], DISPATCH_ROWS, after)
            y8 = _sc_gather_rows(ys.reshape(2 * n_rows, q), st["row_idx"].reshape(1, TOP_K * 2 * t))
            return ys, y8.reshape(TOP_K, 2, t, q)

        out = None
        prev = None
        for p in range(n_parts):
            b0 = p * pb
            mod_p = mod[b0:b0 + pb]
            proj, knt = _inproj(x, mod_p, norm1_g[l], w_in_l, sb_k_norm_g[l],
                                qkv_col0 + SB_WIDTH // LANES, b0, prev and prev["blk"])
            o_hg = _hgrn(proj, hg_lb_logits, hg_norm_g[l], hg_col0, l)
            if prev:
                prev["ys"], prev["y8"] = experts(prev, o_hg)
            o_sb = _sb_attention(proj, knt, sb_q_norm_g[l], qkv_col0, prev and prev["ys"])
            if prev:
                out = _combine(prev["x2"], prev["mod"], prev["y8"], prev["gate8"], prev["b0"], bsz,
                               out, o_sb)
            x2, h2p, logits_t = _merge(x, o_sb, o_hg, proj, mod_p, norm2_g[l], w_sb, w_hg, w_o,
                                       wr_hi, wr_lo, ws_gu, ws_d, b0, out)
            gates_t, rank_t, counts = _route(logits_t, router_bias[l])
            slot8, gate8, blk = _slots(gates_t, rank_t, counts, DISPATCH_ROWS, n_blocks)
            row_idx = (slot8[:, None, :] + plane_off).reshape(TOP_K, 2 * t)
            xs = _sc_scatter_rows(h2p.reshape(2 * t, q), row_idx, 2 * n_rows)
            prev = dict(blk=blk, xs=xs, row_idx=row_idx, x2=x2, mod=mod_p, gate8=gate8, b0=b0)
        _, y8 = experts(prev, None)
        x = _combine(prev["x2"], prev["mod"], y8, prev["gate8"], prev["b0"], bsz, out, None)
    return x
```

```python
import functools

import jax
import jax.numpy as jnp
from jax import lax
from jax.experimental import pallas as pl
from jax.experimental.pallas import tpu as pltpu
from jax.experimental.pallas import tpu_sc as plsc

F32 = jnp.float32
BF16 = jnp.bfloat16

SB_HEADS = 8
SB_HEAD_DIM = 64
SB_WIDTH = SB_HEADS * SB_HEAD_DIM
HG_HEADS = 4
HG_HEAD_DIM = 128
HG_WIDTH = HG_HEADS * HG_HEAD_DIM
HG_CHUNK = 64
N_EXPERTS = 64
TOP_K = 8
N_GROUPS = 8
TOPK_GROUPS = 4
GROUP_SIZE = N_EXPERTS // N_GROUPS
ROUTED_SCALE = 2.5
DISPATCH_ROWS = 1024
GMM_ROW_GROUPS = 4
BATCH_PARTS = 2
N_MOD = 6
EPS = 1e-6
LOG2_E = 1.4426950408889634
SB_DEAD_LOG2 = 160.0

LANES = 128
VMEM_LIMIT = 56 * 1024 * 1024


def _cparams(*sem):
    return pltpu.CompilerParams(dimension_semantics=sem, vmem_limit_bytes=VMEM_LIMIT)


def _silu(t):
    return t * jax.nn.sigmoid(t)


def _dot(a, b):
    return jnp.dot(a, b, preferred_element_type=F32)


def _run_after(body, in_specs, args, after):
    if after is None:
        return body
    pos = len(args)
    in_specs.append(pl.BlockSpec(memory_space=pl.ANY))
    args.append(after)
    return lambda *refs: body(*refs[:pos], *refs[pos + 1:])


def _dot_nt(a, b):
    return lax.dot_general(a, b, (((1,), (1,)), ((), ())), preferred_element_type=F32)


def _split_bf16(t):
    hi = t.astype(BF16)
    lo = (t - hi.astype(F32)).astype(BF16)
    return hi, lo


def _ada_kernel(c_ref, w_ref, b_ref, o_ref):
    cond = _silu(c_ref[...])
    o_ref[...] = _dot(cond, w_ref[...]) + b_ref[...]


def _ada(c, w, b):
    bsz, d = c.shape
    n = w.shape[1]
    tn = 1024
    return pl.pallas_call(
        _ada_kernel,
        grid=(n // tn,),
        in_specs=[pl.BlockSpec((bsz, d), lambda j: (0, 0)),
                  pl.BlockSpec((d, tn), lambda j: (0, j)),
                  pl.BlockSpec((1, tn), lambda j: (0, j))],
        out_specs=pl.BlockSpec((bsz, tn), lambda j: (0, j)),
        out_shape=jax.ShapeDtypeStruct((bsz, n), F32),
        compiler_params=_cparams("parallel"),
    )(c, w, b.reshape(1, n))


def _modulated_norm(x, g, shift, scale):
    y = x * lax.rsqrt(jnp.mean(x * x, axis=-1, keepdims=True) + EPS) * g
    return y * (1.0 + scale) + shift


INPROJ_COLS = 512


def _inproj_kernel(x_ref, mod_ref, g_ref, w_ref, kg_ref, o_ref, knt_ref, *, k_chunk):
    h = _modulated_norm(x_ref[0], g_ref[...], mod_ref[0, 0:1, :], mod_ref[0, 1:2, :]).astype(BF16)
    lo_half = lax.broadcasted_iota(jnp.int32, (1, LANES), 1) < SB_HEAD_DIM
    for j in range(w_ref.shape[1] // INPROJ_COLS):
        cols = slice(j * INPROJ_COLS, (j + 1) * INPROJ_COLS)
        res = _dot(h, w_ref[:, cols])
        o_ref[0, :, cols] = res.astype(o_ref.dtype)
        if j == k_chunk:
            for p in range(INPROJ_COLS // LANES):
                pair = slice(p * LANES, (p + 1) * LANES)
                knt_ref[0, pair, :] = _pair_norm(res[:, pair], kg_ref[...], lo_half).T.astype(BF16)


def _inproj(x, mod, g, w, kg, k_col0, b0, after):
    _, s, d = x.shape
    bsz = mod.shape[0]
    n = w.shape[1]
    tm = min(512, s)
    assert INPROJ_COLS == SB_WIDTH and (k_col0 * LANES) % INPROJ_COLS == 0
    kg2 = jnp.tile(kg.reshape(1, SB_HEAD_DIM), (1, 2))
    in_specs = [pl.BlockSpec((1, tm, d), lambda b, i: (b0 + b, i, 0)),
                pl.BlockSpec((1, N_MOD, d), lambda b, i: (b, 0, 0)),
                pl.BlockSpec((1, d), lambda b, i: (0, 0)),
                pl.BlockSpec((d, n), lambda b, i: (0, 0)),
                pl.BlockSpec((1, LANES), lambda b, i: (0, 0))]
    args = [x, mod, g.reshape(1, d), w, kg2]
    body = _run_after(functools.partial(_inproj_kernel, k_chunk=k_col0 * LANES // INPROJ_COLS),
                      in_specs, args, after)
    return pl.pallas_call(
        body,
        grid=(bsz, s // tm),
        in_specs=in_specs,
        out_specs=[pl.BlockSpec((1, tm, n), lambda b, i: (b, i, 0)),
                   pl.BlockSpec((1, SB_WIDTH, tm), lambda b, i: (b, 0, i))],
        out_shape=[jax.ShapeDtypeStruct((bsz, s, n), BF16),
                   jax.ShapeDtypeStruct((bsz, SB_WIDTH, s), BF16)],
        compiler_params=_cparams("parallel", "parallel"),
    )(*args)


def _pair_norm(t, g, lo_half):
    sq = t * t
    s_lo = jnp.sum(jnp.where(lo_half, sq, 0.0), axis=-1, keepdims=True)
    s_hi = jnp.sum(jnp.where(lo_half, 0.0, sq), axis=-1, keepdims=True)
    ms = jnp.where(lo_half, s_lo, s_hi) * (1.0 / SB_HEAD_DIM)
    return t * lax.rsqrt(ms + EPS) * g


def _neg_abs(t):
    bits = lax.bitcast_convert_type(t, jnp.uint32) | jnp.uint32(0x80000000)
    return lax.bitcast_convert_type(bits, F32)


def _split_trunc(t):
    bits = lax.bitcast_convert_type(t, jnp.uint32) & jnp.uint32(0xFFFF0000)
    hi = lax.bitcast_convert_type(bits, F32)
    return hi.astype(BF16), (t - hi).astype(BF16)


def _sb_kernel(q_ref, knt_ref, v_ref, qg_ref, o_ref, va_ref, vb_ref, *, tq):
    qi = pl.program_id(2)
    s = v_ref.shape[1]
    lane = lax.broadcasted_iota(jnp.int32, (1, LANES), 1)
    lo_half = lane < SB_HEAD_DIM

    tk = tq // 2

    @pl.when(qi == 0)
    def _():
        def prep_block(j, c):
            rows = pl.ds(pl.multiple_of(j * tk, tk), tk)
            vb = v_ref[0, rows, :]
            va_ref[rows, :] = jnp.where(lo_half, vb, jnp.zeros_like(vb))
            vb_ref[rows, :] = jnp.where(lo_half, jnp.zeros_like(vb), vb)
            return c
        lax.fori_loop(0, s // tk, prep_block, 0)

    scale = SB_HEAD_DIM ** -0.5 * LOG2_E
    q = _pair_norm(q_ref[0].astype(F32), qg_ref[...], lo_half) * scale
    q_heads = (jnp.where(lo_half, q, 0.0).astype(BF16), jnp.where(lo_half, 0.0, q).astype(BF16))
    v_heads = (va_ref, vb_ref)

    row = lax.broadcasted_iota(jnp.int32, (tk, tk), 0)
    col = lax.broadcasted_iota(jnp.int32, (tk, tk), 1)
    strict = col < row
    tri = (row >= col).astype(BF16)
    tri2 = jnp.concatenate([tri, tri], axis=0)

    def sweep(streams):
        cols = [[pl.ds(pl.multiple_of(j * tk, tk), tk) for j, _, _ in blocks]
                for _, _, _, blocks in streams]
        z = [[[_dot(qh[h], knt_ref[0, :, c]) for c in cols[i]] for h in range(2)]
             for i, (qh, _, _, _) in enumerate(streams)]
        cs = []
        for i, (_, _, _, blocks) in enumerate(streams):
            cs.append([[None] * len(blocks) for _ in range(2)])
            for h in range(2):
                for b, (_, masked, _) in enumerate(blocks):
                    zb = z[i][h][b]
                    sp = jnp.maximum(zb, 0.0) + jnp.log2(1.0 + jnp.exp2(_neg_abs(zb)))
                    if masked:
                        sp = jnp.where(strict, sp, 0.0)
                    cs[i][h][b] = _dot(jnp.concatenate(_split_trunc(sp), axis=1), tri2)
        results = []
        for i, (_, runs, acc, blocks) in enumerate(streams):
            runs = list(runs)
            for h in range(2):
                for b, (_, masked, gate) in enumerate(blocks):
                    a = jnp.exp2(z[i][h][b] - cs[i][h][b] - runs[h])
                    if masked:
                        a = jnp.where(strict, a, 0.0)
                    vb = v_heads[h][cols[i][b], :]
                    step = cs[i][h][b][:, 0:1]
                    if gate is not None:
                        vb = jnp.where(gate, vb, jnp.zeros_like(vb))
                        step = jnp.where(gate, step, 0.0)
                    acc = acc + _dot(a.astype(BF16), vb)
                    runs[h] = runs[h] + step
            results.append((tuple(runs), acc))
        return results

    zero_run = jnp.zeros((tk, 1), F32)
    zero_acc = jnp.zeros((tk, LANES), F32)
    left, right = 2 * qi, 2 * qi + 1
    q_top = tuple(qh[:tk] for qh in q_heads)
    q_bot = tuple(qh[tk:] for qh in q_heads)
    top, bot = sweep([
        (q_top, (zero_run, zero_run), zero_acc,
         [(left, True, None), (jnp.maximum(left - 1, 0), False, qi > 0)]),
        (q_bot, (zero_run, zero_run), zero_acc, [(right, True, None), (left, False, None)])])

    def min_run(rt, rb):
        return jnp.min(jnp.minimum(jnp.minimum(rt[0], rt[1]), jnp.minimum(rb[0], rb[1])))

    def alive(carry):
        j, _, _, low = carry
        return jnp.logical_and(j >= 0, low < SB_DEAD_LOG2)

    def earlier_block(carry):
        j, (rt, at), (rb, ab), _ = carry
        t, b = sweep([(q_top, rt, at, [(jnp.maximum(j - 1, 0), False, j > 0)]),
                      (q_bot, rb, ab, [(j, False, None)])])
        return j - 1, t, b, min_run(t[0], b[0])

    _, top, bot, _ = lax.while_loop(alive, earlier_block,
                                    (left - 1, top, bot, min_run(top[0], bot[0])))
    o_ref[0] = jnp.concatenate([top[1], bot[1]], axis=0).astype(o_ref.dtype)


def _sb_attention(proj, knt, qg, col0, after):
    bsz, s, _ = proj.shape
    tq = min(512, s)
    npair = SB_WIDTH // LANES
    qg2 = jnp.tile(qg.reshape(1, SB_HEAD_DIM), (1, 2))
    in_specs = [pl.BlockSpec((1, tq, LANES), lambda b, h, i: (b, i, col0 + h)),
                pl.BlockSpec((1, LANES, s), lambda b, h, i: (b, h, 0)),
                pl.BlockSpec((1, s, LANES), lambda b, h, i: (b, 0, col0 + 2 * npair + h)),
                pl.BlockSpec((1, LANES), lambda b, h, i: (0, 0))]
    args = [proj, knt, proj, qg2]
    body = _run_after(functools.partial(_sb_kernel, tq=tq), in_specs, args, after)
    return pl.pallas_call(
        body,
        grid=(bsz, npair, s // tq),
        in_specs=in_specs,
        out_specs=pl.BlockSpec((1, tq, LANES), lambda b, h, i: (b, i, h)),
        out_shape=jax.ShapeDtypeStruct((bsz, s, SB_WIDTH), BF16),
        scratch_shapes=[pltpu.VMEM((s, LANES), BF16),
                        pltpu.VMEM((s, LANES), BF16)],
        compiler_params=_cparams("parallel", "parallel", "arbitrary"),
    )(*args)


HG_CHUNKS_PER_STEP = 4


def _hgrn_kernel(f_ref, i_ref, q_ref, g_ref, lbl_ref, ng_ref, o_ref, st_ref, *, layer):
    ts = f_ref.shape[1]
    c = HG_CHUNK
    dh = HG_HEAD_DIM

    @pl.when(pl.program_id(1) == 0)
    def _():
        st_ref[...] = jnp.zeros_like(st_ref)

    lg = lbl_ref[...]
    e = jnp.exp(lg - jnp.max(lg, axis=0, keepdims=True))
    lb = jnp.sum(e[:layer + 1], axis=0, keepdims=True) / jnp.sum(e, axis=0, keepdims=True)

    row = lax.broadcasted_iota(jnp.int32, (c, c), 0)
    col = lax.broadcasted_iota(jnp.int32, (c, c), 1)
    causal = col <= row
    lower = causal.astype(BF16)
    width = f_ref.shape[2]

    def body(it, carry):
        states = [st_ref[h] for h in range(HG_HEADS)]
        for u in range(HG_CHUNKS_PER_STEP):
            rows = pl.ds(pl.multiple_of((it * HG_CHUNKS_PER_STEP + u) * c, c), c)
            forget = lb + (1.0 - lb) * jax.nn.sigmoid(f_ref[0, rows, :].astype(F32))
            kc = 1.0 - forget
            hi, lo = _split_trunc(jnp.log(forget))
            cum = _dot(lower, jnp.concatenate([hi, lo], axis=1))
            gc = cum[:, :width] + cum[:, width:]
            g_mid = gc[c // 2 - 1:c // 2, :]
            g_last = gc[c - 1:c, :]
            qe = _silu(q_ref[0, rows, :].astype(F32)) * jnp.exp(gc - g_mid)
            ke = kc * jnp.exp(g_mid - gc)
            qg = (qe * jnp.exp(g_mid)).astype(BF16)
            kd = (ke * jnp.exp(g_last - g_mid)).astype(BF16)
            qe = qe.astype(BF16)
            ke = ke.astype(BF16)
            decay = jnp.exp(g_last)
            v = i_ref[0, rows, :]
            v_t = v.astype(F32).T.astype(BF16)
            outs = []
            for h in range(HG_HEADS):
                sl = slice(h * dh, (h + 1) * dh)
                scores = _dot_nt(qe[:, sl], ke[:, sl])
                o = (_dot(jnp.where(causal, scores, 0.0).astype(BF16), v[:, sl])
                     + _dot_nt(qg[:, sl], states[h].astype(BF16)))
                outs.append(o * lax.rsqrt(jnp.mean(o * o, axis=-1, keepdims=True) + EPS))
                states[h] = states[h] * decay[:, sl] + _dot(v_t[sl, :], kd[:, sl])
            y = jnp.concatenate(outs, axis=1) * ng_ref[...] * _silu(g_ref[0, rows, :].astype(F32))
            o_ref[0, rows, :] = y.astype(o_ref.dtype)
        for h in range(HG_HEADS):
            st_ref[h] = states[h]
        return carry

    lax.fori_loop(0, ts // (c * HG_CHUNKS_PER_STEP), body, 0)


def _hgrn(proj, lb_logits, norm_g, col0, layer):
    bsz, s, _ = proj.shape
    nl = lb_logits.shape[0]
    ts = min(1024, s)
    blk0 = col0 * LANES // HG_WIDTH
    spec = lambda off: pl.BlockSpec((1, ts, HG_WIDTH), lambda b, i: (b, i, blk0 + off))
    return pl.pallas_call(
        functools.partial(_hgrn_kernel, layer=layer),
        grid=(bsz, s // ts),
        in_specs=[spec(0), spec(1), spec(2), spec(3),
                  pl.BlockSpec((nl, HG_WIDTH), lambda b, i: (0, 0)),
                  pl.BlockSpec((1, HG_WIDTH), lambda b, i: (0, 0))],
        out_specs=pl.BlockSpec((1, ts, HG_WIDTH), lambda b, i: (b, i, 0)),
        out_shape=jax.ShapeDtypeStruct((bsz, s, HG_WIDTH), BF16),
        scratch_shapes=[pltpu.VMEM((HG_HEADS, HG_HEAD_DIM, HG_HEAD_DIM), F32)],
        compiler_params=_cparams("parallel", "arbitrary"),
    )(proj, proj, proj, proj, lb_logits, jnp.tile(norm_g.reshape(1, HG_HEAD_DIM), (1, HG_HEADS)))


def _pack_halves(t):
    bits = lax.bitcast_convert_type(t.astype(BF16).astype(F32), jnp.uint32)
    w = t.shape[1] // 2
    return lax.bitcast_convert_type(bits[:, :w] | (bits[:, w:] >> 16), jnp.int32)


def _unpack_halves(p):
    u = lax.bitcast_convert_type(p, jnp.uint32)
    return (lax.bitcast_convert_type(u & jnp.uint32(0xFFFF0000), F32),
            lax.bitcast_convert_type(u << 16, F32))


def _store_planes(ref, rows, t):
    p = _pack_halves(t)
    q = p.shape[1] // 2
    ref[0, rows, :] = p[:, :q]
    ref[1, rows, :] = p[:, q:]


def _load_planes(p0, p1):
    a0, b0 = _unpack_halves(p0)
    a1, b1 = _unpack_halves(p1)
    return jnp.concatenate([a0, a1, b0, b1], axis=1)


def _swiglu_hidden(h, w_gu):
    gu = _dot(h, w_gu)
    hid = w_gu.shape[1] // 2
    return _silu(gu[:, :hid]) * gu[:, hid:]


def _merge_kernel(x_ref, osb_ref, ohg_ref, gsb_ref, ghg_ref, mod_ref, g2_ref, wsb_ref, whg_ref,
                  wout_ref, wrh_ref, wrl_ref, wsgu_ref, wsd_ref, x2_ref, h2p_ref, lg_ref):
    m_sb = _dot(osb_ref[0], wsb_ref[...])
    m_hg = _dot(ohg_ref[0], whg_ref[...])
    merged = (jax.nn.sigmoid(gsb_ref[0].astype(F32)) * m_sb
              + jax.nn.sigmoid(ghg_ref[0].astype(F32)) * m_hg)
    x1 = x_ref[0] + mod_ref[0, 2:3, :] * _dot(merged.astype(BF16), wout_ref[...])
    h2 = _modulated_norm(x1, g2_ref[...], mod_ref[0, 3:4, :], mod_ref[0, 4:5, :])
    hi, lo = _split_bf16(h2)
    _store_planes(h2p_ref, slice(None), h2)
    lg_ref[...] = _dot_nt(wrh_ref[...], hi) + _dot_nt(wrh_ref[...], lo) + _dot_nt(wrl_ref[...], hi)
    shared = _dot(_swiglu_hidden(hi, wsgu_ref[...]).astype(BF16), wsd_ref[...])
    x2_ref[0] = x1 + mod_ref[0, 5:6, :] * shared


def _merge(x, o_sb, o_hg, proj, mod, g2, w_sb, w_hg, w_out, wr_hi, wr_lo, ws_gu, ws_d, b0, after):
    _, s, d = x.shape
    bsz = mod.shape[0]
    tm = min(512, s)
    ns = s // tm
    full = lambda shape: pl.BlockSpec(shape, lambda b, i: (0,) * len(shape))
    in_specs = [pl.BlockSpec((1, tm, d), lambda b, i: (b0 + b, i, 0)),
                pl.BlockSpec((1, tm, SB_WIDTH), lambda b, i: (b, i, 0)),
                pl.BlockSpec((1, tm, HG_WIDTH), lambda b, i: (b, i, 0)),
                pl.BlockSpec((1, tm, d), lambda b, i: (b, i, 0)),
                pl.BlockSpec((1, tm, d), lambda b, i: (b, i, 1)),
                pl.BlockSpec((1, N_MOD, d), lambda b, i: (b, 0, 0)),
                full((1, d)), full(w_sb.shape), full(w_hg.shape), full(w_out.shape),
                full(wr_hi.shape), full(wr_lo.shape), full(ws_gu.shape), full(ws_d.shape)]
    args = [x, o_sb, o_hg, proj, proj, mod, g2.reshape(1, d), w_sb, w_hg, w_out, wr_hi, wr_lo,
            ws_gu, ws_d]
    body = _run_after(_merge_kernel, in_specs, args, after)
    return pl.pallas_call(
        body,
        grid=(bsz, ns),
        in_specs=in_specs,
        out_specs=[pl.BlockSpec((1, tm, d), lambda b, i: (b, i, 0)),
                   pl.BlockSpec((2, tm, d // 4), lambda b, i: (0, b * ns + i, 0)),
                   pl.BlockSpec((N_EXPERTS, tm), lambda b, i: (0, b * ns + i))],
        out_shape=[jax.ShapeDtypeStruct((bsz, s, d), F32),
                   jax.ShapeDtypeStruct((2, bsz * s, d // 4), jnp.int32),
                   jax.ShapeDtypeStruct((N_EXPERTS, bsz * s), F32)],
        compiler_params=_cparams("parallel", "parallel"),
    )(*args)


def _first_argmax(vals, idx, sentinel):
    m = jnp.max(vals, axis=0, keepdims=True)
    first = jnp.min(jnp.where(vals == m, idx, sentinel), axis=0, keepdims=True)
    return m, first


def _route_kernel(lg_ref, bias_ref, gates_ref, rank_ref, cnt_ref, run_ref):
    tn = lg_ref.shape[1]

    @pl.when(pl.program_id(0) == 0)
    def _():
        run_ref[...] = jnp.zeros_like(run_ref)

    neg = -jnp.inf
    scores = jax.nn.sigmoid(lg_ref[...])
    choice = scores + bias_ref[...]

    gidx = lax.broadcasted_iota(jnp.int32, (GROUP_SIZE, tn), 0)
    group_rows = []
    for g in range(N_GROUPS):
        cg = choice[g * GROUP_SIZE:(g + 1) * GROUP_SIZE, :]
        m1, i1 = _first_argmax(cg, gidx, GROUP_SIZE)
        m2 = jnp.max(jnp.where(gidx == i1, neg, cg), axis=0, keepdims=True)
        group_rows.append(m1 + m2)
    work = jnp.concatenate(group_rows, axis=0)
    ggi = lax.broadcasted_iota(jnp.int32, (N_GROUPS, tn), 0)
    gmask = jnp.zeros((N_GROUPS, tn), F32)
    for _ in range(TOPK_GROUPS):
        _, first = _first_argmax(work, ggi, N_GROUPS)
        pick = ggi == first
        gmask = jnp.where(pick, 1.0, gmask)
        work = jnp.where(pick, neg, work)

    masked = jnp.concatenate(
        [jnp.where(gmask[g:g + 1, :] > 0.0, choice[g * GROUP_SIZE:(g + 1) * GROUP_SIZE, :], neg)
         for g in range(N_GROUPS)], axis=0)
    eidx = lax.broadcasted_iota(jnp.int32, (N_EXPERTS, tn), 0)
    sel = jnp.zeros((N_EXPERTS, tn), F32)
    for _ in range(TOP_K):
        _, first = _first_argmax(masked, eidx, N_EXPERTS)
        pick = eidx == first
        sel = jnp.where(pick, 1.0, sel)
        masked = jnp.where(pick, neg, masked)

    chosen = jnp.where(sel > 0.0, scores, 0.0)
    gates_ref[...] = chosen / jnp.sum(chosen, axis=0, keepdims=True) * ROUTED_SCALE

    r = lax.broadcasted_iota(jnp.int32, (tn, tn), 0)
    c = lax.broadcasted_iota(jnp.int32, (tn, tn), 1)
    local = _dot(sel.astype(BF16), (r < c).astype(BF16))
    run = run_ref[:, 0:1]
    rank_ref[...] = jnp.where(sel > 0.0, run + local, -1.0)
    total = run + jnp.sum(sel, axis=1, keepdims=True)
    run_ref[...] = jnp.broadcast_to(total, run_ref.shape)
    cnt_ref[...] = jnp.broadcast_to(total, cnt_ref.shape)


def _route(logits_t, bias):
    e, t = logits_t.shape
    tn = min(1024, t)
    return pl.pallas_call(
        _route_kernel,
        grid=(t // tn,),
        in_specs=[pl.BlockSpec((e, tn), lambda i: (0, i)),
                  pl.BlockSpec((e, 1), lambda i: (0, 0))],
        out_specs=[pl.BlockSpec((e, tn), lambda i: (0, i)),
                   pl.BlockSpec((e, tn), lambda i: (0, i)),
                   pl.BlockSpec((e, LANES), lambda i: (0, 0))],
        out_shape=[jax.ShapeDtypeStruct((e, t), F32),
                   jax.ShapeDtypeStruct((e, t), F32),
                   jax.ShapeDtypeStruct((e, LANES), F32)],
        scratch_shapes=[pltpu.VMEM((e, LANES), F32)],
        compiler_params=_cparams("arbitrary"),
    )(logits_t, bias.reshape(e, 1))


def _slots_kernel(gates_ref, rank_ref, cnt_ref, slot_ref, gate8_ref, blk_ref, *, rows, n_blocks):
    ne, tn = gates_ref.shape
    cnt = cnt_ref[...]
    nblk = jnp.floor((cnt + (rows - 1.0)) * (1.0 / rows))
    er = lax.broadcasted_iota(jnp.int32, (ne, ne), 0)
    ec = lax.broadcasted_iota(jnp.int32, (ne, ne), 1)
    lower = (ec < er).astype(BF16)
    pad_start = _dot(lower, nblk.astype(BF16))[:, 0:1] * rows
    pad_end = pad_start + nblk[:, 0:1] * rows

    rank = rank_ref[...]
    sel = rank >= 0.0
    slot_e = pad_start + rank
    kidx = _dot(lower, sel.astype(BF16))
    gates = gates_ref[...]
    slot_rows, gate_rows = [], []
    for k in range(TOP_K):
        m = jnp.logical_and(sel, kidx == k)
        slot_rows.append(jnp.sum(jnp.where(m, slot_e, 0.0), axis=0, keepdims=True))
        gate_rows.append(jnp.sum(jnp.where(m, gates, 0.0), axis=0, keepdims=True))
    slot_ref[...] = jnp.concatenate(slot_rows, axis=0).astype(jnp.int32)
    gate8_ref[...] = jnp.concatenate(gate_rows, axis=0).T

    nbp = blk_ref.shape[1]
    bstart = lax.broadcasted_iota(jnp.int32, (1, nbp), 1).astype(F32) * rows
    e_of = jnp.sum((pad_end <= bstart).astype(F32), axis=0, keepdims=True)
    e_of = jnp.minimum(e_of, ne - 1.0)
    eidx = lax.broadcasted_iota(jnp.int32, (ne, nbp), 0).astype(F32)
    valid_e = jnp.clip(cnt[:, 0:1] - (bstart - pad_start), 0.0, rows)
    valid = jnp.sum(jnp.where(eidx == e_of, valid_e, 0.0), axis=0, keepdims=True)
    total = pad_end[ne - 1:ne, :]
    used = bstart < total
    bidx = bstart * (1.0 / rows)
    src = jnp.where(used, bidx, total * (1.0 / rows) - 1.0)
    dst = jnp.where(used, bidx, float(n_blocks))
    blk_ref[...] = jnp.concatenate(
        [e_of, valid, src, dst, jnp.zeros((blk_ref.shape[0] - 4, nbp), F32)], axis=0).astype(jnp.int32)


def _slots(gates_t, rank_t, counts, rows, n_blocks):
    nbp = -(-n_blocks // LANES) * LANES
    e, t = gates_t.shape
    assert t // rows <= 256
    tn = min(1024, t)
    return pl.pallas_call(
        functools.partial(_slots_kernel, rows=rows, n_blocks=n_blocks),
        grid=(t // tn,),
        in_specs=[pl.BlockSpec((e, tn), lambda i: (0, i)),
                  pl.BlockSpec((e, tn), lambda i: (0, i)),
                  pl.BlockSpec((e, LANES), lambda i: (0, 0))],
        out_specs=[pl.BlockSpec((TOP_K, tn), lambda i: (0, i)),
                   pl.BlockSpec((tn, TOP_K), lambda i: (i, 0)),
                   pl.BlockSpec((8, nbp), lambda i: (0, 0))],
        out_shape=[jax.ShapeDtypeStruct((TOP_K, t), jnp.int32),
                   jax.ShapeDtypeStruct((t, TOP_K), F32),
                   jax.ShapeDtypeStruct((8, nbp), jnp.int32)],
        compiler_params=_cparams("arbitrary"),
    )(gates_t, rank_t, counts)


SC_WINDOW = 128


def _sc_mesh():
    return plsc.VectorSubcoreMesh(core_axis_name="core", subcore_axis_name="subcore")


def _sc_gather_rows(table, idx):
    n = idx.shape[1]
    w = table.shape[1]

    @pl.kernel(out_type=jax.ShapeDtypeStruct((n, w), table.dtype), mesh=_sc_mesh())
    def gather(t_hbm, i_hbm, o_hbm):
        def body(i_vmem, o_vmem):
            pltpu.sync_copy(t_hbm.at[i_vmem.at[0]], o_vmem)

        pltpu.emit_pipeline(
            body, grid=(n // SC_WINDOW,),
            in_specs=[pl.BlockSpec((1, SC_WINDOW), lambda i: (0, i))],
            out_specs=[pl.BlockSpec((SC_WINDOW, w), lambda i: (i, 0))],
            core_axis_name=("core", "subcore"),
            dimension_semantics=(pltpu.PARALLEL,),
        )(i_hbm, o_hbm)

    return gather(table, idx)


def _sc_scatter_rows(rows, idx, n_out):
    fan, m = idx.shape
    w = rows.shape[1]

    @pl.kernel(out_type=jax.ShapeDtypeStruct((n_out, w), rows.dtype), mesh=_sc_mesh())
    def scatter(r_hbm, i_hbm, o_hbm):
        def body(r_vmem, i_vmem):
            for k in range(fan):
                pltpu.sync_copy(r_vmem, o_hbm.at[i_vmem.at[k]])

        pltpu.emit_pipeline(
            body, grid=(m // SC_WINDOW,),
            in_specs=[pl.BlockSpec((SC_WINDOW, w), lambda i: (i, 0)),
                      pl.BlockSpec((fan, SC_WINDOW), lambda i: (0, i))],
            out_specs=[],
            core_axis_name=("core", "subcore"),
            dimension_semantics=(pltpu.PARALLEL,),
        )(r_hbm, i_hbm)

    return scatter(rows, idx)


def _gmm_kernel(be_ref, bv_ref, bs_ref, bd_ref, xs_ref, wg_ref, wu_ref, wd_ref, ys_ref, wgu_s, wd_s):
    b = pl.program_id(0)
    valid = bv_ref[b]
    hid = wg_ref.shape[2]

    @pl.when(jnp.logical_or(b == 0, be_ref[b] != be_ref[jnp.maximum(b - 1, 0)]))
    def _():
        wgu_s[:, :hid] = wg_ref[0].astype(BF16)
        wgu_s[:, hid:] = wu_ref[0].astype(BF16)
        wd_s[...] = wd_ref[0].astype(BF16)

    rows = xs_ref.shape[1]
    group = rows // GMM_ROW_GROUPS
    n_groups = (valid + (group - 1)) // group

    def variant(n):
        used = n * group
        x = _load_planes(xs_ref[0, :used, :], xs_ref[1, :used, :]).astype(BF16)
        act = _swiglu_hidden(x, wgu_s[...]).astype(BF16)
        for r0 in range(0, used, group):
            y = _dot(act[r0:r0 + group], wd_s[...])
            row = r0 + lax.broadcasted_iota(jnp.int32, (group, 1), 0)
            _store_planes(ys_ref, slice(r0, r0 + group), jnp.where(row < valid, y, 0.0))
        if used < rows:
            ys_ref[:, used:, :] = jnp.zeros((2, rows - used, ys_ref.shape[2]), ys_ref.dtype)

    for n in range(1, GMM_ROW_GROUPS + 1):
        pl.when(n_groups == n)(functools.partial(variant, n))

    @pl.when(valid <= 0)
    def _():
        ys_ref[...] = jnp.zeros_like(ys_ref)


def _gmm(blk, xs, we_gate, we_up, we_down, rows, after):
    _, n_rows, q = xs.shape
    _, d, hid = we_gate.shape
    w_spec = lambda w: pl.BlockSpec((1,) + w.shape[1:], lambda b, be, bv, bs, bd: (be[b], 0, 0))
    in_specs = [pl.BlockSpec((2, rows, q), lambda b, be, bv, bs, bd: (0, bs[b], 0)),
                w_spec(we_gate), w_spec(we_up), w_spec(we_down)]
    args = [xs, we_gate, we_up, we_down]
    n_prefetch = 4
    body = _gmm_kernel
    if after is not None:
        pos = n_prefetch + len(args)
        in_specs.append(pl.BlockSpec(memory_space=pl.ANY))
        args.append(after)
        body = lambda *refs: _gmm_kernel(*refs[:pos], *refs[pos + 1:])
    return pl.pallas_call(
        body,
        grid_spec=pltpu.PrefetchScalarGridSpec(
            num_scalar_prefetch=n_prefetch,
            grid=(n_rows // rows - 1,),
            in_specs=in_specs,
            out_specs=pl.BlockSpec((2, rows, q), lambda b, be, bv, bs, bd: (0, bd[b], 0)),
            scratch_shapes=[pltpu.VMEM((d, 2 * hid), BF16), pltpu.VMEM((hid, d), BF16)]),
        out_shape=jax.ShapeDtypeStruct(xs.shape, jnp.int32),
        compiler_params=_cparams("arbitrary"),
    )(blk[0], blk[1], blk[2], blk[3], *args)


def _combine_kernel(x2_ref, mod_ref, y8_ref, g8_ref, *rest):
    o_ref = rest[-1]
    g8 = g8_ref[...]
    routed = None
    for k in range(TOP_K):
        term = g8[:, k:k + 1] * _load_planes(y8_ref[k, 0], y8_ref[k, 1])
        routed = term if routed is None else routed + term
    o_ref[0] = x2_ref[0] + mod_ref[0, 5:6, :] * routed


def _combine(x2, mod, y8, gate8, b0, total, earlier, after):
    bsz, s, d = x2.shape
    tm = min(256, s)
    ns = s // tm
    in_specs = [pl.BlockSpec((1, tm, d), lambda b, i: (b, i, 0)),
                pl.BlockSpec((1, N_MOD, d), lambda b, i: (b, 0, 0)),
                pl.BlockSpec((TOP_K, 2, tm, d // 4), lambda b, i: (0, 0, b * ns + i, 0)),
                pl.BlockSpec((tm, TOP_K), lambda b, i: (b * ns + i, 0))]
    args = [x2, mod, y8, gate8]
    aliases = {}
    if earlier is not None:
        in_specs.append(pl.BlockSpec(memory_space=pl.ANY))
        args.append(earlier)
        aliases = {len(args) - 1: 0}
    if after is not None:
        in_specs.append(pl.BlockSpec(memory_space=pl.ANY))
        args.append(after)
    return pl.pallas_call(
        _combine_kernel,
        grid=(bsz, ns),
        in_specs=in_specs,
        out_specs=pl.BlockSpec((1, tm, d), lambda b, i: (b0 + b, i, 0)),
        out_shape=jax.ShapeDtypeStruct((total, s, d), F32),
        input_output_aliases=aliases,
        compiler_params=_cparams("parallel", "parallel"),
    )(*args)


def kernel(x, c, w_ada, b_ada, norm1_g, w_in, sb_q_norm_g, sb_k_norm_g, hg_lb_logits, hg_norm_g,
           w_branch_sb, w_branch_hg, w_out, norm2_g, w_router, router_bias, w_e_gate, w_e_up,
           w_e_down, w_s_gate, w_s_up, w_s_down):
    bsz, s, d = x.shape
    depth = w_ada.shape[0]
    n_gate_cols = 2 * d
    qkv_col0 = n_gate_cols // LANES
    hg_col0 = qkv_col0 + 3 * SB_WIDTH // LANES
    for l in range(depth):
        n_mix = 3 * SB_WIDTH + 4 * HG_WIDTH
        w_in_l = jnp.concatenate([w_in[l][:, n_mix:], w_in[l][:, :n_mix]], axis=1).astype(BF16)
        wr_t = w_router[l].T
        wr_hi = wr_t.astype(BF16)
        wr_lo = (wr_t - wr_hi.astype(F32)).astype(BF16)
        ws_gu = jnp.concatenate([w_s_gate[l], w_s_up[l]], axis=1).astype(BF16)

        w_sb, w_hg = w_branch_sb[l].astype(BF16), w_branch_hg[l].astype(BF16)
        w_o, ws_d = w_out[l].astype(BF16), w_s_down[l].astype(BF16)

        mod = _ada(c, w_ada[l], b_ada[l]).reshape(bsz, N_MOD, d)
        n_parts = BATCH_PARTS if bsz % BATCH_PARTS == 0 else 1
        pb = bsz // n_parts
        t = pb * s
        q = d // 4
        n_blocks = -(-(t * TOP_K + N_EXPERTS * (DISPATCH_ROWS - 1)) // DISPATCH_ROWS)
        n_rows = (n_blocks + 1) * DISPATCH_ROWS
        plane_off = jnp.array([0, n_rows], jnp.int32)[None, :, None]

        def experts(st, after):
            ys = _gmm(st["blk"], st["xs"].reshape(2, n_rows, q), w_e_gate[l], w_e_up[l],
                      w_e_down[l], DISPATCH_ROWS, after)
            y8 = _sc_gather_rows(ys.reshape(2 * n_rows, q), st["row_idx"].reshape(1, TOP_K * 2 * t))
            return ys, y8.reshape(TOP_K, 2, t, q)

        out = None
        prev = None
        for p in range(n_parts):
            b0 = p * pb
            mod_p = mod[b0:b0 + pb]
            proj, knt = _inproj(x, mod_p, norm1_g[l], w_in_l, sb_k_norm_g[l],
                                qkv_col0 + SB_WIDTH // LANES, b0, prev and prev["blk"])
            o_hg = _hgrn(proj, hg_lb_logits, hg_norm_g[l], hg_col0, l)
            if prev:
                prev["ys"], prev["y8"] = experts(prev, o_hg)
            o_sb = _sb_attention(proj, knt, sb_q_norm_g[l], qkv_col0, prev and prev["ys"])
            if prev:
                out = _combine(prev["x2"], prev["mod"], prev["y8"], prev["gate8"], prev["b0"], bsz,
                               out, o_sb)
            x2, h2p, logits_t = _merge(x, o_sb, o_hg, proj, mod_p, norm2_g[l], w_sb, w_hg, w_o,
                                       wr_hi, wr_lo, ws_gu, ws_d, b0, out)
            gates_t, rank_t, counts = _route(logits_t, router_bias[l])
            slot8, gate8, blk = _slots(gates_t, rank_t, counts, DISPATCH_ROWS, n_blocks)
            row_idx = (slot8[:, None, :] + plane_off).reshape(TOP_K, 2 * t)
            xs = _sc_scatter_rows(h2p.reshape(2 * t, q), row_idx, 2 * n_rows)
            prev = dict(blk=blk, xs=xs, row_idx=row_idx, x2=x2, mod=mod_p, gate8=gate8, b0=b0)
        _, y8 = experts(prev, None)
        x = _combine(prev["x2"], prev["mod"], y8, prev["gate8"], prev["b0"], bsz, out, None)
    return x
```

```python
import functools

import jax
import jax.numpy as jnp
from jax import lax
from jax.experimental import pallas as pl
from jax.experimental.pallas import tpu as pltpu
from jax.experimental.pallas import tpu_sc as plsc

F32 = jnp.float32
BF16 = jnp.bfloat16

SB_HEADS = 8
SB_HEAD_DIM = 64
SB_WIDTH = SB_HEADS * SB_HEAD_DIM
HG_HEADS = 4
HG_HEAD_DIM = 128
HG_WIDTH = HG_HEADS * HG_HEAD_DIM
HG_CHUNK = 64
N_EXPERTS = 64
TOP_K = 8
N_GROUPS = 8
TOPK_GROUPS = 4
GROUP_SIZE = N_EXPERTS // N_GROUPS
ROUTED_SCALE = 2.5
DISPATCH_ROWS = 1024
BATCH_PARTS = 2
N_MOD = 6
EPS = 1e-6
LOG2_E = 1.4426950408889634
SB_DEAD_LOG2 = 160.0

LANES = 128
VMEM_LIMIT = 56 * 1024 * 1024


def _cparams(*sem):
    return pltpu.CompilerParams(dimension_semantics=sem, vmem_limit_bytes=VMEM_LIMIT)


def _silu(t):
    return t * jax.nn.sigmoid(t)


def _dot(a, b):
    return jnp.dot(a, b, preferred_element_type=F32)


def _run_after(body, in_specs, args, after):
    if after is None:
        return body
    pos = len(args)
    in_specs.append(pl.BlockSpec(memory_space=pl.ANY))
    args.append(after)
    return lambda *refs: body(*refs[:pos], *refs[pos + 1:])


def _dot_nt(a, b):
    return lax.dot_general(a, b, (((1,), (1,)), ((), ())), preferred_element_type=F32)


def _split_bf16(t):
    hi = t.astype(BF16)
    lo = (t - hi.astype(F32)).astype(BF16)
    return hi, lo


def _ada_kernel(c_ref, w_ref, b_ref, o_ref):
    cond = _silu(c_ref[...])
    o_ref[...] = _dot(cond, w_ref[...]) + b_ref[...]


def _ada(c, w, b):
    bsz, d = c.shape
    n = w.shape[1]
    tn = 1024
    return pl.pallas_call(
        _ada_kernel,
        grid=(n // tn,),
        in_specs=[pl.BlockSpec((bsz, d), lambda j: (0, 0)),
                  pl.BlockSpec((d, tn), lambda j: (0, j)),
                  pl.BlockSpec((1, tn), lambda j: (0, j))],
        out_specs=pl.BlockSpec((bsz, tn), lambda j: (0, j)),
        out_shape=jax.ShapeDtypeStruct((bsz, n), F32),
        compiler_params=_cparams("parallel"),
    )(c, w, b.reshape(1, n))


def _modulated_norm(x, g, shift, scale):
    y = x * lax.rsqrt(jnp.mean(x * x, axis=-1, keepdims=True) + EPS) * g
    return y * (1.0 + scale) + shift


INPROJ_COLS = 512


def _inproj_kernel(x_ref, mod_ref, g_ref, w_ref, kg_ref, o_ref, knt_ref, *, k_chunk):
    h = _modulated_norm(x_ref[0], g_ref[...], mod_ref[0, 0:1, :], mod_ref[0, 1:2, :]).astype(BF16)
    lo_half = lax.broadcasted_iota(jnp.int32, (1, LANES), 1) < SB_HEAD_DIM
    for j in range(w_ref.shape[1] // INPROJ_COLS):
        cols = slice(j * INPROJ_COLS, (j + 1) * INPROJ_COLS)
        res = _dot(h, w_ref[:, cols])
        o_ref[0, :, cols] = res.astype(o_ref.dtype)
        if j == k_chunk:
            for p in range(INPROJ_COLS // LANES):
                pair = slice(p * LANES, (p + 1) * LANES)
                knt_ref[0, pair, :] = _pair_norm(res[:, pair], kg_ref[...], lo_half).T.astype(BF16)


def _inproj(x, mod, g, w, kg, k_col0, b0, after):
    _, s, d = x.shape
    bsz = mod.shape[0]
    n = w.shape[1]
    tm = min(512, s)
    assert INPROJ_COLS == SB_WIDTH and (k_col0 * LANES) % INPROJ_COLS == 0
    kg2 = jnp.tile(kg.reshape(1, SB_HEAD_DIM), (1, 2))
    in_specs = [pl.BlockSpec((1, tm, d), lambda b, i: (b0 + b, i, 0)),
                pl.BlockSpec((1, N_MOD, d), lambda b, i: (b, 0, 0)),
                pl.BlockSpec((1, d), lambda b, i: (0, 0)),
                pl.BlockSpec((d, n), lambda b, i: (0, 0)),
                pl.BlockSpec((1, LANES), lambda b, i: (0, 0))]
    args = [x, mod, g.reshape(1, d), w, kg2]
    body = _run_after(functools.partial(_inproj_kernel, k_chunk=k_col0 * LANES // INPROJ_COLS),
                      in_specs, args, after)
    return pl.pallas_call(
        body,
        grid=(bsz, s // tm),
        in_specs=in_specs,
        out_specs=[pl.BlockSpec((1, tm, n), lambda b, i: (b, i, 0)),
                   pl.BlockSpec((1, SB_WIDTH, tm), lambda b, i: (b, 0, i))],
        out_shape=[jax.ShapeDtypeStruct((bsz, s, n), BF16),
                   jax.ShapeDtypeStruct((bsz, SB_WIDTH, s), BF16)],
        compiler_params=_cparams("parallel", "parallel"),
    )(*args)


def _pair_norm(t, g, lo_half):
    sq = t * t
    s_lo = jnp.sum(jnp.where(lo_half, sq, 0.0), axis=-1, keepdims=True)
    s_hi = jnp.sum(jnp.where(lo_half, 0.0, sq), axis=-1, keepdims=True)
    ms = jnp.where(lo_half, s_lo, s_hi) * (1.0 / SB_HEAD_DIM)
    return t * lax.rsqrt(ms + EPS) * g


def _neg_abs(t):
    bits = lax.bitcast_convert_type(t, jnp.uint32) | jnp.uint32(0x80000000)
    return lax.bitcast_convert_type(bits, F32)


def _split_trunc(t):
    bits = lax.bitcast_convert_type(t, jnp.uint32) & jnp.uint32(0xFFFF0000)
    hi = lax.bitcast_convert_type(bits, F32)
    return hi.astype(BF16), (t - hi).astype(BF16)


def _sb_kernel(q_ref, knt_ref, v_ref, qg_ref, o_ref, va_ref, vb_ref, *, tq):
    qi = pl.program_id(2)
    s = v_ref.shape[1]
    lane = lax.broadcasted_iota(jnp.int32, (1, LANES), 1)
    lo_half = lane < SB_HEAD_DIM

    tk = tq // 2

    @pl.when(qi == 0)
    def _():
        def prep_block(j, c):
            rows = pl.ds(pl.multiple_of(j * tk, tk), tk)
            vb = v_ref[0, rows, :]
            va_ref[rows, :] = jnp.where(lo_half, vb, jnp.zeros_like(vb))
            vb_ref[rows, :] = jnp.where(lo_half, jnp.zeros_like(vb), vb)
            return c
        lax.fori_loop(0, s // tk, prep_block, 0)

    scale = SB_HEAD_DIM ** -0.5 * LOG2_E
    q = _pair_norm(q_ref[0].astype(F32), qg_ref[...], lo_half) * scale
    q_heads = (jnp.where(lo_half, q, 0.0).astype(BF16), jnp.where(lo_half, 0.0, q).astype(BF16))
    v_heads = (va_ref, vb_ref)

    row = lax.broadcasted_iota(jnp.int32, (tk, tk), 0)
    col = lax.broadcasted_iota(jnp.int32, (tk, tk), 1)
    strict = col < row
    tri = (row >= col).astype(BF16)
    tri2 = jnp.concatenate([tri, tri], axis=0)

    def sweep(streams):
        cols = [[pl.ds(pl.multiple_of(j * tk, tk), tk) for j, _, _ in blocks]
                for _, _, _, blocks in streams]
        z = [[[_dot(qh[h], knt_ref[0, :, c]) for c in cols[i]] for h in range(2)]
             for i, (qh, _, _, _) in enumerate(streams)]
        cs = []
        for i, (_, _, _, blocks) in enumerate(streams):
            cs.append([[None] * len(blocks) for _ in range(2)])
            for h in range(2):
                for b, (_, masked, _) in enumerate(blocks):
                    zb = z[i][h][b]
                    sp = jnp.maximum(zb, 0.0) + jnp.log2(1.0 + jnp.exp2(_neg_abs(zb)))
                    if masked:
                        sp = jnp.where(strict, sp, 0.0)
                    cs[i][h][b] = _dot(jnp.concatenate(_split_trunc(sp), axis=1), tri2)
        results = []
        for i, (_, runs, acc, blocks) in enumerate(streams):
            runs = list(runs)
            for h in range(2):
                for b, (_, masked, gate) in enumerate(blocks):
                    a = jnp.exp2(z[i][h][b] - cs[i][h][b] - runs[h])
                    if masked:
                        a = jnp.where(strict, a, 0.0)
                    vb = v_heads[h][cols[i][b], :]
                    step = cs[i][h][b][:, 0:1]
                    if gate is not None:
                        vb = jnp.where(gate, vb, jnp.zeros_like(vb))
                        step = jnp.where(gate, step, 0.0)
                    acc = acc + _dot(a.astype(BF16), vb)
                    runs[h] = runs[h] + step
            results.append((tuple(runs), acc))
        return results

    zero_run = jnp.zeros((tk, 1), F32)
    zero_acc = jnp.zeros((tk, LANES), F32)
    left, right = 2 * qi, 2 * qi + 1
    q_top = tuple(qh[:tk] for qh in q_heads)
    q_bot = tuple(qh[tk:] for qh in q_heads)
    top, bot = sweep([
        (q_top, (zero_run, zero_run), zero_acc,
         [(left, True, None), (jnp.maximum(left - 1, 0), False, qi > 0)]),
        (q_bot, (zero_run, zero_run), zero_acc, [(right, True, None), (left, False, None)])])

    def min_run(rt, rb):
        return jnp.min(jnp.minimum(jnp.minimum(rt[0], rt[1]), jnp.minimum(rb[0], rb[1])))

    def alive(carry):
        j, _, _, low = carry
        return jnp.logical_and(j >= 0, low < SB_DEAD_LOG2)

    def earlier_block(carry):
        j, (rt, at), (rb, ab), _ = carry
        t, b = sweep([(q_top, rt, at, [(jnp.maximum(j - 1, 0), False, j > 0)]),
                      (q_bot, rb, ab, [(j, False, None)])])
        return j - 1, t, b, min_run(t[0], b[0])

    _, top, bot, _ = lax.while_loop(alive, earlier_block,
                                    (left - 1, top, bot, min_run(top[0], bot[0])))
    o_ref[0] = jnp.concatenate([top[1], bot[1]], axis=0).astype(o_ref.dtype)


def _sb_attention(proj, knt, qg, col0, after):
    bsz, s, _ = proj.shape
    tq = min(512, s)
    npair = SB_WIDTH // LANES
    qg2 = jnp.tile(qg.reshape(1, SB_HEAD_DIM), (1, 2))
    in_specs = [pl.BlockSpec((1, tq, LANES), lambda b, h, i: (b, i, col0 + h)),
                pl.BlockSpec((1, LANES, s), lambda b, h, i: (b, h, 0)),
                pl.BlockSpec((1, s, LANES), lambda b, h, i: (b, 0, col0 + 2 * npair + h)),
                pl.BlockSpec((1, LANES), lambda b, h, i: (0, 0))]
    args = [proj, knt, proj, qg2]
    body = _run_after(functools.partial(_sb_kernel, tq=tq), in_specs, args, after)
    return pl.pallas_call(
        body,
        grid=(bsz, npair, s // tq),
        in_specs=in_specs,
        out_specs=pl.BlockSpec((1, tq, LANES), lambda b, h, i: (b, i, h)),
        out_shape=jax.ShapeDtypeStruct((bsz, s, SB_WIDTH), BF16),
        scratch_shapes=[pltpu.VMEM((s, LANES), BF16),
                        pltpu.VMEM((s, LANES), BF16)],
        compiler_params=_cparams("parallel", "parallel", "arbitrary"),
    )(*args)


HG_CHUNKS_PER_STEP = 4


def _hgrn_kernel(f_ref, i_ref, q_ref, g_ref, lbl_ref, ng_ref, o_ref, st_ref, *, layer):
    ts = f_ref.shape[1]
    c = HG_CHUNK
    dh = HG_HEAD_DIM

    @pl.when(pl.program_id(1) == 0)
    def _():
        st_ref[...] = jnp.zeros_like(st_ref)

    lg = lbl_ref[...]
    e = jnp.exp(lg - jnp.max(lg, axis=0, keepdims=True))
    lb = jnp.sum(e[:layer + 1], axis=0, keepdims=True) / jnp.sum(e, axis=0, keepdims=True)

    row = lax.broadcasted_iota(jnp.int32, (c, c), 0)
    col = lax.broadcasted_iota(jnp.int32, (c, c), 1)
    causal = col <= row
    lower = causal.astype(BF16)
    width = f_ref.shape[2]

    def body(it, carry):
        states = [st_ref[h] for h in range(HG_HEADS)]
        for u in range(HG_CHUNKS_PER_STEP):
            rows = pl.ds(pl.multiple_of((it * HG_CHUNKS_PER_STEP + u) * c, c), c)
            forget = lb + (1.0 - lb) * jax.nn.sigmoid(f_ref[0, rows, :].astype(F32))
            kc = 1.0 - forget
            hi, lo = _split_trunc(jnp.log(forget))
            cum = _dot(lower, jnp.concatenate([hi, lo], axis=1))
            gc = cum[:, :width] + cum[:, width:]
            g_mid = gc[c // 2 - 1:c // 2, :]
            g_last = gc[c - 1:c, :]
            qe = _silu(q_ref[0, rows, :].astype(F32)) * jnp.exp(gc - g_mid)
            ke = kc * jnp.exp(g_mid - gc)
            qg = (qe * jnp.exp(g_mid)).astype(BF16)
            kd = (ke * jnp.exp(g_last - g_mid)).astype(BF16)
            qe = qe.astype(BF16)
            ke = ke.astype(BF16)
            decay = jnp.exp(g_last)
            v = i_ref[0, rows, :]
            v_t = v.astype(F32).T.astype(BF16)
            outs = []
            for h in range(HG_HEADS):
                sl = slice(h * dh, (h + 1) * dh)
                scores = _dot_nt(qe[:, sl], ke[:, sl])
                o = (_dot(jnp.where(causal, scores, 0.0).astype(BF16), v[:, sl])
                     + _dot_nt(qg[:, sl], states[h].astype(BF16)))
                outs.append(o * lax.rsqrt(jnp.mean(o * o, axis=-1, keepdims=True) + EPS))
                states[h] = states[h] * decay[:, sl] + _dot(v_t[sl, :], kd[:, sl])
            y = jnp.concatenate(outs, axis=1) * ng_ref[...] * _silu(g_ref[0, rows, :].astype(F32))
            o_ref[0, rows, :] = y.astype(o_ref.dtype)
        for h in range(HG_HEADS):
            st_ref[h] = states[h]
        return carry

    lax.fori_loop(0, ts // (c * HG_CHUNKS_PER_STEP), body, 0)


def _hgrn(proj, lb_logits, norm_g, col0, layer):
    bsz, s, _ = proj.shape
    nl = lb_logits.shape[0]
    ts = min(1024, s)
    blk0 = col0 * LANES // HG_WIDTH
    spec = lambda off: pl.BlockSpec((1, ts, HG_WIDTH), lambda b, i: (b, i, blk0 + off))
    return pl.pallas_call(
        functools.partial(_hgrn_kernel, layer=layer),
        grid=(bsz, s // ts),
        in_specs=[spec(0), spec(1), spec(2), spec(3),
                  pl.BlockSpec((nl, HG_WIDTH), lambda b, i: (0, 0)),
                  pl.BlockSpec((1, HG_WIDTH), lambda b, i: (0, 0))],
        out_specs=pl.BlockSpec((1, ts, HG_WIDTH), lambda b, i: (b, i, 0)),
        out_shape=jax.ShapeDtypeStruct((bsz, s, HG_WIDTH), BF16),
        scratch_shapes=[pltpu.VMEM((HG_HEADS, HG_HEAD_DIM, HG_HEAD_DIM), F32)],
        compiler_params=_cparams("parallel", "arbitrary"),
    )(proj, proj, proj, proj, lb_logits, jnp.tile(norm_g.reshape(1, HG_HEAD_DIM), (1, HG_HEADS)))


def _pack_halves(t):
    bits = lax.bitcast_convert_type(t.astype(BF16).astype(F32), jnp.uint32)
    w = t.shape[1] // 2
    return lax.bitcast_convert_type(bits[:, :w] | (bits[:, w:] >> 16), jnp.int32)


def _unpack_halves(p):
    u = lax.bitcast_convert_type(p, jnp.uint32)
    return (lax.bitcast_convert_type(u & jnp.uint32(0xFFFF0000), F32),
            lax.bitcast_convert_type(u << 16, F32))


def _store_planes(ref, rows, t):
    p = _pack_halves(t)
    q = p.shape[1] // 2
    ref[0, rows, :] = p[:, :q]
    ref[1, rows, :] = p[:, q:]


def _load_planes(p0, p1):
    a0, b0 = _unpack_halves(p0)
    a1, b1 = _unpack_halves(p1)
    return jnp.concatenate([a0, a1, b0, b1], axis=1)


def _swiglu_hidden(h, w_gu):
    gu = _dot(h, w_gu)
    hid = w_gu.shape[1] // 2
    return _silu(gu[:, :hid]) * gu[:, hid:]


def _merge_kernel(x_ref, osb_ref, ohg_ref, gsb_ref, ghg_ref, mod_ref, g2_ref, wsb_ref, whg_ref,
                  wout_ref, wrh_ref, wrl_ref, x1_ref, h2p_ref, lg_ref):
    m_sb = _dot(osb_ref[0], wsb_ref[...])
    m_hg = _dot(ohg_ref[0], whg_ref[...])
    merged = (jax.nn.sigmoid(gsb_ref[0].astype(F32)) * m_sb
              + jax.nn.sigmoid(ghg_ref[0].astype(F32)) * m_hg)
    x1 = x_ref[0] + mod_ref[0, 2:3, :] * _dot(merged.astype(BF16), wout_ref[...])
    x1_ref[0] = x1
    h2 = _modulated_norm(x1, g2_ref[...], mod_ref[0, 3:4, :], mod_ref[0, 4:5, :])
    hi, lo = _split_bf16(h2)
    _store_planes(h2p_ref, slice(None), h2)
    lg_ref[...] = _dot_nt(wrh_ref[...], hi) + _dot_nt(wrh_ref[...], lo) + _dot_nt(wrl_ref[...], hi)


def _merge(x, o_sb, o_hg, proj, mod, g2, w_sb, w_hg, w_out, wr_hi, wr_lo, b0, after):
    _, s, d = x.shape
    bsz = mod.shape[0]
    tm = min(512, s)
    ns = s // tm
    full = lambda shape: pl.BlockSpec(shape, lambda b, i: (0,) * len(shape))
    in_specs = [pl.BlockSpec((1, tm, d), lambda b, i: (b0 + b, i, 0)),
                pl.BlockSpec((1, tm, SB_WIDTH), lambda b, i: (b, i, 0)),
                pl.BlockSpec((1, tm, HG_WIDTH), lambda b, i: (b, i, 0)),
                pl.BlockSpec((1, tm, d), lambda b, i: (b, i, 0)),
                pl.BlockSpec((1, tm, d), lambda b, i: (b, i, 1)),
                pl.BlockSpec((1, N_MOD, d), lambda b, i: (b, 0, 0)),
                full((1, d)), full(w_sb.shape), full(w_hg.shape), full(w_out.shape),
                full(wr_hi.shape), full(wr_lo.shape)]
    args = [x, o_sb, o_hg, proj, proj, mod, g2.reshape(1, d), w_sb, w_hg, w_out, wr_hi, wr_lo]
    body = _run_after(_merge_kernel, in_specs, args, after)
    return pl.pallas_call(
        body,
        grid=(bsz, ns),
        in_specs=in_specs,
        out_specs=[pl.BlockSpec((1, tm, d), lambda b, i: (b, i, 0)),
                   pl.BlockSpec((2, tm, d // 4), lambda b, i: (0, b * ns + i, 0)),
                   pl.BlockSpec((N_EXPERTS, tm), lambda b, i: (0, b * ns + i))],
        out_shape=[jax.ShapeDtypeStruct((bsz, s, d), F32),
                   jax.ShapeDtypeStruct((2, bsz * s, d // 4), jnp.int32),
                   jax.ShapeDtypeStruct((N_EXPERTS, bsz * s), F32)],
        compiler_params=_cparams("parallel", "parallel"),
    )(*args)


def _first_argmax(vals, idx, sentinel):
    m = jnp.max(vals, axis=0, keepdims=True)
    first = jnp.min(jnp.where(vals == m, idx, sentinel), axis=0, keepdims=True)
    return m, first


def _route_kernel(lg_ref, bias_ref, gates_ref, rank_ref, cnt_ref, run_ref):
    tn = lg_ref.shape[1]

    @pl.when(pl.program_id(0) == 0)
    def _():
        run_ref[...] = jnp.zeros_like(run_ref)

    neg = -jnp.inf
    scores = jax.nn.sigmoid(lg_ref[...])
    choice = scores + bias_ref[...]

    gidx = lax.broadcasted_iota(jnp.int32, (GROUP_SIZE, tn), 0)
    group_rows = []
    for g in range(N_GROUPS):
        cg = choice[g * GROUP_SIZE:(g + 1) * GROUP_SIZE, :]
        m1, i1 = _first_argmax(cg, gidx, GROUP_SIZE)
        m2 = jnp.max(jnp.where(gidx == i1, neg, cg), axis=0, keepdims=True)
        group_rows.append(m1 + m2)
    work = jnp.concatenate(group_rows, axis=0)
    ggi = lax.broadcasted_iota(jnp.int32, (N_GROUPS, tn), 0)
    gmask = jnp.zeros((N_GROUPS, tn), F32)
    for _ in range(TOPK_GROUPS):
        _, first = _first_argmax(work, ggi, N_GROUPS)
        pick = ggi == first
        gmask = jnp.where(pick, 1.0, gmask)
        work = jnp.where(pick, neg, work)

    masked = jnp.concatenate(
        [jnp.where(gmask[g:g + 1, :] > 0.0, choice[g * GROUP_SIZE:(g + 1) * GROUP_SIZE, :], neg)
         for g in range(N_GROUPS)], axis=0)
    eidx = lax.broadcasted_iota(jnp.int32, (N_EXPERTS, tn), 0)
    sel = jnp.zeros((N_EXPERTS, tn), F32)
    for _ in range(TOP_K):
        _, first = _first_argmax(masked, eidx, N_EXPERTS)
        pick = eidx == first
        sel = jnp.where(pick, 1.0, sel)
        masked = jnp.where(pick, neg, masked)

    chosen = jnp.where(sel > 0.0, scores, 0.0)
    gates_ref[...] = chosen / jnp.sum(chosen, axis=0, keepdims=True) * ROUTED_SCALE

    r = lax.broadcasted_iota(jnp.int32, (tn, tn), 0)
    c = lax.broadcasted_iota(jnp.int32, (tn, tn), 1)
    local = _dot(sel.astype(BF16), (r < c).astype(BF16))
    run = run_ref[:, 0:1]
    rank_ref[...] = jnp.where(sel > 0.0, run + local, -1.0)
    total = run + jnp.sum(sel, axis=1, keepdims=True)
    run_ref[...] = jnp.broadcast_to(total, run_ref.shape)
    cnt_ref[...] = jnp.broadcast_to(total, cnt_ref.shape)


def _route(logits_t, bias):
    e, t = logits_t.shape
    tn = min(1024, t)
    return pl.pallas_call(
        _route_kernel,
        grid=(t // tn,),
        in_specs=[pl.BlockSpec((e, tn), lambda i: (0, i)),
                  pl.BlockSpec((e, 1), lambda i: (0, 0))],
        out_specs=[pl.BlockSpec((e, tn), lambda i: (0, i)),
                   pl.BlockSpec((e, tn), lambda i: (0, i)),
                   pl.BlockSpec((e, LANES), lambda i: (0, 0))],
        out_shape=[jax.ShapeDtypeStruct((e, t), F32),
                   jax.ShapeDtypeStruct((e, t), F32),
                   jax.ShapeDtypeStruct((e, LANES), F32)],
        scratch_shapes=[pltpu.VMEM((e, LANES), F32)],
        compiler_params=_cparams("arbitrary"),
    )(logits_t, bias.reshape(e, 1))


def _slots_kernel(gates_ref, rank_ref, cnt_ref, slot_ref, gate8_ref, blk_ref, *, rows, n_blocks):
    ne, tn = gates_ref.shape
    cnt = cnt_ref[...]
    nblk = jnp.floor((cnt + (rows - 1.0)) * (1.0 / rows))
    er = lax.broadcasted_iota(jnp.int32, (ne, ne), 0)
    ec = lax.broadcasted_iota(jnp.int32, (ne, ne), 1)
    lower = (ec < er).astype(BF16)
    pad_start = _dot(lower, nblk.astype(BF16))[:, 0:1] * rows
    pad_end = pad_start + nblk[:, 0:1] * rows

    rank = rank_ref[...]
    sel = rank >= 0.0
    slot_e = pad_start + rank
    kidx = _dot(lower, sel.astype(BF16))
    gates = gates_ref[...]
    slot_rows, gate_rows = [], []
    for k in range(TOP_K):
        m = jnp.logical_and(sel, kidx == k)
        slot_rows.append(jnp.sum(jnp.where(m, slot_e, 0.0), axis=0, keepdims=True))
        gate_rows.append(jnp.sum(jnp.where(m, gates, 0.0), axis=0, keepdims=True))
    slot_ref[...] = jnp.concatenate(slot_rows, axis=0).astype(jnp.int32)
    gate8_ref[...] = jnp.concatenate(gate_rows, axis=0).T

    nbp = blk_ref.shape[1]
    bstart = lax.broadcasted_iota(jnp.int32, (1, nbp), 1).astype(F32) * rows
    e_of = jnp.sum((pad_end <= bstart).astype(F32), axis=0, keepdims=True)
    e_of = jnp.minimum(e_of, ne - 1.0)
    eidx = lax.broadcasted_iota(jnp.int32, (ne, nbp), 0).astype(F32)
    valid_e = jnp.clip(cnt[:, 0:1] - (bstart - pad_start), 0.0, rows)
    valid = jnp.sum(jnp.where(eidx == e_of, valid_e, 0.0), axis=0, keepdims=True)
    total = pad_end[ne - 1:ne, :]
    used = bstart < total
    bidx = bstart * (1.0 / rows)
    src = jnp.where(used, bidx, total * (1.0 / rows) - 1.0)
    dst = jnp.where(used, bidx, float(n_blocks))
    blk_ref[...] = jnp.concatenate(
        [e_of, valid, src, dst, jnp.zeros((blk_ref.shape[0] - 4, nbp), F32)], axis=0).astype(jnp.int32)


def _slots(gates_t, rank_t, counts, rows, n_blocks):
    nbp = -(-n_blocks // LANES) * LANES
    e, t = gates_t.shape
    assert t // rows <= 256
    tn = min(1024, t)
    return pl.pallas_call(
        functools.partial(_slots_kernel, rows=rows, n_blocks=n_blocks),
        grid=(t // tn,),
        in_specs=[pl.BlockSpec((e, tn), lambda i: (0, i)),
                  pl.BlockSpec((e, tn), lambda i: (0, i)),
                  pl.BlockSpec((e, LANES), lambda i: (0, 0))],
        out_specs=[pl.BlockSpec((TOP_K, tn), lambda i: (0, i)),
                   pl.BlockSpec((tn, TOP_K), lambda i: (i, 0)),
                   pl.BlockSpec((8, nbp), lambda i: (0, 0))],
        out_shape=[jax.ShapeDtypeStruct((TOP_K, t), jnp.int32),
                   jax.ShapeDtypeStruct((t, TOP_K), F32),
                   jax.ShapeDtypeStruct((8, nbp), jnp.int32)],
        compiler_params=_cparams("arbitrary"),
    )(gates_t, rank_t, counts)


SC_WINDOW = 128


def _sc_mesh():
    return plsc.VectorSubcoreMesh(core_axis_name="core", subcore_axis_name="subcore")


def _sc_gather_rows(table, idx):
    n = idx.shape[1]
    w = table.shape[1]

    @pl.kernel(out_type=jax.ShapeDtypeStruct((n, w), table.dtype), mesh=_sc_mesh())
    def gather(t_hbm, i_hbm, o_hbm):
        def body(i_vmem, o_vmem):
            pltpu.sync_copy(t_hbm.at[i_vmem.at[0]], o_vmem)

        pltpu.emit_pipeline(
            body, grid=(n // SC_WINDOW,),
            in_specs=[pl.BlockSpec((1, SC_WINDOW), lambda i: (0, i))],
            out_specs=[pl.BlockSpec((SC_WINDOW, w), lambda i: (i, 0))],
            core_axis_name=("core", "subcore"),
            dimension_semantics=(pltpu.PARALLEL,),
        )(i_hbm, o_hbm)

    return gather(table, idx)


def _sc_scatter_rows(rows, idx, n_out):
    fan, m = idx.shape
    w = rows.shape[1]

    @pl.kernel(out_type=jax.ShapeDtypeStruct((n_out, w), rows.dtype), mesh=_sc_mesh())
    def scatter(r_hbm, i_hbm, o_hbm):
        def body(r_vmem, i_vmem):
            for k in range(fan):
                pltpu.sync_copy(r_vmem, o_hbm.at[i_vmem.at[k]])

        pltpu.emit_pipeline(
            body, grid=(m // SC_WINDOW,),
            in_specs=[pl.BlockSpec((SC_WINDOW, w), lambda i: (i, 0)),
                      pl.BlockSpec((fan, SC_WINDOW), lambda i: (0, i))],
            out_specs=[],
            core_axis_name=("core", "subcore"),
            dimension_semantics=(pltpu.PARALLEL,),
        )(r_hbm, i_hbm)

    return scatter(rows, idx)


def _gmm_kernel(be_ref, bv_ref, bs_ref, bd_ref, xs_ref, wg_ref, wu_ref, wd_ref, ys_ref, wgu_s, wd_s):
    b = pl.program_id(0)
    valid = bv_ref[b]
    hid = wg_ref.shape[2]

    @pl.when(jnp.logical_or(b == 0, be_ref[b] != be_ref[jnp.maximum(b - 1, 0)]))
    def _():
        wgu_s[:, :hid] = wg_ref[0].astype(BF16)
        wgu_s[:, hid:] = wu_ref[0].astype(BF16)
        wd_s[...] = wd_ref[0].astype(BF16)

    @pl.when(valid > 0)
    def _():
        x = _load_planes(xs_ref[0], xs_ref[1]).astype(BF16)
        y = _dot(_swiglu_hidden(x, wgu_s[...]).astype(BF16), wd_s[...])
        row = lax.broadcasted_iota(jnp.int32, (y.shape[0], 1), 0)
        _store_planes(ys_ref, slice(None), jnp.where(row < valid, y, 0.0))

    @pl.when(valid <= 0)
    def _():
        ys_ref[...] = jnp.zeros_like(ys_ref)


def _gmm(blk, xs, we_gate, we_up, we_down, rows, after):
    _, n_rows, q = xs.shape
    _, d, hid = we_gate.shape
    w_spec = lambda w: pl.BlockSpec((1,) + w.shape[1:], lambda b, be, bv, bs, bd: (be[b], 0, 0))
    in_specs = [pl.BlockSpec((2, rows, q), lambda b, be, bv, bs, bd: (0, bs[b], 0)),
                w_spec(we_gate), w_spec(we_up), w_spec(we_down)]
    args = [xs, we_gate, we_up, we_down]
    n_prefetch = 4
    body = _gmm_kernel
    if after is not None:
        pos = n_prefetch + len(args)
        in_specs.append(pl.BlockSpec(memory_space=pl.ANY))
        args.append(after)
        body = lambda *refs: _gmm_kernel(*refs[:pos], *refs[pos + 1:])
    return pl.pallas_call(
        body,
        grid_spec=pltpu.PrefetchScalarGridSpec(
            num_scalar_prefetch=n_prefetch,
            grid=(n_rows // rows - 1,),
            in_specs=in_specs,
            out_specs=pl.BlockSpec((2, rows, q), lambda b, be, bv, bs, bd: (0, bd[b], 0)),
            scratch_shapes=[pltpu.VMEM((d, 2 * hid), BF16), pltpu.VMEM((hid, d), BF16)]),
        out_shape=jax.ShapeDtypeStruct(xs.shape, jnp.int32),
        compiler_params=_cparams("arbitrary"),
    )(blk[0], blk[1], blk[2], blk[3], *args)


def _combine_kernel(x1_ref, mod_ref, y8_ref, g8_ref, h2p_ref, wsgu_ref, wsd_ref, *rest):
    o_ref = rest[-1]
    g8 = g8_ref[...]
    h2 = _load_planes(h2p_ref[0], h2p_ref[1]).astype(BF16)
    ffn = _dot(_swiglu_hidden(h2, wsgu_ref[...]).astype(BF16), wsd_ref[...])
    for k in range(TOP_K):
        ffn = ffn + g8[:, k:k + 1] * _load_planes(y8_ref[k, 0], y8_ref[k, 1])
    o_ref[0] = x1_ref[0] + mod_ref[0, 5:6, :] * ffn


def _combine(x1, mod, y8, gate8, h2p, ws_gu, ws_d, b0, total, earlier, after):
    bsz, s, d = x1.shape
    tm = min(256, s)
    ns = s // tm
    full = lambda shape: pl.BlockSpec(shape, lambda b, i: (0,) * len(shape))
    in_specs = [pl.BlockSpec((1, tm, d), lambda b, i: (b, i, 0)),
                pl.BlockSpec((1, N_MOD, d), lambda b, i: (b, 0, 0)),
                pl.BlockSpec((TOP_K, 2, tm, d // 4), lambda b, i: (0, 0, b * ns + i, 0)),
                pl.BlockSpec((tm, TOP_K), lambda b, i: (b * ns + i, 0)),
                pl.BlockSpec((2, tm, d // 4), lambda b, i: (0, b * ns + i, 0)),
                full(ws_gu.shape), full(ws_d.shape)]
    args = [x1, mod, y8, gate8, h2p, ws_gu, ws_d]
    aliases = {}
    if earlier is not None:
        in_specs.append(pl.BlockSpec(memory_space=pl.ANY))
        args.append(earlier)
        aliases = {len(args) - 1: 0}
    if after is not None:
        in_specs.append(pl.BlockSpec(memory_space=pl.ANY))
        args.append(after)
    return pl.pallas_call(
        _combine_kernel,
        grid=(bsz, ns),
        in_specs=in_specs,
        out_specs=pl.BlockSpec((1, tm, d), lambda b, i: (b0 + b, i, 0)),
        out_shape=jax.ShapeDtypeStruct((total, s, d), F32),
        input_output_aliases=aliases,
        compiler_params=_cparams("parallel", "parallel"),
    )(*args)


def kernel(x, c, w_ada, b_ada, norm1_g, w_in, sb_q_norm_g, sb_k_norm_g, hg_lb_logits, hg_norm_g,
           w_branch_sb, w_branch_hg, w_out, norm2_g, w_router, router_bias, w_e_gate, w_e_up,
           w_e_down, w_s_gate, w_s_up, w_s_down):
    bsz, s, d = x.shape
    depth = w_ada.shape[0]
    n_gate_cols = 2 * d
    qkv_col0 = n_gate_cols // LANES
    hg_col0 = qkv_col0 + 3 * SB_WIDTH // LANES
    for l in range(depth):
        n_mix = 3 * SB_WIDTH + 4 * HG_WIDTH
        w_in_l = jnp.concatenate([w_in[l][:, n_mix:], w_in[l][:, :n_mix]], axis=1).astype(BF16)
        wr_t = w_router[l].T
        wr_hi = wr_t.astype(BF16)
        wr_lo = (wr_t - wr_hi.astype(F32)).astype(BF16)
        ws_gu = jnp.concatenate([w_s_gate[l], w_s_up[l]], axis=1).astype(BF16)

        w_sb, w_hg = w_branch_sb[l].astype(BF16), w_branch_hg[l].astype(BF16)
        w_o, ws_d = w_out[l].astype(BF16), w_s_down[l].astype(BF16)

        mod = _ada(c, w_ada[l], b_ada[l]).reshape(bsz, N_MOD, d)
        n_parts = BATCH_PARTS if bsz % BATCH_PARTS == 0 else 1
        pb = bsz // n_parts
        t = pb * s
        q = d // 4
        n_blocks = -(-(t * TOP_K + N_EXPERTS * (DISPATCH_ROWS - 1)) // DISPATCH_ROWS)
        n_rows = (n_blocks + 1) * DISPATCH_ROWS
        plane_off = jnp.array([0, n_rows], jnp.int32)[None, :, None]

        def experts(st, after):
            ys = _gmm(st["blk"], st["xs"].reshape(2, n_rows, q), w_e_gate[l], w_e_up[l],
                      w_e_down[l], DISPATCH_ROWS, after)
            y8 = _sc_gather_rows(ys.reshape(2 * n_rows, q), st["row_idx"].reshape(1, TOP_K * 2 * t))
            return ys, y8.reshape(TOP_K, 2, t, q)

        out = None
        prev = None
        for p in range(n_parts):
            b0 = p * pb
            mod_p = mod[b0:b0 + pb]
            proj, knt = _inproj(x, mod_p, norm1_g[l], w_in_l, sb_k_norm_g[l],
                                qkv_col0 + SB_WIDTH // LANES, b0, prev and prev["blk"])
            o_hg = _hgrn(proj, hg_lb_logits, hg_norm_g[l], hg_col0, l)
            if prev:
                prev["ys"], prev["y8"] = experts(prev, o_hg)
            o_sb = _sb_attention(proj, knt, sb_q_norm_g[l], qkv_col0, prev and prev["ys"])
            if prev:
                out = _combine(prev["x1"], prev["mod"], prev["y8"], prev["gate8"], prev["h2p"],
                               ws_gu, ws_d, prev["b0"], bsz, out, o_sb)
            x1, h2p, logits_t = _merge(x, o_sb, o_hg, proj, mod_p, norm2_g[l], w_sb, w_hg, w_o,
                                       wr_hi, wr_lo, b0, out)
            gates_t, rank_t, counts = _route(logits_t, router_bias[l])
            slot8, gate8, blk = _slots(gates_t, rank_t, counts, DISPATCH_ROWS, n_blocks)
            row_idx = (slot8[:, None, :] + plane_off).reshape(TOP_K, 2 * t)
            xs = _sc_scatter_rows(h2p.reshape(2 * t, q), row_idx, 2 * n_rows)
            prev = dict(blk=blk, xs=xs, row_idx=row_idx, x1=x1, h2p=h2p, mod=mod_p, gate8=gate8,
                        b0=b0)
        _, y8 = experts(prev, None)
        x = _combine(prev["x1"], prev["mod"], y8, prev["gate8"], prev["h2p"], ws_gu, ws_d,
                     prev["b0"], bsz, out, None)
    return x
```

```python
import functools

import jax
import jax.numpy as jnp
from jax import lax
from jax.experimental import pallas as pl
from jax.experimental.pallas import tpu as pltpu
from jax.experimental.pallas import tpu_sc as plsc

F32 = jnp.float32
BF16 = jnp.bfloat16

SB_HEADS = 8
SB_HEAD_DIM = 64
SB_WIDTH = SB_HEADS * SB_HEAD_DIM
HG_HEADS = 4
HG_HEAD_DIM = 128
HG_WIDTH = HG_HEADS * HG_HEAD_DIM
HG_CHUNK = 64
N_EXPERTS = 64
TOP_K = 8
N_GROUPS = 8
TOPK_GROUPS = 4
GROUP_SIZE = N_EXPERTS // N_GROUPS
ROUTED_SCALE = 2.5
DISPATCH_ROWS = 1024
BATCH_PARTS = 2
N_MOD = 6
EPS = 1e-6
LOG2_E = 1.4426950408889634
SB_DEAD_LOG2 = 160.0

LANES = 128
VMEM_LIMIT = 56 * 1024 * 1024

ADA_COLS = 1024
ROW_TILE = 512
SB_QUERY_BLOCK = 512
HG_SEQ_TILE = 1024
ROUTE_TOKENS = 1024
COMBINE_ROWS = 512


def _cparams(*sem):
    return pltpu.CompilerParams(dimension_semantics=sem, vmem_limit_bytes=VMEM_LIMIT)


def _silu(t):
    return t * jax.nn.sigmoid(t)


def _dot(a, b):
    return jnp.dot(a, b, preferred_element_type=F32)


def _run_after(body, in_specs, args, after):
    if after is None:
        return body
    pos = len(args)
    in_specs.append(pl.BlockSpec(memory_space=pl.ANY))
    args.append(after)
    return lambda *refs: body(*refs[:pos], *refs[pos + 1:])


def _dot_nt(a, b):
    return lax.dot_general(a, b, (((1,), (1,)), ((), ())), preferred_element_type=F32)


def _split_bf16(t):
    hi = t.astype(BF16)
    lo = (t - hi.astype(F32)).astype(BF16)
    return hi, lo


def _ada_kernel(c_ref, w_ref, b_ref, o_ref):
    cond = _silu(c_ref[...])
    o_ref[...] = _dot(cond, w_ref[...]) + b_ref[...]


def _ada(c, w, b):
    bsz, d = c.shape
    n = w.shape[1]
    tn = min(ADA_COLS, n)
    return pl.pallas_call(
        _ada_kernel,
        grid=(n // tn,),
        in_specs=[pl.BlockSpec((bsz, d), lambda j: (0, 0)),
                  pl.BlockSpec((d, tn), lambda j: (0, j)),
                  pl.BlockSpec((1, tn), lambda j: (0, j))],
        out_specs=pl.BlockSpec((bsz, tn), lambda j: (0, j)),
        out_shape=jax.ShapeDtypeStruct((bsz, n), F32),
        compiler_params=_cparams("parallel"),
    )(c, w, b.reshape(1, n))


def _modulated_norm(x, g, shift, scale):
    y = x * lax.rsqrt(jnp.mean(x * x, axis=-1, keepdims=True) + EPS) * g
    return y * (1.0 + scale) + shift


INPROJ_COLS = 512


def _inproj_kernel(x_ref, mod_ref, g_ref, w_ref, kg_ref, o_ref, knt_ref, *, k_chunk):
    h = _modulated_norm(x_ref[0], g_ref[...], mod_ref[0, 0:1, :], mod_ref[0, 1:2, :]).astype(BF16)
    lo_half = lax.broadcasted_iota(jnp.int32, (1, LANES), 1) < SB_HEAD_DIM
    for j in range(w_ref.shape[1] // INPROJ_COLS):
        cols = slice(j * INPROJ_COLS, (j + 1) * INPROJ_COLS)
        res = _dot(h, w_ref[:, cols])
        o_ref[0, :, cols] = res.astype(o_ref.dtype)
        if j == k_chunk:
            for p in range(INPROJ_COLS // LANES):
                pair = slice(p * LANES, (p + 1) * LANES)
                knt_ref[0, pair, :] = _pair_norm(res[:, pair], kg_ref[...], lo_half).T.astype(BF16)


def _inproj(x, mod, g, w, kg, k_col0, b0, after):
    _, s, d = x.shape
    bsz = mod.shape[0]
    n = w.shape[1]
    tm = min(ROW_TILE, s)
    assert INPROJ_COLS == SB_WIDTH and (k_col0 * LANES) % INPROJ_COLS == 0
    kg2 = jnp.tile(kg.reshape(1, SB_HEAD_DIM), (1, 2))
    in_specs = [pl.BlockSpec((1, tm, d), lambda b, i: (b0 + b, i, 0)),
                pl.BlockSpec((1, N_MOD, d), lambda b, i: (b, 0, 0)),
                pl.BlockSpec((1, d), lambda b, i: (0, 0)),
                pl.BlockSpec((d, n), lambda b, i: (0, 0)),
                pl.BlockSpec((1, LANES), lambda b, i: (0, 0))]
    args = [x, mod, g.reshape(1, d), w, kg2]
    body = _run_after(functools.partial(_inproj_kernel, k_chunk=k_col0 * LANES // INPROJ_COLS),
                      in_specs, args, after)
    return pl.pallas_call(
        body,
        grid=(bsz, s // tm),
        in_specs=in_specs,
        out_specs=[pl.BlockSpec((1, tm, n), lambda b, i: (b, i, 0)),
                   pl.BlockSpec((1, SB_WIDTH, tm), lambda b, i: (b, 0, i))],
        out_shape=[jax.ShapeDtypeStruct((bsz, s, n), BF16),
                   jax.ShapeDtypeStruct((bsz, SB_WIDTH, s), BF16)],
        compiler_params=_cparams("parallel", "parallel"),
    )(*args)


def _pair_norm(t, g, lo_half):
    sq = t * t
    s_lo = jnp.sum(jnp.where(lo_half, sq, 0.0), axis=-1, keepdims=True)
    s_hi = jnp.sum(jnp.where(lo_half, 0.0, sq), axis=-1, keepdims=True)
    ms = jnp.where(lo_half, s_lo, s_hi) * (1.0 / SB_HEAD_DIM)
    return t * lax.rsqrt(ms + EPS) * g


def _neg_abs(t):
    bits = lax.bitcast_convert_type(t, jnp.uint32) | jnp.uint32(0x80000000)
    return lax.bitcast_convert_type(bits, F32)


def _split_trunc(t):
    bits = lax.bitcast_convert_type(t, jnp.uint32) & jnp.uint32(0xFFFF0000)
    hi = lax.bitcast_convert_type(bits, F32)
    return hi.astype(BF16), (t - hi).astype(BF16)


def _sb_kernel(q_ref, knt_ref, v_ref, qg_ref, o_ref, va_ref, vb_ref, *, tq):
    qi = pl.program_id(2)
    s = v_ref.shape[1]
    lane = lax.broadcasted_iota(jnp.int32, (1, LANES), 1)
    lo_half = lane < SB_HEAD_DIM

    tk = tq // 2

    @pl.when(qi == 0)
    def _():
        def prep_block(j, c):
            rows = pl.ds(pl.multiple_of(j * tk, tk), tk)
            vb = v_ref[0, rows, :]
            va_ref[rows, :] = jnp.where(lo_half, vb, jnp.zeros_like(vb))
            vb_ref[rows, :] = jnp.where(lo_half, jnp.zeros_like(vb), vb)
            return c
        lax.fori_loop(0, s // tk, prep_block, 0)

    scale = SB_HEAD_DIM ** -0.5 * LOG2_E
    q = _pair_norm(q_ref[0].astype(F32), qg_ref[...], lo_half) * scale
    q_heads = (jnp.where(lo_half, q, 0.0).astype(BF16), jnp.where(lo_half, 0.0, q).astype(BF16))
    v_heads = (va_ref, vb_ref)

    row = lax.broadcasted_iota(jnp.int32, (tk, tk), 0)
    col = lax.broadcasted_iota(jnp.int32, (tk, tk), 1)
    strict = col < row
    tri = (row >= col).astype(BF16)
    tri2 = jnp.concatenate([tri, tri], axis=0)

    def sweep(streams):
        cols = [[pl.ds(pl.multiple_of(j * tk, tk), tk) for j, _, _ in blocks]
                for _, _, _, blocks in streams]
        z = [[[_dot(qh[h], knt_ref[0, :, c]) for c in cols[i]] for h in range(2)]
             for i, (qh, _, _, _) in enumerate(streams)]
        cs = []
        for i, (_, _, _, blocks) in enumerate(streams):
            cs.append([[None] * len(blocks) for _ in range(2)])
            for h in range(2):
                for b, (_, masked, _) in enumerate(blocks):
                    zb = z[i][h][b]
                    sp = jnp.maximum(zb, 0.0) + jnp.log2(1.0 + jnp.exp2(_neg_abs(zb)))
                    if masked:
                        sp = jnp.where(strict, sp, 0.0)
                    cs[i][h][b] = _dot(jnp.concatenate(_split_trunc(sp), axis=1), tri2)
        results = []
        for i, (_, runs, acc, blocks) in enumerate(streams):
            runs = list(runs)
            for h in range(2):
                for b, (_, masked, gate) in enumerate(blocks):
                    a = jnp.exp2(z[i][h][b] - cs[i][h][b] - runs[h])
                    if masked:
                        a = jnp.where(strict, a, 0.0)
                    vb = v_heads[h][cols[i][b], :]
                    step = cs[i][h][b][:, 0:1]
                    if gate is not None:
                        vb = jnp.where(gate, vb, jnp.zeros_like(vb))
                        step = jnp.where(gate, step, 0.0)
                    acc = acc + _dot(a.astype(BF16), vb)
                    runs[h] = runs[h] + step
            results.append((tuple(runs), acc))
        return results

    zero_run = jnp.zeros((tk, 1), F32)
    zero_acc = jnp.zeros((tk, LANES), F32)
    left, right = 2 * qi, 2 * qi + 1
    q_top = tuple(qh[:tk] for qh in q_heads)
    q_bot = tuple(qh[tk:] for qh in q_heads)
    top, bot = sweep([
        (q_top, (zero_run, zero_run), zero_acc,
         [(left, True, None), (jnp.maximum(left - 1, 0), False, qi > 0)]),
        (q_bot, (zero_run, zero_run), zero_acc, [(right, True, None), (left, False, None)])])

    def min_run(rt, rb):
        return jnp.min(jnp.minimum(jnp.minimum(rt[0], rt[1]), jnp.minimum(rb[0], rb[1])))

    def alive(carry):
        j, _, _, low = carry
        return jnp.logical_and(j >= 0, low < SB_DEAD_LOG2)

    def earlier_block(carry):
        j, (rt, at), (rb, ab), _ = carry
        t, b = sweep([(q_top, rt, at, [(jnp.maximum(j - 1, 0), False, j > 0)]),
                      (q_bot, rb, ab, [(j, False, None)])])
        return j - 1, t, b, min_run(t[0], b[0])

    _, top, bot, _ = lax.while_loop(alive, earlier_block,
                                    (left - 1, top, bot, min_run(top[0], bot[0])))
    o_ref[0] = jnp.concatenate([top[1], bot[1]], axis=0).astype(o_ref.dtype)


def _sb_attention(proj, knt, qg, col0, after):
    bsz, s, _ = proj.shape
    tq = min(SB_QUERY_BLOCK, s)
    npair = SB_WIDTH // LANES
    qg2 = jnp.tile(qg.reshape(1, SB_HEAD_DIM), (1, 2))
    in_specs = [pl.BlockSpec((1, tq, LANES), lambda b, h, i: (b, i, col0 + h)),
                pl.BlockSpec((1, LANES, s), lambda b, h, i: (b, h, 0)),
                pl.BlockSpec((1, s, LANES), lambda b, h, i: (b, 0, col0 + 2 * npair + h)),
                pl.BlockSpec((1, LANES), lambda b, h, i: (0, 0))]
    args = [proj, knt, proj, qg2]
    body = _run_after(functools.partial(_sb_kernel, tq=tq), in_specs, args, after)
    return pl.pallas_call(
        body,
        grid=(bsz, npair, s // tq),
        in_specs=in_specs,
        out_specs=pl.BlockSpec((1, tq, LANES), lambda b, h, i: (b, i, h)),
        out_shape=jax.ShapeDtypeStruct((bsz, s, SB_WIDTH), BF16),
        scratch_shapes=[pltpu.VMEM((s, LANES), BF16),
                        pltpu.VMEM((s, LANES), BF16)],
        compiler_params=_cparams("parallel", "parallel", "arbitrary"),
    )(*args)


HG_CHUNKS_PER_STEP = 4


def _hgrn_kernel(f_ref, i_ref, q_ref, g_ref, lbl_ref, ng_ref, o_ref, st_ref, *, layer):
    ts = f_ref.shape[1]
    c = HG_CHUNK
    dh = HG_HEAD_DIM

    @pl.when(pl.program_id(1) == 0)
    def _():
        st_ref[...] = jnp.zeros_like(st_ref)

    lg = lbl_ref[...]
    e = jnp.exp(lg - jnp.max(lg, axis=0, keepdims=True))
    lb = jnp.sum(e[:layer + 1], axis=0, keepdims=True) / jnp.sum(e, axis=0, keepdims=True)

    row = lax.broadcasted_iota(jnp.int32, (c, c), 0)
    col = lax.broadcasted_iota(jnp.int32, (c, c), 1)
    causal = col <= row
    lower = causal.astype(BF16)
    width = f_ref.shape[2]

    def body(it, carry):
        states = [st_ref[h] for h in range(HG_HEADS)]
        for u in range(HG_CHUNKS_PER_STEP):
            rows = pl.ds(pl.multiple_of((it * HG_CHUNKS_PER_STEP + u) * c, c), c)
            forget = lb + (1.0 - lb) * jax.nn.sigmoid(f_ref[0, rows, :].astype(F32))
            kc = 1.0 - forget
            hi, lo = _split_trunc(jnp.log(forget))
            cum = _dot(lower, jnp.concatenate([hi, lo], axis=1))
            gc = cum[:, :width] + cum[:, width:]
            g_mid = gc[c // 2 - 1:c // 2, :]
            g_last = gc[c - 1:c, :]
            qe = _silu(q_ref[0, rows, :].astype(F32)) * jnp.exp(gc - g_mid)
            ke = kc * jnp.exp(g_mid - gc)
            qg = (qe * jnp.exp(g_mid)).astype(BF16)
            kd = (ke * jnp.exp(g_last - g_mid)).astype(BF16)
            qe = qe.astype(BF16)
            ke = ke.astype(BF16)
            decay = jnp.exp(g_last)
            v = i_ref[0, rows, :]
            v_t = v.astype(F32).T.astype(BF16)
            outs = []
            for h in range(HG_HEADS):
                sl = slice(h * dh, (h + 1) * dh)
                scores = _dot_nt(qe[:, sl], ke[:, sl])
                o = (_dot(jnp.where(causal, scores, 0.0).astype(BF16), v[:, sl])
                     + _dot_nt(qg[:, sl], states[h].astype(BF16)))
                outs.append(o * lax.rsqrt(jnp.mean(o * o, axis=-1, keepdims=True) + EPS))
                states[h] = states[h] * decay[:, sl] + _dot(v_t[sl, :], kd[:, sl])
            y = jnp.concatenate(outs, axis=1) * ng_ref[...] * _silu(g_ref[0, rows, :].astype(F32))
            o_ref[0, rows, :] = y.astype(o_ref.dtype)
        for h in range(HG_HEADS):
            st_ref[h] = states[h]
        return carry

    lax.fori_loop(0, ts // (c * HG_CHUNKS_PER_STEP), body, 0)


def _hgrn(proj, lb_logits, norm_g, col0, layer):
    bsz, s, _ = proj.shape
    nl = lb_logits.shape[0]
    ts = min(HG_SEQ_TILE, s)
    blk0 = col0 * LANES // HG_WIDTH
    spec = lambda off: pl.BlockSpec((1, ts, HG_WIDTH), lambda b, i: (b, i, blk0 + off))
    return pl.pallas_call(
        functools.partial(_hgrn_kernel, layer=layer),
        grid=(bsz, s // ts),
        in_specs=[spec(0), spec(1), spec(2), spec(3),
                  pl.BlockSpec((nl, HG_WIDTH), lambda b, i: (0, 0)),
                  pl.BlockSpec((1, HG_WIDTH), lambda b, i: (0, 0))],
        out_specs=pl.BlockSpec((1, ts, HG_WIDTH), lambda b, i: (b, i, 0)),
        out_shape=jax.ShapeDtypeStruct((bsz, s, HG_WIDTH), BF16),
        scratch_shapes=[pltpu.VMEM((HG_HEADS, HG_HEAD_DIM, HG_HEAD_DIM), F32)],
        compiler_params=_cparams("parallel", "arbitrary"),
    )(proj, proj, proj, proj, lb_logits, jnp.tile(norm_g.reshape(1, HG_HEAD_DIM), (1, HG_HEADS)))


def _pack_halves(t):
    bits = lax.bitcast_convert_type(t.astype(BF16).astype(F32), jnp.uint32)
    w = t.shape[1] // 2
    return lax.bitcast_convert_type(bits[:, :w] | (bits[:, w:] >> 16), jnp.int32)


def _unpack_halves(p):
    u = lax.bitcast_convert_type(p, jnp.uint32)
    return (lax.bitcast_convert_type(u & jnp.uint32(0xFFFF0000), F32),
            lax.bitcast_convert_type(u << 16, F32))


def _store_planes(ref, rows, t):
    p = _pack_halves(t)
    q = p.shape[1] // 2
    ref[0, rows, :] = p[:, :q]
    ref[1, rows, :] = p[:, q:]


def _load_planes(p0, p1):
    a0, b0 = _unpack_halves(p0)
    a1, b1 = _unpack_halves(p1)
    return jnp.concatenate([a0, a1, b0, b1], axis=1)


def _swiglu_hidden(h, w_gu):
    gu = _dot(h, w_gu)
    hid = w_gu.shape[1] // 2
    return _silu(gu[:, :hid]) * gu[:, hid:]


def _merge_kernel(x_ref, osb_ref, ohg_ref, gsb_ref, ghg_ref, mod_ref, g2_ref, wsb_ref, whg_ref,
                  wout_ref, wrh_ref, wrl_ref, x1_ref, h2p_ref, lg_ref):
    m_sb = _dot(osb_ref[0], wsb_ref[...])
    m_hg = _dot(ohg_ref[0], whg_ref[...])
    merged = (jax.nn.sigmoid(gsb_ref[0].astype(F32)) * m_sb
              + jax.nn.sigmoid(ghg_ref[0].astype(F32)) * m_hg)
    x1 = x_ref[0] + mod_ref[0, 2:3, :] * _dot(merged.astype(BF16), wout_ref[...])
    x1_ref[0] = x1
    h2 = _modulated_norm(x1, g2_ref[...], mod_ref[0, 3:4, :], mod_ref[0, 4:5, :])
    hi, lo = _split_bf16(h2)
    _store_planes(h2p_ref, slice(None), h2)
    lg_ref[...] = _dot_nt(wrh_ref[...], hi) + _dot_nt(wrh_ref[...], lo) + _dot_nt(wrl_ref[...], hi)


def _merge(x, o_sb, o_hg, proj, mod, g2, w_sb, w_hg, w_out, wr_hi, wr_lo, b0, after):
    _, s, d = x.shape
    bsz = mod.shape[0]
    tm = min(ROW_TILE, s)
    ns = s // tm
    full = lambda shape: pl.BlockSpec(shape, lambda b, i: (0,) * len(shape))
    in_specs = [pl.BlockSpec((1, tm, d), lambda b, i: (b0 + b, i, 0)),
                pl.BlockSpec((1, tm, SB_WIDTH), lambda b, i: (b, i, 0)),
                pl.BlockSpec((1, tm, HG_WIDTH), lambda b, i: (b, i, 0)),
                pl.BlockSpec((1, tm, d), lambda b, i: (b, i, 0)),
                pl.BlockSpec((1, tm, d), lambda b, i: (b, i, 1)),
                pl.BlockSpec((1, N_MOD, d), lambda b, i: (b, 0, 0)),
                full((1, d)), full(w_sb.shape), full(w_hg.shape), full(w_out.shape),
                full(wr_hi.shape), full(wr_lo.shape)]
    args = [x, o_sb, o_hg, proj, proj, mod, g2.reshape(1, d), w_sb, w_hg, w_out, wr_hi, wr_lo]
    body = _run_after(_merge_kernel, in_specs, args, after)
    return pl.pallas_call(
        body,
        grid=(bsz, ns),
        in_specs=in_specs,
        out_specs=[pl.BlockSpec((1, tm, d), lambda b, i: (b, i, 0)),
                   pl.BlockSpec((2, tm, d // 4), lambda b, i: (0, b * ns + i, 0)),
                   pl.BlockSpec((N_EXPERTS, tm), lambda b, i: (0, b * ns + i))],
        out_shape=[jax.ShapeDtypeStruct((bsz, s, d), F32),
                   jax.ShapeDtypeStruct((2, bsz * s, d // 4), jnp.int32),
                   jax.ShapeDtypeStruct((N_EXPERTS, bsz * s), F32)],
        compiler_params=_cparams("parallel", "parallel"),
    )(*args)


def _first_argmax(vals, idx, sentinel):
    m = jnp.max(vals, axis=0, keepdims=True)
    first = jnp.min(jnp.where(vals == m, idx, sentinel), axis=0, keepdims=True)
    return m, first


def _route_kernel(lg_ref, bias_ref, gates_ref, rank_ref, cnt_ref, run_ref):
    tn = lg_ref.shape[1]

    @pl.when(pl.program_id(0) == 0)
    def _():
        run_ref[...] = jnp.zeros_like(run_ref)

    neg = -jnp.inf
    scores = jax.nn.sigmoid(lg_ref[...])
    choice = scores + bias_ref[...]

    gidx = lax.broadcasted_iota(jnp.int32, (GROUP_SIZE, tn), 0)
    group_rows = []
    for g in range(N_GROUPS):
        cg = choice[g * GROUP_SIZE:(g + 1) * GROUP_SIZE, :]
        m1, i1 = _first_argmax(cg, gidx, GROUP_SIZE)
        m2 = jnp.max(jnp.where(gidx == i1, neg, cg), axis=0, keepdims=True)
        group_rows.append(m1 + m2)
    work = jnp.concatenate(group_rows, axis=0)
    ggi = lax.broadcasted_iota(jnp.int32, (N_GROUPS, tn), 0)
    gmask = jnp.zeros((N_GROUPS, tn), F32)
    for _ in range(TOPK_GROUPS):
        _, first = _first_argmax(work, ggi, N_GROUPS)
        pick = ggi == first
        gmask = jnp.where(pick, 1.0, gmask)
        work = jnp.where(pick, neg, work)

    masked = jnp.concatenate(
        [jnp.where(gmask[g:g + 1, :] > 0.0, choice[g * GROUP_SIZE:(g + 1) * GROUP_SIZE, :], neg)
         for g in range(N_GROUPS)], axis=0)
    eidx = lax.broadcasted_iota(jnp.int32, (N_EXPERTS, tn), 0)
    sel = jnp.zeros((N_EXPERTS, tn), F32)
    for _ in range(TOP_K):
        _, first = _first_argmax(masked, eidx, N_EXPERTS)
        pick = eidx == first
        sel = jnp.where(pick, 1.0, sel)
        masked = jnp.where(pick, neg, masked)

    chosen = jnp.where(sel > 0.0, scores, 0.0)
    gates_ref[...] = chosen / jnp.sum(chosen, axis=0, keepdims=True) * ROUTED_SCALE

    r = lax.broadcasted_iota(jnp.int32, (tn, tn), 0)
    c = lax.broadcasted_iota(jnp.int32, (tn, tn), 1)
    local = _dot(sel.astype(BF16), (r < c).astype(BF16))
    run = run_ref[:, 0:1]
    rank_ref[...] = jnp.where(sel > 0.0, run + local, -1.0)
    total = run + jnp.sum(sel, axis=1, keepdims=True)
    run_ref[...] = jnp.broadcast_to(total, run_ref.shape)
    cnt_ref[...] = jnp.broadcast_to(total, cnt_ref.shape)


def _route(logits_t, bias):
    e, t = logits_t.shape
    tn = min(ROUTE_TOKENS, t)
    return pl.pallas_call(
        _route_kernel,
        grid=(t // tn,),
        in_specs=[pl.BlockSpec((e, tn), lambda i: (0, i)),
                  pl.BlockSpec((e, 1), lambda i: (0, 0))],
        out_specs=[pl.BlockSpec((e, tn), lambda i: (0, i)),
                   pl.BlockSpec((e, tn), lambda i: (0, i)),
                   pl.BlockSpec((e, LANES), lambda i: (0, 0))],
        out_shape=[jax.ShapeDtypeStruct((e, t), F32),
                   jax.ShapeDtypeStruct((e, t), F32),
                   jax.ShapeDtypeStruct((e, LANES), F32)],
        scratch_shapes=[pltpu.VMEM((e, LANES), F32)],
        compiler_params=_cparams("arbitrary"),
    )(logits_t, bias.reshape(e, 1))


def _slots_kernel(gates_ref, rank_ref, cnt_ref, slot_ref, gate8_ref, blk_ref, *, rows, n_blocks):
    ne, tn = gates_ref.shape
    cnt = cnt_ref[...]
    nblk = jnp.floor((cnt + (rows - 1.0)) * (1.0 / rows))
    er = lax.broadcasted_iota(jnp.int32, (ne, ne), 0)
    ec = lax.broadcasted_iota(jnp.int32, (ne, ne), 1)
    lower = (ec < er).astype(BF16)
    pad_start = _dot(lower, nblk.astype(BF16))[:, 0:1] * rows
    pad_end = pad_start + nblk[:, 0:1] * rows

    rank = rank_ref[...]
    sel = rank >= 0.0
    slot_e = pad_start + rank
    kidx = _dot(lower, sel.astype(BF16))
    gates = gates_ref[...]
    slot_rows, gate_rows = [], []
    for k in range(TOP_K):
        m = jnp.logical_and(sel, kidx == k)
        slot_rows.append(jnp.sum(jnp.where(m, slot_e, 0.0), axis=0, keepdims=True))
        gate_rows.append(jnp.sum(jnp.where(m, gates, 0.0), axis=0, keepdims=True))
    slot_ref[...] = jnp.concatenate(slot_rows, axis=0).astype(jnp.int32)
    gate8_ref[...] = jnp.concatenate(gate_rows, axis=0).T

    nbp = blk_ref.shape[1]
    bstart = lax.broadcasted_iota(jnp.int32, (1, nbp), 1).astype(F32) * rows
    e_of = jnp.sum((pad_end <= bstart).astype(F32), axis=0, keepdims=True)
    e_of = jnp.minimum(e_of, ne - 1.0)
    eidx = lax.broadcasted_iota(jnp.int32, (ne, nbp), 0).astype(F32)
    valid_e = jnp.clip(cnt[:, 0:1] - (bstart - pad_start), 0.0, rows)
    valid = jnp.sum(jnp.where(eidx == e_of, valid_e, 0.0), axis=0, keepdims=True)
    total = pad_end[ne - 1:ne, :]
    used = bstart < total
    bidx = bstart * (1.0 / rows)
    src = jnp.where(used, bidx, total * (1.0 / rows) - 1.0)
    dst = jnp.where(used, bidx, float(n_blocks))
    blk_ref[...] = jnp.concatenate(
        [e_of, valid, src, dst, jnp.zeros((blk_ref.shape[0] - 4, nbp), F32)], axis=0).astype(jnp.int32)


def _slots(gates_t, rank_t, counts, rows, n_blocks):
    nbp = -(-n_blocks // LANES) * LANES
    e, t = gates_t.shape
    assert t // rows <= 256
    tn = min(ROUTE_TOKENS, t)
    return pl.pallas_call(
        functools.partial(_slots_kernel, rows=rows, n_blocks=n_blocks),
        grid=(t // tn,),
        in_specs=[pl.BlockSpec((e, tn), lambda i: (0, i)),
                  pl.BlockSpec((e, tn), lambda i: (0, i)),
                  pl.BlockSpec((e, LANES), lambda i: (0, 0))],
        out_specs=[pl.BlockSpec((TOP_K, tn), lambda i: (0, i)),
                   pl.BlockSpec((tn, TOP_K), lambda i: (i, 0)),
                   pl.BlockSpec((8, nbp), lambda i: (0, 0))],
        out_shape=[jax.ShapeDtypeStruct((TOP_K, t), jnp.int32),
                   jax.ShapeDtypeStruct((t, TOP_K), F32),
                   jax.ShapeDtypeStruct((8, nbp), jnp.int32)],
        compiler_params=_cparams("arbitrary"),
    )(gates_t, rank_t, counts)


SC_WINDOW = 128


def _sc_mesh():
    return plsc.VectorSubcoreMesh(core_axis_name="core", subcore_axis_name="subcore")


def _sc_gather_rows(table, idx):
    n = idx.shape[1]
    w = table.shape[1]

    @pl.kernel(out_type=jax.ShapeDtypeStruct((n, w), table.dtype), mesh=_sc_mesh())
    def gather(t_hbm, i_hbm, o_hbm):
        def body(i_vmem, o_vmem):
            pltpu.sync_copy(t_hbm.at[i_vmem.at[0]], o_vmem)

        pltpu.emit_pipeline(
            body, grid=(n // SC_WINDOW,),
            in_specs=[pl.BlockSpec((1, SC_WINDOW), lambda i: (0, i))],
            out_specs=[pl.BlockSpec((SC_WINDOW, w), lambda i: (i, 0))],
            core_axis_name=("core", "subcore"),
            dimension_semantics=(pltpu.PARALLEL,),
        )(i_hbm, o_hbm)

    return gather(table, idx)


def _sc_scatter_rows(rows, idx, n_out):
    fan, m = idx.shape
    w = rows.shape[1]

    @pl.kernel(out_type=jax.ShapeDtypeStruct((n_out, w), rows.dtype), mesh=_sc_mesh())
    def scatter(r_hbm, i_hbm, o_hbm):
        def body(r_vmem, i_vmem):
            for k in range(fan):
                pltpu.sync_copy(r_vmem, o_hbm.at[i_vmem.at[k]])

        pltpu.emit_pipeline(
            body, grid=(m // SC_WINDOW,),
            in_specs=[pl.BlockSpec((SC_WINDOW, w), lambda i: (i, 0)),
                      pl.BlockSpec((fan, SC_WINDOW), lambda i: (0, i))],
            out_specs=[],
            core_axis_name=("core", "subcore"),
            dimension_semantics=(pltpu.PARALLEL,),
        )(r_hbm, i_hbm)

    return scatter(rows, idx)


def _gmm_kernel(be_ref, bv_ref, bs_ref, bd_ref, xs_ref, wg_ref, wu_ref, wd_ref, ys_ref, wgu_s, wd_s):
    b = pl.program_id(0)
    valid = bv_ref[b]
    hid = wg_ref.shape[2]

    @pl.when(jnp.logical_or(b == 0, be_ref[b] != be_ref[jnp.maximum(b - 1, 0)]))
    def _():
        wgu_s[:, :hid] = wg_ref[0].astype(BF16)
        wgu_s[:, hid:] = wu_ref[0].astype(BF16)
        wd_s[...] = wd_ref[0].astype(BF16)

    @pl.when(valid > 0)
    def _():
        x = _load_planes(xs_ref[0], xs_ref[1]).astype(BF16)
        y = _dot(_swiglu_hidden(x, wgu_s[...]).astype(BF16), wd_s[...])
        row = lax.broadcasted_iota(jnp.int32, (y.shape[0], 1), 0)
        _store_planes(ys_ref, slice(None), jnp.where(row < valid, y, 0.0))

    @pl.when(valid <= 0)
    def _():
        ys_ref[...] = jnp.zeros_like(ys_ref)


def _gmm(blk, xs, we_gate, we_up, we_down, rows, after):
    _, n_rows, q = xs.shape
    _, d, hid = we_gate.shape
    w_spec = lambda w: pl.BlockSpec((1,) + w.shape[1:], lambda b, be, bv, bs, bd: (be[b], 0, 0))
    in_specs = [pl.BlockSpec((2, rows, q), lambda b, be, bv, bs, bd: (0, bs[b], 0)),
                w_spec(we_gate), w_spec(we_up), w_spec(we_down)]
    args = [xs, we_gate, we_up, we_down]
    n_prefetch = 4
    body = _gmm_kernel
    if after is not None:
        pos = n_prefetch + len(args)
        in_specs.append(pl.BlockSpec(memory_space=pl.ANY))
        args.append(after)
        body = lambda *refs: _gmm_kernel(*refs[:pos], *refs[pos + 1:])
    return pl.pallas_call(
        body,
        grid_spec=pltpu.PrefetchScalarGridSpec(
            num_scalar_prefetch=n_prefetch,
            grid=(n_rows // rows - 1,),
            in_specs=in_specs,
            out_specs=pl.BlockSpec((2, rows, q), lambda b, be, bv, bs, bd: (0, bd[b], 0)),
            scratch_shapes=[pltpu.VMEM((d, 2 * hid), BF16), pltpu.VMEM((hid, d), BF16)]),
        out_shape=jax.ShapeDtypeStruct(xs.shape, jnp.int32),
        compiler_params=_cparams("arbitrary"),
    )(blk[0], blk[1], blk[2], blk[3], *args)


def _combine_kernel(x1_ref, mod_ref, y8_ref, g8_ref, h2p_ref, wsgu_ref, wsd_ref, *rest):
    o_ref = rest[-1]
    g8 = g8_ref[...]
    h2 = _load_planes(h2p_ref[0], h2p_ref[1]).astype(BF16)
    ffn = _dot(_swiglu_hidden(h2, wsgu_ref[...]).astype(BF16), wsd_ref[...])
    for k in range(TOP_K):
        ffn = ffn + g8[:, k:k + 1] * _load_planes(y8_ref[k, 0], y8_ref[k, 1])
    o_ref[0] = x1_ref[0] + mod_ref[0, 5:6, :] * ffn


def _combine(x1, mod, y8, gate8, h2p, ws_gu, ws_d, b0, total, earlier, after):
    bsz, s, d = x1.shape
    tm = min(COMBINE_ROWS, s)
    ns = s // tm
    full = lambda shape: pl.BlockSpec(shape, lambda b, i: (0,) * len(shape))
    in_specs = [pl.BlockSpec((1, tm, d), lambda b, i: (b, i, 0)),
                pl.BlockSpec((1, N_MOD, d), lambda b, i: (b, 0, 0)),
                pl.BlockSpec((TOP_K, 2, tm, d // 4), lambda b, i: (0, 0, b * ns + i, 0)),
                pl.BlockSpec((tm, TOP_K), lambda b, i: (b * ns + i, 0)),
                pl.BlockSpec((2, tm, d // 4), lambda b, i: (0, b * ns + i, 0)),
                full(ws_gu.shape), full(ws_d.shape)]
    args = [x1, mod, y8, gate8, h2p, ws_gu, ws_d]
    aliases = {}
    if earlier is not None:
        in_specs.append(pl.BlockSpec(memory_space=pl.ANY))
        args.append(earlier)
        aliases = {len(args) - 1: 0}
    if after is not None:
        in_specs.append(pl.BlockSpec(memory_space=pl.ANY))
        args.append(after)
    return pl.pallas_call(
        _combine_kernel,
        grid=(bsz, ns),
        in_specs=in_specs,
        out_specs=pl.BlockSpec((1, tm, d), lambda b, i: (b0 + b, i, 0)),
        out_shape=jax.ShapeDtypeStruct((total, s, d), F32),
        input_output_aliases=aliases,
        compiler_params=_cparams("parallel", "parallel"),
    )(*args)


def kernel(x, c, w_ada, b_ada, norm1_g, w_in, sb_q_norm_g, sb_k_norm_g, hg_lb_logits, hg_norm_g,
           w_branch_sb, w_branch_hg, w_out, norm2_g, w_router, router_bias, w_e_gate, w_e_up,
           w_e_down, w_s_gate, w_s_up, w_s_down):
    bsz, s, d = x.shape
    depth = w_ada.shape[0]
    n_gate_cols = 2 * d
    qkv_col0 = n_gate_cols // LANES
    hg_col0 = qkv_col0 + 3 * SB_WIDTH // LANES
    for l in range(depth):
        n_mix = 3 * SB_WIDTH + 4 * HG_WIDTH
        w_in_l = jnp.concatenate([w_in[l][:, n_mix:], w_in[l][:, :n_mix]], axis=1).astype(BF16)
        wr_t = w_router[l].T
        wr_hi = wr_t.astype(BF16)
        wr_lo = (wr_t - wr_hi.astype(F32)).astype(BF16)
        ws_gu = jnp.concatenate([w_s_gate[l], w_s_up[l]], axis=1).astype(BF16)

        w_sb, w_hg = w_branch_sb[l].astype(BF16), w_branch_hg[l].astype(BF16)
        w_o, ws_d = w_out[l].astype(BF16), w_s_down[l].astype(BF16)

        mod = _ada(c, w_ada[l], b_ada[l]).reshape(bsz, N_MOD, d)
        n_parts = BATCH_PARTS if bsz % BATCH_PARTS == 0 else 1
        pb = bsz // n_parts
        t = pb * s
        q = d // 4
        n_blocks = -(-(t * TOP_K + N_EXPERTS * (DISPATCH_ROWS - 1)) // DISPATCH_ROWS)
        n_rows = (n_blocks + 1) * DISPATCH_ROWS
        plane_off = jnp.array([0, n_rows], jnp.int32)[None, :, None]

        def experts(st, after):
            ys = _gmm(st["blk"], st["xs"].reshape(2, n_rows, q), w_e_gate[l], w_e_up[l],
                      w_e_down[l], DISPATCH_ROWS, after)
            y8 = _sc_gather_rows(ys.reshape(2 * n_rows, q), st["row_idx"].reshape(1, TOP_K * 2 * t))
            return ys, y8.reshape(TOP_K, 2, t, q)

        out = None
        prev = None
        for p in range(n_parts):
            b0 = p * pb
            mod_p = mod[b0:b0 + pb]
            proj, knt = _inproj(x, mod_p, norm1_g[l], w_in_l, sb_k_norm_g[l],
                                qkv_col0 + SB_WIDTH // LANES, b0, prev and prev["blk"])
            o_hg = _hgrn(proj, hg_lb_logits, hg_norm_g[l], hg_col0, l)
            if prev:
                prev["ys"], prev["y8"] = experts(prev, o_hg)
            o_sb = _sb_attention(proj, knt, sb_q_norm_g[l], qkv_col0, prev and prev["ys"])
            if prev:
                out = _combine(prev["x1"], prev["mod"], prev["y8"], prev["gate8"], prev["h2p"],
                               ws_gu, ws_d, prev["b0"], bsz, out, o_sb)
            x1, h2p, logits_t = _merge(x, o_sb, o_hg, proj, mod_p, norm2_g[l], w_sb, w_hg, w_o,
                                       wr_hi, wr_lo, b0, out)
            gates_t, rank_t, counts = _route(logits_t, router_bias[l])
            slot8, gate8, blk = _slots(gates_t, rank_t, counts, DISPATCH_ROWS, n_blocks)
            row_idx = (slot8[:, None, :] + plane_off).reshape(TOP_K, 2 * t)
            xs = _sc_scatter_rows(h2p.reshape(2 * t, q), row_idx, 2 * n_rows)
            prev = dict(blk=blk, xs=xs, row_idx=row_idx, x1=x1, h2p=h2p, mod=mod_p, gate8=gate8,
                        b0=b0)
        _, y8 = experts(prev, None)
        x = _combine(prev["x1"], prev["mod"], y8, prev["gate8"], prev["h2p"], ws_gu, ws_d,
                     prev["b0"], bsz, out, None)
    return x
```

```python
import functools

import jax
import jax.numpy as jnp
from jax import lax
from jax.experimental import pallas as pl
from jax.experimental.pallas import tpu as pltpu
from jax.experimental.pallas import tpu_sc as plsc

F32 = jnp.float32
BF16 = jnp.bfloat16

SB_HEADS = 8
SB_HEAD_DIM = 64
SB_WIDTH = SB_HEADS * SB_HEAD_DIM
HG_HEADS = 4
HG_HEAD_DIM = 128
HG_WIDTH = HG_HEADS * HG_HEAD_DIM
HG_CHUNK = 64
N_EXPERTS = 64
TOP_K = 8
N_GROUPS = 8
TOPK_GROUPS = 4
GROUP_SIZE = N_EXPERTS // N_GROUPS
ROUTED_SCALE = 2.5
DISPATCH_ROWS = 1024
BATCH_PARTS = 2
N_MOD = 6
EPS = 1e-6
LOG2_E = 1.4426950408889634
SB_DEAD_LOG2 = 160.0

LANES = 128
VMEM_LIMIT = 56 * 1024 * 1024

ADA_COLS = 1024
ROW_TILE = 512
SB_QUERY_BLOCK = 512
HG_SEQ_TILE = 1024
ROUTE_TOKENS = 1024
COMBINE_ROWS = 512


def _cparams(*sem):
    return pltpu.CompilerParams(dimension_semantics=sem, vmem_limit_bytes=VMEM_LIMIT)


def _silu(t):
    return t * jax.nn.sigmoid(t)


def _dot(a, b):
    return jnp.dot(a, b, preferred_element_type=F32)


def _run_after(body, in_specs, args, after):
    if after is None:
        return body
    pos = len(args)
    in_specs.append(pl.BlockSpec(memory_space=pl.ANY))
    args.append(after)
    return lambda *refs: body(*refs[:pos], *refs[pos + 1:])


def _dot_nt(a, b):
    return lax.dot_general(a, b, (((1,), (1,)), ((), ())), preferred_element_type=F32)


def _split_bf16(t):
    hi = t.astype(BF16)
    lo = (t - hi.astype(F32)).astype(BF16)
    return hi, lo


def _ada_kernel(c_ref, w_ref, b_ref, o_ref):
    cond = _silu(c_ref[...])
    o_ref[...] = _dot(cond, w_ref[...]) + b_ref[...]


def _ada(c, w, b):
    bsz, d = c.shape
    n = w.shape[1]
    tn = min(ADA_COLS, n)
    return pl.pallas_call(
        _ada_kernel,
        grid=(n // tn,),
        in_specs=[pl.BlockSpec((bsz, d), lambda j: (0, 0)),
                  pl.BlockSpec((d, tn), lambda j: (0, j)),
                  pl.BlockSpec((1, tn), lambda j: (0, j))],
        out_specs=pl.BlockSpec((bsz, tn), lambda j: (0, j)),
        out_shape=jax.ShapeDtypeStruct((bsz, n), F32),
        compiler_params=_cparams("parallel"),
    )(c, w, b.reshape(1, n))


def _modulated_norm(x, g, shift, scale):
    y = x * lax.rsqrt(jnp.mean(x * x, axis=-1, keepdims=True) + EPS) * g
    return y * (1.0 + scale) + shift


INPROJ_COLS = 512


def _inproj_kernel(x_ref, mod_ref, g_ref, w_ref, qg_ref, kg_ref, o_ref, qa_ref, qb_ref, knt_ref, *,
                   q_chunk):
    h = _modulated_norm(x_ref[0], g_ref[...], mod_ref[0, 0:1, :], mod_ref[0, 1:2, :]).astype(BF16)
    lo_half = lax.broadcasted_iota(jnp.int32, (1, LANES), 1) < SB_HEAD_DIM
    for j in range(w_ref.shape[1] // INPROJ_COLS):
        cols = slice(j * INPROJ_COLS, (j + 1) * INPROJ_COLS)
        res = _dot(h, w_ref[:, cols])
        o_ref[0, :, cols] = res.astype(o_ref.dtype)
        if j == q_chunk:
            for p in range(INPROJ_COLS // LANES):
                pair = slice(p * LANES, (p + 1) * LANES)
                qn = _pair_norm(res[:, pair], qg_ref[...], lo_half) * (SB_HEAD_DIM ** -0.5 * LOG2_E)
                qa_ref[0, :, pair] = jnp.where(lo_half, qn, 0.0).astype(BF16)
                qb_ref[0, :, pair] = jnp.where(lo_half, 0.0, qn).astype(BF16)
        if j == q_chunk + 1:
            for p in range(INPROJ_COLS // LANES):
                pair = slice(p * LANES, (p + 1) * LANES)
                knt_ref[0, pair, :] = _pair_norm(res[:, pair], kg_ref[...], lo_half).T.astype(BF16)


def _inproj(x, mod, g, w, qg, kg, q_col0, b0, after):
    _, s, d = x.shape
    bsz = mod.shape[0]
    n = w.shape[1]
    tm = min(ROW_TILE, s)
    assert INPROJ_COLS == SB_WIDTH and (q_col0 * LANES) % INPROJ_COLS == 0
    pair_gain = lambda t: jnp.tile(t.reshape(1, SB_HEAD_DIM), (1, 2))
    in_specs = [pl.BlockSpec((1, tm, d), lambda b, i: (b0 + b, i, 0)),
                pl.BlockSpec((1, N_MOD, d), lambda b, i: (b, 0, 0)),
                pl.BlockSpec((1, d), lambda b, i: (0, 0)),
                pl.BlockSpec((d, n), lambda b, i: (0, 0)),
                pl.BlockSpec((1, LANES), lambda b, i: (0, 0)),
                pl.BlockSpec((1, LANES), lambda b, i: (0, 0))]
    args = [x, mod, g.reshape(1, d), w, pair_gain(qg), pair_gain(kg)]
    body = _run_after(functools.partial(_inproj_kernel, q_chunk=q_col0 * LANES // INPROJ_COLS),
                      in_specs, args, after)
    return pl.pallas_call(
        body,
        grid=(bsz, s // tm),
        in_specs=in_specs,
        out_specs=[pl.BlockSpec((1, tm, n), lambda b, i: (b, i, 0)),
                   pl.BlockSpec((1, tm, SB_WIDTH), lambda b, i: (b, i, 0)),
                   pl.BlockSpec((1, tm, SB_WIDTH), lambda b, i: (b, i, 0)),
                   pl.BlockSpec((1, SB_WIDTH, tm), lambda b, i: (b, 0, i))],
        out_shape=[jax.ShapeDtypeStruct((bsz, s, n), BF16),
                   jax.ShapeDtypeStruct((bsz, s, SB_WIDTH), BF16),
                   jax.ShapeDtypeStruct((bsz, s, SB_WIDTH), BF16),
                   jax.ShapeDtypeStruct((bsz, SB_WIDTH, s), BF16)],
        compiler_params=_cparams("parallel", "parallel"),
    )(*args)


def _pair_norm(t, g, lo_half):
    sq = t * t
    s_lo = jnp.sum(jnp.where(lo_half, sq, 0.0), axis=-1, keepdims=True)
    s_hi = jnp.sum(jnp.where(lo_half, 0.0, sq), axis=-1, keepdims=True)
    ms = jnp.where(lo_half, s_lo, s_hi) * (1.0 / SB_HEAD_DIM)
    return t * lax.rsqrt(ms + EPS) * g


def _neg_abs(t):
    bits = lax.bitcast_convert_type(t, jnp.uint32) | jnp.uint32(0x80000000)
    return lax.bitcast_convert_type(bits, F32)


def _split_trunc(t):
    bits = lax.bitcast_convert_type(t, jnp.uint32) & jnp.uint32(0xFFFF0000)
    hi = lax.bitcast_convert_type(bits, F32)
    return hi.astype(BF16), (t - hi).astype(BF16)


def _sb_kernel(qa_ref, qb_ref, knt_ref, v_ref, tri2_ref, o_ref, va_ref, vb_ref, *, tq):
    qi = pl.program_id(2)
    s = v_ref.shape[1]
    lane = lax.broadcasted_iota(jnp.int32, (1, LANES), 1)
    lo_half = lane < SB_HEAD_DIM

    tk = tq // 2

    @pl.when(qi == 0)
    def _():
        def prep_block(j, c):
            rows = pl.ds(pl.multiple_of(j * tk, tk), tk)
            vb = v_ref[0, rows, :]
            va_ref[rows, :] = jnp.where(lo_half, vb, jnp.zeros_like(vb))
            vb_ref[rows, :] = jnp.where(lo_half, jnp.zeros_like(vb), vb)
            return c
        lax.fori_loop(0, s // tk, prep_block, 0)

    q_heads = (qa_ref[0], qb_ref[0])
    v_heads = (va_ref, vb_ref)

    row = lax.broadcasted_iota(jnp.int32, (tk, tk), 0)
    col = lax.broadcasted_iota(jnp.int32, (tk, tk), 1)
    strict = col < row
    tri2 = tri2_ref[...]

    def sweep(streams):
        cols = [[pl.ds(pl.multiple_of(j * tk, tk), tk) for j, _, _ in blocks]
                for _, _, _, blocks in streams]
        z = [[[_dot(qh[h], knt_ref[0, :, c]) for c in cols[i]] for h in range(2)]
             for i, (qh, _, _, _) in enumerate(streams)]
        cs = []
        for i, (_, _, _, blocks) in enumerate(streams):
            cs.append([[None] * len(blocks) for _ in range(2)])
            for h in range(2):
                for b, (_, masked, _) in enumerate(blocks):
                    zb = z[i][h][b]
                    sp = jnp.maximum(zb, 0.0) + jnp.log2(1.0 + jnp.exp2(_neg_abs(zb)))
                    if masked:
                        sp = jnp.where(strict, sp, 0.0)
                    cs[i][h][b] = _dot(jnp.concatenate(_split_trunc(sp), axis=1), tri2)
        results = []
        for i, (_, runs, acc, blocks) in enumerate(streams):
            runs = list(runs)
            for h in range(2):
                for b, (_, masked, gate) in enumerate(blocks):
                    a = jnp.exp2(z[i][h][b] - cs[i][h][b] - runs[h])
                    if masked:
                        a = jnp.where(strict, a, 0.0)
                    vb = v_heads[h][cols[i][b], :]
                    step = cs[i][h][b][:, 0:1]
                    if gate is not None:
                        vb = jnp.where(gate, vb, jnp.zeros_like(vb))
                        step = jnp.where(gate, step, 0.0)
                    acc = acc + _dot(a.astype(BF16), vb)
                    runs[h] = runs[h] + step
            results.append((tuple(runs), acc))
        return results

    zero_run = jnp.zeros((tk, 1), F32)
    zero_acc = jnp.zeros((tk, LANES), F32)
    left, right = 2 * qi, 2 * qi + 1
    q_top = tuple(qh[:tk] for qh in q_heads)
    q_bot = tuple(qh[tk:] for qh in q_heads)
    top, bot = sweep([
        (q_top, (zero_run, zero_run), zero_acc,
         [(left, True, None), (jnp.maximum(left - 1, 0), False, qi > 0)]),
        (q_bot, (zero_run, zero_run), zero_acc, [(right, True, None), (left, False, None)])])

    def min_run(rt, rb):
        return jnp.min(jnp.minimum(jnp.minimum(rt[0], rt[1]), jnp.minimum(rb[0], rb[1])))

    def alive(carry):
        j, _, _, low = carry
        return jnp.logical_and(j >= 0, low < SB_DEAD_LOG2)

    def earlier_block(carry):
        j, (rt, at), (rb, ab), _ = carry
        t, b = sweep([(q_top, rt, at, [(jnp.maximum(j - 1, 0), False, j > 0)]),
                      (q_bot, rb, ab, [(j, False, None)])])
        return j - 1, t, b, min_run(t[0], b[0])

    _, top, bot, _ = lax.while_loop(alive, earlier_block,
                                    (left - 1, top, bot, min_run(top[0], bot[0])))
    o_ref[0] = jnp.concatenate([top[1], bot[1]], axis=0).astype(o_ref.dtype)


def _sb_attention(proj, qa, qb, knt, col0, after):
    bsz, s, _ = proj.shape
    tq = min(SB_QUERY_BLOCK, s)
    tk = tq // 2
    npair = SB_WIDTH // LANES
    tri = (jnp.arange(tk)[:, None] >= jnp.arange(tk)[None, :]).astype(BF16)
    tri2 = jnp.concatenate([tri, tri], axis=0)
    in_specs = [pl.BlockSpec((1, tq, LANES), lambda b, h, i: (b, i, h)),
                pl.BlockSpec((1, tq, LANES), lambda b, h, i: (b, i, h)),
                pl.BlockSpec((1, LANES, s), lambda b, h, i: (b, h, 0)),
                pl.BlockSpec((1, s, LANES), lambda b, h, i: (b, 0, col0 + 2 * npair + h)),
                pl.BlockSpec((2 * tk, tk), lambda b, h, i: (0, 0))]
    args = [qa, qb, knt, proj, tri2]
    body = _run_after(functools.partial(_sb_kernel, tq=tq), in_specs, args, after)
    return pl.pallas_call(
        body,
        grid=(bsz, npair, s // tq),
        in_specs=in_specs,
        out_specs=pl.BlockSpec((1, tq, LANES), lambda b, h, i: (b, i, h)),
        out_shape=jax.ShapeDtypeStruct((bsz, s, SB_WIDTH), BF16),
        scratch_shapes=[pltpu.VMEM((s, LANES), BF16),
                        pltpu.VMEM((s, LANES), BF16)],
        compiler_params=_cparams("parallel", "parallel", "arbitrary"),
    )(*args)


HG_CHUNKS_PER_STEP = 4


def _hgrn_kernel(f_ref, i_ref, q_ref, g_ref, lbl_ref, ng_ref, o_ref, st_ref, *, layer):
    ts = f_ref.shape[1]
    c = HG_CHUNK
    dh = HG_HEAD_DIM

    @pl.when(pl.program_id(1) == 0)
    def _():
        st_ref[...] = jnp.zeros_like(st_ref)

    lg = lbl_ref[...]
    e = jnp.exp(lg - jnp.max(lg, axis=0, keepdims=True))
    lb = jnp.sum(e[:layer + 1], axis=0, keepdims=True) / jnp.sum(e, axis=0, keepdims=True)

    row = lax.broadcasted_iota(jnp.int32, (c, c), 0)
    col = lax.broadcasted_iota(jnp.int32, (c, c), 1)
    causal = col <= row
    lower = causal.astype(BF16)
    width = f_ref.shape[2]

    def body(it, carry):
        states = [st_ref[h] for h in range(HG_HEADS)]
        for u in range(HG_CHUNKS_PER_STEP):
            rows = pl.ds(pl.multiple_of((it * HG_CHUNKS_PER_STEP + u) * c, c), c)
            forget = lb + (1.0 - lb) * jax.nn.sigmoid(f_ref[0, rows, :].astype(F32))
            kc = 1.0 - forget
            hi, lo = _split_trunc(jnp.log(forget))
            cum = _dot(lower, jnp.concatenate([hi, lo], axis=1))
            gc = cum[:, :width] + cum[:, width:]
            g_mid = gc[c // 2 - 1:c // 2, :]
            g_last = gc[c - 1:c, :]
            qe = _silu(q_ref[0, rows, :].astype(F32)) * jnp.exp(gc - g_mid)
            ke = kc * jnp.exp(g_mid - gc)
            qg = (qe * jnp.exp(g_mid)).astype(BF16)
            kd = (ke * jnp.exp(g_last - g_mid)).astype(BF16)
            qe = qe.astype(BF16)
            ke = ke.astype(BF16)
            decay = jnp.exp(g_last)
            v = i_ref[0, rows, :]
            v_t = v.astype(F32).T.astype(BF16)
            outs = []
            for h in range(HG_HEADS):
                sl = slice(h * dh, (h + 1) * dh)
                scores = _dot_nt(qe[:, sl], ke[:, sl])
                o = (_dot(jnp.where(causal, scores, 0.0).astype(BF16), v[:, sl])
                     + _dot_nt(qg[:, sl], states[h].astype(BF16)))
                outs.append(o * lax.rsqrt(jnp.mean(o * o, axis=-1, keepdims=True) + EPS))
                states[h] = states[h] * decay[:, sl] + _dot(v_t[sl, :], kd[:, sl])
            y = jnp.concatenate(outs, axis=1) * ng_ref[...] * _silu(g_ref[0, rows, :].astype(F32))
            o_ref[0, rows, :] = y.astype(o_ref.dtype)
        for h in range(HG_HEADS):
            st_ref[h] = states[h]
        return carry

    lax.fori_loop(0, ts // (c * HG_CHUNKS_PER_STEP), body, 0)


def _hgrn(proj, lb_logits, norm_g, col0, layer):
    bsz, s, _ = proj.shape
    nl = lb_logits.shape[0]
    ts = min(HG_SEQ_TILE, s)
    blk0 = col0 * LANES // HG_WIDTH
    spec = lambda off: pl.BlockSpec((1, ts, HG_WIDTH), lambda b, i: (b, i, blk0 + off))
    return pl.pallas_call(
        functools.partial(_hgrn_kernel, layer=layer),
        grid=(bsz, s // ts),
        in_specs=[spec(0), spec(1), spec(2), spec(3),
                  pl.BlockSpec((nl, HG_WIDTH), lambda b, i: (0, 0)),
                  pl.BlockSpec((1, HG_WIDTH), lambda b, i: (0, 0))],
        out_specs=pl.BlockSpec((1, ts, HG_WIDTH), lambda b, i: (b, i, 0)),
        out_shape=jax.ShapeDtypeStruct((bsz, s, HG_WIDTH), BF16),
        scratch_shapes=[pltpu.VMEM((HG_HEADS, HG_HEAD_DIM, HG_HEAD_DIM), F32)],
        compiler_params=_cparams("parallel", "arbitrary"),
    )(proj, proj, proj, proj, lb_logits, jnp.tile(norm_g.reshape(1, HG_HEAD_DIM), (1, HG_HEADS)))


def _pack_halves(t):
    bits = lax.bitcast_convert_type(t.astype(BF16).astype(F32), jnp.uint32)
    w = t.shape[1] // 2
    return lax.bitcast_convert_type(bits[:, :w] | (bits[:, w:] >> 16), jnp.int32)


def _unpack_halves(p):
    u = lax.bitcast_convert_type(p, jnp.uint32)
    return (lax.bitcast_convert_type(u & jnp.uint32(0xFFFF0000), F32),
            lax.bitcast_convert_type(u << 16, F32))


def _store_planes(ref, rows, t):
    p = _pack_halves(t)
    q = p.shape[1] // 2
    ref[0, rows, :] = p[:, :q]
    ref[1, rows, :] = p[:, q:]


def _load_planes(p0, p1):
    a0, b0 = _unpack_halves(p0)
    a1, b1 = _unpack_halves(p1)
    return jnp.concatenate([a0, a1, b0, b1], axis=1)


def _swiglu_hidden(h, w_gu):
    gu = _dot(h, w_gu)
    hid = w_gu.shape[1] // 2
    return _silu(gu[:, :hid]) * gu[:, hid:]


def _merge_kernel(x_ref, osb_ref, ohg_ref, gsb_ref, ghg_ref, mod_ref, g2_ref, wsb_ref, whg_ref,
                  wout_ref, wrh_ref, wrl_ref, x1_ref, h2p_ref, lg_ref):
    m_sb = _dot(osb_ref[0], wsb_ref[...])
    m_hg = _dot(ohg_ref[0], whg_ref[...])
    merged = (jax.nn.sigmoid(gsb_ref[0].astype(F32)) * m_sb
              + jax.nn.sigmoid(ghg_ref[0].astype(F32)) * m_hg)
    x1 = x_ref[0] + mod_ref[0, 2:3, :] * _dot(merged.astype(BF16), wout_ref[...])
    x1_ref[0] = x1
    h2 = _modulated_norm(x1, g2_ref[...], mod_ref[0, 3:4, :], mod_ref[0, 4:5, :])
    hi, lo = _split_bf16(h2)
    _store_planes(h2p_ref, slice(None), h2)
    lg_ref[...] = _dot_nt(wrh_ref[...], hi) + _dot_nt(wrh_ref[...], lo) + _dot_nt(wrl_ref[...], hi)


def _merge(x, o_sb, o_hg, proj, mod, g2, w_sb, w_hg, w_out, wr_hi, wr_lo, b0, after):
    _, s, d = x.shape
    bsz = mod.shape[0]
    tm = min(ROW_TILE, s)
    ns = s // tm
    full = lambda shape: pl.BlockSpec(shape, lambda b, i: (0,) * len(shape))
    in_specs = [pl.BlockSpec((1, tm, d), lambda b, i: (b0 + b, i, 0)),
                pl.BlockSpec((1, tm, SB_WIDTH), lambda b, i: (b, i, 0)),
                pl.BlockSpec((1, tm, HG_WIDTH), lambda b, i: (b, i, 0)),
                pl.BlockSpec((1, tm, d), lambda b, i: (b, i, 0)),
                pl.BlockSpec((1, tm, d), lambda b, i: (b, i, 1)),
                pl.BlockSpec((1, N_MOD, d), lambda b, i: (b, 0, 0)),
                full((1, d)), full(w_sb.shape), full(w_hg.shape), full(w_out.shape),
                full(wr_hi.shape), full(wr_lo.shape)]
    args = [x, o_sb, o_hg, proj, proj, mod, g2.reshape(1, d), w_sb, w_hg, w_out, wr_hi, wr_lo]
    body = _run_after(_merge_kernel, in_specs, args, after)
    return pl.pallas_call(
        body,
        grid=(bsz, ns),
        in_specs=in_specs,
        out_specs=[pl.BlockSpec((1, tm, d), lambda b, i: (b, i, 0)),
                   pl.BlockSpec((2, tm, d // 4), lambda b, i: (0, b * ns + i, 0)),
                   pl.BlockSpec((N_EXPERTS, tm), lambda b, i: (0, b * ns + i))],
        out_shape=[jax.ShapeDtypeStruct((bsz, s, d), F32),
                   jax.ShapeDtypeStruct((2, bsz * s, d // 4), jnp.int32),
                   jax.ShapeDtypeStruct((N_EXPERTS, bsz * s), F32)],
        compiler_params=_cparams("parallel", "parallel"),
    )(*args)


def _first_argmax(vals, idx, sentinel):
    m = jnp.max(vals, axis=0, keepdims=True)
    first = jnp.min(jnp.where(vals == m, idx, sentinel), axis=0, keepdims=True)
    return m, first


def _route_kernel(lg_ref, bias_ref, gates_ref, rank_ref, cnt_ref, run_ref):
    tn = lg_ref.shape[1]

    @pl.when(pl.program_id(0) == 0)
    def _():
        run_ref[...] = jnp.zeros_like(run_ref)

    neg = -jnp.inf
    scores = jax.nn.sigmoid(lg_ref[...])
    choice = scores + bias_ref[...]

    gidx = lax.broadcasted_iota(jnp.int32, (GROUP_SIZE, tn), 0)
    group_rows = []
    for g in range(N_GROUPS):
        cg = choice[g * GROUP_SIZE:(g + 1) * GROUP_SIZE, :]
        m1, i1 = _first_argmax(cg, gidx, GROUP_SIZE)
        m2 = jnp.max(jnp.where(gidx == i1, neg, cg), axis=0, keepdims=True)
        group_rows.append(m1 + m2)
    work = jnp.concatenate(group_rows, axis=0)
    ggi = lax.broadcasted_iota(jnp.int32, (N_GROUPS, tn), 0)
    gmask = jnp.zeros((N_GROUPS, tn), F32)
    for _ in range(TOPK_GROUPS):
        _, first = _first_argmax(work, ggi, N_GROUPS)
        pick = ggi == first
        gmask = jnp.where(pick, 1.0, gmask)
        work = jnp.where(pick, neg, work)

    masked = jnp.concatenate(
        [jnp.where(gmask[g:g + 1, :] > 0.0, choice[g * GROUP_SIZE:(g + 1) * GROUP_SIZE, :], neg)
         for g in range(N_GROUPS)], axis=0)
    eidx = lax.broadcasted_iota(jnp.int32, (N_EXPERTS, tn), 0)
    sel = jnp.zeros((N_EXPERTS, tn), F32)
    for _ in range(TOP_K):
        _, first = _first_argmax(masked, eidx, N_EXPERTS)
        pick = eidx == first
        sel = jnp.where(pick, 1.0, sel)
        masked = jnp.where(pick, neg, masked)

    chosen = jnp.where(sel > 0.0, scores, 0.0)
    gates_ref[...] = chosen / jnp.sum(chosen, axis=0, keepdims=True) * ROUTED_SCALE

    r = lax.broadcasted_iota(jnp.int32, (tn, tn), 0)
    c = lax.broadcasted_iota(jnp.int32, (tn, tn), 1)
    local = _dot(sel.astype(BF16), (r < c).astype(BF16))
    run = run_ref[:, 0:1]
    rank_ref[...] = jnp.where(sel > 0.0, run + local, -1.0)
    total = run + jnp.sum(sel, axis=1, keepdims=True)
    run_ref[...] = jnp.broadcast_to(total, run_ref.shape)
    cnt_ref[...] = jnp.broadcast_to(total, cnt_ref.shape)


def _route(logits_t, bias):
    e, t = logits_t.shape
    tn = min(ROUTE_TOKENS, t)
    return pl.pallas_call(
        _route_kernel,
        grid=(t // tn,),
        in_specs=[pl.BlockSpec((e, tn), lambda i: (0, i)),
                  pl.BlockSpec((e, 1), lambda i: (0, 0))],
        out_specs=[pl.BlockSpec((e, tn), lambda i: (0, i)),
                   pl.BlockSpec((e, tn), lambda i: (0, i)),
                   pl.BlockSpec((e, LANES), lambda i: (0, 0))],
        out_shape=[jax.ShapeDtypeStruct((e, t), F32),
                   jax.ShapeDtypeStruct((e, t), F32),
                   jax.ShapeDtypeStruct((e, LANES), F32)],
        scratch_shapes=[pltpu.VMEM((e, LANES), F32)],
        compiler_params=_cparams("arbitrary"),
    )(logits_t, bias.reshape(e, 1))


def _slots_kernel(gates_ref, rank_ref, cnt_ref, slot_ref, gate8_ref, blk_ref, *, rows, n_blocks):
    ne, tn = gates_ref.shape
    cnt = cnt_ref[...]
    nblk = jnp.floor((cnt + (rows - 1.0)) * (1.0 / rows))
    er = lax.broadcasted_iota(jnp.int32, (ne, ne), 0)
    ec = lax.broadcasted_iota(jnp.int32, (ne, ne), 1)
    lower = (ec < er).astype(BF16)
    pad_start = _dot(lower, nblk.astype(BF16))[:, 0:1] * rows
    pad_end = pad_start + nblk[:, 0:1] * rows

    rank = rank_ref[...]
    sel = rank >= 0.0
    slot_e = pad_start + rank
    kidx = _dot(lower, sel.astype(BF16))
    gates = gates_ref[...]
    slot_rows, gate_rows = [], []
    for k in range(TOP_K):
        m = jnp.logical_and(sel, kidx == k)
        slot_rows.append(jnp.sum(jnp.where(m, slot_e, 0.0), axis=0, keepdims=True))
        gate_rows.append(jnp.sum(jnp.where(m, gates, 0.0), axis=0, keepdims=True))
    slot_ref[...] = jnp.concatenate(slot_rows, axis=0).astype(jnp.int32)
    gate8_ref[...] = jnp.concatenate(gate_rows, axis=0).T

    nbp = blk_ref.shape[1]
    bstart = lax.broadcasted_iota(jnp.int32, (1, nbp), 1).astype(F32) * rows
    e_of = jnp.sum((pad_end <= bstart).astype(F32), axis=0, keepdims=True)
    e_of = jnp.minimum(e_of, ne - 1.0)
    eidx = lax.broadcasted_iota(jnp.int32, (ne, nbp), 0).astype(F32)
    valid_e = jnp.clip(cnt[:, 0:1] - (bstart - pad_start), 0.0, rows)
    valid = jnp.sum(jnp.where(eidx == e_of, valid_e, 0.0), axis=0, keepdims=True)
    total = pad_end[ne - 1:ne, :]
    used = bstart < total
    bidx = bstart * (1.0 / rows)
    src = jnp.where(used, bidx, total * (1.0 / rows) - 1.0)
    dst = jnp.where(used, bidx, float(n_blocks))
    blk_ref[...] = jnp.concatenate(
        [e_of, valid, src, dst, jnp.zeros((blk_ref.shape[0] - 4, nbp), F32)], axis=0).astype(jnp.int32)


def _slots(gates_t, rank_t, counts, rows, n_blocks):
    nbp = -(-n_blocks // LANES) * LANES
    e, t = gates_t.shape
    assert t // rows <= 256
    tn = min(ROUTE_TOKENS, t)
    return pl.pallas_call(
        functools.partial(_slots_kernel, rows=rows, n_blocks=n_blocks),
        grid=(t // tn,),
        in_specs=[pl.BlockSpec((e, tn), lambda i: (0, i)),
                  pl.BlockSpec((e, tn), lambda i: (0, i)),
                  pl.BlockSpec((e, LANES), lambda i: (0, 0))],
        out_specs=[pl.BlockSpec((TOP_K, tn), lambda i: (0, i)),
                   pl.BlockSpec((tn, TOP_K), lambda i: (i, 0)),
                   pl.BlockSpec((8, nbp), lambda i: (0, 0))],
        out_shape=[jax.ShapeDtypeStruct((TOP_K, t), jnp.int32),
                   jax.ShapeDtypeStruct((t, TOP_K), F32),
                   jax.ShapeDtypeStruct((8, nbp), jnp.int32)],
        compiler_params=_cparams("arbitrary"),
    )(gates_t, rank_t, counts)


SC_WINDOW = 128


def _sc_mesh():
    return plsc.VectorSubcoreMesh(core_axis_name="core", subcore_axis_name="subcore")


def _sc_gather_rows(table, idx):
    n = idx.shape[1]
    w = table.shape[1]

    @pl.kernel(out_type=jax.ShapeDtypeStruct((n, w), table.dtype), mesh=_sc_mesh())
    def gather(t_hbm, i_hbm, o_hbm):
        def body(i_vmem, o_vmem):
            pltpu.sync_copy(t_hbm.at[i_vmem.at[0]], o_vmem)

        pltpu.emit_pipeline(
            body, grid=(n // SC_WINDOW,),
            in_specs=[pl.BlockSpec((1, SC_WINDOW), lambda i: (0, i))],
            out_specs=[pl.BlockSpec((SC_WINDOW, w), lambda i: (i, 0))],
            core_axis_name=("core", "subcore"),
            dimension_semantics=(pltpu.PARALLEL,),
        )(i_hbm, o_hbm)

    return gather(table, idx)


def _sc_scatter_rows(rows, idx, n_out):
    fan, m = idx.shape
    w = rows.shape[1]

    @pl.kernel(out_type=jax.ShapeDtypeStruct((n_out, w), rows.dtype), mesh=_sc_mesh())
    def scatter(r_hbm, i_hbm, o_hbm):
        def body(r_vmem, i_vmem):
            for k in range(fan):
                pltpu.sync_copy(r_vmem, o_hbm.at[i_vmem.at[k]])

        pltpu.emit_pipeline(
            body, grid=(m // SC_WINDOW,),
            in_specs=[pl.BlockSpec((SC_WINDOW, w), lambda i: (i, 0)),
                      pl.BlockSpec((fan, SC_WINDOW), lambda i: (0, i))],
            out_specs=[],
            core_axis_name=("core", "subcore"),
            dimension_semantics=(pltpu.PARALLEL,),
        )(r_hbm, i_hbm)

    return scatter(rows, idx)


def _gmm_kernel(be_ref, bv_ref, bs_ref, bd_ref, xs_ref, wg_ref, wu_ref, wd_ref, ys_ref, wgu_s, wd_s):
    b = pl.program_id(0)
    valid = bv_ref[b]
    hid = wg_ref.shape[2]

    @pl.when(jnp.logical_or(b == 0, be_ref[b] != be_ref[jnp.maximum(b - 1, 0)]))
    def _():
        wgu_s[:, :hid] = wg_ref[0].astype(BF16)
        wgu_s[:, hid:] = wu_ref[0].astype(BF16)
        wd_s[...] = wd_ref[0].astype(BF16)

    @pl.when(valid > 0)
    def _():
        x = _load_planes(xs_ref[0], xs_ref[1]).astype(BF16)
        y = _dot(_swiglu_hidden(x, wgu_s[...]).astype(BF16), wd_s[...])
        row = lax.broadcasted_iota(jnp.int32, (y.shape[0], 1), 0)
        _store_planes(ys_ref, slice(None), jnp.where(row < valid, y, 0.0))

    @pl.when(valid <= 0)
    def _():
        ys_ref[...] = jnp.zeros_like(ys_ref)


def _gmm(blk, xs, we_gate, we_up, we_down, rows, after):
    _, n_rows, q = xs.shape
    _, d, hid = we_gate.shape
    w_spec = lambda w: pl.BlockSpec((1,) + w.shape[1:], lambda b, be, bv, bs, bd: (be[b], 0, 0))
    in_specs = [pl.BlockSpec((2, rows, q), lambda b, be, bv, bs, bd: (0, bs[b], 0)),
                w_spec(we_gate), w_spec(we_up), w_spec(we_down)]
    args = [xs, we_gate, we_up, we_down]
    n_prefetch = 4
    body = _gmm_kernel
    if after is not None:
        pos = n_prefetch + len(args)
        in_specs.append(pl.BlockSpec(memory_space=pl.ANY))
        args.append(after)
        body = lambda *refs: _gmm_kernel(*refs[:pos], *refs[pos + 1:])
    return pl.pallas_call(
        body,
        grid_spec=pltpu.PrefetchScalarGridSpec(
            num_scalar_prefetch=n_prefetch,
            grid=(n_rows // rows - 1,),
            in_specs=in_specs,
            out_specs=pl.BlockSpec((2, rows, q), lambda b, be, bv, bs, bd: (0, bd[b], 0)),
            scratch_shapes=[pltpu.VMEM((d, 2 * hid), BF16), pltpu.VMEM((hid, d), BF16)]),
        out_shape=jax.ShapeDtypeStruct(xs.shape, jnp.int32),
        compiler_params=_cparams("arbitrary"),
    )(blk[0], blk[1], blk[2], blk[3], *args)


def _combine_kernel(x1_ref, mod_ref, y8_ref, g8_ref, h2p_ref, wsgu_ref, wsd_ref, *rest):
    o_ref = rest[-1]
    g8 = g8_ref[...]
    h2 = _load_planes(h2p_ref[0], h2p_ref[1]).astype(BF16)
    ffn = _dot(_swiglu_hidden(h2, wsgu_ref[...]).astype(BF16), wsd_ref[...])
    for k in range(TOP_K):
        ffn = ffn + g8[:, k:k + 1] * _load_planes(y8_ref[k, 0], y8_ref[k, 1])
    o_ref[0] = x1_ref[0] + mod_ref[0, 5:6, :] * ffn


def _combine(x1, mod, y8, gate8, h2p, ws_gu, ws_d, b0, total, earlier, after):
    bsz, s, d = x1.shape
    tm = min(COMBINE_ROWS, s)
    ns = s // tm
    full = lambda shape: pl.BlockSpec(shape, lambda b, i: (0,) * len(shape))
    in_specs = [pl.BlockSpec((1, tm, d), lambda b, i: (b, i, 0)),
                pl.BlockSpec((1, N_MOD, d), lambda b, i: (b, 0, 0)),
                pl.BlockSpec((TOP_K, 2, tm, d // 4), lambda b, i: (0, 0, b * ns + i, 0)),
                pl.BlockSpec((tm, TOP_K), lambda b, i: (b * ns + i, 0)),
                pl.BlockSpec((2, tm, d // 4), lambda b, i: (0, b * ns + i, 0)),
                full(ws_gu.shape), full(ws_d.shape)]
    args = [x1, mod, y8, gate8, h2p, ws_gu, ws_d]
    aliases = {}
    if earlier is not None:
        in_specs.append(pl.BlockSpec(memory_space=pl.ANY))
        args.append(earlier)
        aliases = {len(args) - 1: 0}
    if after is not None:
        in_specs.append(pl.BlockSpec(memory_space=pl.ANY))
        args.append(after)
    return pl.pallas_call(
        _combine_kernel,
        grid=(bsz, ns),
        in_specs=in_specs,
        out_specs=pl.BlockSpec((1, tm, d), lambda b, i: (b0 + b, i, 0)),
        out_shape=jax.ShapeDtypeStruct((total, s, d), F32),
        input_output_aliases=aliases,
        compiler_params=_cparams("parallel", "parallel"),
    )(*args)


def kernel(x, c, w_ada, b_ada, norm1_g, w_in, sb_q_norm_g, sb_k_norm_g, hg_lb_logits, hg_norm_g,
           w_branch_sb, w_branch_hg, w_out, norm2_g, w_router, router_bias, w_e_gate, w_e_up,
           w_e_down, w_s_gate, w_s_up, w_s_down):
    bsz, s, d = x.shape
    depth = w_ada.shape[0]
    n_gate_cols = 2 * d
    qkv_col0 = n_gate_cols // LANES
    hg_col0 = qkv_col0 + 3 * SB_WIDTH // LANES
    for l in range(depth):
        n_mix = 3 * SB_WIDTH + 4 * HG_WIDTH
        w_in_l = jnp.concatenate([w_in[l][:, n_mix:], w_in[l][:, :n_mix]], axis=1).astype(BF16)
        wr_t = w_router[l].T
        wr_hi = wr_t.astype(BF16)
        wr_lo = (wr_t - wr_hi.astype(F32)).astype(BF16)
        ws_gu = jnp.concatenate([w_s_gate[l], w_s_up[l]], axis=1).astype(BF16)

        w_sb, w_hg = w_branch_sb[l].astype(BF16), w_branch_hg[l].astype(BF16)
        w_o, ws_d = w_out[l].astype(BF16), w_s_down[l].astype(BF16)

        mod = _ada(c, w_ada[l], b_ada[l]).reshape(bsz, N_MOD, d)
        n_parts = BATCH_PARTS if bsz % BATCH_PARTS == 0 else 1
        pb = bsz // n_parts
        t = pb * s
        q = d // 4
        n_blocks = -(-(t * TOP_K + N_EXPERTS * (DISPATCH_ROWS - 1)) // DISPATCH_ROWS)
        n_rows = (n_blocks + 1) * DISPATCH_ROWS
        plane_off = jnp.array([0, n_rows], jnp.int32)[None, :, None]

        def experts(st, after):
            ys = _gmm(st["blk"], st["xs"].reshape(2, n_rows, q), w_e_gate[l], w_e_up[l],
                      w_e_down[l], DISPATCH_ROWS, after)
            y8 = _sc_gather_rows(ys.reshape(2 * n_rows, q), st["row_idx"].reshape(1, TOP_K * 2 * t))
            return ys, y8.reshape(TOP_K, 2, t, q)

        out = None
        prev = None
        for p in range(n_parts):
            b0 = p * pb
            mod_p = mod[b0:b0 + pb]
            proj, qa, qb, knt = _inproj(x, mod_p, norm1_g[l], w_in_l, sb_q_norm_g[l],
                                        sb_k_norm_g[l], qkv_col0, b0, prev and prev["blk"])
            o_hg = _hgrn(proj, hg_lb_logits, hg_norm_g[l], hg_col0, l)
            if prev:
                prev["ys"], prev["y8"] = experts(prev, o_hg)
            o_sb = _sb_attention(proj, qa, qb, knt, qkv_col0, prev and prev["ys"])
            if prev:
                out = _combine(prev["x1"], prev["mod"], prev["y8"], prev["gate8"], prev["h2p"],
                               ws_gu, ws_d, prev["b0"], bsz, out, o_sb)
            x1, h2p, logits_t = _merge(x, o_sb, o_hg, proj, mod_p, norm2_g[l], w_sb, w_hg, w_o,
                                       wr_hi, wr_lo, b0, out)
            gates_t, rank_t, counts = _route(logits_t, router_bias[l])
            slot8, gate8, blk = _slots(gates_t, rank_t, counts, DISPATCH_ROWS, n_blocks)
            row_idx = (slot8[:, None, :] + plane_off).reshape(TOP_K, 2 * t)
            xs = _sc_scatter_rows(h2p.reshape(2 * t, q), row_idx, 2 * n_rows)
            prev = dict(blk=blk, xs=xs, row_idx=row_idx, x1=x1, h2p=h2p, mod=mod_p, gate8=gate8,
                        b0=b0)
        _, y8 = experts(prev, None)
        x = _combine(prev["x1"], prev["mod"], y8, prev["gate8"], prev["h2p"], ws_gu, ws_d,
                     prev["b0"], bsz, out, None)
    return x
```

```python
import functools

import jax
import jax.numpy as jnp
from jax import lax
from jax.experimental import pallas as pl
from jax.experimental.pallas import tpu as pltpu
from jax.experimental.pallas import tpu_sc as plsc

F32 = jnp.float32
BF16 = jnp.bfloat16

SB_HEADS = 8
SB_HEAD_DIM = 64
SB_WIDTH = SB_HEADS * SB_HEAD_DIM
HG_HEADS = 4
HG_HEAD_DIM = 128
HG_WIDTH = HG_HEADS * HG_HEAD_DIM
HG_CHUNK = 64
N_EXPERTS = 64
TOP_K = 8
N_GROUPS = 8
TOPK_GROUPS = 4
GROUP_SIZE = N_EXPERTS // N_GROUPS
ROUTED_SCALE = 2.5
DISPATCH_ROWS = 1024
BATCH_PARTS = 2
N_MOD = 6
EPS = 1e-6
LOG2_E = 1.4426950408889634
SB_DEAD_LOG2 = 160.0

LANES = 128
VMEM_LIMIT = 56 * 1024 * 1024

ADA_COLS = 1024
ROW_TILE = 512
SB_QUERY_BLOCK = 512
HG_SEQ_TILE = 1024
ROUTE_TOKENS = 1024
COMBINE_ROWS = 512


def _cparams(*sem):
    return pltpu.CompilerParams(dimension_semantics=sem, vmem_limit_bytes=VMEM_LIMIT)


def _silu(t):
    return t * jax.nn.sigmoid(t)


def _dot(a, b):
    return jnp.dot(a, b, preferred_element_type=F32)


def _run_after(body, in_specs, args, after):
    if after is None:
        return body
    pos = len(args)
    in_specs.append(pl.BlockSpec(memory_space=pl.ANY))
    args.append(after)
    return lambda *refs: body(*refs[:pos], *refs[pos + 1:])


def _dot_nt(a, b):
    return lax.dot_general(a, b, (((1,), (1,)), ((), ())), preferred_element_type=F32)


def _split_bf16(t):
    hi = t.astype(BF16)
    lo = (t - hi.astype(F32)).astype(BF16)
    return hi, lo


def _ada_kernel(c_ref, w_ref, b_ref, o_ref):
    cond = _silu(c_ref[...])
    o_ref[...] = _dot(cond, w_ref[...]) + b_ref[...]


def _ada(c, w, b):
    bsz, d = c.shape
    n = w.shape[1]
    tn = min(ADA_COLS, n)
    return pl.pallas_call(
        _ada_kernel,
        grid=(n // tn,),
        in_specs=[pl.BlockSpec((bsz, d), lambda j: (0, 0)),
                  pl.BlockSpec((d, tn), lambda j: (0, j)),
                  pl.BlockSpec((1, tn), lambda j: (0, j))],
        out_specs=pl.BlockSpec((bsz, tn), lambda j: (0, j)),
        out_shape=jax.ShapeDtypeStruct((bsz, n), F32),
        compiler_params=_cparams("parallel"),
    )(c, w, b.reshape(1, n))


def _modulated_norm(x, g, shift, scale):
    y = x * lax.rsqrt(jnp.mean(x * x, axis=-1, keepdims=True) + EPS) * g
    return y * (1.0 + scale) + shift


INPROJ_COLS = 512


def _inproj_kernel(x_ref, mod_ref, g_ref, w_ref, qg_ref, kg_ref, lbl_ref, o_ref, qa_ref, qb_ref,
                   knt_ref, kc_ref, lfh_ref, lfl_ref, qs_ref, gs_ref, *, q_chunk, layer):
    h = _modulated_norm(x_ref[0], g_ref[...], mod_ref[0, 0:1, :], mod_ref[0, 1:2, :]).astype(BF16)
    lo_half = lax.broadcasted_iota(jnp.int32, (1, LANES), 1) < SB_HEAD_DIM
    for j in range(w_ref.shape[1] // INPROJ_COLS):
        cols = slice(j * INPROJ_COLS, (j + 1) * INPROJ_COLS)
        res = _dot(h, w_ref[:, cols])
        o_ref[0, :, cols] = res.astype(o_ref.dtype)
        if j == q_chunk:
            for p in range(INPROJ_COLS // LANES):
                pair = slice(p * LANES, (p + 1) * LANES)
                qn = _pair_norm(res[:, pair], qg_ref[...], lo_half) * (SB_HEAD_DIM ** -0.5 * LOG2_E)
                qa_ref[0, :, pair] = jnp.where(lo_half, qn, 0.0).astype(BF16)
                qb_ref[0, :, pair] = jnp.where(lo_half, 0.0, qn).astype(BF16)
        if j == q_chunk + 1:
            for p in range(INPROJ_COLS // LANES):
                pair = slice(p * LANES, (p + 1) * LANES)
                knt_ref[0, pair, :] = _pair_norm(res[:, pair], kg_ref[...], lo_half).T.astype(BF16)
        if j == q_chunk + 3:
            lg = lbl_ref[...]
            e = jnp.exp(lg - jnp.max(lg, axis=0, keepdims=True))
            lb = jnp.sum(e[:layer + 1], axis=0, keepdims=True) / jnp.sum(e, axis=0, keepdims=True)
            forget = lb + (1.0 - lb) * jax.nn.sigmoid(res)
            kc_ref[0] = (1.0 - forget).astype(BF16)
            lfh_ref[0], lfl_ref[0] = _split_trunc(jnp.log(forget))
        if j == q_chunk + 5:
            qs_ref[0] = _silu(res).astype(BF16)
        if j == q_chunk + 6:
            gs_ref[0] = _silu(res).astype(BF16)


def _inproj(x, mod, g, w, qg, kg, lb_logits, layer, q_col0, b0, after):
    _, s, d = x.shape
    bsz = mod.shape[0]
    n = w.shape[1]
    nl = lb_logits.shape[0]
    tm = min(ROW_TILE, s)
    assert INPROJ_COLS == SB_WIDTH == HG_WIDTH and (q_col0 * LANES) % INPROJ_COLS == 0
    pair_gain = lambda t: jnp.tile(t.reshape(1, SB_HEAD_DIM), (1, 2))
    in_specs = [pl.BlockSpec((1, tm, d), lambda b, i: (b0 + b, i, 0)),
                pl.BlockSpec((1, N_MOD, d), lambda b, i: (b, 0, 0)),
                pl.BlockSpec((1, d), lambda b, i: (0, 0)),
                pl.BlockSpec((d, n), lambda b, i: (0, 0)),
                pl.BlockSpec((1, LANES), lambda b, i: (0, 0)),
                pl.BlockSpec((1, LANES), lambda b, i: (0, 0)),
                pl.BlockSpec((nl, HG_WIDTH), lambda b, i: (0, 0))]
    args = [x, mod, g.reshape(1, d), w, pair_gain(qg), pair_gain(kg), lb_logits]
    body = _run_after(functools.partial(_inproj_kernel, q_chunk=q_col0 * LANES // INPROJ_COLS,
                                        layer=layer), in_specs, args, after)
    rows_spec = pl.BlockSpec((1, tm, INPROJ_COLS), lambda b, i: (b, i, 0))
    rows_shape = jax.ShapeDtypeStruct((bsz, s, INPROJ_COLS), BF16)
    return pl.pallas_call(
        body,
        grid=(bsz, s // tm),
        in_specs=in_specs,
        out_specs=[pl.BlockSpec((1, tm, n), lambda b, i: (b, i, 0)),
                   rows_spec, rows_spec,
                   pl.BlockSpec((1, SB_WIDTH, tm), lambda b, i: (b, 0, i)),
                   rows_spec, rows_spec, rows_spec, rows_spec, rows_spec],
        out_shape=[jax.ShapeDtypeStruct((bsz, s, n), BF16),
                   rows_shape, rows_shape,
                   jax.ShapeDtypeStruct((bsz, SB_WIDTH, s), BF16),
                   rows_shape, rows_shape, rows_shape, rows_shape, rows_shape],
        compiler_params=_cparams("parallel", "parallel"),
    )(*args)


def _pair_norm(t, g, lo_half):
    sq = t * t
    s_lo = jnp.sum(jnp.where(lo_half, sq, 0.0), axis=-1, keepdims=True)
    s_hi = jnp.sum(jnp.where(lo_half, 0.0, sq), axis=-1, keepdims=True)
    ms = jnp.where(lo_half, s_lo, s_hi) * (1.0 / SB_HEAD_DIM)
    return t * lax.rsqrt(ms + EPS) * g


def _neg_abs(t):
    bits = lax.bitcast_convert_type(t, jnp.uint32) | jnp.uint32(0x80000000)
    return lax.bitcast_convert_type(bits, F32)


def _split_trunc(t):
    bits = lax.bitcast_convert_type(t, jnp.uint32) & jnp.uint32(0xFFFF0000)
    hi = lax.bitcast_convert_type(bits, F32)
    return hi.astype(BF16), (t - hi).astype(BF16)


def _sb_kernel(qa_ref, qb_ref, knt_ref, v_ref, tri2_ref, o_ref, va_ref, vb_ref, *, tq):
    qi = pl.program_id(2)
    s = v_ref.shape[1]
    lane = lax.broadcasted_iota(jnp.int32, (1, LANES), 1)
    lo_half = lane < SB_HEAD_DIM

    tk = tq // 2

    @pl.when(qi == 0)
    def _():
        def prep_block(j, c):
            rows = pl.ds(pl.multiple_of(j * tk, tk), tk)
            vb = v_ref[0, rows, :]
            va_ref[rows, :] = jnp.where(lo_half, vb, jnp.zeros_like(vb))
            vb_ref[rows, :] = jnp.where(lo_half, jnp.zeros_like(vb), vb)
            return c
        lax.fori_loop(0, s // tk, prep_block, 0)

    q_heads = (qa_ref[0], qb_ref[0])
    v_heads = (va_ref, vb_ref)

    row = lax.broadcasted_iota(jnp.int32, (tk, tk), 0)
    col = lax.broadcasted_iota(jnp.int32, (tk, tk), 1)
    strict = col < row
    tri2 = tri2_ref[...]

    def sweep(streams):
        cols = [[pl.ds(pl.multiple_of(j * tk, tk), tk) for j, _, _ in blocks]
                for _, _, _, blocks in streams]
        z = [[[_dot(qh[h], knt_ref[0, :, c]) for c in cols[i]] for h in range(2)]
             for i, (qh, _, _, _) in enumerate(streams)]
        cs = []
        for i, (_, _, _, blocks) in enumerate(streams):
            cs.append([[None] * len(blocks) for _ in range(2)])
            for h in range(2):
                for b, (_, masked, _) in enumerate(blocks):
                    zb = z[i][h][b]
                    sp = jnp.maximum(zb, 0.0) + jnp.log2(1.0 + jnp.exp2(_neg_abs(zb)))
                    if masked:
                        sp = jnp.where(strict, sp, 0.0)
                    cs[i][h][b] = _dot(jnp.concatenate(_split_trunc(sp), axis=1), tri2)
        results = []
        for i, (_, runs, acc, blocks) in enumerate(streams):
            runs = list(runs)
            for h in range(2):
                for b, (_, masked, gate) in enumerate(blocks):
                    a = jnp.exp2(z[i][h][b] - cs[i][h][b] - runs[h])
                    if masked:
                        a = jnp.where(strict, a, 0.0)
                    vb = v_heads[h][cols[i][b], :]
                    step = cs[i][h][b][:, 0:1]
                    if gate is not None:
                        vb = jnp.where(gate, vb, jnp.zeros_like(vb))
                        step = jnp.where(gate, step, 0.0)
                    acc = acc + _dot(a.astype(BF16), vb)
                    runs[h] = runs[h] + step
            results.append((tuple(runs), acc))
        return results

    zero_run = jnp.zeros((tk, 1), F32)
    zero_acc = jnp.zeros((tk, LANES), F32)
    left, right = 2 * qi, 2 * qi + 1
    q_top = tuple(qh[:tk] for qh in q_heads)
    q_bot = tuple(qh[tk:] for qh in q_heads)
    top, bot = sweep([
        (q_top, (zero_run, zero_run), zero_acc,
         [(left, True, None), (jnp.maximum(left - 1, 0), False, qi > 0)]),
        (q_bot, (zero_run, zero_run), zero_acc, [(right, True, None), (left, False, None)])])

    def min_run(rt, rb):
        return jnp.min(jnp.minimum(jnp.minimum(rt[0], rt[1]), jnp.minimum(rb[0], rb[1])))

    def alive(carry):
        j, _, _, low = carry
        return jnp.logical_and(j >= 0, low < SB_DEAD_LOG2)

    def earlier_block(carry):
        j, (rt, at), (rb, ab), _ = carry
        t, b = sweep([(q_top, rt, at, [(jnp.maximum(j - 1, 0), False, j > 0)]),
                      (q_bot, rb, ab, [(j, False, None)])])
        return j - 1, t, b, min_run(t[0], b[0])

    _, top, bot, _ = lax.while_loop(alive, earlier_block,
                                    (left - 1, top, bot, min_run(top[0], bot[0])))
    o_ref[0] = jnp.concatenate([top[1], bot[1]], axis=0).astype(o_ref.dtype)


def _sb_attention(proj, qa, qb, knt, col0, after):
    bsz, s, _ = proj.shape
    tq = min(SB_QUERY_BLOCK, s)
    tk = tq // 2
    npair = SB_WIDTH // LANES
    tri = (jnp.arange(tk)[:, None] >= jnp.arange(tk)[None, :]).astype(BF16)
    tri2 = jnp.concatenate([tri, tri], axis=0)
    in_specs = [pl.BlockSpec((1, tq, LANES), lambda b, h, i: (b, i, h)),
                pl.BlockSpec((1, tq, LANES), lambda b, h, i: (b, i, h)),
                pl.BlockSpec((1, LANES, s), lambda b, h, i: (b, h, 0)),
                pl.BlockSpec((1, s, LANES), lambda b, h, i: (b, 0, col0 + 2 * npair + h)),
                pl.BlockSpec((2 * tk, tk), lambda b, h, i: (0, 0))]
    args = [qa, qb, knt, proj, tri2]
    body = _run_after(functools.partial(_sb_kernel, tq=tq), in_specs, args, after)
    return pl.pallas_call(
        body,
        grid=(bsz, npair, s // tq),
        in_specs=in_specs,
        out_specs=pl.BlockSpec((1, tq, LANES), lambda b, h, i: (b, i, h)),
        out_shape=jax.ShapeDtypeStruct((bsz, s, SB_WIDTH), BF16),
        scratch_shapes=[pltpu.VMEM((s, LANES), BF16),
                        pltpu.VMEM((s, LANES), BF16)],
        compiler_params=_cparams("parallel", "parallel", "arbitrary"),
    )(*args)


HG_CHUNKS_PER_STEP = 8


def _hgrn_kernel(kc_ref, lfh_ref, lfl_ref, i_ref, qs_ref, gs_ref, ng_ref, o_ref, st_ref):
    ts = kc_ref.shape[1]
    c = HG_CHUNK
    dh = HG_HEAD_DIM

    @pl.when(pl.program_id(1) == 0)
    def _():
        st_ref[...] = jnp.zeros_like(st_ref)

    row = lax.broadcasted_iota(jnp.int32, (c, c), 0)
    col = lax.broadcasted_iota(jnp.int32, (c, c), 1)
    causal = col <= row
    lower = causal.astype(BF16)
    width = kc_ref.shape[2]

    def body(it, carry):
        chunks = range(HG_CHUNKS_PER_STEP)
        heads = [slice(h * dh, (h + 1) * dh) for h in range(HG_HEADS)]
        rows = [pl.ds(pl.multiple_of((it * HG_CHUNKS_PER_STEP + u) * c, c), c) for u in chunks]
        cum = [_dot(lower, jnp.concatenate([lfh_ref[0, r, :], lfl_ref[0, r, :]], axis=1))
               for r in rows]
        qe, ke, qg, kd, decay, v, v_t = [], [], [], [], [], [], []
        for u in chunks:
            gc = cum[u][:, :width] + cum[u][:, width:]
            g_mid = gc[c // 2 - 1:c // 2, :]
            g_last = gc[c - 1:c, :]
            q_u = qs_ref[0, rows[u], :].astype(F32) * jnp.exp(gc - g_mid)
            k_u = kc_ref[0, rows[u], :].astype(F32) * jnp.exp(g_mid - gc)
            qg.append((q_u * jnp.exp(g_mid)).astype(BF16))
            kd.append((k_u * jnp.exp(g_last - g_mid)).astype(BF16))
            qe.append(q_u.astype(BF16))
            ke.append(k_u.astype(BF16))
            decay.append(jnp.exp(g_last))
            v.append(i_ref[0, rows[u], :])
            v_t.append(v[u].astype(F32).T.astype(BF16))
        scores = [[_dot_nt(qe[u][:, sl], ke[u][:, sl]) for sl in heads] for u in chunks]
        kv_t = [[_dot(v_t[u][sl, :], kd[u][:, sl]) for sl in heads] for u in chunks]
        states = [[st_ref[h] for h in range(HG_HEADS)]]
        for u in chunks:
            states.append([states[u][h] * decay[u][:, sl] + kv_t[u][h] for h, sl in enumerate(heads)])
        for u in chunks:
            outs = []
            for h, sl in enumerate(heads):
                o = (_dot(jnp.where(causal, scores[u][h], 0.0).astype(BF16), v[u][:, sl])
                     + _dot_nt(qg[u][:, sl], states[u][h].astype(BF16)))
                outs.append(o * lax.rsqrt(jnp.mean(o * o, axis=-1, keepdims=True) + EPS))
            y = jnp.concatenate(outs, axis=1) * ng_ref[...] * gs_ref[0, rows[u], :].astype(F32)
            o_ref[0, rows[u], :] = y.astype(o_ref.dtype)
        for h in range(HG_HEADS):
            st_ref[h] = states[-1][h]
        return carry

    lax.fori_loop(0, ts // (c * HG_CHUNKS_PER_STEP), body, 0)


def _hgrn(proj, kc, lfh, lfl, qs, gs, norm_g, i_col0):
    bsz, s, _ = proj.shape
    ts = min(HG_SEQ_TILE, s)
    i_blk = i_col0 * LANES // HG_WIDTH
    term = pl.BlockSpec((1, ts, HG_WIDTH), lambda b, i: (b, i, 0))
    return pl.pallas_call(
        _hgrn_kernel,
        grid=(bsz, s // ts),
        in_specs=[term, term, term,
                  pl.BlockSpec((1, ts, HG_WIDTH), lambda b, i: (b, i, i_blk)),
                  term, term,
                  pl.BlockSpec((1, HG_WIDTH), lambda b, i: (0, 0))],
        out_specs=pl.BlockSpec((1, ts, HG_WIDTH), lambda b, i: (b, i, 0)),
        out_shape=jax.ShapeDtypeStruct((bsz, s, HG_WIDTH), BF16),
        scratch_shapes=[pltpu.VMEM((HG_HEADS, HG_HEAD_DIM, HG_HEAD_DIM), F32)],
        compiler_params=_cparams("parallel", "arbitrary"),
    )(kc, lfh, lfl, proj, qs, gs, jnp.tile(norm_g.reshape(1, HG_HEAD_DIM), (1, HG_HEADS)))


def _pack_halves(t):
    bits = lax.bitcast_convert_type(t.astype(BF16).astype(F32), jnp.uint32)
    w = t.shape[1] // 2
    return lax.bitcast_convert_type(bits[:, :w] | (bits[:, w:] >> 16), jnp.int32)


def _unpack_halves(p):
    u = lax.bitcast_convert_type(p, jnp.uint32)
    return (lax.bitcast_convert_type(u & jnp.uint32(0xFFFF0000), F32),
            lax.bitcast_convert_type(u << 16, F32))


def _store_planes(ref, rows, t):
    p = _pack_halves(t)
    q = p.shape[1] // 2
    ref[0, rows, :] = p[:, :q]
    ref[1, rows, :] = p[:, q:]


def _load_planes(p0, p1):
    a0, b0 = _unpack_halves(p0)
    a1, b1 = _unpack_halves(p1)
    return jnp.concatenate([a0, a1, b0, b1], axis=1)


def _swiglu_hidden(h, w_gu):
    gu = _dot(h, w_gu)
    hid = w_gu.shape[1] // 2
    return _silu(gu[:, :hid]) * gu[:, hid:]


def _merge_kernel(x_ref, osb_ref, ohg_ref, gsb_ref, ghg_ref, mod_ref, g2_ref, wsb_ref, whg_ref,
                  wout_ref, wrh_ref, wrl_ref, x1_ref, h2p_ref, lg_ref):
    m_sb = _dot(osb_ref[0], wsb_ref[...])
    m_hg = _dot(ohg_ref[0], whg_ref[...])
    merged = (jax.nn.sigmoid(gsb_ref[0].astype(F32)) * m_sb
              + jax.nn.sigmoid(ghg_ref[0].astype(F32)) * m_hg)
    x1 = x_ref[0] + mod_ref[0, 2:3, :] * _dot(merged.astype(BF16), wout_ref[...])
    x1_ref[0] = x1
    h2 = _modulated_norm(x1, g2_ref[...], mod_ref[0, 3:4, :], mod_ref[0, 4:5, :])
    hi, lo = _split_bf16(h2)
    _store_planes(h2p_ref, slice(None), h2)
    lg_ref[...] = _dot_nt(wrh_ref[...], hi) + _dot_nt(wrh_ref[...], lo) + _dot_nt(wrl_ref[...], hi)


def _merge(x, o_sb, o_hg, proj, mod, g2, w_sb, w_hg, w_out, wr_hi, wr_lo, b0, after):
    _, s, d = x.shape
    bsz = mod.shape[0]
    tm = min(ROW_TILE, s)
    ns = s // tm
    full = lambda shape: pl.BlockSpec(shape, lambda b, i: (0,) * len(shape))
    in_specs = [pl.BlockSpec((1, tm, d), lambda b, i: (b0 + b, i, 0)),
                pl.BlockSpec((1, tm, SB_WIDTH), lambda b, i: (b, i, 0)),
                pl.BlockSpec((1, tm, HG_WIDTH), lambda b, i: (b, i, 0)),
                pl.BlockSpec((1, tm, d), lambda b, i: (b, i, 0)),
                pl.BlockSpec((1, tm, d), lambda b, i: (b, i, 1)),
                pl.BlockSpec((1, N_MOD, d), lambda b, i: (b, 0, 0)),
                full((1, d)), full(w_sb.shape), full(w_hg.shape), full(w_out.shape),
                full(wr_hi.shape), full(wr_lo.shape)]
    args = [x, o_sb, o_hg, proj, proj, mod, g2.reshape(1, d), w_sb, w_hg, w_out, wr_hi, wr_lo]
    body = _run_after(_merge_kernel, in_specs, args, after)
    return pl.pallas_call(
        body,
        grid=(bsz, ns),
        in_specs=in_specs,
        out_specs=[pl.BlockSpec((1, tm, d), lambda b, i: (b, i, 0)),
                   pl.BlockSpec((2, tm, d // 4), lambda b, i: (0, b * ns + i, 0)),
                   pl.BlockSpec((N_EXPERTS, tm), lambda b, i: (0, b * ns + i))],
        out_shape=[jax.ShapeDtypeStruct((bsz, s, d), F32),
                   jax.ShapeDtypeStruct((2, bsz * s, d // 4), jnp.int32),
                   jax.ShapeDtypeStruct((N_EXPERTS, bsz * s), F32)],
        compiler_params=_cparams("parallel", "parallel"),
    )(*args)


def _first_argmax(vals, idx, sentinel):
    m = jnp.max(vals, axis=0, keepdims=True)
    first = jnp.min(jnp.where(vals == m, idx, sentinel), axis=0, keepdims=True)
    return m, first


def _route_kernel(lg_ref, bias_ref, gates_ref, rank_ref, cnt_ref, run_ref):
    tn = lg_ref.shape[1]

    @pl.when(pl.program_id(0) == 0)
    def _():
        run_ref[...] = jnp.zeros_like(run_ref)

    neg = -jnp.inf
    scores = jax.nn.sigmoid(lg_ref[...])
    choice = scores + bias_ref[...]

    gidx = lax.broadcasted_iota(jnp.int32, (GROUP_SIZE, tn), 0)
    group_rows = []
    for g in range(N_GROUPS):
        cg = choice[g * GROUP_SIZE:(g + 1) * GROUP_SIZE, :]
        m1, i1 = _first_argmax(cg, gidx, GROUP_SIZE)
        m2 = jnp.max(jnp.where(gidx == i1, neg, cg), axis=0, keepdims=True)
        group_rows.append(m1 + m2)
    work = jnp.concatenate(group_rows, axis=0)
    ggi = lax.broadcasted_iota(jnp.int32, (N_GROUPS, tn), 0)
    gmask = jnp.zeros((N_GROUPS, tn), F32)
    for _ in range(TOPK_GROUPS):
        _, first = _first_argmax(work, ggi, N_GROUPS)
        pick = ggi == first
        gmask = jnp.where(pick, 1.0, gmask)
        work = jnp.where(pick, neg, work)

    masked = jnp.concatenate(
        [jnp.where(gmask[g:g + 1, :] > 0.0, choice[g * GROUP_SIZE:(g + 1) * GROUP_SIZE, :], neg)
         for g in range(N_GROUPS)], axis=0)
    eidx = lax.broadcasted_iota(jnp.int32, (N_EXPERTS, tn), 0)
    sel = jnp.zeros((N_EXPERTS, tn), F32)
    for _ in range(TOP_K):
        _, first = _first_argmax(masked, eidx, N_EXPERTS)
        pick = eidx == first
        sel = jnp.where(pick, 1.0, sel)
        masked = jnp.where(pick, neg, masked)

    chosen = jnp.where(sel > 0.0, scores, 0.0)
    gates_ref[...] = chosen / jnp.sum(chosen, axis=0, keepdims=True) * ROUTED_SCALE

    r = lax.broadcasted_iota(jnp.int32, (tn, tn), 0)
    c = lax.broadcasted_iota(jnp.int32, (tn, tn), 1)
    local = _dot(sel.astype(BF16), (r < c).astype(BF16))
    run = run_ref[:, 0:1]
    rank_ref[...] = jnp.where(sel > 0.0, run + local, -1.0)
    total = run + jnp.sum(sel, axis=1, keepdims=True)
    run_ref[...] = jnp.broadcast_to(total, run_ref.shape)
    cnt_ref[...] = jnp.broadcast_to(total, cnt_ref.shape)


def _route(logits_t, bias):
    e, t = logits_t.shape
    tn = min(ROUTE_TOKENS, t)
    return pl.pallas_call(
        _route_kernel,
        grid=(t // tn,),
        in_specs=[pl.BlockSpec((e, tn), lambda i: (0, i)),
                  pl.BlockSpec((e, 1), lambda i: (0, 0))],
        out_specs=[pl.BlockSpec((e, tn), lambda i: (0, i)),
                   pl.BlockSpec((e, tn), lambda i: (0, i)),
                   pl.BlockSpec((e, LANES), lambda i: (0, 0))],
        out_shape=[jax.ShapeDtypeStruct((e, t), F32),
                   jax.ShapeDtypeStruct((e, t), F32),
                   jax.ShapeDtypeStruct((e, LANES), F32)],
        scratch_shapes=[pltpu.VMEM((e, LANES), F32)],
        compiler_params=_cparams("arbitrary"),
    )(logits_t, bias.reshape(e, 1))


def _slots_kernel(gates_ref, rank_ref, cnt_ref, slot_ref, gate8_ref, blk_ref, *, rows, n_blocks):
    ne, tn = gates_ref.shape
    cnt = cnt_ref[...]
    nblk = jnp.floor((cnt + (rows - 1.0)) * (1.0 / rows))
    er = lax.broadcasted_iota(jnp.int32, (ne, ne), 0)
    ec = lax.broadcasted_iota(jnp.int32, (ne, ne), 1)
    lower = (ec < er).astype(BF16)
    pad_start = _dot(lower, nblk.astype(BF16))[:, 0:1] * rows
    pad_end = pad_start + nblk[:, 0:1] * rows

    rank = rank_ref[...]
    sel = rank >= 0.0
    slot_e = pad_start + rank
    kidx = _dot(lower, sel.astype(BF16))
    gates = gates_ref[...]
    slot_rows, gate_rows = [], []
    for k in range(TOP_K):
        m = jnp.logical_and(sel, kidx == k)
        slot_rows.append(jnp.sum(jnp.where(m, slot_e, 0.0), axis=0, keepdims=True))
        gate_rows.append(jnp.sum(jnp.where(m, gates, 0.0), axis=0, keepdims=True))
    slot_ref[...] = jnp.concatenate(slot_rows, axis=0).astype(jnp.int32)
    gate8_ref[...] = jnp.concatenate(gate_rows, axis=0).T

    nbp = blk_ref.shape[1]
    bstart = lax.broadcasted_iota(jnp.int32, (1, nbp), 1).astype(F32) * rows
    e_of = jnp.sum((pad_end <= bstart).astype(F32), axis=0, keepdims=True)
    e_of = jnp.minimum(e_of, ne - 1.0)
    eidx = lax.broadcasted_iota(jnp.int32, (ne, nbp), 0).astype(F32)
    valid_e = jnp.clip(cnt[:, 0:1] - (bstart - pad_start), 0.0, rows)
    valid = jnp.sum(jnp.where(eidx == e_of, valid_e, 0.0), axis=0, keepdims=True)
    total = pad_end[ne - 1:ne, :]
    used = bstart < total
    bidx = bstart * (1.0 / rows)
    src = jnp.where(used, bidx, total * (1.0 / rows) - 1.0)
    dst = jnp.where(used, bidx, float(n_blocks))
    blk_ref[...] = jnp.concatenate(
        [e_of, valid, src, dst, jnp.zeros((blk_ref.shape[0] - 4, nbp), F32)], axis=0).astype(jnp.int32)


def _slots(gates_t, rank_t, counts, rows, n_blocks):
    nbp = -(-n_blocks // LANES) * LANES
    e, t = gates_t.shape
    assert t // rows <= 256
    tn = min(ROUTE_TOKENS, t)
    return pl.pallas_call(
        functools.partial(_slots_kernel, rows=rows, n_blocks=n_blocks),
        grid=(t // tn,),
        in_specs=[pl.BlockSpec((e, tn), lambda i: (0, i)),
                  pl.BlockSpec((e, tn), lambda i: (0, i)),
                  pl.BlockSpec((e, LANES), lambda i: (0, 0))],
        out_specs=[pl.BlockSpec((TOP_K, tn), lambda i: (0, i)),
                   pl.BlockSpec((tn, TOP_K), lambda i: (i, 0)),
                   pl.BlockSpec((8, nbp), lambda i: (0, 0))],
        out_shape=[jax.ShapeDtypeStruct((TOP_K, t), jnp.int32),
                   jax.ShapeDtypeStruct((t, TOP_K), F32),
                   jax.ShapeDtypeStruct((8, nbp), jnp.int32)],
        compiler_params=_cparams("arbitrary"),
    )(gates_t, rank_t, counts)


SC_WINDOW = 128


def _sc_mesh():
    return plsc.VectorSubcoreMesh(core_axis_name="core", subcore_axis_name="subcore")


def _sc_gather_rows(table, idx):
    n = idx.shape[1]
    w = table.shape[1]

    @pl.kernel(out_type=jax.ShapeDtypeStruct((n, w), table.dtype), mesh=_sc_mesh())
    def gather(t_hbm, i_hbm, o_hbm):
        def body(i_vmem, o_vmem):
            pltpu.sync_copy(t_hbm.at[i_vmem.at[0]], o_vmem)

        pltpu.emit_pipeline(
            body, grid=(n // SC_WINDOW,),
            in_specs=[pl.BlockSpec((1, SC_WINDOW), lambda i: (0, i))],
            out_specs=[pl.BlockSpec((SC_WINDOW, w), lambda i: (i, 0))],
            core_axis_name=("core", "subcore"),
            dimension_semantics=(pltpu.PARALLEL,),
        )(i_hbm, o_hbm)

    return gather(table, idx)


def _sc_scatter_rows(rows, idx, n_out):
    fan, m = idx.shape
    w = rows.shape[1]

    @pl.kernel(out_type=jax.ShapeDtypeStruct((n_out, w), rows.dtype), mesh=_sc_mesh())
    def scatter(r_hbm, i_hbm, o_hbm):
        def body(r_vmem, i_vmem):
            for k in range(fan):
                pltpu.sync_copy(r_vmem, o_hbm.at[i_vmem.at[k]])

        pltpu.emit_pipeline(
            body, grid=(m // SC_WINDOW,),
            in_specs=[pl.BlockSpec((SC_WINDOW, w), lambda i: (i, 0)),
                      pl.BlockSpec((fan, SC_WINDOW), lambda i: (0, i))],
            out_specs=[],
            core_axis_name=("core", "subcore"),
            dimension_semantics=(pltpu.PARALLEL,),
        )(r_hbm, i_hbm)

    return scatter(rows, idx)


def _gmm_kernel(be_ref, bv_ref, bs_ref, bd_ref, xs_ref, wg_ref, wu_ref, wd_ref, ys_ref, wgu_s, wd_s):
    b = pl.program_id(0)
    valid = bv_ref[b]
    hid = wg_ref.shape[2]

    @pl.when(jnp.logical_or(b == 0, be_ref[b] != be_ref[jnp.maximum(b - 1, 0)]))
    def _():
        wgu_s[:, :hid] = wg_ref[0].astype(BF16)
        wgu_s[:, hid:] = wu_ref[0].astype(BF16)
        wd_s[...] = wd_ref[0].astype(BF16)

    @pl.when(valid > 0)
    def _():
        x = _load_planes(xs_ref[0], xs_ref[1]).astype(BF16)
        y = _dot(_swiglu_hidden(x, wgu_s[...]).astype(BF16), wd_s[...])
        row = lax.broadcasted_iota(jnp.int32, (y.shape[0], 1), 0)
        _store_planes(ys_ref, slice(None), jnp.where(row < valid, y, 0.0))

    @pl.when(valid <= 0)
    def _():
        ys_ref[...] = jnp.zeros_like(ys_ref)


def _gmm(blk, xs, we_gate, we_up, we_down, rows, after):
    _, n_rows, q = xs.shape
    _, d, hid = we_gate.shape
    w_spec = lambda w: pl.BlockSpec((1,) + w.shape[1:], lambda b, be, bv, bs, bd: (be[b], 0, 0))
    in_specs = [pl.BlockSpec((2, rows, q), lambda b, be, bv, bs, bd: (0, bs[b], 0)),
                w_spec(we_gate), w_spec(we_up), w_spec(we_down)]
    args = [xs, we_gate, we_up, we_down]
    n_prefetch = 4
    body = _gmm_kernel
    if after is not None:
        pos = n_prefetch + len(args)
        in_specs.append(pl.BlockSpec(memory_space=pl.ANY))
        args.append(after)
        body = lambda *refs: _gmm_kernel(*refs[:pos], *refs[pos + 1:])
    return pl.pallas_call(
        body,
        grid_spec=pltpu.PrefetchScalarGridSpec(
            num_scalar_prefetch=n_prefetch,
            grid=(n_rows // rows - 1,),
            in_specs=in_specs,
            out_specs=pl.BlockSpec((2, rows, q), lambda b, be, bv, bs, bd: (0, bd[b], 0)),
            scratch_shapes=[pltpu.VMEM((d, 2 * hid), BF16), pltpu.VMEM((hid, d), BF16)]),
        out_shape=jax.ShapeDtypeStruct(xs.shape, jnp.int32),
        compiler_params=_cparams("arbitrary"),
    )(blk[0], blk[1], blk[2], blk[3], *args)


def _combine_kernel(x1_ref, mod_ref, y8_ref, g8_ref, h2p_ref, wsgu_ref, wsd_ref, *rest):
    o_ref = rest[-1]
    g8 = g8_ref[...]
    h2 = _load_planes(h2p_ref[0], h2p_ref[1]).astype(BF16)
    ffn = _dot(_swiglu_hidden(h2, wsgu_ref[...]).astype(BF16), wsd_ref[...])
    for k in range(TOP_K):
        ffn = ffn + g8[:, k:k + 1] * _load_planes(y8_ref[k, 0], y8_ref[k, 1])
    o_ref[0] = x1_ref[0] + mod_ref[0, 5:6, :] * ffn


def _combine(x1, mod, y8, gate8, h2p, ws_gu, ws_d, b0, total, earlier, after):
    bsz, s, d = x1.shape
    tm = min(COMBINE_ROWS, s)
    ns = s // tm
    full = lambda shape: pl.BlockSpec(shape, lambda b, i: (0,) * len(shape))
    in_specs = [pl.BlockSpec((1, tm, d), lambda b, i: (b, i, 0)),
                pl.BlockSpec((1, N_MOD, d), lambda b, i: (b, 0, 0)),
                pl.BlockSpec((TOP_K, 2, tm, d // 4), lambda b, i: (0, 0, b * ns + i, 0)),
                pl.BlockSpec((tm, TOP_K), lambda b, i: (b * ns + i, 0)),
                pl.BlockSpec((2, tm, d // 4), lambda b, i: (0, b * ns + i, 0)),
                full(ws_gu.shape), full(ws_d.shape)]
    args = [x1, mod, y8, gate8, h2p, ws_gu, ws_d]
    aliases = {}
    if earlier is not None:
        in_specs.append(pl.BlockSpec(memory_space=pl.ANY))
        args.append(earlier)
        aliases = {len(args) - 1: 0}
    if after is not None:
        in_specs.append(pl.BlockSpec(memory_space=pl.ANY))
        args.append(after)
    return pl.pallas_call(
        _combine_kernel,
        grid=(bsz, ns),
        in_specs=in_specs,
        out_specs=pl.BlockSpec((1, tm, d), lambda b, i: (b0 + b, i, 0)),
        out_shape=jax.ShapeDtypeStruct((total, s, d), F32),
        input_output_aliases=aliases,
        compiler_params=_cparams("parallel", "parallel"),
    )(*args)


def kernel(x, c, w_ada, b_ada, norm1_g, w_in, sb_q_norm_g, sb_k_norm_g, hg_lb_logits, hg_norm_g,
           w_branch_sb, w_branch_hg, w_out, norm2_g, w_router, router_bias, w_e_gate, w_e_up,
           w_e_down, w_s_gate, w_s_up, w_s_down):
    bsz, s, d = x.shape
    depth = w_ada.shape[0]
    n_gate_cols = 2 * d
    qkv_col0 = n_gate_cols // LANES
    hg_col0 = qkv_col0 + 3 * SB_WIDTH // LANES
    for l in range(depth):
        n_mix = 3 * SB_WIDTH + 4 * HG_WIDTH
        w_in_l = jnp.concatenate([w_in[l][:, n_mix:], w_in[l][:, :n_mix]], axis=1).astype(BF16)
        wr_t = w_router[l].T
        wr_hi = wr_t.astype(BF16)
        wr_lo = (wr_t - wr_hi.astype(F32)).astype(BF16)
        ws_gu = jnp.concatenate([w_s_gate[l], w_s_up[l]], axis=1).astype(BF16)

        w_sb, w_hg = w_branch_sb[l].astype(BF16), w_branch_hg[l].astype(BF16)
        w_o, ws_d = w_out[l].astype(BF16), w_s_down[l].astype(BF16)

        mod = _ada(c, w_ada[l], b_ada[l]).reshape(bsz, N_MOD, d)
        n_parts = BATCH_PARTS if bsz % BATCH_PARTS == 0 else 1
        pb = bsz // n_parts
        t = pb * s
        q = d // 4
        n_blocks = -(-(t * TOP_K + N_EXPERTS * (DISPATCH_ROWS - 1)) // DISPATCH_ROWS)
        n_rows = (n_blocks + 1) * DISPATCH_ROWS
        plane_off = jnp.array([0, n_rows], jnp.int32)[None, :, None]

        def experts(st, after):
            ys = _gmm(st["blk"], st["xs"].reshape(2, n_rows, q), w_e_gate[l], w_e_up[l],
                      w_e_down[l], DISPATCH_ROWS, after)
            y8 = _sc_gather_rows(ys.reshape(2 * n_rows, q), st["row_idx"].reshape(1, TOP_K * 2 * t))
            return ys, y8.reshape(TOP_K, 2, t, q)

        out = None
        prev = None
        for p in range(n_parts):
            b0 = p * pb
            mod_p = mod[b0:b0 + pb]
            proj, qa, qb, knt, kc, lfh, lfl, qs, gs = _inproj(
                x, mod_p, norm1_g[l], w_in_l, sb_q_norm_g[l], sb_k_norm_g[l], hg_lb_logits, l,
                qkv_col0, b0, prev and prev["blk"])
            o_hg = _hgrn(proj, kc, lfh, lfl, qs, gs, hg_norm_g[l], hg_col0 + HG_WIDTH // LANES)
            if prev:
                prev["ys"], prev["y8"] = experts(prev, o_hg)
            o_sb = _sb_attention(proj, qa, qb, knt, qkv_col0, prev and prev["ys"])
            if prev:
                out = _combine(prev["x1"], prev["mod"], prev["y8"], prev["gate8"], prev["h2p"],
                               ws_gu, ws_d, prev["b0"], bsz, out, o_sb)
            x1, h2p, logits_t = _merge(x, o_sb, o_hg, proj, mod_p, norm2_g[l], w_sb, w_hg, w_o,
                                       wr_hi, wr_lo, b0, out)
            gates_t, rank_t, counts = _route(logits_t, router_bias[l])
            slot8, gate8, blk = _slots(gates_t, rank_t, counts, DISPATCH_ROWS, n_blocks)
            row_idx = (slot8[:, None, :] + plane_off).reshape(TOP_K, 2 * t)
            xs = _sc_scatter_rows(h2p.reshape(2 * t, q), row_idx, 2 * n_rows)
            prev = dict(blk=blk, xs=xs, row_idx=row_idx, x1=x1, h2p=h2p, mod=mod_p, gate8=gate8,
                        b0=b0)
        _, y8 = experts(prev, None)
        x = _combine(prev["x1"], prev["mod"], y8, prev["gate8"], prev["h2p"], ws_gu, ws_d,
                     prev["b0"], bsz, out, None)
    return x
```

```python
import functools

import jax
import jax.numpy as jnp
from jax import lax
from jax.experimental import pallas as pl
from jax.experimental.pallas import tpu as pltpu
from jax.experimental.pallas import tpu_sc as plsc

F32 = jnp.float32
BF16 = jnp.bfloat16

SB_HEADS = 8
SB_HEAD_DIM = 64
SB_WIDTH = SB_HEADS * SB_HEAD_DIM
HG_HEADS = 4
HG_HEAD_DIM = 128
HG_WIDTH = HG_HEADS * HG_HEAD_DIM
HG_CHUNK = 64
N_EXPERTS = 64
TOP_K = 8
N_GROUPS = 8
TOPK_GROUPS = 4
GROUP_SIZE = N_EXPERTS // N_GROUPS
ROUTED_SCALE = 2.5
DISPATCH_ROWS = 1024
BATCH_PARTS = 2
N_MOD = 6
EPS = 1e-6
LOG2_E = 1.4426950408889634
SB_DEAD_LOG2 = 160.0

LANES = 128
VMEM_LIMIT = 56 * 1024 * 1024

ADA_COLS = 1024
ROW_TILE = 512
SB_QUERY_BLOCK = 512
SB_PAIRS_PER_STEP = 2
HG_SEQ_TILE = 1024
ROUTE_TOKENS = 1024
COMBINE_ROWS = 512


def _cparams(*sem):
    return pltpu.CompilerParams(dimension_semantics=sem, vmem_limit_bytes=VMEM_LIMIT)


def _silu(t):
    return t * jax.nn.sigmoid(t)


def _dot(a, b):
    return jnp.dot(a, b, preferred_element_type=F32)


def _run_after(body, in_specs, args, after):
    if after is None:
        return body
    pos = len(args)
    in_specs.append(pl.BlockSpec(memory_space=pl.ANY))
    args.append(after)
    return lambda *refs: body(*refs[:pos], *refs[pos + 1:])


def _dot_nt(a, b):
    return lax.dot_general(a, b, (((1,), (1,)), ((), ())), preferred_element_type=F32)


def _split_bf16(t):
    hi = t.astype(BF16)
    lo = (t - hi.astype(F32)).astype(BF16)
    return hi, lo


def _ada_kernel(c_ref, w_ref, b_ref, o_ref):
    cond = _silu(c_ref[...])
    o_ref[...] = _dot(cond, w_ref[...]) + b_ref[...]


def _ada(c, w, b):
    bsz, d = c.shape
    n = w.shape[1]
    tn = min(ADA_COLS, n)
    return pl.pallas_call(
        _ada_kernel,
        grid=(n // tn,),
        in_specs=[pl.BlockSpec((bsz, d), lambda j: (0, 0)),
                  pl.BlockSpec((d, tn), lambda j: (0, j)),
                  pl.BlockSpec((1, tn), lambda j: (0, j))],
        out_specs=pl.BlockSpec((bsz, tn), lambda j: (0, j)),
        out_shape=jax.ShapeDtypeStruct((bsz, n), F32),
        compiler_params=_cparams("parallel"),
    )(c, w, b.reshape(1, n))


def _modulated_norm(x, g, shift, scale):
    y = x * lax.rsqrt(jnp.mean(x * x, axis=-1, keepdims=True) + EPS) * g
    return y * (1.0 + scale) + shift


INPROJ_COLS = 512


def _inproj_kernel(x_ref, mod_ref, g_ref, w_ref, qg_ref, kg_ref, lbl_ref, o_ref, qa_ref, qb_ref,
                   knt_ref, kc_ref, lfh_ref, lfl_ref, qs_ref, gs_ref, *, q_chunk, layer):
    h = _modulated_norm(x_ref[0], g_ref[...], mod_ref[0, 0:1, :], mod_ref[0, 1:2, :]).astype(BF16)
    lo_half = lax.broadcasted_iota(jnp.int32, (1, LANES), 1) < SB_HEAD_DIM
    for j in range(w_ref.shape[1] // INPROJ_COLS):
        cols = slice(j * INPROJ_COLS, (j + 1) * INPROJ_COLS)
        res = _dot(h, w_ref[:, cols])
        o_ref[0, :, cols] = res.astype(o_ref.dtype)
        if j == q_chunk:
            for p in range(INPROJ_COLS // LANES):
                pair = slice(p * LANES, (p + 1) * LANES)
                qn = _pair_norm(res[:, pair], qg_ref[...], lo_half) * (SB_HEAD_DIM ** -0.5 * LOG2_E)
                qa_ref[0, :, pair] = jnp.where(lo_half, qn, 0.0).astype(BF16)
                qb_ref[0, :, pair] = jnp.where(lo_half, 0.0, qn).astype(BF16)
        if j == q_chunk + 1:
            for p in range(INPROJ_COLS // LANES):
                pair = slice(p * LANES, (p + 1) * LANES)
                knt_ref[0, pair, :] = _pair_norm(res[:, pair], kg_ref[...], lo_half).T.astype(BF16)
        if j == q_chunk + 3:
            lg = lbl_ref[...]
            e = jnp.exp(lg - jnp.max(lg, axis=0, keepdims=True))
            lb = jnp.sum(e[:layer + 1], axis=0, keepdims=True) / jnp.sum(e, axis=0, keepdims=True)
            forget = lb + (1.0 - lb) * jax.nn.sigmoid(res)
            kc_ref[0] = (1.0 - forget).astype(BF16)
            lfh_ref[0], lfl_ref[0] = _split_trunc(jnp.log(forget))
        if j == q_chunk + 5:
            qs_ref[0] = _silu(res).astype(BF16)
        if j == q_chunk + 6:
            gs_ref[0] = _silu(res).astype(BF16)


def _inproj(x, mod, g, w, qg, kg, lb_logits, layer, q_col0, b0, after):
    _, s, d = x.shape
    bsz = mod.shape[0]
    n = w.shape[1]
    nl = lb_logits.shape[0]
    tm = min(ROW_TILE, s)
    assert INPROJ_COLS == SB_WIDTH == HG_WIDTH and (q_col0 * LANES) % INPROJ_COLS == 0
    pair_gain = lambda t: jnp.tile(t.reshape(1, SB_HEAD_DIM), (1, 2))
    in_specs = [pl.BlockSpec((1, tm, d), lambda b, i: (b0 + b, i, 0)),
                pl.BlockSpec((1, N_MOD, d), lambda b, i: (b, 0, 0)),
                pl.BlockSpec((1, d), lambda b, i: (0, 0)),
                pl.BlockSpec((d, n), lambda b, i: (0, 0)),
                pl.BlockSpec((1, LANES), lambda b, i: (0, 0)),
                pl.BlockSpec((1, LANES), lambda b, i: (0, 0)),
                pl.BlockSpec((nl, HG_WIDTH), lambda b, i: (0, 0))]
    args = [x, mod, g.reshape(1, d), w, pair_gain(qg), pair_gain(kg), lb_logits]
    body = _run_after(functools.partial(_inproj_kernel, q_chunk=q_col0 * LANES // INPROJ_COLS,
                                        layer=layer), in_specs, args, after)
    rows_spec = pl.BlockSpec((1, tm, INPROJ_COLS), lambda b, i: (b, i, 0))
    rows_shape = jax.ShapeDtypeStruct((bsz, s, INPROJ_COLS), BF16)
    return pl.pallas_call(
        body,
        grid=(bsz, s // tm),
        in_specs=in_specs,
        out_specs=[pl.BlockSpec((1, tm, n), lambda b, i: (b, i, 0)),
                   rows_spec, rows_spec,
                   pl.BlockSpec((1, SB_WIDTH, tm), lambda b, i: (b, 0, i)),
                   rows_spec, rows_spec, rows_spec, rows_spec, rows_spec],
        out_shape=[jax.ShapeDtypeStruct((bsz, s, n), BF16),
                   rows_shape, rows_shape,
                   jax.ShapeDtypeStruct((bsz, SB_WIDTH, s), BF16),
                   rows_shape, rows_shape, rows_shape, rows_shape, rows_shape],
        compiler_params=_cparams("parallel", "parallel"),
    )(*args)


def _pair_norm(t, g, lo_half):
    sq = t * t
    s_lo = jnp.sum(jnp.where(lo_half, sq, 0.0), axis=-1, keepdims=True)
    s_hi = jnp.sum(jnp.where(lo_half, 0.0, sq), axis=-1, keepdims=True)
    ms = jnp.where(lo_half, s_lo, s_hi) * (1.0 / SB_HEAD_DIM)
    return t * lax.rsqrt(ms + EPS) * g


def _neg_abs(t):
    bits = lax.bitcast_convert_type(t, jnp.uint32) | jnp.uint32(0x80000000)
    return lax.bitcast_convert_type(bits, F32)


def _split_trunc(t):
    bits = lax.bitcast_convert_type(t, jnp.uint32) & jnp.uint32(0xFFFF0000)
    hi = lax.bitcast_convert_type(bits, F32)
    return hi.astype(BF16), (t - hi).astype(BF16)


def _sb_kernel(qa_ref, qb_ref, knt_ref, v_ref, tri2_ref, o_ref, va_ref, vb_ref, *, tq):
    qi = pl.program_id(2)
    s = v_ref.shape[1]
    width = v_ref.shape[2]
    lane = lax.broadcasted_iota(jnp.int32, (1, width), 1)
    lo_half = (lane % LANES) < SB_HEAD_DIM
    pairs = [slice(p * LANES, (p + 1) * LANES) for p in range(width // LANES)]

    tk = tq // 2

    @pl.when(qi == 0)
    def _():
        def prep_block(j, c):
            rows = pl.ds(pl.multiple_of(j * tk, tk), tk)
            vb = v_ref[0, rows, :]
            va_ref[rows, :] = jnp.where(lo_half, vb, jnp.zeros_like(vb))
            vb_ref[rows, :] = jnp.where(lo_half, jnp.zeros_like(vb), vb)
            return c
        lax.fori_loop(0, s // tk, prep_block, 0)

    v_heads = (va_ref, vb_ref)

    row = lax.broadcasted_iota(jnp.int32, (tk, tk), 0)
    col = lax.broadcasted_iota(jnp.int32, (tk, tk), 1)
    strict = col < row
    tri2 = tri2_ref[...]

    def sweep(streams):
        cols = [[pl.ds(pl.multiple_of(j * tk, tk), tk) for j, _, _ in blocks]
                for _, _, _, _, blocks in streams]
        z = [[[_dot(qh[h], knt_ref[0, pair, c]) for c in cols[i]] for h in range(2)]
             for i, (pair, qh, _, _, _) in enumerate(streams)]
        cs = []
        for i, (_, _, _, _, blocks) in enumerate(streams):
            cs.append([[None] * len(blocks) for _ in range(2)])
            for h in range(2):
                for b, (_, masked, _) in enumerate(blocks):
                    zb = z[i][h][b]
                    sp = jnp.maximum(zb, 0.0) + jnp.log2(1.0 + jnp.exp2(_neg_abs(zb)))
                    if masked:
                        sp = jnp.where(strict, sp, 0.0)
                    cs[i][h][b] = _dot(jnp.concatenate(_split_trunc(sp), axis=1), tri2)
        results = []
        for i, (pair, _, runs, acc, blocks) in enumerate(streams):
            runs = list(runs)
            for h in range(2):
                for b, (_, masked, gate) in enumerate(blocks):
                    a = jnp.exp2(z[i][h][b] - cs[i][h][b] - runs[h])
                    if masked:
                        a = jnp.where(strict, a, 0.0)
                    vb = v_heads[h][cols[i][b], pair]
                    step = cs[i][h][b][:, 0:1]
                    if gate is not None:
                        vb = jnp.where(gate, vb, jnp.zeros_like(vb))
                        step = jnp.where(gate, step, 0.0)
                    acc = acc + _dot(a.astype(BF16), vb)
                    runs[h] = runs[h] + step
            results.append((tuple(runs), acc))
        return results

    zero_run = jnp.zeros((tk, 1), F32)
    zero_acc = jnp.zeros((tk, LANES), F32)
    left, right = 2 * qi, 2 * qi + 1
    q_top = [(qa_ref[0, :tk, pair], qb_ref[0, :tk, pair]) for pair in pairs]
    q_bot = [(qa_ref[0, tk:, pair], qb_ref[0, tk:, pair]) for pair in pairs]
    fresh = ((zero_run, zero_run), zero_acc)
    state = sweep(
        [(pair, q_top[p], *fresh, [(left, True, None), (jnp.maximum(left - 1, 0), False, qi > 0)])
         for p, pair in enumerate(pairs)]
        + [(pair, q_bot[p], *fresh, [(right, True, None), (left, False, None)])
           for p, pair in enumerate(pairs)])

    def min_run(st):
        low = None
        for runs, _ in st:
            m = jnp.minimum(runs[0], runs[1])
            low = m if low is None else jnp.minimum(low, m)
        return jnp.min(low)

    def alive(carry):
        j, _, low = carry
        return jnp.logical_and(j >= 0, low < SB_DEAD_LOG2)

    def earlier_block(carry):
        j, st, _ = carry
        n = len(pairs)
        st = sweep(
            [(pair, q_top[p], *st[p], [(jnp.maximum(j - 1, 0), False, j > 0)])
             for p, pair in enumerate(pairs)]
            + [(pair, q_bot[p], *st[n + p], [(j, False, None)]) for p, pair in enumerate(pairs)])
        return j - 1, tuple(st), min_run(st)

    _, state, _ = lax.while_loop(alive, earlier_block, (left - 1, tuple(state), min_run(state)))
    n = len(pairs)
    o_ref[0] = jnp.concatenate(
        [jnp.concatenate([state[p][1], state[n + p][1]], axis=0) for p in range(n)],
        axis=1).astype(o_ref.dtype)


def _sb_attention(proj, qa, qb, knt, col0, after):
    bsz, s, _ = proj.shape
    tq = min(SB_QUERY_BLOCK, s)
    tk = tq // 2
    npair = SB_WIDTH // LANES
    tri = (jnp.arange(tk)[:, None] >= jnp.arange(tk)[None, :]).astype(BF16)
    tri2 = jnp.concatenate([tri, tri], axis=0)
    w = SB_PAIRS_PER_STEP * LANES
    v_blk0 = (col0 + 2 * npair) * LANES // w
    assert npair % SB_PAIRS_PER_STEP == 0 and ((col0 + 2 * npair) * LANES) % w == 0
    in_specs = [pl.BlockSpec((1, tq, w), lambda b, h, i: (b, i, h)),
                pl.BlockSpec((1, tq, w), lambda b, h, i: (b, i, h)),
                pl.BlockSpec((1, w, s), lambda b, h, i: (b, h, 0)),
                pl.BlockSpec((1, s, w), lambda b, h, i: (b, 0, v_blk0 + h)),
                pl.BlockSpec((2 * tk, tk), lambda b, h, i: (0, 0))]
    args = [qa, qb, knt, proj, tri2]
    body = _run_after(functools.partial(_sb_kernel, tq=tq), in_specs, args, after)
    return pl.pallas_call(
        body,
        grid=(bsz, npair // SB_PAIRS_PER_STEP, s // tq),
        in_specs=in_specs,
        out_specs=pl.BlockSpec((1, tq, w), lambda b, h, i: (b, i, h)),
        out_shape=jax.ShapeDtypeStruct((bsz, s, SB_WIDTH), BF16),
        scratch_shapes=[pltpu.VMEM((s, w), BF16),
                        pltpu.VMEM((s, w), BF16)],
        compiler_params=_cparams("parallel", "parallel", "arbitrary"),
    )(*args)


HG_CHUNKS_PER_STEP = 8


def _hgrn_kernel(kc_ref, lfh_ref, lfl_ref, i_ref, qs_ref, gs_ref, ng_ref, o_ref, st_ref):
    ts = kc_ref.shape[1]
    c = HG_CHUNK
    dh = HG_HEAD_DIM

    @pl.when(pl.program_id(1) == 0)
    def _():
        st_ref[...] = jnp.zeros_like(st_ref)

    row = lax.broadcasted_iota(jnp.int32, (c, c), 0)
    col = lax.broadcasted_iota(jnp.int32, (c, c), 1)
    causal = col <= row
    lower = causal.astype(BF16)
    width = kc_ref.shape[2]

    def body(it, carry):
        chunks = range(HG_CHUNKS_PER_STEP)
        heads = [slice(h * dh, (h + 1) * dh) for h in range(HG_HEADS)]
        rows = [pl.ds(pl.multiple_of((it * HG_CHUNKS_PER_STEP + u) * c, c), c) for u in chunks]
        cum = [_dot(lower, jnp.concatenate([lfh_ref[0, r, :], lfl_ref[0, r, :]], axis=1))
               for r in rows]
        qe, ke, qg, kd, decay, v, v_t = [], [], [], [], [], [], []
        for u in chunks:
            gc = cum[u][:, :width] + cum[u][:, width:]
            g_mid = gc[c // 2 - 1:c // 2, :]
            g_last = gc[c - 1:c, :]
            q_u = qs_ref[0, rows[u], :].astype(F32) * jnp.exp(gc - g_mid)
            k_u = kc_ref[0, rows[u], :].astype(F32) * jnp.exp(g_mid - gc)
            qg.append((q_u * jnp.exp(g_mid)).astype(BF16))
            kd.append((k_u * jnp.exp(g_last - g_mid)).astype(BF16))
            qe.append(q_u.astype(BF16))
            ke.append(k_u.astype(BF16))
            decay.append(jnp.exp(g_last))
            v.append(i_ref[0, rows[u], :])
            v_t.append(v[u].astype(F32).T.astype(BF16))
        scores = [[_dot_nt(qe[u][:, sl], ke[u][:, sl]) for sl in heads] for u in chunks]
        kv_t = [[_dot(v_t[u][sl, :], kd[u][:, sl]) for sl in heads] for u in chunks]
        states = [[st_ref[h] for h in range(HG_HEADS)]]
        for u in chunks:
            states.append([states[u][h] * decay[u][:, sl] + kv_t[u][h] for h, sl in enumerate(heads)])
        for u in chunks:
            outs = []
            for h, sl in enumerate(heads):
                o = (_dot(jnp.where(causal, scores[u][h], 0.0).astype(BF16), v[u][:, sl])
                     + _dot_nt(qg[u][:, sl], states[u][h].astype(BF16)))
                outs.append(o * lax.rsqrt(jnp.mean(o * o, axis=-1, keepdims=True) + EPS))
            y = jnp.concatenate(outs, axis=1) * ng_ref[...] * gs_ref[0, rows[u], :].astype(F32)
            o_ref[0, rows[u], :] = y.astype(o_ref.dtype)
        for h in range(HG_HEADS):
            st_ref[h] = states[-1][h]
        return carry

    lax.fori_loop(0, ts // (c * HG_CHUNKS_PER_STEP), body, 0)


def _hgrn(proj, kc, lfh, lfl, qs, gs, norm_g, i_col0):
    bsz, s, _ = proj.shape
    ts = min(HG_SEQ_TILE, s)
    i_blk = i_col0 * LANES // HG_WIDTH
    term = pl.BlockSpec((1, ts, HG_WIDTH), lambda b, i: (b, i, 0))
    return pl.pallas_call(
        _hgrn_kernel,
        grid=(bsz, s // ts),
        in_specs=[term, term, term,
                  pl.BlockSpec((1, ts, HG_WIDTH), lambda b, i: (b, i, i_blk)),
                  term, term,
                  pl.BlockSpec((1, HG_WIDTH), lambda b, i: (0, 0))],
        out_specs=pl.BlockSpec((1, ts, HG_WIDTH), lambda b, i: (b, i, 0)),
        out_shape=jax.ShapeDtypeStruct((bsz, s, HG_WIDTH), BF16),
        scratch_shapes=[pltpu.VMEM((HG_HEADS, HG_HEAD_DIM, HG_HEAD_DIM), F32)],
        compiler_params=_cparams("parallel", "arbitrary"),
    )(kc, lfh, lfl, proj, qs, gs, jnp.tile(norm_g.reshape(1, HG_HEAD_DIM), (1, HG_HEADS)))


def _pack_halves(t):
    bits = lax.bitcast_convert_type(t.astype(BF16).astype(F32), jnp.uint32)
    w = t.shape[1] // 2
    return lax.bitcast_convert_type(bits[:, :w] | (bits[:, w:] >> 16), jnp.int32)


def _unpack_halves(p):
    u = lax.bitcast_convert_type(p, jnp.uint32)
    return (lax.bitcast_convert_type(u & jnp.uint32(0xFFFF0000), F32),
            lax.bitcast_convert_type(u << 16, F32))


def _store_planes(ref, rows, t):
    p = _pack_halves(t)
    q = p.shape[1] // 2
    ref[0, rows, :] = p[:, :q]
    ref[1, rows, :] = p[:, q:]


def _load_planes(p0, p1):
    a0, b0 = _unpack_halves(p0)
    a1, b1 = _unpack_halves(p1)
    return jnp.concatenate([a0, a1, b0, b1], axis=1)


def _swiglu_hidden(h, w_gu):
    gu = _dot(h, w_gu)
    hid = w_gu.shape[1] // 2
    return _silu(gu[:, :hid]) * gu[:, hid:]


def _merge_kernel(x_ref, osb_ref, ohg_ref, gsb_ref, ghg_ref, mod_ref, g2_ref, wsb_ref, whg_ref,
                  wout_ref, wrh_ref, wrl_ref, x1_ref, h2p_ref, lg_ref):
    m_sb = _dot(osb_ref[0], wsb_ref[...])
    m_hg = _dot(ohg_ref[0], whg_ref[...])
    merged = (jax.nn.sigmoid(gsb_ref[0].astype(F32)) * m_sb
              + jax.nn.sigmoid(ghg_ref[0].astype(F32)) * m_hg)
    x1 = x_ref[0] + mod_ref[0, 2:3, :] * _dot(merged.astype(BF16), wout_ref[...])
    x1_ref[0] = x1
    h2 = _modulated_norm(x1, g2_ref[...], mod_ref[0, 3:4, :], mod_ref[0, 4:5, :])
    hi, lo = _split_bf16(h2)
    _store_planes(h2p_ref, slice(None), h2)
    lg_ref[...] = _dot_nt(wrh_ref[...], hi) + _dot_nt(wrh_ref[...], lo) + _dot_nt(wrl_ref[...], hi)


def _merge(x, o_sb, o_hg, proj, mod, g2, w_sb, w_hg, w_out, wr_hi, wr_lo, b0, after):
    _, s, d = x.shape
    bsz = mod.shape[0]
    tm = min(ROW_TILE, s)
    ns = s // tm
    full = lambda shape: pl.BlockSpec(shape, lambda b, i: (0,) * len(shape))
    in_specs = [pl.BlockSpec((1, tm, d), lambda b, i: (b0 + b, i, 0)),
                pl.BlockSpec((1, tm, SB_WIDTH), lambda b, i: (b, i, 0)),
                pl.BlockSpec((1, tm, HG_WIDTH), lambda b, i: (b, i, 0)),
                pl.BlockSpec((1, tm, d), lambda b, i: (b, i, 0)),
                pl.BlockSpec((1, tm, d), lambda b, i: (b, i, 1)),
                pl.BlockSpec((1, N_MOD, d), lambda b, i: (b, 0, 0)),
                full((1, d)), full(w_sb.shape), full(w_hg.shape), full(w_out.shape),
                full(wr_hi.shape), full(wr_lo.shape)]
    args = [x, o_sb, o_hg, proj, proj, mod, g2.reshape(1, d), w_sb, w_hg, w_out, wr_hi, wr_lo]
    body = _run_after(_merge_kernel, in_specs, args, after)
    return pl.pallas_call(
        body,
        grid=(bsz, ns),
        in_specs=in_specs,
        out_specs=[pl.BlockSpec((1, tm, d), lambda b, i: (b, i, 0)),
                   pl.BlockSpec((2, tm, d // 4), lambda b, i: (0, b * ns + i, 0)),
                   pl.BlockSpec((N_EXPERTS, tm), lambda b, i: (0, b * ns + i))],
        out_shape=[jax.ShapeDtypeStruct((bsz, s, d), F32),
                   jax.ShapeDtypeStruct((2, bsz * s, d // 4), jnp.int32),
                   jax.ShapeDtypeStruct((N_EXPERTS, bsz * s), F32)],
        compiler_params=_cparams("parallel", "parallel"),
    )(*args)


def _first_argmax(vals, idx, sentinel):
    m = jnp.max(vals, axis=0, keepdims=True)
    first = jnp.min(jnp.where(vals == m, idx, sentinel), axis=0, keepdims=True)
    return m, first


def _route_kernel(lg_ref, bias_ref, gates_ref, rank_ref, cnt_ref, run_ref):
    tn = lg_ref.shape[1]

    @pl.when(pl.program_id(0) == 0)
    def _():
        run_ref[...] = jnp.zeros_like(run_ref)

    neg = -jnp.inf
    scores = jax.nn.sigmoid(lg_ref[...])
    choice = scores + bias_ref[...]

    gidx = lax.broadcasted_iota(jnp.int32, (GROUP_SIZE, tn), 0)
    group_rows = []
    for g in range(N_GROUPS):
        cg = choice[g * GROUP_SIZE:(g + 1) * GROUP_SIZE, :]
        m1, i1 = _first_argmax(cg, gidx, GROUP_SIZE)
        m2 = jnp.max(jnp.where(gidx == i1, neg, cg), axis=0, keepdims=True)
        group_rows.append(m1 + m2)
    work = jnp.concatenate(group_rows, axis=0)
    ggi = lax.broadcasted_iota(jnp.int32, (N_GROUPS, tn), 0)
    gmask = jnp.zeros((N_GROUPS, tn), F32)
    for _ in range(TOPK_GROUPS):
        _, first = _first_argmax(work, ggi, N_GROUPS)
        pick = ggi == first
        gmask = jnp.where(pick, 1.0, gmask)
        work = jnp.where(pick, neg, work)

    masked = jnp.concatenate(
        [jnp.where(gmask[g:g + 1, :] > 0.0, choice[g * GROUP_SIZE:(g + 1) * GROUP_SIZE, :], neg)
         for g in range(N_GROUPS)], axis=0)
    eidx = lax.broadcasted_iota(jnp.int32, (N_EXPERTS, tn), 0)
    sel = jnp.zeros((N_EXPERTS, tn), F32)
    for _ in range(TOP_K):
        _, first = _first_argmax(masked, eidx, N_EXPERTS)
        pick = eidx == first
        sel = jnp.where(pick, 1.0, sel)
        masked = jnp.where(pick, neg, masked)

    chosen = jnp.where(sel > 0.0, scores, 0.0)
    gates_ref[...] = chosen / jnp.sum(chosen, axis=0, keepdims=True) * ROUTED_SCALE

    r = lax.broadcasted_iota(jnp.int32, (tn, tn), 0)
    c = lax.broadcasted_iota(jnp.int32, (tn, tn), 1)
    local = _dot(sel.astype(BF16), (r < c).astype(BF16))
    run = run_ref[:, 0:1]
    rank_ref[...] = jnp.where(sel > 0.0, run + local, -1.0)
    total = run + jnp.sum(sel, axis=1, keepdims=True)
    run_ref[...] = jnp.broadcast_to(total, run_ref.shape)
    cnt_ref[...] = jnp.broadcast_to(total, cnt_ref.shape)


def _route(logits_t, bias):
    e, t = logits_t.shape
    tn = min(ROUTE_TOKENS, t)
    return pl.pallas_call(
        _route_kernel,
        grid=(t // tn,),
        in_specs=[pl.BlockSpec((e, tn), lambda i: (0, i)),
                  pl.BlockSpec((e, 1), lambda i: (0, 0))],
        out_specs=[pl.BlockSpec((e, tn), lambda i: (0, i)),
                   pl.BlockSpec((e, tn), lambda i: (0, i)),
                   pl.BlockSpec((e, LANES), lambda i: (0, 0))],
        out_shape=[jax.ShapeDtypeStruct((e, t), F32),
                   jax.ShapeDtypeStruct((e, t), F32),
                   jax.ShapeDtypeStruct((e, LANES), F32)],
        scratch_shapes=[pltpu.VMEM((e, LANES), F32)],
        compiler_params=_cparams("arbitrary"),
    )(logits_t, bias.reshape(e, 1))


def _slots_kernel(gates_ref, rank_ref, cnt_ref, slot_ref, gate8_ref, blk_ref, *, rows, n_blocks):
    ne, tn = gates_ref.shape
    cnt = cnt_ref[...]
    nblk = jnp.floor((cnt + (rows - 1.0)) * (1.0 / rows))
    er = lax.broadcasted_iota(jnp.int32, (ne, ne), 0)
    ec = lax.broadcasted_iota(jnp.int32, (ne, ne), 1)
    lower = (ec < er).astype(BF16)
    pad_start = _dot(lower, nblk.astype(BF16))[:, 0:1] * rows
    pad_end = pad_start + nblk[:, 0:1] * rows

    rank = rank_ref[...]
    sel = rank >= 0.0
    slot_e = pad_start + rank
    kidx = _dot(lower, sel.astype(BF16))
    gates = gates_ref[...]
    slot_rows, gate_rows = [], []
    for k in range(TOP_K):
        m = jnp.logical_and(sel, kidx == k)
        slot_rows.append(jnp.sum(jnp.where(m, slot_e, 0.0), axis=0, keepdims=True))
        gate_rows.append(jnp.sum(jnp.where(m, gates, 0.0), axis=0, keepdims=True))
    slot_ref[...] = jnp.concatenate(slot_rows, axis=0).astype(jnp.int32)
    gate8_ref[...] = jnp.concatenate(gate_rows, axis=0).T

    nbp = blk_ref.shape[1]
    bstart = lax.broadcasted_iota(jnp.int32, (1, nbp), 1).astype(F32) * rows
    e_of = jnp.sum((pad_end <= bstart).astype(F32), axis=0, keepdims=True)
    e_of = jnp.minimum(e_of, ne - 1.0)
    eidx = lax.broadcasted_iota(jnp.int32, (ne, nbp), 0).astype(F32)
    valid_e = jnp.clip(cnt[:, 0:1] - (bstart - pad_start), 0.0, rows)
    valid = jnp.sum(jnp.where(eidx == e_of, valid_e, 0.0), axis=0, keepdims=True)
    total = pad_end[ne - 1:ne, :]
    used = bstart < total
    bidx = bstart * (1.0 / rows)
    src = jnp.where(used, bidx, total * (1.0 / rows) - 1.0)
    dst = jnp.where(used, bidx, float(n_blocks))
    blk_ref[...] = jnp.concatenate(
        [e_of, valid, src, dst, jnp.zeros((blk_ref.shape[0] - 4, nbp), F32)], axis=0).astype(jnp.int32)


def _slots(gates_t, rank_t, counts, rows, n_blocks):
    nbp = -(-n_blocks // LANES) * LANES
    e, t = gates_t.shape
    assert t // rows <= 256
    tn = min(ROUTE_TOKENS, t)
    return pl.pallas_call(
        functools.partial(_slots_kernel, rows=rows, n_blocks=n_blocks),
        grid=(t // tn,),
        in_specs=[pl.BlockSpec((e, tn), lambda i: (0, i)),
                  pl.BlockSpec((e, tn), lambda i: (0, i)),
                  pl.BlockSpec((e, LANES), lambda i: (0, 0))],
        out_specs=[pl.BlockSpec((TOP_K, tn), lambda i: (0, i)),
                   pl.BlockSpec((tn, TOP_K), lambda i: (i, 0)),
                   pl.BlockSpec((8, nbp), lambda i: (0, 0))],
        out_shape=[jax.ShapeDtypeStruct((TOP_K, t), jnp.int32),
                   jax.ShapeDtypeStruct((t, TOP_K), F32),
                   jax.ShapeDtypeStruct((8, nbp), jnp.int32)],
        compiler_params=_cparams("arbitrary"),
    )(gates_t, rank_t, counts)


SC_WINDOW = 128


def _sc_mesh():
    return plsc.VectorSubcoreMesh(core_axis_name="core", subcore_axis_name="subcore")


def _sc_gather_rows(table, idx):
    n = idx.shape[1]
    w = table.shape[1]

    @pl.kernel(out_type=jax.ShapeDtypeStruct((n, w), table.dtype), mesh=_sc_mesh())
    def gather(t_hbm, i_hbm, o_hbm):
        def body(i_vmem, o_vmem):
            pltpu.sync_copy(t_hbm.at[i_vmem.at[0]], o_vmem)

        pltpu.emit_pipeline(
            body, grid=(n // SC_WINDOW,),
            in_specs=[pl.BlockSpec((1, SC_WINDOW), lambda i: (0, i))],
            out_specs=[pl.BlockSpec((SC_WINDOW, w), lambda i: (i, 0))],
            core_axis_name=("core", "subcore"),
            dimension_semantics=(pltpu.PARALLEL,),
        )(i_hbm, o_hbm)

    return gather(table, idx)


def _sc_scatter_rows(rows, idx, n_out):
    fan, m = idx.shape
    w = rows.shape[1]

    @pl.kernel(out_type=jax.ShapeDtypeStruct((n_out, w), rows.dtype), mesh=_sc_mesh())
    def scatter(r_hbm, i_hbm, o_hbm):
        def body(r_vmem, i_vmem):
            for k in range(fan):
                pltpu.sync_copy(r_vmem, o_hbm.at[i_vmem.at[k]])

        pltpu.emit_pipeline(
            body, grid=(m // SC_WINDOW,),
            in_specs=[pl.BlockSpec((SC_WINDOW, w), lambda i: (i, 0)),
                      pl.BlockSpec((fan, SC_WINDOW), lambda i: (0, i))],
            out_specs=[],
            core_axis_name=("core", "subcore"),
            dimension_semantics=(pltpu.PARALLEL,),
        )(r_hbm, i_hbm)

    return scatter(rows, idx)


def _gmm_kernel(be_ref, bv_ref, bs_ref, bd_ref, xs_ref, wg_ref, wu_ref, wd_ref, ys_ref, wgu_s, wd_s):
    b = pl.program_id(0)
    valid = bv_ref[b]
    hid = wg_ref.shape[2]

    @pl.when(jnp.logical_or(b == 0, be_ref[b] != be_ref[jnp.maximum(b - 1, 0)]))
    def _():
        wgu_s[:, :hid] = wg_ref[0].astype(BF16)
        wgu_s[:, hid:] = wu_ref[0].astype(BF16)
        wd_s[...] = wd_ref[0].astype(BF16)

    @pl.when(valid > 0)
    def _():
        x = _load_planes(xs_ref[0], xs_ref[1]).astype(BF16)
        y = _dot(_swiglu_hidden(x, wgu_s[...]).astype(BF16), wd_s[...])
        row = lax.broadcasted_iota(jnp.int32, (y.shape[0], 1), 0)
        _store_planes(ys_ref, slice(None), jnp.where(row < valid, y, 0.0))

    @pl.when(valid <= 0)
    def _():
        ys_ref[...] = jnp.zeros_like(ys_ref)


def _gmm(blk, xs, we_gate, we_up, we_down, rows, after):
    _, n_rows, q = xs.shape
    _, d, hid = we_gate.shape
    w_spec = lambda w: pl.BlockSpec((1,) + w.shape[1:], lambda b, be, bv, bs, bd: (be[b], 0, 0))
    in_specs = [pl.BlockSpec((2, rows, q), lambda b, be, bv, bs, bd: (0, bs[b], 0)),
                w_spec(we_gate), w_spec(we_up), w_spec(we_down)]
    args = [xs, we_gate, we_up, we_down]
    n_prefetch = 4
    body = _gmm_kernel
    if after is not None:
        pos = n_prefetch + len(args)
        in_specs.append(pl.BlockSpec(memory_space=pl.ANY))
        args.append(after)
        body = lambda *refs: _gmm_kernel(*refs[:pos], *refs[pos + 1:])
    return pl.pallas_call(
        body,
        grid_spec=pltpu.PrefetchScalarGridSpec(
            num_scalar_prefetch=n_prefetch,
            grid=(n_rows // rows - 1,),
            in_specs=in_specs,
            out_specs=pl.BlockSpec((2, rows, q), lambda b, be, bv, bs, bd: (0, bd[b], 0)),
            scratch_shapes=[pltpu.VMEM((d, 2 * hid), BF16), pltpu.VMEM((hid, d), BF16)]),
        out_shape=jax.ShapeDtypeStruct(xs.shape, jnp.int32),
        compiler_params=_cparams("arbitrary"),
    )(blk[0], blk[1], blk[2], blk[3], *args)


def _combine_kernel(x1_ref, mod_ref, y8_ref, g8_ref, h2p_ref, wsgu_ref, wsd_ref, *rest):
    o_ref = rest[-1]
    g8 = g8_ref[...]
    h2 = _load_planes(h2p_ref[0], h2p_ref[1]).astype(BF16)
    ffn = _dot(_swiglu_hidden(h2, wsgu_ref[...]).astype(BF16), wsd_ref[...])
    for k in range(TOP_K):
        ffn = ffn + g8[:, k:k + 1] * _load_planes(y8_ref[k, 0], y8_ref[k, 1])
    o_ref[0] = x1_ref[0] + mod_ref[0, 5:6, :] * ffn


def _combine(x1, mod, y8, gate8, h2p, ws_gu, ws_d, b0, total, earlier, after):
    bsz, s, d = x1.shape
    tm = min(COMBINE_ROWS, s)
    ns = s // tm
    full = lambda shape: pl.BlockSpec(shape, lambda b, i: (0,) * len(shape))
    in_specs = [pl.BlockSpec((1, tm, d), lambda b, i: (b, i, 0)),
                pl.BlockSpec((1, N_MOD, d), lambda b, i: (b, 0, 0)),
                pl.BlockSpec((TOP_K, 2, tm, d // 4), lambda b, i: (0, 0, b * ns + i, 0)),
                pl.BlockSpec((tm, TOP_K), lambda b, i: (b * ns + i, 0)),
                pl.BlockSpec((2, tm, d // 4), lambda b, i: (0, b * ns + i, 0)),
                full(ws_gu.shape), full(ws_d.shape)]
    args = [x1, mod, y8, gate8, h2p, ws_gu, ws_d]
    aliases = {}
    if earlier is not None:
        in_specs.append(pl.BlockSpec(memory_space=pl.ANY))
        args.append(earlier)
        aliases = {len(args) - 1: 0}
    if after is not None:
        in_specs.append(pl.BlockSpec(memory_space=pl.ANY))
        args.append(after)
    return pl.pallas_call(
        _combine_kernel,
        grid=(bsz, ns),
        in_specs=in_specs,
        out_specs=pl.BlockSpec((1, tm, d), lambda b, i: (b0 + b, i, 0)),
        out_shape=jax.ShapeDtypeStruct((total, s, d), F32),
        input_output_aliases=aliases,
        compiler_params=_cparams("parallel", "parallel"),
    )(*args)


def kernel(x, c, w_ada, b_ada, norm1_g, w_in, sb_q_norm_g, sb_k_norm_g, hg_lb_logits, hg_norm_g,
           w_branch_sb, w_branch_hg, w_out, norm2_g, w_router, router_bias, w_e_gate, w_e_up,
           w_e_down, w_s_gate, w_s_up, w_s_down):
    bsz, s, d = x.shape
    depth = w_ada.shape[0]
    n_gate_cols = 2 * d
    qkv_col0 = n_gate_cols // LANES
    hg_col0 = qkv_col0 + 3 * SB_WIDTH // LANES
    for l in range(depth):
        n_mix = 3 * SB_WIDTH + 4 * HG_WIDTH
        w_in_l = jnp.concatenate([w_in[l][:, n_mix:], w_in[l][:, :n_mix]], axis=1).astype(BF16)
        wr_t = w_router[l].T
        wr_hi = wr_t.astype(BF16)
        wr_lo = (wr_t - wr_hi.astype(F32)).astype(BF16)
        ws_gu = jnp.concatenate([w_s_gate[l], w_s_up[l]], axis=1).astype(BF16)

        w_sb, w_hg = w_branch_sb[l].astype(BF16), w_branch_hg[l].astype(BF16)
        w_o, ws_d = w_out[l].astype(BF16), w_s_down[l].astype(BF16)

        mod = _ada(c, w_ada[l], b_ada[l]).reshape(bsz, N_MOD, d)
        n_parts = BATCH_PARTS if bsz % BATCH_PARTS == 0 else 1
        pb = bsz // n_parts
        t = pb * s
        q = d // 4
        n_blocks = -(-(t * TOP_K + N_EXPERTS * (DISPATCH_ROWS - 1)) // DISPATCH_ROWS)
        n_rows = (n_blocks + 1) * DISPATCH_ROWS
        plane_off = jnp.array([0, n_rows], jnp.int32)[None, :, None]

        def experts(st, after):
            ys = _gmm(st["blk"], st["xs"].reshape(2, n_rows, q), w_e_gate[l], w_e_up[l],
                      w_e_down[l], DISPATCH_ROWS, after)
            y8 = _sc_gather_rows(ys.reshape(2 * n_rows, q), st["row_idx"].reshape(1, TOP_K * 2 * t))
            return ys, y8.reshape(TOP_K, 2, t, q)

        out = None
        prev = None
        for p in range(n_parts):
            b0 = p * pb
            mod_p = mod[b0:b0 + pb]
            proj, qa, qb, knt, kc, lfh, lfl, qs, gs = _inproj(
                x, mod_p, norm1_g[l], w_in_l, sb_q_norm_g[l], sb_k_norm_g[l], hg_lb_logits, l,
                qkv_col0, b0, prev and prev["blk"])
            o_hg = _hgrn(proj, kc, lfh, lfl, qs, gs, hg_norm_g[l], hg_col0 + HG_WIDTH // LANES)
            if prev:
                prev["ys"], prev["y8"] = experts(prev, o_hg)
            o_sb = _sb_attention(proj, qa, qb, knt, qkv_col0, prev and prev["ys"])
            if prev:
                out = _combine(prev["x1"], prev["mod"], prev["y8"], prev["gate8"], prev["h2p"],
                               ws_gu, ws_d, prev["b0"], bsz, out, o_sb)
            x1, h2p, logits_t = _merge(x, o_sb, o_hg, proj, mod_p, norm2_g[l], w_sb, w_hg, w_o,
                                       wr_hi, wr_lo, b0, out)
            gates_t, rank_t, counts = _route(logits_t, router_bias[l])
            slot8, gate8, blk = _slots(gates_t, rank_t, counts, DISPATCH_ROWS, n_blocks)
            row_idx = (slot8[:, None, :] + plane_off).reshape(TOP_K, 2 * t)
            xs = _sc_scatter_rows(h2p.reshape(2 * t, q), row_idx, 2 * n_rows)
            prev = dict(blk=blk, xs=xs, row_idx=row_idx, x1=x1, h2p=h2p, mod=mod_p, gate8=gate8,
                        b0=b0)
        _, y8 = experts(prev, None)
        x = _combine(prev["x1"], prev["mod"], y8, prev["gate8"], prev["h2p"], ws_gu, ws_d,
                     prev["b0"], bsz, out, None)
    return x
```

```python
import functools

import jax
import jax.numpy as jnp
from jax import lax
from jax.experimental import pallas as pl
from jax.experimental.pallas import tpu as pltpu
from jax.experimental.pallas import tpu_sc as plsc

F32 = jnp.float32
BF16 = jnp.bfloat16

SB_HEADS = 8
SB_HEAD_DIM = 64
SB_WIDTH = SB_HEADS * SB_HEAD_DIM
HG_HEADS = 4
HG_HEAD_DIM = 128
HG_WIDTH = HG_HEADS * HG_HEAD_DIM
HG_CHUNK = 64
N_EXPERTS = 64
TOP_K = 8
N_GROUPS = 8
TOPK_GROUPS = 4
GROUP_SIZE = N_EXPERTS // N_GROUPS
ROUTED_SCALE = 2.5
DISPATCH_ROWS = 1024
BATCH_PARTS = 2
N_MOD = 6
EPS = 1e-6
LOG2_E = 1.4426950408889634
SB_DEAD_LOG2 = 160.0

LANES = 128
VMEM_LIMIT = 56 * 1024 * 1024

ADA_COLS = 1024
ROW_TILE = 512
SB_QUERY_BLOCK = 512
SB_PAIRS_PER_STEP = 2
HG_SEQ_TILE = 1024
ROUTE_TOKENS = 1024
COMBINE_ROWS = 512


def _cparams(*sem):
    return pltpu.CompilerParams(dimension_semantics=sem, vmem_limit_bytes=VMEM_LIMIT)


def _silu(t):
    return t * jax.nn.sigmoid(t)


def _dot(a, b):
    return jnp.dot(a, b, preferred_element_type=F32)


def _run_after(body, in_specs, args, after):
    if after is None:
        return body
    pos = len(args)
    in_specs.append(pl.BlockSpec(memory_space=pl.ANY))
    args.append(after)
    return lambda *refs: body(*refs[:pos], *refs[pos + 1:])


def _dot_nt(a, b):
    return lax.dot_general(a, b, (((1,), (1,)), ((), ())), preferred_element_type=F32)


def _split_bf16(t):
    hi = t.astype(BF16)
    lo = (t - hi.astype(F32)).astype(BF16)
    return hi, lo


def _ada_kernel(c_ref, w_ref, b_ref, o_ref):
    cond = _silu(c_ref[...])
    o_ref[...] = _dot(cond, w_ref[...]) + b_ref[...]


def _ada(c, w, b):
    bsz, d = c.shape
    n = w.shape[1]
    tn = min(ADA_COLS, n)
    return pl.pallas_call(
        _ada_kernel,
        grid=(n // tn,),
        in_specs=[pl.BlockSpec((bsz, d), lambda j: (0, 0)),
                  pl.BlockSpec((d, tn), lambda j: (0, j)),
                  pl.BlockSpec((1, tn), lambda j: (0, j))],
        out_specs=pl.BlockSpec((bsz, tn), lambda j: (0, j)),
        out_shape=jax.ShapeDtypeStruct((bsz, n), F32),
        compiler_params=_cparams("parallel"),
    )(c, w, b.reshape(1, n))


def _modulated_norm(x, g, shift, scale):
    y = x * lax.rsqrt(jnp.mean(x * x, axis=-1, keepdims=True) + EPS) * g
    return y * (1.0 + scale) + shift


INPROJ_COLS = 512


def _inproj_kernel(x_ref, mod_ref, g_ref, w_ref, qg_ref, kg_ref, lbl_ref, o_ref, qa_ref, qb_ref,
                   knt_ref, kc_ref, lfh_ref, lfl_ref, qs_ref, gs_ref, *, q_chunk, layer):
    h = _modulated_norm(x_ref[0], g_ref[...], mod_ref[0, 0:1, :], mod_ref[0, 1:2, :]).astype(BF16)
    lo_half = lax.broadcasted_iota(jnp.int32, (1, LANES), 1) < SB_HEAD_DIM
    for j in range(w_ref.shape[1] // INPROJ_COLS):
        cols = slice(j * INPROJ_COLS, (j + 1) * INPROJ_COLS)
        res = _dot(h, w_ref[:, cols])
        o_ref[0, :, cols] = res.astype(o_ref.dtype)
        if j == q_chunk:
            for p in range(INPROJ_COLS // LANES):
                pair = slice(p * LANES, (p + 1) * LANES)
                qn = _pair_norm(res[:, pair], qg_ref[...], lo_half) * (SB_HEAD_DIM ** -0.5 * LOG2_E)
                qa_ref[0, :, pair] = jnp.where(lo_half, qn, 0.0).astype(BF16)
                qb_ref[0, :, pair] = jnp.where(lo_half, 0.0, qn).astype(BF16)
        if j == q_chunk + 1:
            for p in range(INPROJ_COLS // LANES):
                pair = slice(p * LANES, (p + 1) * LANES)
                knt_ref[0, pair, :] = _pair_norm(res[:, pair], kg_ref[...], lo_half).T.astype(BF16)
        if j == q_chunk + 3:
            lg = lbl_ref[...]
            e = jnp.exp(lg - jnp.max(lg, axis=0, keepdims=True))
            lb = jnp.sum(e[:layer + 1], axis=0, keepdims=True) / jnp.sum(e, axis=0, keepdims=True)
            forget = lb + (1.0 - lb) * jax.nn.sigmoid(res)
            kc_ref[0] = (1.0 - forget).astype(BF16)
            lfh_ref[0], lfl_ref[0] = _split_trunc(jnp.log(forget))
        if j == q_chunk + 5:
            qs_ref[0] = _silu(res).astype(BF16)
        if j == q_chunk + 6:
            gs_ref[0] = _silu(res).astype(BF16)


def _inproj(x, mod, g, w, qg, kg, lb_logits, layer, q_col0, b0, after):
    _, s, d = x.shape
    bsz = mod.shape[0]
    n = w.shape[1]
    nl = lb_logits.shape[0]
    tm = min(ROW_TILE, s)
    assert INPROJ_COLS == SB_WIDTH == HG_WIDTH and (q_col0 * LANES) % INPROJ_COLS == 0
    pair_gain = lambda t: jnp.tile(t.reshape(1, SB_HEAD_DIM), (1, 2))
    in_specs = [pl.BlockSpec((1, tm, d), lambda b, i: (b0 + b, i, 0)),
                pl.BlockSpec((1, N_MOD, d), lambda b, i: (b, 0, 0)),
                pl.BlockSpec((1, d), lambda b, i: (0, 0)),
                pl.BlockSpec((d, n), lambda b, i: (0, 0)),
                pl.BlockSpec((1, LANES), lambda b, i: (0, 0)),
                pl.BlockSpec((1, LANES), lambda b, i: (0, 0)),
                pl.BlockSpec((nl, HG_WIDTH), lambda b, i: (0, 0))]
    args = [x, mod, g.reshape(1, d), w, pair_gain(qg), pair_gain(kg), lb_logits]
    body = _run_after(functools.partial(_inproj_kernel, q_chunk=q_col0 * LANES // INPROJ_COLS,
                                        layer=layer), in_specs, args, after)
    rows_spec = pl.BlockSpec((1, tm, INPROJ_COLS), lambda b, i: (b, i, 0))
    rows_shape = jax.ShapeDtypeStruct((bsz, s, INPROJ_COLS), BF16)
    return pl.pallas_call(
        body,
        grid=(bsz, s // tm),
        in_specs=in_specs,
        out_specs=[pl.BlockSpec((1, tm, n), lambda b, i: (b, i, 0)),
                   rows_spec, rows_spec,
                   pl.BlockSpec((1, SB_WIDTH, tm), lambda b, i: (b, 0, i)),
                   rows_spec, rows_spec, rows_spec, rows_spec, rows_spec],
        out_shape=[jax.ShapeDtypeStruct((bsz, s, n), BF16),
                   rows_shape, rows_shape,
                   jax.ShapeDtypeStruct((bsz, SB_WIDTH, s), BF16),
                   rows_shape, rows_shape, rows_shape, rows_shape, rows_shape],
        compiler_params=_cparams("parallel", "parallel"),
    )(*args)


def _pair_norm(t, g, lo_half):
    sq = t * t
    s_lo = jnp.sum(jnp.where(lo_half, sq, 0.0), axis=-1, keepdims=True)
    s_hi = jnp.sum(jnp.where(lo_half, 0.0, sq), axis=-1, keepdims=True)
    ms = jnp.where(lo_half, s_lo, s_hi) * (1.0 / SB_HEAD_DIM)
    return t * lax.rsqrt(ms + EPS) * g


def _neg_abs(t):
    bits = lax.bitcast_convert_type(t, jnp.uint32) | jnp.uint32(0x80000000)
    return lax.bitcast_convert_type(bits, F32)


def _split_trunc(t):
    bits = lax.bitcast_convert_type(t, jnp.uint32) & jnp.uint32(0xFFFF0000)
    hi = lax.bitcast_convert_type(bits, F32)
    return hi.astype(BF16), (t - hi).astype(BF16)


def _sb_kernel(qa_ref, qb_ref, knt_ref, v_ref, tri2_ref, o_ref, va_ref, vb_ref, *, tq):
    qi = pl.program_id(2)
    s = v_ref.shape[1]
    width = v_ref.shape[2]
    lane = lax.broadcasted_iota(jnp.int32, (1, width), 1)
    lo_half = (lane % LANES) < SB_HEAD_DIM
    pairs = [slice(p * LANES, (p + 1) * LANES) for p in range(width // LANES)]

    tk = tq // 2

    @pl.when(qi == 0)
    def _():
        def prep_block(j, c):
            rows = pl.ds(pl.multiple_of(j * tk, tk), tk)
            vb = v_ref[0, rows, :]
            va_ref[rows, :] = jnp.where(lo_half, vb, jnp.zeros_like(vb))
            vb_ref[rows, :] = jnp.where(lo_half, jnp.zeros_like(vb), vb)
            return c
        lax.fori_loop(0, s // tk, prep_block, 0)

    v_heads = (va_ref, vb_ref)

    row = lax.broadcasted_iota(jnp.int32, (tk, tk), 0)
    col = lax.broadcasted_iota(jnp.int32, (tk, tk), 1)
    strict = col < row
    tri2 = tri2_ref[...]

    def sweep(streams):
        cols = [[pl.ds(pl.multiple_of(j * tk, tk), tk) for j, _, _ in blocks]
                for _, _, _, _, blocks in streams]
        z = [[[_dot(qh[h], knt_ref[0, pair, c]) for c in cols[i]] for h in range(2)]
             for i, (pair, qh, _, _, _) in enumerate(streams)]
        cs = []
        for i, (_, _, _, _, blocks) in enumerate(streams):
            cs.append([[None] * len(blocks) for _ in range(2)])
            for h in range(2):
                for b, (_, masked, _) in enumerate(blocks):
                    zb = z[i][h][b]
                    sp = jnp.maximum(zb, 0.0) + jnp.log2(1.0 + jnp.exp2(_neg_abs(zb)))
                    if masked:
                        sp = jnp.where(strict, sp, 0.0)
                    cs[i][h][b] = _dot(jnp.concatenate(_split_trunc(sp), axis=1), tri2)
        results = []
        for i, (pair, _, runs, acc, blocks) in enumerate(streams):
            runs = list(runs)
            for h in range(2):
                for b, (_, masked, gate) in enumerate(blocks):
                    a = jnp.exp2(z[i][h][b] - cs[i][h][b] - runs[h])
                    if masked:
                        a = jnp.where(strict, a, 0.0)
                    vb = v_heads[h][cols[i][b], pair]
                    step = cs[i][h][b][:, 0:1]
                    if gate is not None:
                        vb = jnp.where(gate, vb, jnp.zeros_like(vb))
                        step = jnp.where(gate, step, 0.0)
                    acc = acc + _dot(a.astype(BF16), vb)
                    runs[h] = runs[h] + step
            results.append((tuple(runs), acc))
        return results

    zero_run = jnp.zeros((tk, 1), F32)
    zero_acc = jnp.zeros((tk, LANES), F32)
    left, right = 2 * qi, 2 * qi + 1
    q_top = [(qa_ref[0, :tk, pair], qb_ref[0, :tk, pair]) for pair in pairs]
    q_bot = [(qa_ref[0, tk:, pair], qb_ref[0, tk:, pair]) for pair in pairs]
    fresh = ((zero_run, zero_run), zero_acc)
    state = sweep(
        [(pair, q_top[p], *fresh, [(left, True, None), (jnp.maximum(left - 1, 0), False, qi > 0)])
         for p, pair in enumerate(pairs)]
        + [(pair, q_bot[p], *fresh, [(right, True, None), (left, False, None)])
           for p, pair in enumerate(pairs)])

    def min_run(st):
        low = None
        for runs, _ in st:
            m = jnp.minimum(runs[0], runs[1])
            low = m if low is None else jnp.minimum(low, m)
        return jnp.min(low)

    def alive(carry):
        j, _, low = carry
        return jnp.logical_and(j >= 0, low < SB_DEAD_LOG2)

    def earlier_block(carry):
        j, st, _ = carry
        n = len(pairs)
        st = sweep(
            [(pair, q_top[p], *st[p], [(jnp.maximum(j - 1, 0), False, j > 0)])
             for p, pair in enumerate(pairs)]
            + [(pair, q_bot[p], *st[n + p], [(j, False, None)]) for p, pair in enumerate(pairs)])
        return j - 1, tuple(st), min_run(st)

    _, state, _ = lax.while_loop(alive, earlier_block, (left - 1, tuple(state), min_run(state)))
    n = len(pairs)
    o_ref[0] = jnp.concatenate(
        [jnp.concatenate([state[p][1], state[n + p][1]], axis=0) for p in range(n)],
        axis=1).astype(o_ref.dtype)


def _sb_attention(proj, qa, qb, knt, col0, after):
    bsz, s, _ = proj.shape
    tq = min(SB_QUERY_BLOCK, s)
    tk = tq // 2
    npair = SB_WIDTH // LANES
    tri = (jnp.arange(tk)[:, None] >= jnp.arange(tk)[None, :]).astype(BF16)
    tri2 = jnp.concatenate([tri, tri], axis=0)
    w = SB_PAIRS_PER_STEP * LANES
    v_blk0 = (col0 + 2 * npair) * LANES // w
    assert npair % SB_PAIRS_PER_STEP == 0 and ((col0 + 2 * npair) * LANES) % w == 0
    in_specs = [pl.BlockSpec((1, tq, w), lambda b, h, i: (b, i, h)),
                pl.BlockSpec((1, tq, w), lambda b, h, i: (b, i, h)),
                pl.BlockSpec((1, w, s), lambda b, h, i: (b, h, 0)),
                pl.BlockSpec((1, s, w), lambda b, h, i: (b, 0, v_blk0 + h)),
                pl.BlockSpec((2 * tk, tk), lambda b, h, i: (0, 0))]
    args = [qa, qb, knt, proj, tri2]
    body = _run_after(functools.partial(_sb_kernel, tq=tq), in_specs, args, after)
    return pl.pallas_call(
        body,
        grid=(bsz, npair // SB_PAIRS_PER_STEP, s // tq),
        in_specs=in_specs,
        out_specs=pl.BlockSpec((1, tq, w), lambda b, h, i: (b, i, h)),
        out_shape=jax.ShapeDtypeStruct((bsz, s, SB_WIDTH), BF16),
        scratch_shapes=[pltpu.VMEM((s, w), BF16),
                        pltpu.VMEM((s, w), BF16)],
        compiler_params=_cparams("parallel", "parallel", "arbitrary"),
    )(*args)


HG_CHUNKS_PER_STEP = 8


def _hgrn_kernel(kc_ref, lfh_ref, lfl_ref, i_ref, qs_ref, gs_ref, ng_ref, o_ref, st_ref):
    ts = kc_ref.shape[1]
    c = HG_CHUNK
    dh = HG_HEAD_DIM

    @pl.when(pl.program_id(1) == 0)
    def _():
        st_ref[...] = jnp.zeros_like(st_ref)

    row = lax.broadcasted_iota(jnp.int32, (c, c), 0)
    col = lax.broadcasted_iota(jnp.int32, (c, c), 1)
    causal = col <= row
    lower = causal.astype(BF16)
    width = kc_ref.shape[2]

    def body(it, carry):
        chunks = range(HG_CHUNKS_PER_STEP)
        heads = [slice(h * dh, (h + 1) * dh) for h in range(HG_HEADS)]
        rows = [pl.ds(pl.multiple_of((it * HG_CHUNKS_PER_STEP + u) * c, c), c) for u in chunks]
        cum = [_dot(lower, jnp.concatenate([lfh_ref[0, r, :], lfl_ref[0, r, :]], axis=1))
               for r in rows]
        qe, ke, qg, kd, decay, v, v_t = [], [], [], [], [], [], []
        for u in chunks:
            gc = cum[u][:, :width] + cum[u][:, width:]
            g_mid = gc[c // 2 - 1:c // 2, :]
            g_last = gc[c - 1:c, :]
            q_u = qs_ref[0, rows[u], :].astype(F32) * jnp.exp(gc - g_mid)
            k_u = kc_ref[0, rows[u], :].astype(F32) * jnp.exp(g_mid - gc)
            qg.append((q_u * jnp.exp(g_mid)).astype(BF16))
            kd.append((k_u * jnp.exp(g_last - g_mid)).astype(BF16))
            qe.append(q_u.astype(BF16))
            ke.append(k_u.astype(BF16))
            decay.append(jnp.exp(g_last))
            v.append(i_ref[0, rows[u], :])
            v_t.append(v[u].astype(F32).T.astype(BF16))
        scores = [[_dot_nt(qe[u][:, sl], ke[u][:, sl]) for sl in heads] for u in chunks]
        kv_t = [[_dot(v_t[u][sl, :], kd[u][:, sl]) for sl in heads] for u in chunks]
        states = [[st_ref[h] for h in range(HG_HEADS)]]
        for u in chunks:
            states.append([states[u][h] * decay[u][:, sl] + kv_t[u][h] for h, sl in enumerate(heads)])
        for u in chunks:
            outs = []
            for h, sl in enumerate(heads):
                o = (_dot(jnp.where(causal, scores[u][h], 0.0).astype(BF16), v[u][:, sl])
                     + _dot_nt(qg[u][:, sl], states[u][h].astype(BF16)))
                outs.append(o * lax.rsqrt(jnp.mean(o * o, axis=-1, keepdims=True) + EPS))
            y = jnp.concatenate(outs, axis=1) * ng_ref[...] * gs_ref[0, rows[u], :].astype(F32)
            o_ref[0, rows[u], :] = y.astype(o_ref.dtype)
        for h in range(HG_HEADS):
            st_ref[h] = states[-1][h]
        return carry

    lax.fori_loop(0, ts // (c * HG_CHUNKS_PER_STEP), body, 0)


def _hgrn(proj, kc, lfh, lfl, qs, gs, norm_g, i_col0):
    bsz, s, _ = proj.shape
    ts = min(HG_SEQ_TILE, s)
    i_blk = i_col0 * LANES // HG_WIDTH
    term = pl.BlockSpec((1, ts, HG_WIDTH), lambda b, i: (b, i, 0))
    return pl.pallas_call(
        _hgrn_kernel,
        grid=(bsz, s // ts),
        in_specs=[term, term, term,
                  pl.BlockSpec((1, ts, HG_WIDTH), lambda b, i: (b, i, i_blk)),
                  term, term,
                  pl.BlockSpec((1, HG_WIDTH), lambda b, i: (0, 0))],
        out_specs=pl.BlockSpec((1, ts, HG_WIDTH), lambda b, i: (b, i, 0)),
        out_shape=jax.ShapeDtypeStruct((bsz, s, HG_WIDTH), BF16),
        scratch_shapes=[pltpu.VMEM((HG_HEADS, HG_HEAD_DIM, HG_HEAD_DIM), F32)],
        compiler_params=_cparams("parallel", "arbitrary"),
    )(kc, lfh, lfl, proj, qs, gs, jnp.tile(norm_g.reshape(1, HG_HEAD_DIM), (1, HG_HEADS)))


def _pack_halves(t):
    bits = lax.bitcast_convert_type(t.astype(BF16).astype(F32), jnp.uint32)
    w = t.shape[1] // 2
    return lax.bitcast_convert_type(bits[:, :w] | (bits[:, w:] >> 16), jnp.int32)


def _unpack_halves(p):
    u = lax.bitcast_convert_type(p, jnp.uint32)
    return (lax.bitcast_convert_type(u & jnp.uint32(0xFFFF0000), F32),
            lax.bitcast_convert_type(u << 16, F32))


def _store_planes(ref, rows, t):
    p = _pack_halves(t)
    q = p.shape[1] // 2
    ref[0, rows, :] = p[:, :q]
    ref[1, rows, :] = p[:, q:]


def _load_planes(p0, p1):
    a0, b0 = _unpack_halves(p0)
    a1, b1 = _unpack_halves(p1)
    return jnp.concatenate([a0, a1, b0, b1], axis=1)


def _swiglu_hidden(h, w_gu):
    gu = _dot(h, w_gu)
    hid = w_gu.shape[1] // 2
    return _silu(gu[:, :hid]) * gu[:, hid:]


def _merge_kernel(x_ref, osb_ref, ohg_ref, gsb_ref, ghg_ref, mod_ref, g2_ref, wsb_ref, whg_ref,
                  wout_ref, wrh_ref, wrl_ref, x1_ref, h2p_ref, lg_ref):
    m_sb = _dot(osb_ref[0], wsb_ref[...])
    m_hg = _dot(ohg_ref[0], whg_ref[...])
    merged = (jax.nn.sigmoid(gsb_ref[0].astype(F32)) * m_sb
              + jax.nn.sigmoid(ghg_ref[0].astype(F32)) * m_hg)
    x1 = x_ref[0] + mod_ref[0, 2:3, :] * _dot(merged.astype(BF16), wout_ref[...])
    x1_ref[0] = x1
    h2 = _modulated_norm(x1, g2_ref[...], mod_ref[0, 3:4, :], mod_ref[0, 4:5, :])
    hi, lo = _split_bf16(h2)
    _store_planes(h2p_ref, slice(None), h2)
    lg_ref[...] = _dot_nt(wrh_ref[...], hi) + _dot_nt(wrh_ref[...], lo) + _dot_nt(wrl_ref[...], hi)


def _merge(x, o_sb, o_hg, proj, mod, g2, w_sb, w_hg, w_out, wr_hi, wr_lo, b0, after):
    _, s, d = x.shape
    bsz = mod.shape[0]
    tm = min(ROW_TILE, s)
    ns = s // tm
    full = lambda shape: pl.BlockSpec(shape, lambda b, i: (0,) * len(shape))
    in_specs = [pl.BlockSpec((1, tm, d), lambda b, i: (b0 + b, i, 0)),
                pl.BlockSpec((1, tm, SB_WIDTH), lambda b, i: (b, i, 0)),
                pl.BlockSpec((1, tm, HG_WIDTH), lambda b, i: (b, i, 0)),
                pl.BlockSpec((1, tm, d), lambda b, i: (b, i, 0)),
                pl.BlockSpec((1, tm, d), lambda b, i: (b, i, 1)),
                pl.BlockSpec((1, N_MOD, d), lambda b, i: (b, 0, 0)),
                full((1, d)), full(w_sb.shape), full(w_hg.shape), full(w_out.shape),
                full(wr_hi.shape), full(wr_lo.shape)]
    args = [x, o_sb, o_hg, proj, proj, mod, g2.reshape(1, d), w_sb, w_hg, w_out, wr_hi, wr_lo]
    body = _run_after(_merge_kernel, in_specs, args, after)
    return pl.pallas_call(
        body,
        grid=(bsz, ns),
        in_specs=in_specs,
        out_specs=[pl.BlockSpec((1, tm, d), lambda b, i: (b, i, 0)),
                   pl.BlockSpec((2, tm, d // 4), lambda b, i: (0, b * ns + i, 0)),
                   pl.BlockSpec((N_EXPERTS, tm), lambda b, i: (0, b * ns + i))],
        out_shape=[jax.ShapeDtypeStruct((bsz, s, d), F32),
                   jax.ShapeDtypeStruct((2, bsz * s, d // 4), jnp.int32),
                   jax.ShapeDtypeStruct((N_EXPERTS, bsz * s), F32)],
        compiler_params=_cparams("parallel", "parallel"),
    )(*args)


def _first_argmax(vals, idx, sentinel):
    m = jnp.max(vals, axis=0, keepdims=True)
    first = jnp.min(jnp.where(vals == m, idx, sentinel), axis=0, keepdims=True)
    return m, first


def _route_kernel(lg_ref, bias_ref, gates_ref, rank_ref, cnt_ref, run_ref):
    tn = lg_ref.shape[1]

    @pl.when(pl.program_id(0) == 0)
    def _():
        run_ref[...] = jnp.zeros_like(run_ref)

    neg = -jnp.inf
    scores = jax.nn.sigmoid(lg_ref[...])
    choice = scores + bias_ref[...]

    gidx = lax.broadcasted_iota(jnp.int32, (GROUP_SIZE, tn), 0)
    group_rows = []
    for g in range(N_GROUPS):
        cg = choice[g * GROUP_SIZE:(g + 1) * GROUP_SIZE, :]
        m1, i1 = _first_argmax(cg, gidx, GROUP_SIZE)
        m2 = jnp.max(jnp.where(gidx == i1, neg, cg), axis=0, keepdims=True)
        group_rows.append(m1 + m2)
    work = jnp.concatenate(group_rows, axis=0)
    ggi = lax.broadcasted_iota(jnp.int32, (N_GROUPS, tn), 0)
    gmask = jnp.zeros((N_GROUPS, tn), F32)
    for _ in range(TOPK_GROUPS):
        _, first = _first_argmax(work, ggi, N_GROUPS)
        pick = ggi == first
        gmask = jnp.where(pick, 1.0, gmask)
        work = jnp.where(pick, neg, work)

    masked = jnp.concatenate(
        [jnp.where(gmask[g:g + 1, :] > 0.0, choice[g * GROUP_SIZE:(g + 1) * GROUP_SIZE, :], neg)
         for g in range(N_GROUPS)], axis=0)
    eidx = lax.broadcasted_iota(jnp.int32, (N_EXPERTS, tn), 0)
    sel = jnp.zeros((N_EXPERTS, tn), F32)
    for _ in range(TOP_K):
        _, first = _first_argmax(masked, eidx, N_EXPERTS)
        pick = eidx == first
        sel = jnp.where(pick, 1.0, sel)
        masked = jnp.where(pick, neg, masked)

    chosen = jnp.where(sel > 0.0, scores, 0.0)
    gates_ref[...] = chosen / jnp.sum(chosen, axis=0, keepdims=True) * ROUTED_SCALE

    r = lax.broadcasted_iota(jnp.int32, (tn, tn), 0)
    c = lax.broadcasted_iota(jnp.int32, (tn, tn), 1)
    local = _dot(sel.astype(BF16), (r < c).astype(BF16))
    run = run_ref[:, 0:1]
    rank_ref[...] = jnp.where(sel > 0.0, run + local, -1.0)
    total = run + jnp.sum(sel, axis=1, keepdims=True)
    run_ref[...] = jnp.broadcast_to(total, run_ref.shape)
    cnt_ref[...] = jnp.broadcast_to(total, cnt_ref.shape)


def _route(logits_t, bias):
    e, t = logits_t.shape
    tn = min(ROUTE_TOKENS, t)
    return pl.pallas_call(
        _route_kernel,
        grid=(t // tn,),
        in_specs=[pl.BlockSpec((e, tn), lambda i: (0, i)),
                  pl.BlockSpec((e, 1), lambda i: (0, 0))],
        out_specs=[pl.BlockSpec((e, tn), lambda i: (0, i)),
                   pl.BlockSpec((e, tn), lambda i: (0, i)),
                   pl.BlockSpec((e, LANES), lambda i: (0, 0))],
        out_shape=[jax.ShapeDtypeStruct((e, t), F32),
                   jax.ShapeDtypeStruct((e, t), F32),
                   jax.ShapeDtypeStruct((e, LANES), F32)],
        scratch_shapes=[pltpu.VMEM((e, LANES), F32)],
        compiler_params=_cparams("arbitrary"),
    )(logits_t, bias.reshape(e, 1))


def _slots_kernel(gates_ref, rank_ref, cnt_ref, slot_ref, gate8_ref, blk_ref, *, rows, n_blocks):
    ne, tn = gates_ref.shape
    cnt = cnt_ref[...]
    nblk = jnp.floor((cnt + (rows - 1.0)) * (1.0 / rows))
    er = lax.broadcasted_iota(jnp.int32, (ne, ne), 0)
    ec = lax.broadcasted_iota(jnp.int32, (ne, ne), 1)
    lower = (ec < er).astype(BF16)
    pad_start = _dot(lower, nblk.astype(BF16))[:, 0:1] * rows
    pad_end = pad_start + nblk[:, 0:1] * rows

    rank = rank_ref[...]
    sel = rank >= 0.0
    slot_e = pad_start + rank
    kidx = _dot(lower, sel.astype(BF16))
    gates = gates_ref[...]
    slot_rows, gate_rows = [], []
    for k in range(TOP_K):
        m = jnp.logical_and(sel, kidx == k)
        slot_rows.append(jnp.sum(jnp.where(m, slot_e, 0.0), axis=0, keepdims=True))
        gate_rows.append(jnp.sum(jnp.where(m, gates, 0.0), axis=0, keepdims=True))
    slot_ref[...] = jnp.concatenate(slot_rows, axis=0).astype(jnp.int32)
    gate8_ref[...] = jnp.concatenate(gate_rows, axis=0).T

    nbp = blk_ref.shape[1]
    bstart = lax.broadcasted_iota(jnp.int32, (1, nbp), 1).astype(F32) * rows
    e_of = jnp.sum((pad_end <= bstart).astype(F32), axis=0, keepdims=True)
    e_of = jnp.minimum(e_of, ne - 1.0)
    eidx = lax.broadcasted_iota(jnp.int32, (ne, nbp), 0).astype(F32)
    valid_e = jnp.clip(cnt[:, 0:1] - (bstart - pad_start), 0.0, rows)
    valid = jnp.sum(jnp.where(eidx == e_of, valid_e, 0.0), axis=0, keepdims=True)
    total = pad_end[ne - 1:ne, :]
    used = bstart < total
    bidx = bstart * (1.0 / rows)
    src = jnp.where(used, bidx, total * (1.0 / rows) - 1.0)
    dst = jnp.where(used, bidx, float(n_blocks))
    blk_ref[...] = jnp.concatenate(
        [e_of, valid, src, dst, jnp.zeros((blk_ref.shape[0] - 4, nbp), F32)], axis=0).astype(jnp.int32)


def _slots(gates_t, rank_t, counts, rows, n_blocks):
    nbp = -(-n_blocks // LANES) * LANES
    e, t = gates_t.shape
    assert t // rows <= 256
    tn = min(ROUTE_TOKENS, t)
    return pl.pallas_call(
        functools.partial(_slots_kernel, rows=rows, n_blocks=n_blocks),
        grid=(t // tn,),
        in_specs=[pl.BlockSpec((e, tn), lambda i: (0, i)),
                  pl.BlockSpec((e, tn), lambda i: (0, i)),
                  pl.BlockSpec((e, LANES), lambda i: (0, 0))],
        out_specs=[pl.BlockSpec((TOP_K, tn), lambda i: (0, i)),
                   pl.BlockSpec((tn, TOP_K), lambda i: (i, 0)),
                   pl.BlockSpec((8, nbp), lambda i: (0, 0))],
        out_shape=[jax.ShapeDtypeStruct((TOP_K, t), jnp.int32),
                   jax.ShapeDtypeStruct((t, TOP_K), F32),
                   jax.ShapeDtypeStruct((8, nbp), jnp.int32)],
        compiler_params=_cparams("arbitrary"),
    )(gates_t, rank_t, counts)


SC_WINDOW = 128


def _sc_mesh():
    return plsc.VectorSubcoreMesh(core_axis_name="core", subcore_axis_name="subcore")


def _sc_gather_rows(table, idx):
    n = idx.shape[1]
    w = table.shape[1]

    @pl.kernel(out_type=jax.ShapeDtypeStruct((n, w), table.dtype), mesh=_sc_mesh())
    def gather(t_hbm, i_hbm, o_hbm):
        def body(i_vmem, o_vmem):
            pltpu.sync_copy(t_hbm.at[i_vmem.at[0]], o_vmem)

        pltpu.emit_pipeline(
            body, grid=(n // SC_WINDOW,),
            in_specs=[pl.BlockSpec((1, SC_WINDOW), lambda i: (0, i))],
            out_specs=[pl.BlockSpec((SC_WINDOW, w), lambda i: (i, 0))],
            core_axis_name=("core", "subcore"),
            dimension_semantics=(pltpu.PARALLEL,),
        )(i_hbm, o_hbm)

    return gather(table, idx)


def _sc_scatter_rows(rows, idx, n_out):
    fan, m = idx.shape
    w = rows.shape[1]

    @pl.kernel(out_type=jax.ShapeDtypeStruct((n_out, w), rows.dtype), mesh=_sc_mesh())
    def scatter(r_hbm, i_hbm, o_hbm):
        def body(r_vmem, i_vmem):
            for k in range(fan):
                pltpu.sync_copy(r_vmem, o_hbm.at[i_vmem.at[k]])

        pltpu.emit_pipeline(
            body, grid=(m // SC_WINDOW,),
            in_specs=[pl.BlockSpec((SC_WINDOW, w), lambda i: (i, 0)),
                      pl.BlockSpec((fan, SC_WINDOW), lambda i: (0, i))],
            out_specs=[],
            core_axis_name=("core", "subcore"),
            dimension_semantics=(pltpu.PARALLEL,),
        )(r_hbm, i_hbm)

    return scatter(rows, idx)


def _gmm_kernel(be_ref, bv_ref, bs_ref, bd_ref, xs_ref, wg0, wu0, wd0, wg1, wu1, wd1, ys_ref,
                wgu0_s, wd0_s, wgu1_s, wd1_s):
    b = pl.program_id(0)
    rows = xs_ref.shape[1] // 2
    hid = wg0.shape[2]
    sets = ((wg0, wu0, wd0, wgu0_s, wd0_s), (wg1, wu1, wd1, wgu1_s, wd1_s))

    for half, (wg, wu, wd, wgu_s, wd_s) in enumerate(sets):
        blk = 2 * b + half

        @pl.when(jnp.logical_or(b == 0, be_ref[blk] != be_ref[jnp.maximum(blk - 2, 0)]))
        def _(wg=wg, wu=wu, wd=wd, wgu_s=wgu_s, wd_s=wd_s):
            wgu_s[:, :hid] = wg[0].astype(BF16)
            wgu_s[:, hid:] = wu[0].astype(BF16)
            wd_s[...] = wd[0].astype(BF16)

    any_valid = bv_ref[2 * b] + bv_ref[2 * b + 1] > 0

    @pl.when(any_valid)
    def _():
        for half, (_, _, _, wgu_s, wd_s) in enumerate(sets):
            valid = bv_ref[2 * b + half]
            r = slice(half * rows, (half + 1) * rows)
            x = _load_planes(xs_ref[0, r, :], xs_ref[1, r, :]).astype(BF16)
            y = _dot(_swiglu_hidden(x, wgu_s[...]).astype(BF16), wd_s[...])
            row = lax.broadcasted_iota(jnp.int32, (rows, 1), 0)
            _store_planes(ys_ref, r, jnp.where(row < valid, y, 0.0))

    @pl.when(jnp.logical_not(any_valid))
    def _():
        ys_ref[...] = jnp.zeros_like(ys_ref)


def _gmm(blk, xs, we_gate, we_up, we_down, rows, after):
    _, n_rows, q = xs.shape
    _, d, hid = we_gate.shape
    n_pairs = n_rows // (2 * rows) - 1
    w_spec = lambda w, half: pl.BlockSpec((1,) + w.shape[1:],
                                          lambda b, be, bv, bs, bd: (be[2 * b + half], 0, 0))
    in_specs = [pl.BlockSpec((2, 2 * rows, q), lambda b, be, bv, bs, bd: (0, bs[2 * b] // 2, 0)),
                w_spec(we_gate, 0), w_spec(we_up, 0), w_spec(we_down, 0),
                w_spec(we_gate, 1), w_spec(we_up, 1), w_spec(we_down, 1)]
    args = [xs, we_gate, we_up, we_down, we_gate, we_up, we_down]
    n_prefetch = 4
    body = _gmm_kernel
    if after is not None:
        pos = n_prefetch + len(args)
        in_specs.append(pl.BlockSpec(memory_space=pl.ANY))
        args.append(after)
        body = lambda *refs: _gmm_kernel(*refs[:pos], *refs[pos + 1:])
    return pl.pallas_call(
        body,
        grid_spec=pltpu.PrefetchScalarGridSpec(
            num_scalar_prefetch=n_prefetch,
            grid=(n_pairs,),
            in_specs=in_specs,
            out_specs=pl.BlockSpec((2, 2 * rows, q),
                                   lambda b, be, bv, bs, bd: (0, bd[2 * b] // 2, 0)),
            scratch_shapes=[pltpu.VMEM((d, 2 * hid), BF16), pltpu.VMEM((hid, d), BF16),
                            pltpu.VMEM((d, 2 * hid), BF16), pltpu.VMEM((hid, d), BF16)]),
        out_shape=jax.ShapeDtypeStruct(xs.shape, jnp.int32),
        compiler_params=_cparams("arbitrary"),
    )(blk[0], blk[1], blk[2], blk[3], *args)


def _combine_kernel(x1_ref, mod_ref, y8_ref, g8_ref, h2p_ref, wsgu_ref, wsd_ref, *rest):
    o_ref = rest[-1]
    g8 = g8_ref[...]
    h2 = _load_planes(h2p_ref[0], h2p_ref[1]).astype(BF16)
    ffn = _dot(_swiglu_hidden(h2, wsgu_ref[...]).astype(BF16), wsd_ref[...])
    for k in range(TOP_K):
        ffn = ffn + g8[:, k:k + 1] * _load_planes(y8_ref[k, 0], y8_ref[k, 1])
    o_ref[0] = x1_ref[0] + mod_ref[0, 5:6, :] * ffn


def _combine(x1, mod, y8, gate8, h2p, ws_gu, ws_d, b0, total, earlier, after):
    bsz, s, d = x1.shape
    tm = min(COMBINE_ROWS, s)
    ns = s // tm
    full = lambda shape: pl.BlockSpec(shape, lambda b, i: (0,) * len(shape))
    in_specs = [pl.BlockSpec((1, tm, d), lambda b, i: (b, i, 0)),
                pl.BlockSpec((1, N_MOD, d), lambda b, i: (b, 0, 0)),
                pl.BlockSpec((TOP_K, 2, tm, d // 4), lambda b, i: (0, 0, b * ns + i, 0)),
                pl.BlockSpec((tm, TOP_K), lambda b, i: (b * ns + i, 0)),
                pl.BlockSpec((2, tm, d // 4), lambda b, i: (0, b * ns + i, 0)),
                full(ws_gu.shape), full(ws_d.shape)]
    args = [x1, mod, y8, gate8, h2p, ws_gu, ws_d]
    aliases = {}
    if earlier is not None:
        in_specs.append(pl.BlockSpec(memory_space=pl.ANY))
        args.append(earlier)
        aliases = {len(args) - 1: 0}
    if after is not None:
        in_specs.append(pl.BlockSpec(memory_space=pl.ANY))
        args.append(after)
    return pl.pallas_call(
        _combine_kernel,
        grid=(bsz, ns),
        in_specs=in_specs,
        out_specs=pl.BlockSpec((1, tm, d), lambda b, i: (b0 + b, i, 0)),
        out_shape=jax.ShapeDtypeStruct((total, s, d), F32),
        input_output_aliases=aliases,
        compiler_params=_cparams("parallel", "parallel"),
    )(*args)


def kernel(x, c, w_ada, b_ada, norm1_g, w_in, sb_q_norm_g, sb_k_norm_g, hg_lb_logits, hg_norm_g,
           w_branch_sb, w_branch_hg, w_out, norm2_g, w_router, router_bias, w_e_gate, w_e_up,
           w_e_down, w_s_gate, w_s_up, w_s_down):
    bsz, s, d = x.shape
    depth = w_ada.shape[0]
    n_gate_cols = 2 * d
    qkv_col0 = n_gate_cols // LANES
    hg_col0 = qkv_col0 + 3 * SB_WIDTH // LANES
    for l in range(depth):
        n_mix = 3 * SB_WIDTH + 4 * HG_WIDTH
        w_in_l = jnp.concatenate([w_in[l][:, n_mix:], w_in[l][:, :n_mix]], axis=1).astype(BF16)
        wr_t = w_router[l].T
        wr_hi = wr_t.astype(BF16)
        wr_lo = (wr_t - wr_hi.astype(F32)).astype(BF16)
        ws_gu = jnp.concatenate([w_s_gate[l], w_s_up[l]], axis=1).astype(BF16)

        w_sb, w_hg = w_branch_sb[l].astype(BF16), w_branch_hg[l].astype(BF16)
        w_o, ws_d = w_out[l].astype(BF16), w_s_down[l].astype(BF16)

        mod = _ada(c, w_ada[l], b_ada[l]).reshape(bsz, N_MOD, d)
        n_parts = BATCH_PARTS if bsz % BATCH_PARTS == 0 else 1
        pb = bsz // n_parts
        t = pb * s
        q = d // 4
        n_blocks = -(-(t * TOP_K + N_EXPERTS * (DISPATCH_ROWS - 1)) // DISPATCH_ROWS)
        n_blocks += n_blocks % 2
        n_rows = (n_blocks + 2) * DISPATCH_ROWS
        plane_off = jnp.array([0, n_rows], jnp.int32)[None, :, None]

        def experts(st, after):
            ys = _gmm(st["blk"], st["xs"].reshape(2, n_rows, q), w_e_gate[l], w_e_up[l],
                      w_e_down[l], DISPATCH_ROWS, after)
            y8 = _sc_gather_rows(ys.reshape(2 * n_rows, q), st["row_idx"].reshape(1, TOP_K * 2 * t))
            return ys, y8.reshape(TOP_K, 2, t, q)

        out = None
        prev = None
        for p in range(n_parts):
            b0 = p * pb
            mod_p = mod[b0:b0 + pb]
            proj, qa, qb, knt, kc, lfh, lfl, qs, gs = _inproj(
                x, mod_p, norm1_g[l], w_in_l, sb_q_norm_g[l], sb_k_norm_g[l], hg_lb_logits, l,
                qkv_col0, b0, prev and prev["blk"])
            o_hg = _hgrn(proj, kc, lfh, lfl, qs, gs, hg_norm_g[l], hg_col0 + HG_WIDTH // LANES)
            if prev:
                prev["ys"], prev["y8"] = experts(prev, o_hg)
            o_sb = _sb_attention(proj, qa, qb, knt, qkv_col0, prev and prev["ys"])
            if prev:
                out = _combine(prev["x1"], prev["mod"], prev["y8"], prev["gate8"], prev["h2p"],
                               ws_gu, ws_d, prev["b0"], bsz, out, o_sb)
            x1, h2p, logits_t = _merge(x, o_sb, o_hg, proj, mod_p, norm2_g[l], w_sb, w_hg, w_o,
                                       wr_hi, wr_lo, b0, out)
            gates_t, rank_t, counts = _route(logits_t, router_bias[l])
            slot8, gate8, blk = _slots(gates_t, rank_t, counts, DISPATCH_ROWS, n_blocks)
            row_idx = (slot8[:, None, :] + plane_off).reshape(TOP_K, 2 * t)
            xs = _sc_scatter_rows(h2p.reshape(2 * t, q), row_idx, 2 * n_rows)
            prev = dict(blk=blk, xs=xs, row_idx=row_idx, x1=x1, h2p=h2p, mod=mod_p, gate8=gate8,
                        b0=b0)
        _, y8 = experts(prev, None)
        x = _combine(prev["x1"], prev["mod"], y8, prev["gate8"], prev["h2p"], ws_gu, ws_d,
                     prev["b0"], bsz, out, None)
    return x
```

```python
import functools

import jax
import jax.numpy as jnp
from jax import lax
from jax.experimental import pallas as pl
from jax.experimental.pallas import tpu as pltpu
from jax.experimental.pallas import tpu_sc as plsc

F32 = jnp.float32
BF16 = jnp.bfloat16

SB_HEADS = 8
SB_HEAD_DIM = 64
SB_WIDTH = SB_HEADS * SB_HEAD_DIM
HG_HEADS = 4
HG_HEAD_DIM = 128
HG_WIDTH = HG_HEADS * HG_HEAD_DIM
HG_CHUNK = 64
N_EXPERTS = 64
TOP_K = 8
N_GROUPS = 8
TOPK_GROUPS = 4
GROUP_SIZE = N_EXPERTS // N_GROUPS
ROUTED_SCALE = 2.5
DISPATCH_ROWS = 1024
BATCH_PARTS = 2
N_MOD = 6
EPS = 1e-6
LOG2_E = 1.4426950408889634
SB_DEAD_LOG2 = 160.0

LANES = 128
VMEM_LIMIT = 56 * 1024 * 1024

ADA_COLS = 1024
ROW_TILE = 512
SB_QUERY_BLOCK = 512
SB_PAIRS_PER_STEP = 4
HG_SEQ_TILE = 1024
ROUTE_TOKENS = 1024
COMBINE_ROWS = 512


def _cparams(*sem):
    return pltpu.CompilerParams(dimension_semantics=sem, vmem_limit_bytes=VMEM_LIMIT)


def _silu(t):
    return t * jax.nn.sigmoid(t)


def _dot(a, b):
    return jnp.dot(a, b, preferred_element_type=F32)


def _run_after(body, in_specs, args, after):
    if after is None:
        return body
    pos = len(args)
    in_specs.append(pl.BlockSpec(memory_space=pl.ANY))
    args.append(after)
    return lambda *refs: body(*refs[:pos], *refs[pos + 1:])


def _dot_nt(a, b):
    return lax.dot_general(a, b, (((1,), (1,)), ((), ())), preferred_element_type=F32)


def _split_bf16(t):
    hi = t.astype(BF16)
    lo = (t - hi.astype(F32)).astype(BF16)
    return hi, lo


def _ada_kernel(c_ref, w_ref, b_ref, o_ref):
    cond = _silu(c_ref[...])
    o_ref[...] = _dot(cond, w_ref[...]) + b_ref[...]


def _ada(c, w, b):
    bsz, d = c.shape
    n = w.shape[1]
    tn = min(ADA_COLS, n)
    return pl.pallas_call(
        _ada_kernel,
        grid=(n // tn,),
        in_specs=[pl.BlockSpec((bsz, d), lambda j: (0, 0)),
                  pl.BlockSpec((d, tn), lambda j: (0, j)),
                  pl.BlockSpec((1, tn), lambda j: (0, j))],
        out_specs=pl.BlockSpec((bsz, tn), lambda j: (0, j)),
        out_shape=jax.ShapeDtypeStruct((bsz, n), F32),
        compiler_params=_cparams("parallel"),
    )(c, w, b.reshape(1, n))


def _modulated_norm(x, g, shift, scale):
    y = x * lax.rsqrt(jnp.mean(x * x, axis=-1, keepdims=True) + EPS) * g
    return y * (1.0 + scale) + shift


INPROJ_COLS = 512


def _inproj_kernel(x_ref, mod_ref, g_ref, w_ref, qg_ref, kg_ref, lbl_ref, o_ref, qa_ref, qb_ref,
                   knt_ref, kc_ref, lfh_ref, lfl_ref, qs_ref, gs_ref, *, q_chunk, layer):
    h = _modulated_norm(x_ref[0], g_ref[...], mod_ref[0, 0:1, :], mod_ref[0, 1:2, :]).astype(BF16)
    lo_half = lax.broadcasted_iota(jnp.int32, (1, LANES), 1) < SB_HEAD_DIM
    for j in range(w_ref.shape[1] // INPROJ_COLS):
        cols = slice(j * INPROJ_COLS, (j + 1) * INPROJ_COLS)
        res = _dot(h, w_ref[:, cols])
        o_ref[0, :, cols] = res.astype(o_ref.dtype)
        if j == q_chunk:
            for p in range(INPROJ_COLS // LANES):
                pair = slice(p * LANES, (p + 1) * LANES)
                qn = _pair_norm(res[:, pair], qg_ref[...], lo_half) * (SB_HEAD_DIM ** -0.5 * LOG2_E)
                qa_ref[0, :, pair] = jnp.where(lo_half, qn, 0.0).astype(BF16)
                qb_ref[0, :, pair] = jnp.where(lo_half, 0.0, qn).astype(BF16)
        if j == q_chunk + 1:
            for p in range(INPROJ_COLS // LANES):
                pair = slice(p * LANES, (p + 1) * LANES)
                knt_ref[0, pair, :] = _pair_norm(res[:, pair], kg_ref[...], lo_half).T.astype(BF16)
        if j == q_chunk + 3:
            lg = lbl_ref[...]
            e = jnp.exp(lg - jnp.max(lg, axis=0, keepdims=True))
            lb = jnp.sum(e[:layer + 1], axis=0, keepdims=True) / jnp.sum(e, axis=0, keepdims=True)
            forget = lb + (1.0 - lb) * jax.nn.sigmoid(res)
            kc_ref[0] = (1.0 - forget).astype(BF16)
            lfh_ref[0], lfl_ref[0] = _split_trunc(jnp.log(forget))
        if j == q_chunk + 5:
            qs_ref[0] = _silu(res).astype(BF16)
        if j == q_chunk + 6:
            gs_ref[0] = _silu(res).astype(BF16)


def _inproj(x, mod, g, w, qg, kg, lb_logits, layer, q_col0, b0, after):
    _, s, d = x.shape
    bsz = mod.shape[0]
    n = w.shape[1]
    nl = lb_logits.shape[0]
    tm = min(ROW_TILE, s)
    assert INPROJ_COLS == SB_WIDTH == HG_WIDTH and (q_col0 * LANES) % INPROJ_COLS == 0
    pair_gain = lambda t: jnp.tile(t.reshape(1, SB_HEAD_DIM), (1, 2))
    in_specs = [pl.BlockSpec((1, tm, d), lambda b, i: (b0 + b, i, 0)),
                pl.BlockSpec((1, N_MOD, d), lambda b, i: (b, 0, 0)),
                pl.BlockSpec((1, d), lambda b, i: (0, 0)),
                pl.BlockSpec((d, n), lambda b, i: (0, 0)),
                pl.BlockSpec((1, LANES), lambda b, i: (0, 0)),
                pl.BlockSpec((1, LANES), lambda b, i: (0, 0)),
                pl.BlockSpec((nl, HG_WIDTH), lambda b, i: (0, 0))]
    args = [x, mod, g.reshape(1, d), w, pair_gain(qg), pair_gain(kg), lb_logits]
    body = _run_after(functools.partial(_inproj_kernel, q_chunk=q_col0 * LANES // INPROJ_COLS,
                                        layer=layer), in_specs, args, after)
    rows_spec = pl.BlockSpec((1, tm, INPROJ_COLS), lambda b, i: (b, i, 0))
    rows_shape = jax.ShapeDtypeStruct((bsz, s, INPROJ_COLS), BF16)
    return pl.pallas_call(
        body,
        grid=(bsz, s // tm),
        in_specs=in_specs,
        out_specs=[pl.BlockSpec((1, tm, n), lambda b, i: (b, i, 0)),
                   rows_spec, rows_spec,
                   pl.BlockSpec((1, SB_WIDTH, tm), lambda b, i: (b, 0, i)),
                   rows_spec, rows_spec, rows_spec, rows_spec, rows_spec],
        out_shape=[jax.ShapeDtypeStruct((bsz, s, n), BF16),
                   rows_shape, rows_shape,
                   jax.ShapeDtypeStruct((bsz, SB_WIDTH, s), BF16),
                   rows_shape, rows_shape, rows_shape, rows_shape, rows_shape],
        compiler_params=_cparams("parallel", "parallel"),
    )(*args)


def _pair_norm(t, g, lo_half):
    sq = t * t
    s_lo = jnp.sum(jnp.where(lo_half, sq, 0.0), axis=-1, keepdims=True)
    s_hi = jnp.sum(jnp.where(lo_half, 0.0, sq), axis=-1, keepdims=True)
    ms = jnp.where(lo_half, s_lo, s_hi) * (1.0 / SB_HEAD_DIM)
    return t * lax.rsqrt(ms + EPS) * g


def _neg_abs(t):
    bits = lax.bitcast_convert_type(t, jnp.uint32) | jnp.uint32(0x80000000)
    return lax.bitcast_convert_type(bits, F32)


def _split_trunc(t):
    bits = lax.bitcast_convert_type(t, jnp.uint32) & jnp.uint32(0xFFFF0000)
    hi = lax.bitcast_convert_type(bits, F32)
    return hi.astype(BF16), (t - hi).astype(BF16)


def _sb_kernel(qa_ref, qb_ref, knt_ref, v_ref, tri2_ref, o_ref, va_ref, vb_ref, *, tq):
    qi = pl.program_id(2)
    s = v_ref.shape[1]
    width = v_ref.shape[2]
    lane = lax.broadcasted_iota(jnp.int32, (1, width), 1)
    lo_half = (lane % LANES) < SB_HEAD_DIM
    pairs = [slice(p * LANES, (p + 1) * LANES) for p in range(width // LANES)]

    tk = tq // 2

    @pl.when(qi == 0)
    def _():
        def prep_block(j, c):
            rows = pl.ds(pl.multiple_of(j * tk, tk), tk)
            vb = v_ref[0, rows, :]
            va_ref[rows, :] = jnp.where(lo_half, vb, jnp.zeros_like(vb))
            vb_ref[rows, :] = jnp.where(lo_half, jnp.zeros_like(vb), vb)
            return c
        lax.fori_loop(0, s // tk, prep_block, 0)

    v_heads = (va_ref, vb_ref)

    row = lax.broadcasted_iota(jnp.int32, (tk, tk), 0)
    col = lax.broadcasted_iota(jnp.int32, (tk, tk), 1)
    strict = col < row
    tri2 = tri2_ref[...]

    def sweep(streams):
        cols = [[pl.ds(pl.multiple_of(j * tk, tk), tk) for j, _, _ in blocks]
                for _, _, _, _, blocks in streams]
        z = [[[_dot(qh[h], knt_ref[0, pair, c]) for c in cols[i]] for h in range(2)]
             for i, (pair, qh, _, _, _) in enumerate(streams)]
        cs = []
        for i, (_, _, _, _, blocks) in enumerate(streams):
            cs.append([[None] * len(blocks) for _ in range(2)])
            for h in range(2):
                for b, (_, masked, _) in enumerate(blocks):
                    zb = z[i][h][b]
                    sp = jnp.maximum(zb, 0.0) + jnp.log2(1.0 + jnp.exp2(_neg_abs(zb)))
                    if masked:
                        sp = jnp.where(strict, sp, 0.0)
                    cs[i][h][b] = _dot(jnp.concatenate(_split_trunc(sp), axis=1), tri2)
        results = []
        for i, (pair, _, runs, acc, blocks) in enumerate(streams):
            runs = list(runs)
            for h in range(2):
                for b, (_, masked, gate) in enumerate(blocks):
                    a = jnp.exp2(z[i][h][b] - cs[i][h][b] - runs[h])
                    if masked:
                        a = jnp.where(strict, a, 0.0)
                    vb = v_heads[h][cols[i][b], pair]
                    step = cs[i][h][b][:, 0:1]
                    if gate is not None:
                        vb = jnp.where(gate, vb, jnp.zeros_like(vb))
                        step = jnp.where(gate, step, 0.0)
                    acc = acc + _dot(a.astype(BF16), vb)
                    runs[h] = runs[h] + step
            results.append((tuple(runs), acc))
        return results

    zero_run = jnp.zeros((tk, 1), F32)
    zero_acc = jnp.zeros((tk, LANES), F32)
    left, right = 2 * qi, 2 * qi + 1
    q_top = [(qa_ref[0, :tk, pair], qb_ref[0, :tk, pair]) for pair in pairs]
    q_bot = [(qa_ref[0, tk:, pair], qb_ref[0, tk:, pair]) for pair in pairs]
    fresh = ((zero_run, zero_run), zero_acc)
    state = sweep(
        [(pair, q_top[p], *fresh, [(left, True, None), (jnp.maximum(left - 1, 0), False, qi > 0)])
         for p, pair in enumerate(pairs)]
        + [(pair, q_bot[p], *fresh, [(right, True, None), (left, False, None)])
           for p, pair in enumerate(pairs)])

    def min_run(st):
        low = None
        for runs, _ in st:
            m = jnp.minimum(runs[0], runs[1])
            low = m if low is None else jnp.minimum(low, m)
        return jnp.min(low)

    def alive(carry):
        j, _, low = carry
        return jnp.logical_and(j >= 0, low < SB_DEAD_LOG2)

    def earlier_block(carry):
        j, st, _ = carry
        n = len(pairs)
        st = sweep(
            [(pair, q_top[p], *st[p], [(jnp.maximum(j - 1, 0), False, j > 0)])
             for p, pair in enumerate(pairs)]
            + [(pair, q_bot[p], *st[n + p], [(j, False, None)]) for p, pair in enumerate(pairs)])
        return j - 1, tuple(st), min_run(st)

    _, state, _ = lax.while_loop(alive, earlier_block, (left - 1, tuple(state), min_run(state)))
    n = len(pairs)
    o_ref[0] = jnp.concatenate(
        [jnp.concatenate([state[p][1], state[n + p][1]], axis=0) for p in range(n)],
        axis=1).astype(o_ref.dtype)


def _sb_attention(proj, qa, qb, knt, col0, after):
    bsz, s, _ = proj.shape
    tq = min(SB_QUERY_BLOCK, s)
    tk = tq // 2
    npair = SB_WIDTH // LANES
    tri = (jnp.arange(tk)[:, None] >= jnp.arange(tk)[None, :]).astype(BF16)
    tri2 = jnp.concatenate([tri, tri], axis=0)
    w = SB_PAIRS_PER_STEP * LANES
    v_blk0 = (col0 + 2 * npair) * LANES // w
    assert npair % SB_PAIRS_PER_STEP == 0 and ((col0 + 2 * npair) * LANES) % w == 0
    in_specs = [pl.BlockSpec((1, tq, w), lambda b, h, i: (b, i, h)),
                pl.BlockSpec((1, tq, w), lambda b, h, i: (b, i, h)),
                pl.BlockSpec((1, w, s), lambda b, h, i: (b, h, 0)),
                pl.BlockSpec((1, s, w), lambda b, h, i: (b, 0, v_blk0 + h)),
                pl.BlockSpec((2 * tk, tk), lambda b, h, i: (0, 0))]
    args = [qa, qb, knt, proj, tri2]
    body = _run_after(functools.partial(_sb_kernel, tq=tq), in_specs, args, after)
    return pl.pallas_call(
        body,
        grid=(bsz, npair // SB_PAIRS_PER_STEP, s // tq),
        in_specs=in_specs,
        out_specs=pl.BlockSpec((1, tq, w), lambda b, h, i: (b, i, h)),
        out_shape=jax.ShapeDtypeStruct((bsz, s, SB_WIDTH), BF16),
        scratch_shapes=[pltpu.VMEM((s, w), BF16),
                        pltpu.VMEM((s, w), BF16)],
        compiler_params=_cparams("parallel", "parallel", "arbitrary"),
    )(*args)


HG_CHUNKS_PER_STEP = 8


def _hgrn_kernel(kc_ref, lfh_ref, lfl_ref, i_ref, qs_ref, gs_ref, ng_ref, o_ref, st_ref):
    ts = kc_ref.shape[1]
    c = HG_CHUNK
    dh = HG_HEAD_DIM

    @pl.when(pl.program_id(1) == 0)
    def _():
        st_ref[...] = jnp.zeros_like(st_ref)

    row = lax.broadcasted_iota(jnp.int32, (c, c), 0)
    col = lax.broadcasted_iota(jnp.int32, (c, c), 1)
    causal = col <= row
    lower = causal.astype(BF16)
    width = kc_ref.shape[2]

    def body(it, carry):
        chunks = range(HG_CHUNKS_PER_STEP)
        heads = [slice(h * dh, (h + 1) * dh) for h in range(HG_HEADS)]
        rows = [pl.ds(pl.multiple_of((it * HG_CHUNKS_PER_STEP + u) * c, c), c) for u in chunks]
        cum = [_dot(lower, jnp.concatenate([lfh_ref[0, r, :], lfl_ref[0, r, :]], axis=1))
               for r in rows]
        qe, ke, qg, kd, decay, v, v_t = [], [], [], [], [], [], []
        for u in chunks:
            gc = cum[u][:, :width] + cum[u][:, width:]
            g_mid = gc[c // 2 - 1:c // 2, :]
            g_last = gc[c - 1:c, :]
            q_u = qs_ref[0, rows[u], :].astype(F32) * jnp.exp(gc - g_mid)
            k_u = kc_ref[0, rows[u], :].astype(F32) * jnp.exp(g_mid - gc)
            qg.append((q_u * jnp.exp(g_mid)).astype(BF16))
            kd.append((k_u * jnp.exp(g_last - g_mid)).astype(BF16))
            qe.append(q_u.astype(BF16))
            ke.append(k_u.astype(BF16))
            decay.append(jnp.exp(g_last))
            v.append(i_ref[0, rows[u], :])
            v_t.append(v[u].astype(F32).T.astype(BF16))
        scores = [[_dot_nt(qe[u][:, sl], ke[u][:, sl]) for sl in heads] for u in chunks]
        kv_t = [[_dot(v_t[u][sl, :], kd[u][:, sl]) for sl in heads] for u in chunks]
        states = [[st_ref[h] for h in range(HG_HEADS)]]
        for u in chunks:
            states.append([states[u][h] * decay[u][:, sl] + kv_t[u][h] for h, sl in enumerate(heads)])
        for u in chunks:
            outs = []
            for h, sl in enumerate(heads):
                o = (_dot(jnp.where(causal, scores[u][h], 0.0).astype(BF16), v[u][:, sl])
                     + _dot_nt(qg[u][:, sl], states[u][h].astype(BF16)))
                outs.append(o * lax.rsqrt(jnp.mean(o * o, axis=-1, keepdims=True) + EPS))
            y = jnp.concatenate(outs, axis=1) * ng_ref[...] * gs_ref[0, rows[u], :].astype(F32)
            o_ref[0, rows[u], :] = y.astype(o_ref.dtype)
        for h in range(HG_HEADS):
            st_ref[h] = states[-1][h]
        return carry

    lax.fori_loop(0, ts // (c * HG_CHUNKS_PER_STEP), body, 0)


def _hgrn(proj, kc, lfh, lfl, qs, gs, norm_g, i_col0):
    bsz, s, _ = proj.shape
    ts = min(HG_SEQ_TILE, s)
    i_blk = i_col0 * LANES // HG_WIDTH
    term = pl.BlockSpec((1, ts, HG_WIDTH), lambda b, i: (b, i, 0))
    return pl.pallas_call(
        _hgrn_kernel,
        grid=(bsz, s // ts),
        in_specs=[term, term, term,
                  pl.BlockSpec((1, ts, HG_WIDTH), lambda b, i: (b, i, i_blk)),
                  term, term,
                  pl.BlockSpec((1, HG_WIDTH), lambda b, i: (0, 0))],
        out_specs=pl.BlockSpec((1, ts, HG_WIDTH), lambda b, i: (b, i, 0)),
        out_shape=jax.ShapeDtypeStruct((bsz, s, HG_WIDTH), BF16),
        scratch_shapes=[pltpu.VMEM((HG_HEADS, HG_HEAD_DIM, HG_HEAD_DIM), F32)],
        compiler_params=_cparams("parallel", "arbitrary"),
    )(kc, lfh, lfl, proj, qs, gs, jnp.tile(norm_g.reshape(1, HG_HEAD_DIM), (1, HG_HEADS)))


def _pack_halves(t):
    bits = lax.bitcast_convert_type(t.astype(BF16).astype(F32), jnp.uint32)
    w = t.shape[1] // 2
    return lax.bitcast_convert_type(bits[:, :w] | (bits[:, w:] >> 16), jnp.int32)


def _unpack_halves(p):
    u = lax.bitcast_convert_type(p, jnp.uint32)
    return (lax.bitcast_convert_type(u & jnp.uint32(0xFFFF0000), F32),
            lax.bitcast_convert_type(u << 16, F32))


def _store_planes(ref, rows, t):
    p = _pack_halves(t)
    q = p.shape[1] // 2
    ref[0, rows, :] = p[:, :q]
    ref[1, rows, :] = p[:, q:]


def _load_planes(p0, p1):
    a0, b0 = _unpack_halves(p0)
    a1, b1 = _unpack_halves(p1)
    return jnp.concatenate([a0, a1, b0, b1], axis=1)


def _swiglu_hidden(h, w_gu):
    gu = _dot(h, w_gu)
    hid = w_gu.shape[1] // 2
    return _silu(gu[:, :hid]) * gu[:, hid:]


def _merge_kernel(x_ref, osb_ref, ohg_ref, gsb_ref, ghg_ref, mod_ref, g2_ref, wsb_ref, whg_ref,
                  wout_ref, wrh_ref, wrl_ref, x1_ref, h2p_ref, lg_ref):
    m_sb = _dot(osb_ref[0], wsb_ref[...])
    m_hg = _dot(ohg_ref[0], whg_ref[...])
    merged = (jax.nn.sigmoid(gsb_ref[0].astype(F32)) * m_sb
              + jax.nn.sigmoid(ghg_ref[0].astype(F32)) * m_hg)
    x1 = x_ref[0] + mod_ref[0, 2:3, :] * _dot(merged.astype(BF16), wout_ref[...])
    x1_ref[0] = x1
    h2 = _modulated_norm(x1, g2_ref[...], mod_ref[0, 3:4, :], mod_ref[0, 4:5, :])
    hi, lo = _split_bf16(h2)
    _store_planes(h2p_ref, slice(None), h2)
    lg_ref[...] = _dot_nt(wrh_ref[...], hi) + _dot_nt(wrh_ref[...], lo) + _dot_nt(wrl_ref[...], hi)


def _merge(x, o_sb, o_hg, proj, mod, g2, w_sb, w_hg, w_out, wr_hi, wr_lo, b0, after):
    _, s, d = x.shape
    bsz = mod.shape[0]
    tm = min(ROW_TILE, s)
    ns = s // tm
    full = lambda shape: pl.BlockSpec(shape, lambda b, i: (0,) * len(shape))
    in_specs = [pl.BlockSpec((1, tm, d), lambda b, i: (b0 + b, i, 0)),
                pl.BlockSpec((1, tm, SB_WIDTH), lambda b, i: (b, i, 0)),
                pl.BlockSpec((1, tm, HG_WIDTH), lambda b, i: (b, i, 0)),
                pl.BlockSpec((1, tm, d), lambda b, i: (b, i, 0)),
                pl.BlockSpec((1, tm, d), lambda b, i: (b, i, 1)),
                pl.BlockSpec((1, N_MOD, d), lambda b, i: (b, 0, 0)),
                full((1, d)), full(w_sb.shape), full(w_hg.shape), full(w_out.shape),
                full(wr_hi.shape), full(wr_lo.shape)]
    args = [x, o_sb, o_hg, proj, proj, mod, g2.reshape(1, d), w_sb, w_hg, w_out, wr_hi, wr_lo]
    body = _run_after(_merge_kernel, in_specs, args, after)
    return pl.pallas_call(
        body,
        grid=(bsz, ns),
        in_specs=in_specs,
        out_specs=[pl.BlockSpec((1, tm, d), lambda b, i: (b, i, 0)),
                   pl.BlockSpec((2, tm, d // 4), lambda b, i: (0, b * ns + i, 0)),
                   pl.BlockSpec((N_EXPERTS, tm), lambda b, i: (0, b * ns + i))],
        out_shape=[jax.ShapeDtypeStruct((bsz, s, d), F32),
                   jax.ShapeDtypeStruct((2, bsz * s, d // 4), jnp.int32),
                   jax.ShapeDtypeStruct((N_EXPERTS, bsz * s), F32)],
        compiler_params=_cparams("parallel", "parallel"),
    )(*args)


def _first_argmax(vals, idx, sentinel):
    m = jnp.max(vals, axis=0, keepdims=True)
    first = jnp.min(jnp.where(vals == m, idx, sentinel), axis=0, keepdims=True)
    return m, first


def _route_kernel(lg_ref, bias_ref, gates_ref, rank_ref, cnt_ref, run_ref):
    tn = lg_ref.shape[1]

    @pl.when(pl.program_id(0) == 0)
    def _():
        run_ref[...] = jnp.zeros_like(run_ref)

    neg = -jnp.inf
    scores = jax.nn.sigmoid(lg_ref[...])
    choice = scores + bias_ref[...]

    gidx = lax.broadcasted_iota(jnp.int32, (GROUP_SIZE, tn), 0)
    group_rows = []
    for g in range(N_GROUPS):
        cg = choice[g * GROUP_SIZE:(g + 1) * GROUP_SIZE, :]
        m1, i1 = _first_argmax(cg, gidx, GROUP_SIZE)
        m2 = jnp.max(jnp.where(gidx == i1, neg, cg), axis=0, keepdims=True)
        group_rows.append(m1 + m2)
    work = jnp.concatenate(group_rows, axis=0)
    ggi = lax.broadcasted_iota(jnp.int32, (N_GROUPS, tn), 0)
    gmask = jnp.zeros((N_GROUPS, tn), F32)
    for _ in range(TOPK_GROUPS):
        _, first = _first_argmax(work, ggi, N_GROUPS)
        pick = ggi == first
        gmask = jnp.where(pick, 1.0, gmask)
        work = jnp.where(pick, neg, work)

    masked = jnp.concatenate(
        [jnp.where(gmask[g:g + 1, :] > 0.0, choice[g * GROUP_SIZE:(g + 1) * GROUP_SIZE, :], neg)
         for g in range(N_GROUPS)], axis=0)
    eidx = lax.broadcasted_iota(jnp.int32, (N_EXPERTS, tn), 0)
    sel = jnp.zeros((N_EXPERTS, tn), F32)
    for _ in range(TOP_K):
        _, first = _first_argmax(masked, eidx, N_EXPERTS)
        pick = eidx == first
        sel = jnp.where(pick, 1.0, sel)
        masked = jnp.where(pick, neg, masked)

    chosen = jnp.where(sel > 0.0, scores, 0.0)
    gates_ref[...] = chosen / jnp.sum(chosen, axis=0, keepdims=True) * ROUTED_SCALE

    r = lax.broadcasted_iota(jnp.int32, (tn, tn), 0)
    c = lax.broadcasted_iota(jnp.int32, (tn, tn), 1)
    local = _dot(sel.astype(BF16), (r < c).astype(BF16))
    run = run_ref[:, 0:1]
    rank_ref[...] = jnp.where(sel > 0.0, run + local, -1.0)
    total = run + jnp.sum(sel, axis=1, keepdims=True)
    run_ref[...] = jnp.broadcast_to(total, run_ref.shape)
    cnt_ref[...] = jnp.broadcast_to(total, cnt_ref.shape)


def _route(logits_t, bias):
    e, t = logits_t.shape
    tn = min(ROUTE_TOKENS, t)
    return pl.pallas_call(
        _route_kernel,
        grid=(t // tn,),
        in_specs=[pl.BlockSpec((e, tn), lambda i: (0, i)),
                  pl.BlockSpec((e, 1), lambda i: (0, 0))],
        out_specs=[pl.BlockSpec((e, tn), lambda i: (0, i)),
                   pl.BlockSpec((e, tn), lambda i: (0, i)),
                   pl.BlockSpec((e, LANES), lambda i: (0, 0))],
        out_shape=[jax.ShapeDtypeStruct((e, t), F32),
                   jax.ShapeDtypeStruct((e, t), F32),
                   jax.ShapeDtypeStruct((e, LANES), F32)],
        scratch_shapes=[pltpu.VMEM((e, LANES), F32)],
        compiler_params=_cparams("arbitrary"),
    )(logits_t, bias.reshape(e, 1))


def _slots_kernel(gates_ref, rank_ref, cnt_ref, slot_ref, gate8_ref, blk_ref, *, rows, n_blocks):
    ne, tn = gates_ref.shape
    cnt = cnt_ref[...]
    nblk = jnp.floor((cnt + (rows - 1.0)) * (1.0 / rows))
    er = lax.broadcasted_iota(jnp.int32, (ne, ne), 0)
    ec = lax.broadcasted_iota(jnp.int32, (ne, ne), 1)
    lower = (ec < er).astype(BF16)
    pad_start = _dot(lower, nblk.astype(BF16))[:, 0:1] * rows
    pad_end = pad_start + nblk[:, 0:1] * rows

    rank = rank_ref[...]
    sel = rank >= 0.0
    slot_e = pad_start + rank
    kidx = _dot(lower, sel.astype(BF16))
    gates = gates_ref[...]
    slot_rows, gate_rows = [], []
    for k in range(TOP_K):
        m = jnp.logical_and(sel, kidx == k)
        slot_rows.append(jnp.sum(jnp.where(m, slot_e, 0.0), axis=0, keepdims=True))
        gate_rows.append(jnp.sum(jnp.where(m, gates, 0.0), axis=0, keepdims=True))
    slot_ref[...] = jnp.concatenate(slot_rows, axis=0).astype(jnp.int32)
    gate8_ref[...] = jnp.concatenate(gate_rows, axis=0).T

    nbp = blk_ref.shape[1]
    bstart = lax.broadcasted_iota(jnp.int32, (1, nbp), 1).astype(F32) * rows
    e_of = jnp.sum((pad_end <= bstart).astype(F32), axis=0, keepdims=True)
    e_of = jnp.minimum(e_of, ne - 1.0)
    eidx = lax.broadcasted_iota(jnp.int32, (ne, nbp), 0).astype(F32)
    valid_e = jnp.clip(cnt[:, 0:1] - (bstart - pad_start), 0.0, rows)
    valid = jnp.sum(jnp.where(eidx == e_of, valid_e, 0.0), axis=0, keepdims=True)
    total = pad_end[ne - 1:ne, :]
    used = bstart < total
    bidx = bstart * (1.0 / rows)
    src = jnp.where(used, bidx, total * (1.0 / rows) - 1.0)
    dst = jnp.where(used, bidx, float(n_blocks))
    blk_ref[...] = jnp.concatenate(
        [e_of, valid, src, dst, jnp.zeros((blk_ref.shape[0] - 4, nbp), F32)], axis=0).astype(jnp.int32)


def _slots(gates_t, rank_t, counts, rows, n_blocks):
    nbp = -(-n_blocks // LANES) * LANES
    e, t = gates_t.shape
    assert t // rows <= 256
    tn = min(ROUTE_TOKENS, t)
    return pl.pallas_call(
        functools.partial(_slots_kernel, rows=rows, n_blocks=n_blocks),
        grid=(t // tn,),
        in_specs=[pl.BlockSpec((e, tn), lambda i: (0, i)),
                  pl.BlockSpec((e, tn), lambda i: (0, i)),
                  pl.BlockSpec((e, LANES), lambda i: (0, 0))],
        out_specs=[pl.BlockSpec((TOP_K, tn), lambda i: (0, i)),
                   pl.BlockSpec((tn, TOP_K), lambda i: (i, 0)),
                   pl.BlockSpec((8, nbp), lambda i: (0, 0))],
        out_shape=[jax.ShapeDtypeStruct((TOP_K, t), jnp.int32),
                   jax.ShapeDtypeStruct((t, TOP_K), F32),
                   jax.ShapeDtypeStruct((8, nbp), jnp.int32)],
        compiler_params=_cparams("arbitrary"),
    )(gates_t, rank_t, counts)


SC_WINDOW = 128


def _sc_mesh():
    return plsc.VectorSubcoreMesh(core_axis_name="core", subcore_axis_name="subcore")


def _sc_gather_rows(table, idx):
    n = idx.shape[1]
    w = table.shape[1]

    @pl.kernel(out_type=jax.ShapeDtypeStruct((n, w), table.dtype), mesh=_sc_mesh())
    def gather(t_hbm, i_hbm, o_hbm):
        def body(i_vmem, o_vmem):
            pltpu.sync_copy(t_hbm.at[i_vmem.at[0]], o_vmem)

        pltpu.emit_pipeline(
            body, grid=(n // SC_WINDOW,),
            in_specs=[pl.BlockSpec((1, SC_WINDOW), lambda i: (0, i))],
            out_specs=[pl.BlockSpec((SC_WINDOW, w), lambda i: (i, 0))],
            core_axis_name=("core", "subcore"),
            dimension_semantics=(pltpu.PARALLEL,),
        )(i_hbm, o_hbm)

    return gather(table, idx)


def _sc_scatter_rows(rows, idx, n_out):
    fan, m = idx.shape
    w = rows.shape[1]

    @pl.kernel(out_type=jax.ShapeDtypeStruct((n_out, w), rows.dtype), mesh=_sc_mesh())
    def scatter(r_hbm, i_hbm, o_hbm):
        def body(r_vmem, i_vmem):
            for k in range(fan):
                pltpu.sync_copy(r_vmem, o_hbm.at[i_vmem.at[k]])

        pltpu.emit_pipeline(
            body, grid=(m // SC_WINDOW,),
            in_specs=[pl.BlockSpec((SC_WINDOW, w), lambda i: (i, 0)),
                      pl.BlockSpec((fan, SC_WINDOW), lambda i: (0, i))],
            out_specs=[],
            core_axis_name=("core", "subcore"),
            dimension_semantics=(pltpu.PARALLEL,),
        )(r_hbm, i_hbm)

    return scatter(rows, idx)


def _gmm_kernel(be_ref, bv_ref, bs_ref, bd_ref, xs_ref, wg0, wu0, wd0, wg1, wu1, wd1, ys_ref,
                wgu0_s, wd0_s, wgu1_s, wd1_s):
    b = pl.program_id(0)
    rows = xs_ref.shape[1] // 2
    hid = wg0.shape[2]
    sets = ((wg0, wu0, wd0, wgu0_s, wd0_s), (wg1, wu1, wd1, wgu1_s, wd1_s))

    for half, (wg, wu, wd, wgu_s, wd_s) in enumerate(sets):
        blk = 2 * b + half

        @pl.when(jnp.logical_or(b == 0, be_ref[blk] != be_ref[jnp.maximum(blk - 2, 0)]))
        def _(wg=wg, wu=wu, wd=wd, wgu_s=wgu_s, wd_s=wd_s):
            wgu_s[:, :hid] = wg[0].astype(BF16)
            wgu_s[:, hid:] = wu[0].astype(BF16)
            wd_s[...] = wd[0].astype(BF16)

    any_valid = bv_ref[2 * b] + bv_ref[2 * b + 1] > 0

    @pl.when(any_valid)
    def _():
        for half, (_, _, _, wgu_s, wd_s) in enumerate(sets):
            valid = bv_ref[2 * b + half]
            r = slice(half * rows, (half + 1) * rows)
            x = _load_planes(xs_ref[0, r, :], xs_ref[1, r, :]).astype(BF16)
            y = _dot(_swiglu_hidden(x, wgu_s[...]).astype(BF16), wd_s[...])
            row = lax.broadcasted_iota(jnp.int32, (rows, 1), 0)
            _store_planes(ys_ref, r, jnp.where(row < valid, y, 0.0))

    @pl.when(jnp.logical_not(any_valid))
    def _():
        ys_ref[...] = jnp.zeros_like(ys_ref)


def _gmm(blk, xs, we_gate, we_up, we_down, rows, after):
    _, n_rows, q = xs.shape
    _, d, hid = we_gate.shape
    n_pairs = n_rows // (2 * rows) - 1
    w_spec = lambda w, half: pl.BlockSpec((1,) + w.shape[1:],
                                          lambda b, be, bv, bs, bd: (be[2 * b + half], 0, 0))
    in_specs = [pl.BlockSpec((2, 2 * rows, q), lambda b, be, bv, bs, bd: (0, bs[2 * b] // 2, 0)),
                w_spec(we_gate, 0), w_spec(we_up, 0), w_spec(we_down, 0),
                w_spec(we_gate, 1), w_spec(we_up, 1), w_spec(we_down, 1)]
    args = [xs, we_gate, we_up, we_down, we_gate, we_up, we_down]
    n_prefetch = 4
    body = _gmm_kernel
    if after is not None:
        pos = n_prefetch + len(args)
        in_specs.append(pl.BlockSpec(memory_space=pl.ANY))
        args.append(after)
        body = lambda *refs: _gmm_kernel(*refs[:pos], *refs[pos + 1:])
    return pl.pallas_call(
        body,
        grid_spec=pltpu.PrefetchScalarGridSpec(
            num_scalar_prefetch=n_prefetch,
            grid=(n_pairs,),
            in_specs=in_specs,
            out_specs=pl.BlockSpec((2, 2 * rows, q),
                                   lambda b, be, bv, bs, bd: (0, bd[2 * b] // 2, 0)),
            scratch_shapes=[pltpu.VMEM((d, 2 * hid), BF16), pltpu.VMEM((hid, d), BF16),
                            pltpu.VMEM((d, 2 * hid), BF16), pltpu.VMEM((hid, d), BF16)]),
        out_shape=jax.ShapeDtypeStruct(xs.shape, jnp.int32),
        compiler_params=_cparams("arbitrary"),
    )(blk[0], blk[1], blk[2], blk[3], *args)


def _combine_kernel(x1_ref, mod_ref, y8_ref, g8_ref, h2p_ref, wsgu_ref, wsd_ref, *rest):
    o_ref = rest[-1]
    g8 = g8_ref[...]
    h2 = _load_planes(h2p_ref[0], h2p_ref[1]).astype(BF16)
    ffn = _dot(_swiglu_hidden(h2, wsgu_ref[...]).astype(BF16), wsd_ref[...])
    for k in range(TOP_K):
        ffn = ffn + g8[:, k:k + 1] * _load_planes(y8_ref[k, 0], y8_ref[k, 1])
    o_ref[0] = x1_ref[0] + mod_ref[0, 5:6, :] * ffn


def _combine(x1, mod, y8, gate8, h2p, ws_gu, ws_d, b0, total, earlier, after):
    bsz, s, d = x1.shape
    tm = min(COMBINE_ROWS, s)
    ns = s // tm
    full = lambda shape: pl.BlockSpec(shape, lambda b, i: (0,) * len(shape))
    in_specs = [pl.BlockSpec((1, tm, d), lambda b, i: (b, i, 0)),
                pl.BlockSpec((1, N_MOD, d), lambda b, i: (b, 0, 0)),
                pl.BlockSpec((TOP_K, 2, tm, d // 4), lambda b, i: (0, 0, b * ns + i, 0)),
                pl.BlockSpec((tm, TOP_K), lambda b, i: (b * ns + i, 0)),
                pl.BlockSpec((2, tm, d // 4), lambda b, i: (0, b * ns + i, 0)),
                full(ws_gu.shape), full(ws_d.shape)]
    args = [x1, mod, y8, gate8, h2p, ws_gu, ws_d]
    aliases = {}
    if earlier is not None:
        in_specs.append(pl.BlockSpec(memory_space=pl.ANY))
        args.append(earlier)
        aliases = {len(args) - 1: 0}
    if after is not None:
        in_specs.append(pl.BlockSpec(memory_space=pl.ANY))
        args.append(after)
    return pl.pallas_call(
        _combine_kernel,
        grid=(bsz, ns),
        in_specs=in_specs,
        out_specs=pl.BlockSpec((1, tm, d), lambda b, i: (b0 + b, i, 0)),
        out_shape=jax.ShapeDtypeStruct((total, s, d), F32),
        input_output_aliases=aliases,
        compiler_params=_cparams("parallel", "parallel"),
    )(*args)


def kernel(x, c, w_ada, b_ada, norm1_g, w_in, sb_q_norm_g, sb_k_norm_g, hg_lb_logits, hg_norm_g,
           w_branch_sb, w_branch_hg, w_out, norm2_g, w_router, router_bias, w_e_gate, w_e_up,
           w_e_down, w_s_gate, w_s_up, w_s_down):
    bsz, s, d = x.shape
    depth = w_ada.shape[0]
    n_gate_cols = 2 * d
    qkv_col0 = n_gate_cols // LANES
    hg_col0 = qkv_col0 + 3 * SB_WIDTH // LANES
    for l in range(depth):
        n_mix = 3 * SB_WIDTH + 4 * HG_WIDTH
        w_in_l = jnp.concatenate([w_in[l][:, n_mix:], w_in[l][:, :n_mix]], axis=1).astype(BF16)
        wr_t = w_router[l].T
        wr_hi = wr_t.astype(BF16)
        wr_lo = (wr_t - wr_hi.astype(F32)).astype(BF16)
        ws_gu = jnp.concatenate([w_s_gate[l], w_s_up[l]], axis=1).astype(BF16)

        w_sb, w_hg = w_branch_sb[l].astype(BF16), w_branch_hg[l].astype(BF16)
        w_o, ws_d = w_out[l].astype(BF16), w_s_down[l].astype(BF16)

        mod = _ada(c, w_ada[l], b_ada[l]).reshape(bsz, N_MOD, d)
        n_parts = BATCH_PARTS if bsz % BATCH_PARTS == 0 else 1
        pb = bsz // n_parts
        t = pb * s
        q = d // 4
        n_blocks = -(-(t * TOP_K + N_EXPERTS * (DISPATCH_ROWS - 1)) // DISPATCH_ROWS)
        n_blocks += n_blocks % 2
        n_rows = (n_blocks + 2) * DISPATCH_ROWS
        plane_off = jnp.array([0, n_rows], jnp.int32)[None, :, None]

        def experts(st, after):
            ys = _gmm(st["blk"], st["xs"].reshape(2, n_rows, q), w_e_gate[l], w_e_up[l],
                      w_e_down[l], DISPATCH_ROWS, after)
            y8 = _sc_gather_rows(ys.reshape(2 * n_rows, q), st["row_idx"].reshape(1, TOP_K * 2 * t))
            return ys, y8.reshape(TOP_K, 2, t, q)

        out = None
        prev = None
        for p in range(n_parts):
            b0 = p * pb
            mod_p = mod[b0:b0 + pb]
            proj, qa, qb, knt, kc, lfh, lfl, qs, gs = _inproj(
                x, mod_p, norm1_g[l], w_in_l, sb_q_norm_g[l], sb_k_norm_g[l], hg_lb_logits, l,
                qkv_col0, b0, prev and prev["blk"])
            o_hg = _hgrn(proj, kc, lfh, lfl, qs, gs, hg_norm_g[l], hg_col0 + HG_WIDTH // LANES)
            if prev:
                prev["ys"], prev["y8"] = experts(prev, o_hg)
            o_sb = _sb_attention(proj, qa, qb, knt, qkv_col0, prev and prev["ys"])
            if prev:
                out = _combine(prev["x1"], prev["mod"], prev["y8"], prev["gate8"], prev["h2p"],
                               ws_gu, ws_d, prev["b0"], bsz, out, o_sb)
            x1, h2p, logits_t = _merge(x, o_sb, o_hg, proj, mod_p, norm2_g[l], w_sb, w_hg, w_o,
                                       wr_hi, wr_lo, b0, out)
            gates_t, rank_t, counts = _route(logits_t, router_bias[l])
            slot8, gate8, blk = _slots(gates_t, rank_t, counts, DISPATCH_ROWS, n_blocks)
            row_idx = (slot8[:, None, :] + plane_off).reshape(TOP_K, 2 * t)
            xs = _sc_scatter_rows(h2p.reshape(2 * t, q), row_idx, 2 * n_rows)
            prev = dict(blk=blk, xs=xs, row_idx=row_idx, x1=x1, h2p=h2p, mod=mod_p, gate8=gate8,
                        b0=b0)
        _, y8 = experts(prev, None)
        x = _combine(prev["x1"], prev["mod"], y8, prev["gate8"], prev["h2p"], ws_gu, ws_d,
                     prev["b0"], bsz, out, None)
    return x
```

```python
import functools

import jax
import jax.numpy as jnp
from jax import lax
from jax.experimental import pallas as pl
from jax.experimental.pallas import tpu as pltpu
from jax.experimental.pallas import tpu_sc as plsc

F32 = jnp.float32
BF16 = jnp.bfloat16

SB_HEADS = 8
SB_HEAD_DIM = 64
SB_WIDTH = SB_HEADS * SB_HEAD_DIM
HG_HEADS = 4
HG_HEAD_DIM = 128
HG_WIDTH = HG_HEADS * HG_HEAD_DIM
HG_CHUNK = 64
N_EXPERTS = 64
TOP_K = 8
N_GROUPS = 8
TOPK_GROUPS = 4
GROUP_SIZE = N_EXPERTS // N_GROUPS
ROUTED_SCALE = 2.5
DISPATCH_ROWS = 1024
BATCH_PARTS = 2
N_MOD = 6
EPS = 1e-6
LOG2_E = 1.4426950408889634
SB_DEAD_LOG2 = 160.0

LANES = 128
VMEM_LIMIT = 56 * 1024 * 1024

ADA_COLS = 1024
ROW_TILE = 512
MERGE_ROWS = 1024
SB_QUERY_BLOCK = 512
SB_PAIRS_PER_STEP = 4
HG_SEQ_TILE = 1024
ROUTE_TOKENS = 1024
COMBINE_ROWS = 512


def _cparams(*sem):
    return pltpu.CompilerParams(dimension_semantics=sem, vmem_limit_bytes=VMEM_LIMIT)


def _silu(t):
    return t * jax.nn.sigmoid(t)


def _dot(a, b):
    return jnp.dot(a, b, preferred_element_type=F32)


def _run_after(body, in_specs, args, after):
    if after is None:
        return body
    pos = len(args)
    in_specs.append(pl.BlockSpec(memory_space=pl.ANY))
    args.append(after)
    return lambda *refs: body(*refs[:pos], *refs[pos + 1:])


def _dot_nt(a, b):
    return lax.dot_general(a, b, (((1,), (1,)), ((), ())), preferred_element_type=F32)


def _split_bf16(t):
    hi = t.astype(BF16)
    lo = (t - hi.astype(F32)).astype(BF16)
    return hi, lo


def _ada_kernel(c_ref, w_ref, b_ref, o_ref):
    cond = _silu(c_ref[...])
    o_ref[...] = _dot(cond, w_ref[...]) + b_ref[...]


def _ada(c, w, b):
    bsz, d = c.shape
    n = w.shape[1]
    tn = min(ADA_COLS, n)
    return pl.pallas_call(
        _ada_kernel,
        grid=(n // tn,),
        in_specs=[pl.BlockSpec((bsz, d), lambda j: (0, 0)),
                  pl.BlockSpec((d, tn), lambda j: (0, j)),
                  pl.BlockSpec((1, tn), lambda j: (0, j))],
        out_specs=pl.BlockSpec((bsz, tn), lambda j: (0, j)),
        out_shape=jax.ShapeDtypeStruct((bsz, n), F32),
        compiler_params=_cparams("parallel"),
    )(c, w, b.reshape(1, n))


def _modulated_norm(x, g, shift, scale):
    y = x * lax.rsqrt(jnp.mean(x * x, axis=-1, keepdims=True) + EPS) * g
    return y * (1.0 + scale) + shift


INPROJ_COLS = 512


def _inproj_kernel(x_ref, mod_ref, g_ref, w_ref, qg_ref, kg_ref, lbl_ref, o_ref, qa_ref, qb_ref,
                   knt_ref, kc_ref, lfh_ref, lfl_ref, qs_ref, gs_ref, *, q_chunk, layer):
    h = _modulated_norm(x_ref[0], g_ref[...], mod_ref[0, 0:1, :], mod_ref[0, 1:2, :]).astype(BF16)
    lo_half = lax.broadcasted_iota(jnp.int32, (1, LANES), 1) < SB_HEAD_DIM
    for j in range(w_ref.shape[1] // INPROJ_COLS):
        cols = slice(j * INPROJ_COLS, (j + 1) * INPROJ_COLS)
        res = _dot(h, w_ref[:, cols])
        o_ref[0, :, cols] = res.astype(o_ref.dtype)
        if j == q_chunk:
            for p in range(INPROJ_COLS // LANES):
                pair = slice(p * LANES, (p + 1) * LANES)
                qn = _pair_norm(res[:, pair], qg_ref[...], lo_half) * (SB_HEAD_DIM ** -0.5 * LOG2_E)
                qa_ref[0, :, pair] = jnp.where(lo_half, qn, 0.0).astype(BF16)
                qb_ref[0, :, pair] = jnp.where(lo_half, 0.0, qn).astype(BF16)
        if j == q_chunk + 1:
            for p in range(INPROJ_COLS // LANES):
                pair = slice(p * LANES, (p + 1) * LANES)
                knt_ref[0, pair, :] = _pair_norm(res[:, pair], kg_ref[...], lo_half).T.astype(BF16)
        if j == q_chunk + 3:
            lg = lbl_ref[...]
            e = jnp.exp(lg - jnp.max(lg, axis=0, keepdims=True))
            lb = jnp.sum(e[:layer + 1], axis=0, keepdims=True) / jnp.sum(e, axis=0, keepdims=True)
            forget = lb + (1.0 - lb) * jax.nn.sigmoid(res)
            kc_ref[0] = (1.0 - forget).astype(BF16)
            lfh_ref[0], lfl_ref[0] = _split_trunc(jnp.log(forget))
        if j == q_chunk + 5:
            qs_ref[0] = _silu(res).astype(BF16)
        if j == q_chunk + 6:
            gs_ref[0] = _silu(res).astype(BF16)


def _inproj(x, mod, g, w, qg, kg, lb_logits, layer, q_col0, b0, after):
    _, s, d = x.shape
    bsz = mod.shape[0]
    n = w.shape[1]
    nl = lb_logits.shape[0]
    tm = min(ROW_TILE, s)
    assert INPROJ_COLS == SB_WIDTH == HG_WIDTH and (q_col0 * LANES) % INPROJ_COLS == 0
    pair_gain = lambda t: jnp.tile(t.reshape(1, SB_HEAD_DIM), (1, 2))
    in_specs = [pl.BlockSpec((1, tm, d), lambda b, i: (b0 + b, i, 0)),
                pl.BlockSpec((1, N_MOD, d), lambda b, i: (b, 0, 0)),
                pl.BlockSpec((1, d), lambda b, i: (0, 0)),
                pl.BlockSpec((d, n), lambda b, i: (0, 0)),
                pl.BlockSpec((1, LANES), lambda b, i: (0, 0)),
                pl.BlockSpec((1, LANES), lambda b, i: (0, 0)),
                pl.BlockSpec((nl, HG_WIDTH), lambda b, i: (0, 0))]
    args = [x, mod, g.reshape(1, d), w, pair_gain(qg), pair_gain(kg), lb_logits]
    body = _run_after(functools.partial(_inproj_kernel, q_chunk=q_col0 * LANES // INPROJ_COLS,
                                        layer=layer), in_specs, args, after)
    rows_spec = pl.BlockSpec((1, tm, INPROJ_COLS), lambda b, i: (b, i, 0))
    rows_shape = jax.ShapeDtypeStruct((bsz, s, INPROJ_COLS), BF16)
    return pl.pallas_call(
        body,
        grid=(bsz, s // tm),
        in_specs=in_specs,
        out_specs=[pl.BlockSpec((1, tm, n), lambda b, i: (b, i, 0)),
                   rows_spec, rows_spec,
                   pl.BlockSpec((1, SB_WIDTH, tm), lambda b, i: (b, 0, i)),
                   rows_spec, rows_spec, rows_spec, rows_spec, rows_spec],
        out_shape=[jax.ShapeDtypeStruct((bsz, s, n), BF16),
                   rows_shape, rows_shape,
                   jax.ShapeDtypeStruct((bsz, SB_WIDTH, s), BF16),
                   rows_shape, rows_shape, rows_shape, rows_shape, rows_shape],
        compiler_params=_cparams("parallel", "parallel"),
    )(*args)


def _pair_norm(t, g, lo_half):
    sq = t * t
    s_lo = jnp.sum(jnp.where(lo_half, sq, 0.0), axis=-1, keepdims=True)
    s_hi = jnp.sum(jnp.where(lo_half, 0.0, sq), axis=-1, keepdims=True)
    ms = jnp.where(lo_half, s_lo, s_hi) * (1.0 / SB_HEAD_DIM)
    return t * lax.rsqrt(ms + EPS) * g


def _neg_abs(t):
    bits = lax.bitcast_convert_type(t, jnp.uint32) | jnp.uint32(0x80000000)
    return lax.bitcast_convert_type(bits, F32)


def _split_trunc(t):
    bits = lax.bitcast_convert_type(t, jnp.uint32) & jnp.uint32(0xFFFF0000)
    hi = lax.bitcast_convert_type(bits, F32)
    return hi.astype(BF16), (t - hi).astype(BF16)


def _sb_kernel(qa_ref, qb_ref, knt_ref, v_ref, tri2_ref, o_ref, va_ref, vb_ref, *, tq):
    qi = pl.program_id(2)
    s = v_ref.shape[1]
    width = v_ref.shape[2]
    lane = lax.broadcasted_iota(jnp.int32, (1, width), 1)
    lo_half = (lane % LANES) < SB_HEAD_DIM
    pairs = [slice(p * LANES, (p + 1) * LANES) for p in range(width // LANES)]

    tk = tq // 2

    @pl.when(qi == 0)
    def _():
        def prep_block(j, c):
            rows = pl.ds(pl.multiple_of(j * tk, tk), tk)
            vb = v_ref[0, rows, :]
            va_ref[rows, :] = jnp.where(lo_half, vb, jnp.zeros_like(vb))
            vb_ref[rows, :] = jnp.where(lo_half, jnp.zeros_like(vb), vb)
            return c
        lax.fori_loop(0, s // tk, prep_block, 0)

    v_heads = (va_ref, vb_ref)

    row = lax.broadcasted_iota(jnp.int32, (tk, tk), 0)
    col = lax.broadcasted_iota(jnp.int32, (tk, tk), 1)
    strict = col < row
    tri2 = tri2_ref[...]

    def sweep(streams):
        cols = [[pl.ds(pl.multiple_of(j * tk, tk), tk) for j, _, _ in blocks]
                for _, _, _, _, blocks in streams]
        z = [[[_dot(qh[h], knt_ref[0, pair, c]) for c in cols[i]] for h in range(2)]
             for i, (pair, qh, _, _, _) in enumerate(streams)]
        cs = []
        for i, (_, _, _, _, blocks) in enumerate(streams):
            cs.append([[None] * len(blocks) for _ in range(2)])
            for h in range(2):
                for b, (_, masked, _) in enumerate(blocks):
                    zb = z[i][h][b]
                    sp = jnp.maximum(zb, 0.0) + jnp.log2(1.0 + jnp.exp2(_neg_abs(zb)))
                    if masked:
                        sp = jnp.where(strict, sp, 0.0)
                    cs[i][h][b] = _dot(jnp.concatenate(_split_trunc(sp), axis=1), tri2)
        results = []
        for i, (pair, _, runs, acc, blocks) in enumerate(streams):
            runs = list(runs)
            for h in range(2):
                for b, (_, masked, gate) in enumerate(blocks):
                    a = jnp.exp2(z[i][h][b] - cs[i][h][b] - runs[h])
                    if masked:
                        a = jnp.where(strict, a, 0.0)
                    vb = v_heads[h][cols[i][b], pair]
                    step = cs[i][h][b][:, 0:1]
                    if gate is not None:
                        vb = jnp.where(gate, vb, jnp.zeros_like(vb))
                        step = jnp.where(gate, step, 0.0)
                    acc = acc + _dot(a.astype(BF16), vb)
                    runs[h] = runs[h] + step
            results.append((tuple(runs), acc))
        return results

    zero_run = jnp.zeros((tk, 1), F32)
    zero_acc = jnp.zeros((tk, LANES), F32)
    left, right = 2 * qi, 2 * qi + 1
    q_top = [(qa_ref[0, :tk, pair], qb_ref[0, :tk, pair]) for pair in pairs]
    q_bot = [(qa_ref[0, tk:, pair], qb_ref[0, tk:, pair]) for pair in pairs]
    fresh = ((zero_run, zero_run), zero_acc)
    state = sweep(
        [(pair, q_top[p], *fresh, [(left, True, None), (jnp.maximum(left - 1, 0), False, qi > 0)])
         for p, pair in enumerate(pairs)]
        + [(pair, q_bot[p], *fresh, [(right, True, None), (left, False, None)])
           for p, pair in enumerate(pairs)])

    def min_run(st):
        low = None
        for runs, _ in st:
            m = jnp.minimum(runs[0], runs[1])
            low = m if low is None else jnp.minimum(low, m)
        return jnp.min(low)

    def alive(carry):
        j, _, low = carry
        return jnp.logical_and(j >= 0, low < SB_DEAD_LOG2)

    def earlier_block(carry):
        j, st, _ = carry
        n = len(pairs)
        st = sweep(
            [(pair, q_top[p], *st[p], [(jnp.maximum(j - 1, 0), False, j > 0)])
             for p, pair in enumerate(pairs)]
            + [(pair, q_bot[p], *st[n + p], [(j, False, None)]) for p, pair in enumerate(pairs)])
        return j - 1, tuple(st), min_run(st)

    _, state, _ = lax.while_loop(alive, earlier_block, (left - 1, tuple(state), min_run(state)))
    n = len(pairs)
    o_ref[0] = jnp.concatenate(
        [jnp.concatenate([state[p][1], state[n + p][1]], axis=0) for p in range(n)],
        axis=1).astype(o_ref.dtype)


def _sb_attention(proj, qa, qb, knt, col0, after):
    bsz, s, _ = proj.shape
    tq = min(SB_QUERY_BLOCK, s)
    tk = tq // 2
    npair = SB_WIDTH // LANES
    tri = (jnp.arange(tk)[:, None] >= jnp.arange(tk)[None, :]).astype(BF16)
    tri2 = jnp.concatenate([tri, tri], axis=0)
    w = SB_PAIRS_PER_STEP * LANES
    v_blk0 = (col0 + 2 * npair) * LANES // w
    assert npair % SB_PAIRS_PER_STEP == 0 and ((col0 + 2 * npair) * LANES) % w == 0
    in_specs = [pl.BlockSpec((1, tq, w), lambda b, h, i: (b, i, h)),
                pl.BlockSpec((1, tq, w), lambda b, h, i: (b, i, h)),
                pl.BlockSpec((1, w, s), lambda b, h, i: (b, h, 0)),
                pl.BlockSpec((1, s, w), lambda b, h, i: (b, 0, v_blk0 + h)),
                pl.BlockSpec((2 * tk, tk), lambda b, h, i: (0, 0))]
    args = [qa, qb, knt, proj, tri2]
    body = _run_after(functools.partial(_sb_kernel, tq=tq), in_specs, args, after)
    return pl.pallas_call(
        body,
        grid=(bsz, npair // SB_PAIRS_PER_STEP, s // tq),
        in_specs=in_specs,
        out_specs=pl.BlockSpec((1, tq, w), lambda b, h, i: (b, i, h)),
        out_shape=jax.ShapeDtypeStruct((bsz, s, SB_WIDTH), BF16),
        scratch_shapes=[pltpu.VMEM((s, w), BF16),
                        pltpu.VMEM((s, w), BF16)],
        compiler_params=_cparams("parallel", "parallel", "arbitrary"),
    )(*args)


HG_CHUNKS_PER_STEP = 8


def _hgrn_kernel(kc_ref, lfh_ref, lfl_ref, i_ref, qs_ref, gs_ref, ng_ref, o_ref, st_ref):
    ts = kc_ref.shape[1]
    c = HG_CHUNK
    dh = HG_HEAD_DIM

    @pl.when(pl.program_id(1) == 0)
    def _():
        st_ref[...] = jnp.zeros_like(st_ref)

    row = lax.broadcasted_iota(jnp.int32, (c, c), 0)
    col = lax.broadcasted_iota(jnp.int32, (c, c), 1)
    causal = col <= row
    lower = causal.astype(BF16)
    width = kc_ref.shape[2]

    def body(it, carry):
        chunks = range(HG_CHUNKS_PER_STEP)
        heads = [slice(h * dh, (h + 1) * dh) for h in range(HG_HEADS)]
        rows = [pl.ds(pl.multiple_of((it * HG_CHUNKS_PER_STEP + u) * c, c), c) for u in chunks]
        cum = [_dot(lower, jnp.concatenate([lfh_ref[0, r, :], lfl_ref[0, r, :]], axis=1))
               for r in rows]
        qe, ke, qg, kd, decay, v, v_t = [], [], [], [], [], [], []
        for u in chunks:
            gc = cum[u][:, :width] + cum[u][:, width:]
            g_mid = gc[c // 2 - 1:c // 2, :]
            g_last = gc[c - 1:c, :]
            q_u = qs_ref[0, rows[u], :].astype(F32) * jnp.exp(gc - g_mid)
            k_u = kc_ref[0, rows[u], :].astype(F32) * jnp.exp(g_mid - gc)
            qg.append((q_u * jnp.exp(g_mid)).astype(BF16))
            kd.append((k_u * jnp.exp(g_last - g_mid)).astype(BF16))
            qe.append(q_u.astype(BF16))
            ke.append(k_u.astype(BF16))
            decay.append(jnp.exp(g_last))
            v.append(i_ref[0, rows[u], :])
            v_t.append(v[u].astype(F32).T.astype(BF16))
        scores = [[_dot_nt(qe[u][:, sl], ke[u][:, sl]) for sl in heads] for u in chunks]
        kv_t = [[_dot(v_t[u][sl, :], kd[u][:, sl]) for sl in heads] for u in chunks]
        states = [[st_ref[h] for h in range(HG_HEADS)]]
        for u in chunks:
            states.append([states[u][h] * decay[u][:, sl] + kv_t[u][h] for h, sl in enumerate(heads)])
        for u in chunks:
            outs = []
            for h, sl in enumerate(heads):
                o = (_dot(jnp.where(causal, scores[u][h], 0.0).astype(BF16), v[u][:, sl])
                     + _dot_nt(qg[u][:, sl], states[u][h].astype(BF16)))
                outs.append(o * lax.rsqrt(jnp.mean(o * o, axis=-1, keepdims=True) + EPS))
            y = jnp.concatenate(outs, axis=1) * ng_ref[...] * gs_ref[0, rows[u], :].astype(F32)
            o_ref[0, rows[u], :] = y.astype(o_ref.dtype)
        for h in range(HG_HEADS):
            st_ref[h] = states[-1][h]
        return carry

    lax.fori_loop(0, ts // (c * HG_CHUNKS_PER_STEP), body, 0)


def _hgrn(proj, kc, lfh, lfl, qs, gs, norm_g, i_col0):
    bsz, s, _ = proj.shape
    ts = min(HG_SEQ_TILE, s)
    i_blk = i_col0 * LANES // HG_WIDTH
    term = pl.BlockSpec((1, ts, HG_WIDTH), lambda b, i: (b, i, 0))
    return pl.pallas_call(
        _hgrn_kernel,
        grid=(bsz, s // ts),
        in_specs=[term, term, term,
                  pl.BlockSpec((1, ts, HG_WIDTH), lambda b, i: (b, i, i_blk)),
                  term, term,
                  pl.BlockSpec((1, HG_WIDTH), lambda b, i: (0, 0))],
        out_specs=pl.BlockSpec((1, ts, HG_WIDTH), lambda b, i: (b, i, 0)),
        out_shape=jax.ShapeDtypeStruct((bsz, s, HG_WIDTH), BF16),
        scratch_shapes=[pltpu.VMEM((HG_HEADS, HG_HEAD_DIM, HG_HEAD_DIM), F32)],
        compiler_params=_cparams("parallel", "arbitrary"),
    )(kc, lfh, lfl, proj, qs, gs, jnp.tile(norm_g.reshape(1, HG_HEAD_DIM), (1, HG_HEADS)))


def _pack_halves(t):
    bits = lax.bitcast_convert_type(t.astype(BF16).astype(F32), jnp.uint32)
    w = t.shape[1] // 2
    return lax.bitcast_convert_type(bits[:, :w] | (bits[:, w:] >> 16), jnp.int32)


def _unpack_halves(p):
    u = lax.bitcast_convert_type(p, jnp.uint32)
    return (lax.bitcast_convert_type(u & jnp.uint32(0xFFFF0000), F32),
            lax.bitcast_convert_type(u << 16, F32))


def _store_planes(ref, rows, t):
    p = _pack_halves(t)
    q = p.shape[1] // 2
    ref[0, rows, :] = p[:, :q]
    ref[1, rows, :] = p[:, q:]


def _load_planes(p0, p1):
    a0, b0 = _unpack_halves(p0)
    a1, b1 = _unpack_halves(p1)
    return jnp.concatenate([a0, a1, b0, b1], axis=1)


def _swiglu_hidden(h, w_gu):
    gu = _dot(h, w_gu)
    hid = w_gu.shape[1] // 2
    return _silu(gu[:, :hid]) * gu[:, hid:]


def _merge_kernel(x_ref, osb_ref, ohg_ref, gsb_ref, ghg_ref, mod_ref, g2_ref, wsb_ref, whg_ref,
                  wout_ref, wrh_ref, wrl_ref, x1_ref, h2p_ref, lg_ref):
    m_sb = _dot(osb_ref[0], wsb_ref[...])
    m_hg = _dot(ohg_ref[0], whg_ref[...])
    merged = (jax.nn.sigmoid(gsb_ref[0].astype(F32)) * m_sb
              + jax.nn.sigmoid(ghg_ref[0].astype(F32)) * m_hg)
    x1 = x_ref[0] + mod_ref[0, 2:3, :] * _dot(merged.astype(BF16), wout_ref[...])
    x1_ref[0] = x1
    h2 = _modulated_norm(x1, g2_ref[...], mod_ref[0, 3:4, :], mod_ref[0, 4:5, :])
    hi, lo = _split_bf16(h2)
    _store_planes(h2p_ref, slice(None), h2)
    lg_ref[...] = _dot_nt(wrh_ref[...], hi) + _dot_nt(wrh_ref[...], lo) + _dot_nt(wrl_ref[...], hi)


def _merge(x, o_sb, o_hg, proj, mod, g2, w_sb, w_hg, w_out, wr_hi, wr_lo, b0, after):
    _, s, d = x.shape
    bsz = mod.shape[0]
    tm = min(MERGE_ROWS, s)
    ns = s // tm
    full = lambda shape: pl.BlockSpec(shape, lambda b, i: (0,) * len(shape))
    in_specs = [pl.BlockSpec((1, tm, d), lambda b, i: (b0 + b, i, 0)),
                pl.BlockSpec((1, tm, SB_WIDTH), lambda b, i: (b, i, 0)),
                pl.BlockSpec((1, tm, HG_WIDTH), lambda b, i: (b, i, 0)),
                pl.BlockSpec((1, tm, d), lambda b, i: (b, i, 0)),
                pl.BlockSpec((1, tm, d), lambda b, i: (b, i, 1)),
                pl.BlockSpec((1, N_MOD, d), lambda b, i: (b, 0, 0)),
                full((1, d)), full(w_sb.shape), full(w_hg.shape), full(w_out.shape),
                full(wr_hi.shape), full(wr_lo.shape)]
    args = [x, o_sb, o_hg, proj, proj, mod, g2.reshape(1, d), w_sb, w_hg, w_out, wr_hi, wr_lo]
    body = _run_after(_merge_kernel, in_specs, args, after)
    return pl.pallas_call(
        body,
        grid=(bsz, ns),
        in_specs=in_specs,
        out_specs=[pl.BlockSpec((1, tm, d), lambda b, i: (b, i, 0)),
                   pl.BlockSpec((2, tm, d // 4), lambda b, i: (0, b * ns + i, 0)),
                   pl.BlockSpec((N_EXPERTS, tm), lambda b, i: (0, b * ns + i))],
        out_shape=[jax.ShapeDtypeStruct((bsz, s, d), F32),
                   jax.ShapeDtypeStruct((2, bsz * s, d // 4), jnp.int32),
                   jax.ShapeDtypeStruct((N_EXPERTS, bsz * s), F32)],
        compiler_params=_cparams("parallel", "parallel"),
    )(*args)


def _first_argmax(vals, idx, sentinel):
    m = jnp.max(vals, axis=0, keepdims=True)
    first = jnp.min(jnp.where(vals == m, idx, sentinel), axis=0, keepdims=True)
    return m, first


def _route_kernel(lg_ref, bias_ref, gates_ref, rank_ref, cnt_ref, run_ref):
    tn = lg_ref.shape[1]

    @pl.when(pl.program_id(0) == 0)
    def _():
        run_ref[...] = jnp.zeros_like(run_ref)

    neg = -jnp.inf
    scores = jax.nn.sigmoid(lg_ref[...])
    choice = scores + bias_ref[...]

    gidx = lax.broadcasted_iota(jnp.int32, (GROUP_SIZE, tn), 0)
    group_rows = []
    for g in range(N_GROUPS):
        cg = choice[g * GROUP_SIZE:(g + 1) * GROUP_SIZE, :]
        m1, i1 = _first_argmax(cg, gidx, GROUP_SIZE)
        m2 = jnp.max(jnp.where(gidx == i1, neg, cg), axis=0, keepdims=True)
        group_rows.append(m1 + m2)
    work = jnp.concatenate(group_rows, axis=0)
    ggi = lax.broadcasted_iota(jnp.int32, (N_GROUPS, tn), 0)
    gmask = jnp.zeros((N_GROUPS, tn), F32)
    for _ in range(TOPK_GROUPS):
        _, first = _first_argmax(work, ggi, N_GROUPS)
        pick = ggi == first
        gmask = jnp.where(pick, 1.0, gmask)
        work = jnp.where(pick, neg, work)

    masked = jnp.concatenate(
        [jnp.where(gmask[g:g + 1, :] > 0.0, choice[g * GROUP_SIZE:(g + 1) * GROUP_SIZE, :], neg)
         for g in range(N_GROUPS)], axis=0)
    eidx = lax.broadcasted_iota(jnp.int32, (N_EXPERTS, tn), 0)
    sel = jnp.zeros((N_EXPERTS, tn), F32)
    for _ in range(TOP_K):
        _, first = _first_argmax(masked, eidx, N_EXPERTS)
        pick = eidx == first
        sel = jnp.where(pick, 1.0, sel)
        masked = jnp.where(pick, neg, masked)

    chosen = jnp.where(sel > 0.0, scores, 0.0)
    gates_ref[...] = chosen / jnp.sum(chosen, axis=0, keepdims=True) * ROUTED_SCALE

    r = lax.broadcasted_iota(jnp.int32, (tn, tn), 0)
    c = lax.broadcasted_iota(jnp.int32, (tn, tn), 1)
    local = _dot(sel.astype(BF16), (r < c).astype(BF16))
    run = run_ref[:, 0:1]
    rank_ref[...] = jnp.where(sel > 0.0, run + local, -1.0)
    total = run + jnp.sum(sel, axis=1, keepdims=True)
    run_ref[...] = jnp.broadcast_to(total, run_ref.shape)
    cnt_ref[...] = jnp.broadcast_to(total, cnt_ref.shape)


def _route(logits_t, bias):
    e, t = logits_t.shape
    tn = min(ROUTE_TOKENS, t)
    return pl.pallas_call(
        _route_kernel,
        grid=(t // tn,),
        in_specs=[pl.BlockSpec((e, tn), lambda i: (0, i)),
                  pl.BlockSpec((e, 1), lambda i: (0, 0))],
        out_specs=[pl.BlockSpec((e, tn), lambda i: (0, i)),
                   pl.BlockSpec((e, tn), lambda i: (0, i)),
                   pl.BlockSpec((e, LANES), lambda i: (0, 0))],
        out_shape=[jax.ShapeDtypeStruct((e, t), F32),
                   jax.ShapeDtypeStruct((e, t), F32),
                   jax.ShapeDtypeStruct((e, LANES), F32)],
        scratch_shapes=[pltpu.VMEM((e, LANES), F32)],
        compiler_params=_cparams("arbitrary"),
    )(logits_t, bias.reshape(e, 1))


def _slots_kernel(gates_ref, rank_ref, cnt_ref, slot_ref, gate8_ref, blk_ref, *, rows, n_blocks):
    ne, tn = gates_ref.shape
    cnt = cnt_ref[...]
    nblk = jnp.floor((cnt + (rows - 1.0)) * (1.0 / rows))
    er = lax.broadcasted_iota(jnp.int32, (ne, ne), 0)
    ec = lax.broadcasted_iota(jnp.int32, (ne, ne), 1)
    lower = (ec < er).astype(BF16)
    pad_start = _dot(lower, nblk.astype(BF16))[:, 0:1] * rows
    pad_end = pad_start + nblk[:, 0:1] * rows

    rank = rank_ref[...]
    sel = rank >= 0.0
    slot_e = pad_start + rank
    kidx = _dot(lower, sel.astype(BF16))
    gates = gates_ref[...]
    slot_rows, gate_rows = [], []
    for k in range(TOP_K):
        m = jnp.logical_and(sel, kidx == k)
        slot_rows.append(jnp.sum(jnp.where(m, slot_e, 0.0), axis=0, keepdims=True))
        gate_rows.append(jnp.sum(jnp.where(m, gates, 0.0), axis=0, keepdims=True))
    slot_ref[...] = jnp.concatenate(slot_rows, axis=0).astype(jnp.int32)
    gate8_ref[...] = jnp.concatenate(gate_rows, axis=0).T

    nbp = blk_ref.shape[1]
    bstart = lax.broadcasted_iota(jnp.int32, (1, nbp), 1).astype(F32) * rows
    e_of = jnp.sum((pad_end <= bstart).astype(F32), axis=0, keepdims=True)
    e_of = jnp.minimum(e_of, ne - 1.0)
    eidx = lax.broadcasted_iota(jnp.int32, (ne, nbp), 0).astype(F32)
    valid_e = jnp.clip(cnt[:, 0:1] - (bstart - pad_start), 0.0, rows)
    valid = jnp.sum(jnp.where(eidx == e_of, valid_e, 0.0), axis=0, keepdims=True)
    total = pad_end[ne - 1:ne, :]
    used = bstart < total
    bidx = bstart * (1.0 / rows)
    src = jnp.where(used, bidx, total * (1.0 / rows) - 1.0)
    dst = jnp.where(used, bidx, float(n_blocks))
    blk_ref[...] = jnp.concatenate(
        [e_of, valid, src, dst, jnp.zeros((blk_ref.shape[0] - 4, nbp), F32)], axis=0).astype(jnp.int32)


def _slots(gates_t, rank_t, counts, rows, n_blocks):
    nbp = -(-n_blocks // LANES) * LANES
    e, t = gates_t.shape
    assert t // rows <= 256
    tn = min(ROUTE_TOKENS, t)
    return pl.pallas_call(
        functools.partial(_slots_kernel, rows=rows, n_blocks=n_blocks),
        grid=(t // tn,),
        in_specs=[pl.BlockSpec((e, tn), lambda i: (0, i)),
                  pl.BlockSpec((e, tn), lambda i: (0, i)),
                  pl.BlockSpec((e, LANES), lambda i: (0, 0))],
        out_specs=[pl.BlockSpec((TOP_K, tn), lambda i: (0, i)),
                   pl.BlockSpec((tn, TOP_K), lambda i: (i, 0)),
                   pl.BlockSpec((8, nbp), lambda i: (0, 0))],
        out_shape=[jax.ShapeDtypeStruct((TOP_K, t), jnp.int32),
                   jax.ShapeDtypeStruct((t, TOP_K), F32),
                   jax.ShapeDtypeStruct((8, nbp), jnp.int32)],
        compiler_params=_cparams("arbitrary"),
    )(gates_t, rank_t, counts)


SC_WINDOW = 128


def _sc_mesh():
    return plsc.VectorSubcoreMesh(core_axis_name="core", subcore_axis_name="subcore")


def _sc_gather_rows(table, idx):
    n = idx.shape[1]
    w = table.shape[1]

    @pl.kernel(out_type=jax.ShapeDtypeStruct((n, w), table.dtype), mesh=_sc_mesh())
    def gather(t_hbm, i_hbm, o_hbm):
        def body(i_vmem, o_vmem):
            pltpu.sync_copy(t_hbm.at[i_vmem.at[0]], o_vmem)

        pltpu.emit_pipeline(
            body, grid=(n // SC_WINDOW,),
            in_specs=[pl.BlockSpec((1, SC_WINDOW), lambda i: (0, i))],
            out_specs=[pl.BlockSpec((SC_WINDOW, w), lambda i: (i, 0))],
            core_axis_name=("core", "subcore"),
            dimension_semantics=(pltpu.PARALLEL,),
        )(i_hbm, o_hbm)

    return gather(table, idx)


def _sc_scatter_rows(rows, idx, n_out):
    fan, m = idx.shape
    w = rows.shape[1]

    @pl.kernel(out_type=jax.ShapeDtypeStruct((n_out, w), rows.dtype), mesh=_sc_mesh())
    def scatter(r_hbm, i_hbm, o_hbm):
        def body(r_vmem, i_vmem):
            for k in range(fan):
                pltpu.sync_copy(r_vmem, o_hbm.at[i_vmem.at[k]])

        pltpu.emit_pipeline(
            body, grid=(m // SC_WINDOW,),
            in_specs=[pl.BlockSpec((SC_WINDOW, w), lambda i: (i, 0)),
                      pl.BlockSpec((fan, SC_WINDOW), lambda i: (0, i))],
            out_specs=[],
            core_axis_name=("core", "subcore"),
            dimension_semantics=(pltpu.PARALLEL,),
        )(r_hbm, i_hbm)

    return scatter(rows, idx)


def _gmm_kernel(be_ref, bv_ref, bs_ref, bd_ref, xs_ref, wg0, wu0, wd0, wg1, wu1, wd1, ys_ref,
                wgu0_s, wd0_s, wgu1_s, wd1_s):
    b = pl.program_id(0)
    rows = xs_ref.shape[1] // 2
    hid = wg0.shape[2]
    sets = ((wg0, wu0, wd0, wgu0_s, wd0_s), (wg1, wu1, wd1, wgu1_s, wd1_s))

    for half, (wg, wu, wd, wgu_s, wd_s) in enumerate(sets):
        blk = 2 * b + half

        @pl.when(jnp.logical_or(b == 0, be_ref[blk] != be_ref[jnp.maximum(blk - 2, 0)]))
        def _(wg=wg, wu=wu, wd=wd, wgu_s=wgu_s, wd_s=wd_s):
            wgu_s[:, :hid] = wg[0].astype(BF16)
            wgu_s[:, hid:] = wu[0].astype(BF16)
            wd_s[...] = wd[0].astype(BF16)

    any_valid = bv_ref[2 * b] + bv_ref[2 * b + 1] > 0

    @pl.when(any_valid)
    def _():
        for half, (_, _, _, wgu_s, wd_s) in enumerate(sets):
            valid = bv_ref[2 * b + half]
            r = slice(half * rows, (half + 1) * rows)
            x = _load_planes(xs_ref[0, r, :], xs_ref[1, r, :]).astype(BF16)
            y = _dot(_swiglu_hidden(x, wgu_s[...]).astype(BF16), wd_s[...])
            row = lax.broadcasted_iota(jnp.int32, (rows, 1), 0)
            _store_planes(ys_ref, r, jnp.where(row < valid, y, 0.0))

    @pl.when(jnp.logical_not(any_valid))
    def _():
        ys_ref[...] = jnp.zeros_like(ys_ref)


def _gmm(blk, xs, we_gate, we_up, we_down, rows, after):
    _, n_rows, q = xs.shape
    _, d, hid = we_gate.shape
    n_pairs = n_rows // (2 * rows) - 1
    w_spec = lambda w, half: pl.BlockSpec((1,) + w.shape[1:],
                                          lambda b, be, bv, bs, bd: (be[2 * b + half], 0, 0))
    in_specs = [pl.BlockSpec((2, 2 * rows, q), lambda b, be, bv, bs, bd: (0, bs[2 * b] // 2, 0)),
                w_spec(we_gate, 0), w_spec(we_up, 0), w_spec(we_down, 0),
                w_spec(we_gate, 1), w_spec(we_up, 1), w_spec(we_down, 1)]
    args = [xs, we_gate, we_up, we_down, we_gate, we_up, we_down]
    n_prefetch = 4
    body = _gmm_kernel
    if after is not None:
        pos = n_prefetch + len(args)
        in_specs.append(pl.BlockSpec(memory_space=pl.ANY))
        args.append(after)
        body = lambda *refs: _gmm_kernel(*refs[:pos], *refs[pos + 1:])
    return pl.pallas_call(
        body,
        grid_spec=pltpu.PrefetchScalarGridSpec(
            num_scalar_prefetch=n_prefetch,
            grid=(n_pairs,),
            in_specs=in_specs,
            out_specs=pl.BlockSpec((2, 2 * rows, q),
                                   lambda b, be, bv, bs, bd: (0, bd[2 * b] // 2, 0)),
            scratch_shapes=[pltpu.VMEM((d, 2 * hid), BF16), pltpu.VMEM((hid, d), BF16),
                            pltpu.VMEM((d, 2 * hid), BF16), pltpu.VMEM((hid, d), BF16)]),
        out_shape=jax.ShapeDtypeStruct(xs.shape, jnp.int32),
        compiler_params=_cparams("arbitrary"),
    )(blk[0], blk[1], blk[2], blk[3], *args)


def _combine_kernel(x1_ref, mod_ref, y8_ref, g8_ref, h2p_ref, wsgu_ref, wsd_ref, *rest):
    o_ref = rest[-1]
    g8 = g8_ref[...]
    h2 = _load_planes(h2p_ref[0], h2p_ref[1]).astype(BF16)
    ffn = _dot(_swiglu_hidden(h2, wsgu_ref[...]).astype(BF16), wsd_ref[...])
    for k in range(TOP_K):
        ffn = ffn + g8[:, k:k + 1] * _load_planes(y8_ref[k, 0], y8_ref[k, 1])
    o_ref[0] = x1_ref[0] + mod_ref[0, 5:6, :] * ffn


def _combine(x1, mod, y8, gate8, h2p, ws_gu, ws_d, b0, total, earlier, after):
    bsz, s, d = x1.shape
    tm = min(COMBINE_ROWS, s)
    ns = s // tm
    full = lambda shape: pl.BlockSpec(shape, lambda b, i: (0,) * len(shape))
    in_specs = [pl.BlockSpec((1, tm, d), lambda b, i: (b, i, 0)),
                pl.BlockSpec((1, N_MOD, d), lambda b, i: (b, 0, 0)),
                pl.BlockSpec((TOP_K, 2, tm, d // 4), lambda b, i: (0, 0, b * ns + i, 0)),
                pl.BlockSpec((tm, TOP_K), lambda b, i: (b * ns + i, 0)),
                pl.BlockSpec((2, tm, d // 4), lambda b, i: (0, b * ns + i, 0)),
                full(ws_gu.shape), full(ws_d.shape)]
    args = [x1, mod, y8, gate8, h2p, ws_gu, ws_d]
    aliases = {}
    if earlier is not None:
        in_specs.append(pl.BlockSpec(memory_space=pl.ANY))
        args.append(earlier)
        aliases = {len(args) - 1: 0}
    if after is not None:
        in_specs.append(pl.BlockSpec(memory_space=pl.ANY))
        args.append(after)
    return pl.pallas_call(
        _combine_kernel,
        grid=(bsz, ns),
        in_specs=in_specs,
        out_specs=pl.BlockSpec((1, tm, d), lambda b, i: (b0 + b, i, 0)),
        out_shape=jax.ShapeDtypeStruct((total, s, d), F32),
        input_output_aliases=aliases,
        compiler_params=_cparams("parallel", "parallel"),
    )(*args)


def kernel(x, c, w_ada, b_ada, norm1_g, w_in, sb_q_norm_g, sb_k_norm_g, hg_lb_logits, hg_norm_g,
           w_branch_sb, w_branch_hg, w_out, norm2_g, w_router, router_bias, w_e_gate, w_e_up,
           w_e_down, w_s_gate, w_s_up, w_s_down):
    bsz, s, d = x.shape
    depth = w_ada.shape[0]
    n_gate_cols = 2 * d
    qkv_col0 = n_gate_cols // LANES
    hg_col0 = qkv_col0 + 3 * SB_WIDTH // LANES
    for l in range(depth):
        n_mix = 3 * SB_WIDTH + 4 * HG_WIDTH
        w_in_l = jnp.concatenate([w_in[l][:, n_mix:], w_in[l][:, :n_mix]], axis=1).astype(BF16)
        wr_t = w_router[l].T
        wr_hi = wr_t.astype(BF16)
        wr_lo = (wr_t - wr_hi.astype(F32)).astype(BF16)
        ws_gu = jnp.concatenate([w_s_gate[l], w_s_up[l]], axis=1).astype(BF16)

        w_sb, w_hg = w_branch_sb[l].astype(BF16), w_branch_hg[l].astype(BF16)
        w_o, ws_d = w_out[l].astype(BF16), w_s_down[l].astype(BF16)

        mod = _ada(c, w_ada[l], b_ada[l]).reshape(bsz, N_MOD, d)
        n_parts = BATCH_PARTS if bsz % BATCH_PARTS == 0 else 1
        pb = bsz // n_parts
        t = pb * s
        q = d // 4
        n_blocks = -(-(t * TOP_K + N_EXPERTS * (DISPATCH_ROWS - 1)) // DISPATCH_ROWS)
        n_blocks += n_blocks % 2
        n_rows = (n_blocks + 2) * DISPATCH_ROWS
        plane_off = jnp.array([0, n_rows], jnp.int32)[None, :, None]

        def experts(st, after):
            ys = _gmm(st["blk"], st["xs"].reshape(2, n_rows, q), w_e_gate[l], w_e_up[l],
                      w_e_down[l], DISPATCH_ROWS, after)
            y8 = _sc_gather_rows(ys.reshape(2 * n_rows, q), st["row_idx"].reshape(1, TOP_K * 2 * t))
            return ys, y8.reshape(TOP_K, 2, t, q)

        out = None
        prev = None
        for p in range(n_parts):
            b0 = p * pb
            mod_p = mod[b0:b0 + pb]
            proj, qa, qb, knt, kc, lfh, lfl, qs, gs = _inproj(
                x, mod_p, norm1_g[l], w_in_l, sb_q_norm_g[l], sb_k_norm_g[l], hg_lb_logits, l,
                qkv_col0, b0, prev and prev["blk"])
            o_hg = _hgrn(proj, kc, lfh, lfl, qs, gs, hg_norm_g[l], hg_col0 + HG_WIDTH // LANES)
            if prev:
                prev["ys"], prev["y8"] = experts(prev, o_hg)
            o_sb = _sb_attention(proj, qa, qb, knt, qkv_col0, prev and prev["ys"])
            if prev:
                out = _combine(prev["x1"], prev["mod"], prev["y8"], prev["gate8"], prev["h2p"],
                               ws_gu, ws_d, prev["b0"], bsz, out, o_sb)
            x1, h2p, logits_t = _merge(x, o_sb, o_hg, proj, mod_p, norm2_g[l], w_sb, w_hg, w_o,
                                       wr_hi, wr_lo, b0, out)
            gates_t, rank_t, counts = _route(logits_t, router_bias[l])
            slot8, gate8, blk = _slots(gates_t, rank_t, counts, DISPATCH_ROWS, n_blocks)
            row_idx = (slot8[:, None, :] + plane_off).reshape(TOP_K, 2 * t)
            xs = _sc_scatter_rows(h2p.reshape(2 * t, q), row_idx, 2 * n_rows)
            prev = dict(blk=blk, xs=xs, row_idx=row_idx, x1=x1, h2p=h2p, mod=mod_p, gate8=gate8,
                        b0=b0)
        _, y8 = experts(prev, None)
        x = _combine(prev["x1"], prev["mod"], y8, prev["gate8"], prev["h2p"], ws_gu, ws_d,
                     prev["b0"], bsz, out, None)
    return x
```
